```python
import jax, jax.numpy as jnp
from jax import lax
import numpy as np

D_MODEL = 1024
BATCH = 4
SEQ = 4096
DEPTH = 1

CHUNK = 64
Q_BLOCK = 128
GLA_HEADS = 4
GLA_DK = 128
GLA_DV = 256
GLA_RANK = 16
GLA_TAU = 16.0
GLA_QK = GLA_HEADS * GLA_DK
GLA_V = GLA_HEADS * GLA_DV
FOX_HEADS = 8
FOX_HD = 128
FOX_W = FOX_HEADS * FOX_HD
N_EXPERTS = 32
TOP_K = 4
D_FF = 1024
SWIGLU_LIMIT = 7.0
SWIGLU_ALPHA = 1.702
MOE_BLOCK = 128
LN_EPS = 1e-5
DEEPNORM_ALPHA = (2 * DEPTH) ** 0.25
DEEPNORM_BETA = (8 * DEPTH) ** -0.25
IN_SPLITS = (GLA_QK, GLA_QK, GLA_V, GLA_RANK, GLA_V, FOX_W, FOX_W, FOX_W, FOX_HEADS, D_MODEL, D_MODEL)
IN_COLS = sum(IN_SPLITS)

kernel_name = "gla_fox_gated_hybrid_moe_deepnorm"


def split_cols(t, sizes):
    idx, acc = [], 0
    for s in sizes[:-1]:
        acc += s
        idx.append(acc)
    return jnp.split(t, idx, axis=-1)


def layer_norm(x, g, b):
    xf = x.astype(jnp.float32)
    mu = jnp.mean(xf, axis=-1, keepdims=True)
    var = jnp.mean(jnp.square(xf - mu), axis=-1, keepdims=True)
    y = (xf - mu) * lax.rsqrt(var + LN_EPS) * g.astype(jnp.float32) + b.astype(jnp.float32)
    return y.astype(x.dtype)


def gla_branch(q, k, v, gate_low, r, w_gate_up, b_gate, g_norm):
    B, S, _ = q.shape
    n = S // CHUNK
    f32 = jnp.float32
    log_a = jax.nn.log_sigmoid((gate_low @ w_gate_up + b_gate).astype(f32)) / GLA_TAU
    qh = q.astype(f32).reshape(B, n, CHUNK, GLA_HEADS, GLA_DK) * (GLA_DK ** -0.5)
    kh = k.astype(f32).reshape(B, n, CHUNK, GLA_HEADS, GLA_DK)
    vh = v.astype(f32).reshape(B, n, CHUNK, GLA_HEADS, GLA_DV)
    la = log_a.reshape(B, n, CHUNK, GLA_HEADS, GLA_DK)
    cum = jnp.cumsum(la, axis=2)
    total = cum[:, :, -1]
    k_dec = kh * jnp.exp(total[:, :, None] - cum)
    d_state = jnp.einsum('bnchk,bnchv->bnhkv', k_dec, vh)

    def step(state, inp):
        dec, ds = inp
        new = jnp.exp(dec)[..., None] * state + ds
        return new, new

    s0 = jnp.zeros((B, GLA_HEADS, GLA_DK, GLA_DV), f32)
    _, states = lax.scan(step, s0, (jnp.moveaxis(total, 1, 0), jnp.moveaxis(d_state, 1, 0)))
    states = jnp.moveaxis(states, 0, 1)
    o = jnp.einsum('bnchk,bnhkv->bnchv', qh, states)
    o = o * lax.rsqrt(jnp.mean(jnp.square(o), axis=-1, keepdims=True) + LN_EPS) * g_norm.astype(f32)
    o = o.reshape(B, S, GLA_V) * jax.nn.silu(r.astype(f32))
    return o.astype(q.dtype)


def fox_branch(q, k, v, f_logit):
    B, S, _ = q.shape
    f32 = jnp.float32
    qh = q.reshape(B, S, FOX_HEADS, FOX_HD).transpose(0, 2, 1, 3)
    kh = k.reshape(B, S, FOX_HEADS, FOX_HD).transpose(0, 2, 1, 3)
    vh = v.reshape(B, S, FOX_HEADS, FOX_HD).transpose(0, 2, 1, 3)
    cum_f = jnp.cumsum(jax.nn.log_sigmoid(f_logit.astype(f32)), axis=1).transpose(0, 2, 1)
    scale = FOX_HD ** -0.5
    outs = []
    for i in range(S // Q_BLOCK):
        lo, hi = i * Q_BLOCK, (i + 1) * Q_BLOCK
        s = jnp.einsum('bhqd,bhkd->bhqk', qh[:, :, lo:hi], kh[:, :, :hi]).astype(f32) * scale
        s = s + cum_f[:, :, lo:hi, None] - cum_f[:, :, None, :hi]
        causal = (lo + jnp.arange(Q_BLOCK))[:, None] >= jnp.arange(hi)[None, :]
        p = jax.nn.softmax(jnp.where(causal, s, -jnp.inf), axis=-1)
        outs.append(jnp.einsum('bhqk,bhkd->bhqd', p.astype(vh.dtype), vh[:, :, :hi]))
    o = jnp.concatenate(outs, axis=2)
    return o.transpose(0, 2, 1, 3).reshape(B, S, FOX_W)


def moe_ffn(h, w_router, b_router, w_gate_up, b_gate_up, w_down, b_down):
    B, S, D = h.shape
    T = B * S
    xt = h.reshape(T, D)
    logits = (xt @ w_router + b_router).astype(jnp.float32)
    top_val, top_idx = lax.top_k(logits, TOP_K)
    gates = jax.nn.softmax(top_val, axis=-1)
    A = T * TOP_K
    flat_e = top_idx.reshape(A)
    flat_tok = jnp.repeat(jnp.arange(T, dtype=jnp.int32), TOP_K)
    flat_g = gates.reshape(A)
    order = jnp.argsort(flat_e)
    se = flat_e[order]
    counts = jnp.bincount(flat_e, length=N_EXPERTS)
    starts = jnp.cumsum(counts) - counts
    padded = (counts + MOE_BLOCK - 1) // MOE_BLOCK * MOE_BLOCK
    pad_ends = jnp.cumsum(padded)
    pad_starts = pad_ends - padded
    dest = pad_starts[se] + jnp.arange(A, dtype=jnp.int32) - starts[se]
    n_blocks = (A + N_EXPERTS * (MOE_BLOCK - 1) + MOE_BLOCK - 1) // MOE_BLOCK
    P = n_blocks * MOE_BLOCK
    slot_tok = jnp.zeros((P,), jnp.int32).at[dest].set(flat_tok[order])
    slot_g = jnp.zeros((P,), jnp.float32).at[dest].set(flat_g[order])
    blk_start = jnp.arange(n_blocks, dtype=jnp.int32) * MOE_BLOCK
    blk_e = jnp.minimum(jnp.sum(blk_start[:, None] >= pad_ends[None, :], axis=1), N_EXPERTS - 1)
    xs = xt[slot_tok].reshape(n_blocks, MOE_BLOCK, D)

    def expert_block(args):
        xb, e = args
        gu = xb @ w_gate_up[e] + b_gate_up[e]
        gate = jnp.minimum(gu[:, :D_FF], SWIGLU_LIMIT)
        up = jnp.clip(gu[:, D_FF:], -SWIGLU_LIMIT, SWIGLU_LIMIT)
        glu = gate * jax.nn.sigmoid(SWIGLU_ALPHA * gate)
        return ((up + 1.0) * glu) @ w_down[e] + b_down[e]

    ys = lax.map(expert_block, (xs, blk_e)).reshape(P, D)
    ys = ys * slot_g[:, None].astype(ys.dtype)
    out = jax.ops.segment_sum(ys, slot_tok, num_segments=T)
    return out.reshape(B, S, D)


def setup_inputs(seed: int = 0) -> dict:
    key = jax.random.key(seed)
    ks = jax.random.split(key, 24)
    L, D, E = DEPTH, D_MODEL, N_EXPERTS
    nrm = lambda k, shape: jax.random.normal(k, shape, jnp.float32)
    col_scale = np.ones((IN_COLS,), np.float32)
    off = 2 * GLA_QK
    col_scale[off:off + GLA_V] = DEEPNORM_BETA
    off = 2 * GLA_QK + GLA_V + GLA_RANK + GLA_V + 2 * FOX_W
    col_scale[off:off + FOX_W] = DEEPNORM_BETA
    w_in = nrm(ks[1], (L, D, IN_COLS)) * (D ** -0.5) * jnp.asarray(col_scale)
    return {
        "x": nrm(ks[0], (BATCH, SEQ, D)),
        "ln_in_g": 1.0 + 0.02 * nrm(ks[2], (D,)),
        "ln_in_b": 0.02 * nrm(ks[3], (D,)),
        "w_in": w_in,
        "w_gla_gate_up": nrm(ks[4], (L, GLA_RANK, GLA_QK)) * (GLA_RANK ** -0.5),
        "b_gla_gate": 1.0 + 0.5 * nrm(ks[5], (L, GLA_QK)),
        "g_gla_norm": 1.0 + 0.02 * nrm(ks[6], (L, GLA_DV)),
        "b_forget": 4.0 + 0.5 * nrm(ks[7], (L, FOX_HEADS)),
        "w_gla_proj": nrm(ks[8], (L, GLA_V, D)) * (GLA_V ** -0.5) * DEEPNORM_BETA,
        "w_fox_proj": nrm(ks[9], (L, FOX_W, D)) * (FOX_W ** -0.5) * DEEPNORM_BETA,
        "w_out": nrm(ks[10], (L, D, D)) * (D ** -0.5) * DEEPNORM_BETA,
        "ln_mix_g": 1.0 + 0.02 * nrm(ks[11], (L, D)),
        "ln_mix_b": 0.02 * nrm(ks[12], (L, D)),
        "w_router": nrm(ks[13], (L, D, E)) * (D ** -0.5),
        "b_router": 0.01 * nrm(ks[14], (L, E)),
        "w_gate_up": nrm(ks[15], (L, E, D, 2 * D_FF)) * (D ** -0.5) * DEEPNORM_BETA,
        "b_gate_up": 0.02 * nrm(ks[16], (L, E, 2 * D_FF)),
        "w_down": nrm(ks[17], (L, E, D_FF, D)) * (D_FF ** -0.5) * DEEPNORM_BETA,
        "b_down": 0.02 * nrm(ks[18], (L, E, D)),
        "ln_ffn_g": 1.0 + 0.02 * nrm(ks[19], (L, D)),
        "ln_ffn_b": 0.02 * nrm(ks[20], (L, D)),
    }


def reference(x, ln_in_g, ln_in_b, w_in, w_gla_gate_up, b_gla_gate, g_gla_norm, b_forget,
              w_gla_proj, w_fox_proj, w_out, ln_mix_g, ln_mix_b, w_router, b_router,
              w_gate_up, b_gate_up, w_down, b_down, ln_ffn_g, ln_ffn_b):
    h = layer_norm(x, ln_in_g, ln_in_b)
    for l in range(DEPTH):
        proj = h @ w_in[l]
        (gq, gk, gv, g_low, g_r, fq, fk, fv, f_f, z_gla, z_fox) = split_cols(proj, IN_SPLITS)
        y_gla = gla_branch(gq, gk, gv, g_low, g_r, w_gla_gate_up[l], b_gla_gate[l], g_gla_norm[l]) @ w_gla_proj[l]
        y_fox = fox_branch(fq, fk, fv, f_f + b_forget[l]) @ w_fox_proj[l]
        mixed = jax.nn.sigmoid(z_gla) * y_gla + jax.nn.sigmoid(z_fox) * y_fox
        h = layer_norm(DEEPNORM_ALPHA * h + mixed @ w_out[l], ln_mix_g[l], ln_mix_b[l])
        ffn = moe_ffn(h, w_router[l], b_router[l], w_gate_up[l], b_gate_up[l], w_down[l], b_down[l])
        h = layer_norm(DEEPNORM_ALPHA * h + ffn, ln_ffn_g[l], ln_ffn_b[l])
    return h
```

```python
import functools

import jax
import jax.numpy as jnp
from jax import lax
from jax.experimental import pallas as pl
from jax.experimental.pallas import tpu as pltpu

F32 = jnp.float32
BF16 = jnp.bfloat16
I32 = jnp.int32
U32 = jnp.uint32

D_MODEL = 1024
CHUNK = 64
GLA_HEADS, GLA_DK, GLA_DV, GLA_RANK, GLA_TAU = 4, 128, 256, 16, 16.0
GLA_QK = GLA_HEADS * GLA_DK
GLA_V = GLA_HEADS * GLA_DV
FOX_HEADS, FOX_HD = 8, 128
FOX_W = FOX_HEADS * FOX_HD
N_EXPERTS, TOP_K, D_FF = 32, 4, 1024
SWIGLU_LIMIT, SWIGLU_ALPHA = 7.0, 1.702
LN_EPS = 1e-5
DEPTH = 1
DEEPNORM_ALPHA = (2 * DEPTH) ** 0.25
IN_SPLITS = (GLA_QK, GLA_QK, GLA_V, GLA_RANK, GLA_V, FOX_W, FOX_W, FOX_W, FOX_HEADS, D_MODEL, D_MODEL)

LANES = 128
SUBLANES = 8
VMEM_BYTES_V7X = 64 * 1024 * 1024

FORGET_COL0 = GLA_RANK
FORGET_PIECES = 3

TM_PROJ = 512
GLA_ROWS = 512
FOX_TQ = 512
FOX_TK = 512
CUM_ROWS = 256
RANK_TW = 512
MOE_BLK = 256
DISPATCH_TOK = 256
COMBINE_TOK = 256


def _cparams(sem, vmem_mib):
    return pltpu.CompilerParams(dimension_semantics=sem, vmem_limit_bytes=vmem_mib * 1024 * 1024)


def _log_sigmoid(z):
    return jnp.minimum(z, 0.0) - jnp.log1p(jnp.exp(-jnp.abs(z)))


def _layer_norm(x, g, b):
    mu = jnp.mean(x, axis=-1, keepdims=True)
    xc = x - mu
    var = jnp.mean(xc * xc, axis=-1, keepdims=True)
    return xc * lax.rsqrt(var + LN_EPS) * g + b


def _split_bf16(x, pieces):
    out = []
    for _ in range(pieces):
        p = x.astype(BF16)
        out.append(p)
        x = x - p.astype(F32)
    return out


_PROJ_WIDTHS = (GLA_QK, GLA_QK, GLA_V, GLA_V, FOX_W, FOX_W, FOX_W, D_MODEL, D_MODEL)


def _ln_inproj_kernel(x_ref, g_ref, b_ref, w_ref, ws_ref, h_ref, *out_refs):
    wide_refs, small_ref = out_refs[:-1], out_refs[-1]
    h = _layer_norm(x_ref[...], g_ref[...], b_ref[...])
    h_ref[...] = h
    hb = h.astype(BF16)
    off = 0
    for o_ref in wide_refs:
        n = o_ref.shape[1]
        o_ref[...] = jnp.dot(hb, w_ref[:, off:off + n], preferred_element_type=F32).astype(o_ref.dtype)
        off += n
    small_ref[...] = jnp.dot(hb, ws_ref[...], preferred_element_type=F32)


def _ln_inproj(x2, g, b, w_wide, w_small):
    t = x2.shape[0]
    tm = TM_PROJ
    row = lambda w: pl.BlockSpec((tm, w), lambda i: (i, 0))
    const = lambda a: pl.BlockSpec(a.shape, lambda i: (0, 0), pipeline_mode=pl.Buffered(1))
    out_shape = ([jax.ShapeDtypeStruct((t, D_MODEL), F32)]
                 + [jax.ShapeDtypeStruct((t, w), BF16) for w in _PROJ_WIDTHS]
                 + [jax.ShapeDtypeStruct((t, LANES), F32)])
    out_specs = [row(D_MODEL)] + [row(w) for w in _PROJ_WIDTHS] + [row(LANES)]
    return pl.pallas_call(
        _ln_inproj_kernel,
        out_shape=out_shape,
        grid=(t // tm,),
        in_specs=[row(D_MODEL), const(g), const(b), const(w_wide), const(w_small)],
        out_specs=out_specs,
        compiler_params=_cparams(("parallel",), 52),
        name="ln_inproj",
    )(x2, g, b, w_wide, w_small)


def _forget_cum_kernel(small_ref, bias_ref, sel_ref, kf_ref, *, inv_scale):
    s = small_ref.shape[0]
    r = CUM_ROWS
    ri = lax.broadcasted_iota(I32, (r, r), 0)
    ci = lax.broadcasted_iota(I32, (r, r), 1)
    tri = jnp.where(ci <= ri, 1.0, 0.0).astype(BF16)
    sel = sel_ref[...]
    carry = jnp.zeros((1, LANES), F32)
    for blk in range(s // r):
        rows = pl.ds(blk * r, r)
        ls = _log_sigmoid(small_ref[rows, :] + bias_ref[...])
        cum = carry
        for p in _split_bf16(ls, 3):
            cum = cum + jnp.dot(tri, p, preferred_element_type=F32)
        carry = cum[r - 1:r, :]
        neg = cum * (-inv_scale)
        p1, p2, p3 = _split_bf16(neg, FORGET_PIECES)
        kf = (p1.astype(F32) * sel[0:1, :] + p2.astype(F32) * sel[1:2, :] + p3.astype(F32) * sel[2:3, :])
        kf_ref[rows, :] = kf.astype(BF16)


def _forget_cum(small, bias_row, sel, batch, seq):
    return pl.pallas_call(
        functools.partial(_forget_cum_kernel, inv_scale=float(FOX_HD) ** 0.5),
        out_shape=jax.ShapeDtypeStruct((batch * seq, LANES), BF16),
        grid=(batch,),
        in_specs=[pl.BlockSpec((seq, LANES), lambda b: (b, 0)),
                  pl.BlockSpec((1, LANES), lambda b: (0, 0)),
                  pl.BlockSpec((SUBLANES, LANES), lambda b: (0, 0))],
        out_specs=pl.BlockSpec((seq, LANES), lambda b: (b, 0)),
        compiler_params=_cparams(("parallel",), 32),
        name="forget_cum",
    )(small, bias_row, sel)


def _gla_kernel(q_ref, k_ref, v_ref, r_ref, small_ref, wgu_ref, bg_ref, gn_ref, o_ref, state_ref):
    rows = q_ref.shape[0]

    @pl.when(pl.program_id(1) == 0)
    def _():
        state_ref[...] = jnp.zeros_like(state_ref)

    z = jnp.dot(small_ref[...].astype(BF16), wgu_ref[...], preferred_element_type=F32) + bg_ref[...]
    la = _log_sigmoid(z) * (1.0 / GLA_TAU)
    ri = lax.broadcasted_iota(I32, (rows, rows), 0)
    ci = lax.broadcasted_iota(I32, (rows, rows), 1)
    shift = CHUNK.bit_length() - 1
    same = lax.shift_right_logical(ri, shift) == lax.shift_right_logical(ci, shift)
    tri = jnp.where(same & (ci <= ri), 1.0, 0.0).astype(BF16)
    ones = jnp.where(same, 1.0, 0.0).astype(BF16)
    cum = jnp.zeros((rows, GLA_QK), F32)
    tot = jnp.zeros((rows, GLA_QK), F32)
    for p in _split_bf16(la, 2):
        cum = cum + jnp.dot(tri, p, preferred_element_type=F32)
        tot = tot + jnp.dot(ones, p, preferred_element_type=F32)
    kd = (k_ref[...].astype(F32) * jnp.exp(tot - cum)).astype(BF16)
    dec = jnp.exp(tot)

    scale = float(GLA_DK) ** -0.5
    for c in range(rows // CHUNK):
        cr = slice(c * CHUNK, (c + 1) * CHUNK)
        for h in range(GLA_HEADS):
            kc = slice(h * GLA_DK, (h + 1) * GLA_DK)
            vc = slice(h * GLA_DV, (h + 1) * GLA_DV)
            d_state = lax.dot_general(v_ref[cr, vc], kd[cr, kc], (((0,), (0,)), ((), ())),
                                      preferred_element_type=F32)
            st = state_ref[h] * dec[c * CHUNK:c * CHUNK + 1, kc] + d_state
            state_ref[h] = st
            o = lax.dot_general(q_ref[cr, kc], st.astype(BF16), (((1,), (1,)), ((), ())),
                                preferred_element_type=F32) * scale
            o = o * lax.rsqrt(jnp.mean(o * o, axis=-1, keepdims=True) + LN_EPS) * gn_ref[...]
            r = r_ref[cr, vc].astype(F32)
            o_ref[cr, vc] = (o * (r * jax.nn.sigmoid(r))).astype(o_ref.dtype)


def _gla(gq, gk, gv, gr, small, wgu_pad, b_gate, g_norm, batch, seq):
    rows = GLA_ROWS
    nsteps = seq // rows
    row = lambda w: pl.BlockSpec((rows, w), lambda b, i: (b * nsteps + i, 0))
    const = lambda a: pl.BlockSpec(a.shape, lambda b, i: (0, 0))
    return pl.pallas_call(
        _gla_kernel,
        out_shape=jax.ShapeDtypeStruct((batch * seq, GLA_V), BF16),
        grid=(batch, nsteps),
        in_specs=[row(GLA_QK), row(GLA_QK), row(GLA_V), row(GLA_V), row(LANES),
                  const(wgu_pad), const(b_gate), const(g_norm)],
        out_specs=row(GLA_V),
        scratch_shapes=[pltpu.VMEM((GLA_HEADS, GLA_DV, GLA_DK), F32)],
        compiler_params=_cparams(("parallel", "arbitrary"), 40),
        name="gla",
    )(gq, gk, gv, gr, small, wgu_pad, b_gate, g_norm)


def _fox_kernel(q_ref, k_ref, v_ref, kf_ref, qsel_ref, o_ref):
    tq, tk = FOX_TQ, FOX_TK
    i = pl.program_id(2)
    log2e = 1.4426950408889634
    c = (float(FOX_HD) ** -0.5) * log2e
    qa = jnp.concatenate([q_ref[...], jnp.broadcast_to(qsel_ref[0:1, :], (tq, LANES))], axis=1)

    def tile(j, carry, masked):
        m, l, acc = carry
        rows = pl.ds(pl.multiple_of(j * tk, tk), tk)
        ka = jnp.concatenate([k_ref[rows, :], kf_ref[rows, :]], axis=1)
        s = lax.dot_general(ka, qa, (((1,), (1,)), ((), ())), preferred_element_type=F32) * c
        if masked:
            key = j * tk + lax.broadcasted_iota(I32, (tk, tq), 0)
            qry = i * tq + lax.broadcasted_iota(I32, (tk, tq), 1)
            s = jnp.where(key <= qry, s, -jnp.inf)
        m_new = jnp.maximum(m, jnp.max(s, axis=0, keepdims=True))
        alpha = jnp.exp2(m - m_new)
        p = jnp.exp2(s - m_new)
        l = alpha * l + jnp.sum(p, axis=0, keepdims=True)
        pv = lax.dot_general(v_ref[rows, :], p.astype(BF16), (((0,), (0,)), ((), ())),
                             preferred_element_type=F32)
        return m_new, l, alpha * acc + pv

    carry = (jnp.full((1, tq), -jnp.inf, F32), jnp.zeros((1, tq), F32), jnp.zeros((FOX_HD, tq), F32))
    ratio = tq // tk
    carry = lax.fori_loop(0, i * ratio, lambda j, cr: tile(j, cr, False), carry)
    for d in range(ratio):
        carry = tile(i * ratio + d, carry, True)
    _, l, acc = carry
    o_ref[...] = (acc / l).T.astype(o_ref.dtype)


def _fox(fq, fk, fv, kf, qsel, batch, seq):
    nq = seq // FOX_TQ
    return pl.pallas_call(
        _fox_kernel,
        out_shape=jax.ShapeDtypeStruct((batch * seq, FOX_W), BF16),
        grid=(batch, FOX_HEADS, nq),
        in_specs=[pl.BlockSpec((FOX_TQ, FOX_HD), lambda b, h, i: (b * nq + i, h)),
                  pl.BlockSpec((seq, FOX_HD), lambda b, h, i: (b, h)),
                  pl.BlockSpec((seq, FOX_HD), lambda b, h, i: (b, h)),
                  pl.BlockSpec((seq, LANES), lambda b, h, i: (b, 0)),
                  pl.BlockSpec((SUBLANES, LANES), lambda b, h, i: (h, 0))],
        out_specs=pl.BlockSpec((FOX_TQ, FOX_HD), lambda b, h, i: (b * nq + i, h)),
        compiler_params=_cparams(("parallel", "parallel", "arbitrary"), 40),
        name="fox",
    )(fq, fk, fv, kf, qsel)


def _pack_bf16_pair(lo, hi):
    lo_bits = pltpu.bitcast(lo.astype(BF16).astype(F32), U32)
    hi_bits = pltpu.bitcast(hi.astype(BF16).astype(F32), U32)
    return (hi_bits & jnp.uint32(0xFFFF0000)) | (lo_bits >> 16)


def _unpack_bf16_pair(w):
    lo = pltpu.bitcast(w << 16, F32).astype(BF16)
    hi = pltpu.bitcast(w & jnp.uint32(0xFFFF0000), F32).astype(BF16)
    return jnp.concatenate([lo, hi], axis=1)


def _post_mix_kernel(gla_ref, fox_ref, zg_ref, zf_ref, h_ref, wg_ref, wf_ref, wo_ref, g_ref, b_ref,
                     wr_ref, br_ref, h2_ref, h2p_ref, idx_ref, route_ref):
    tm = h_ref.shape[0]
    y_gla = jnp.dot(gla_ref[...], wg_ref[...], preferred_element_type=F32)
    y_fox = jnp.dot(fox_ref[...], wf_ref[...], preferred_element_type=F32)
    mixed = (jax.nn.sigmoid(zg_ref[...].astype(F32)) * y_gla
             + jax.nn.sigmoid(zf_ref[...].astype(F32)) * y_fox)
    pre = DEEPNORM_ALPHA * h_ref[...] + jnp.dot(mixed.astype(BF16), wo_ref[...], preferred_element_type=F32)
    h2 = _layer_norm(pre, g_ref[...], b_ref[...])
    h2_ref[...] = h2
    half = D_MODEL // 2
    h2p_ref[...] = _pack_bf16_pair(h2[:, :half], h2[:, half:])

    logits = lax.dot_general(wr_ref[...], h2.astype(BF16), (((1,), (1,)), ((), ())),
                             preferred_element_type=F32)
    logits = logits + jnp.tile(br_ref[...], (1, tm // LANES))
    eidx = lax.broadcasted_iota(I32, (N_EXPERTS, tm), 0)
    vals, idxs = [], []
    for _ in range(TOP_K):
        mx = jnp.max(logits, axis=0, keepdims=True)
        ik = jnp.min(jnp.where(logits == mx, eidx, N_EXPERTS), axis=0, keepdims=True)
        vals.append(mx)
        idxs.append(ik)
        logits = jnp.where(eidx == ik, -jnp.inf, logits)
    exps = [jnp.exp(v - vals[0]) for v in vals]
    denom = exps[0] + exps[1] + exps[2] + exps[3]
    rid = lax.broadcasted_iota(I32, (SUBLANES, tm), 0)
    idx8 = jnp.zeros((SUBLANES, tm), I32)
    gate8 = jnp.zeros((SUBLANES, tm), F32)
    for k in range(TOP_K):
        idx8 = jnp.where(rid == k, idxs[k], idx8)
        gate8 = jnp.where(rid == k, exps[k] / denom, gate8)
    idx_ref[...] = idx8
    gates = jnp.concatenate([gate8, jnp.zeros((LANES - SUBLANES, tm), F32)], axis=0)
    route_ref[...] = gates.T


def _post_mix(gla_o, fox_o, zg, zf, h, wg, wf, wo, g, b, wr_t, br_rep):
    t = h.shape[0]
    tm = TM_PROJ
    row = lambda w: pl.BlockSpec((tm, w), lambda i: (i, 0))
    const = lambda a: pl.BlockSpec(a.shape, lambda i: (0, 0))
    return pl.pallas_call(
        _post_mix_kernel,
        out_shape=[jax.ShapeDtypeStruct((t, D_MODEL), F32),
                   jax.ShapeDtypeStruct((t, D_MODEL // 2), U32),
                   jax.ShapeDtypeStruct((SUBLANES, t), I32),
                   jax.ShapeDtypeStruct((t, LANES), F32)],
        grid=(t // tm,),
        in_specs=[row(GLA_V), row(FOX_W), row(D_MODEL), row(D_MODEL), row(D_MODEL),
                  const(wg), const(wf), const(wo), const(g), const(b), const(wr_t), const(br_rep)],
        out_specs=[row(D_MODEL), row(D_MODEL // 2),
                   pl.BlockSpec((SUBLANES, tm), lambda i: (0, i)), row(LANES)],
        compiler_params=_cparams(("parallel",), 48),
        name="post_mix",
    )(gla_o, fox_o, zg, zf, h, wg, wf, wo, g, b, wr_t, br_rep)


def _onehot_rows(idx8, tw):
    eidx = lax.broadcasted_iota(I32, (N_EXPERTS, tw), 0)
    hit = eidx == idx8[0:1, :]
    for k in range(1, TOP_K):
        hit = hit | (eidx == idx8[k:k + 1, :])
    return eidx, hit


def _rank_kernel(idx_ref, rank_ref, counts_ref, carry_ref):
    tw = idx_ref.shape[1]

    @pl.when(pl.program_id(0) == 0)
    def _():
        carry_ref[...] = jnp.zeros_like(carry_ref)

    idx8 = idx_ref[...]
    eidx, hit = _onehot_rows(idx8, tw)
    onehot = jnp.where(hit, 1.0, 0.0).astype(BF16)
    ri = lax.broadcasted_iota(I32, (tw, tw + LANES), 0)
    ci = lax.broadcasted_iota(I32, (tw, tw + LANES), 1)
    upper = jnp.where((ri < ci) | (ci >= tw), 1.0, 0.0).astype(BF16)
    cnt = jnp.dot(onehot, upper, preferred_element_type=F32)
    before = cnt[:, :tw] + jnp.tile(carry_ref[...], (1, tw // LANES))
    rid = lax.broadcasted_iota(I32, (SUBLANES, tw), 0)
    rank8 = jnp.zeros((SUBLANES, tw), I32)
    for k in range(TOP_K):
        rk = jnp.sum(jnp.where(eidx == idx8[k:k + 1, :], before, 0.0), axis=0, keepdims=True)
        rank8 = jnp.where(rid == k, rk.astype(I32), rank8)
    rank_ref[...] = rank8
    carry_ref[...] = carry_ref[...] + cnt[:, tw:]
    counts_ref[...] = carry_ref[...]


def _rank(idx_t):
    t = idx_t.shape[1]
    tw = RANK_TW
    return pl.pallas_call(
        _rank_kernel,
        out_shape=[jax.ShapeDtypeStruct((SUBLANES, t), I32),
                   jax.ShapeDtypeStruct((N_EXPERTS, LANES), F32)],
        grid=(t // tw,),
        in_specs=[pl.BlockSpec((SUBLANES, tw), lambda i: (0, i))],
        out_specs=[pl.BlockSpec((SUBLANES, tw), lambda i: (0, i)),
                   pl.BlockSpec((N_EXPERTS, LANES), lambda i: (0, 0))],
        scratch_shapes=[pltpu.VMEM((N_EXPERTS, LANES), F32)],
        compiler_params=_cparams(("arbitrary",), 32),
        name="rank",
    )(idx_t)


def _dest_kernel(idx_ref, rank_ref, counts_ref, dest_ref, blk_e_ref, ends_ref, *, nb_pad):
    tw = idx_ref.shape[1]
    nblk = jnp.floor((counts_ref[...] + (MOE_BLK - 1)) * (1.0 / MOE_BLK))
    ri = lax.broadcasted_iota(I32, (N_EXPERTS, N_EXPERTS), 0)
    ci = lax.broadcasted_iota(I32, (N_EXPERTS, N_EXPERTS), 1)
    tri = jnp.where(ci <= ri, 1.0, 0.0).astype(BF16)
    end_blk = jnp.dot(tri, nblk.astype(BF16), preferred_element_type=F32)
    start_row = (end_blk - nblk) * float(MOE_BLK)
    idx8 = idx_ref[...]
    eidx = lax.broadcasted_iota(I32, (N_EXPERTS, tw), 0)
    start_t = jnp.tile(start_row, (1, tw // LANES))
    rid = lax.broadcasted_iota(I32, (SUBLANES, tw), 0)
    dest8 = jnp.zeros((SUBLANES, tw), I32)
    for k in range(TOP_K):
        st = jnp.sum(jnp.where(eidx == idx8[k:k + 1, :], start_t, 0.0), axis=0, keepdims=True)
        dest8 = jnp.where(rid == k, st.astype(I32), dest8)
    dest_ref[...] = dest8 + rank_ref[...]
    bid = lax.broadcasted_iota(I32, (N_EXPERTS, nb_pad), 1).astype(F32)
    ends_t = jnp.tile(end_blk, (1, nb_pad // LANES))
    be = jnp.sum(jnp.where(ends_t <= bid, 1.0, 0.0), axis=0, keepdims=True)
    blk_e_ref[...] = jnp.broadcast_to(jnp.minimum(be, N_EXPERTS - 1.0), (SUBLANES, nb_pad)).astype(I32)
    ends_ref[...] = end_blk.astype(I32)


def _dest(idx_t, rank_t, counts, nb_pad):
    t = idx_t.shape[1]
    tw = RANK_TW
    tok = pl.BlockSpec((SUBLANES, tw), lambda i: (0, i))
    return pl.pallas_call(
        functools.partial(_dest_kernel, nb_pad=nb_pad),
        out_shape=[jax.ShapeDtypeStruct((SUBLANES, t), I32),
                   jax.ShapeDtypeStruct((SUBLANES, nb_pad), I32),
                   jax.ShapeDtypeStruct((N_EXPERTS, LANES), I32)],
        grid=(t // tw,),
        in_specs=[tok, tok, pl.BlockSpec((N_EXPERTS, LANES), lambda i: (0, 0))],
        out_specs=[tok, pl.BlockSpec((SUBLANES, nb_pad), lambda i: (0, 0)),
                   pl.BlockSpec((N_EXPERTS, LANES), lambda i: (0, 0))],
        compiler_params=_cparams(("arbitrary",), 32),
        name="dest",
    )(idx_t, rank_t, counts)


def _row_copy(src_ref, src_row, dst_ref, dst_row, sem):
    return pltpu.make_async_copy(src_ref.at[pl.ds(src_row, 1), :], dst_ref.at[pl.ds(dst_row, 1), :], sem)


def _dispatch_kernel(ends_ref, dest_ref, h2p_ref, xs_ref, zero_ref, sem, zsem):
    td = h2p_ref.shape[0]

    nb = xs_ref.shape[0] // MOE_BLK
    n_used = ends_ref[N_EXPERTS - 1]

    def zero_block(blk):
        start = pl.multiple_of(blk * MOE_BLK, MOE_BLK)
        return pltpu.make_async_copy(zero_ref, xs_ref.at[pl.ds(start, MOE_BLK), :], zsem)

    def has_rows(e):
        prev = jnp.where(e == 0, 0, ends_ref[jnp.maximum(e - 1, 0)])
        return ends_ref[e] > prev

    @pl.when(pl.program_id(0) == 0)
    def _():
        zero_ref[...] = jnp.zeros_like(zero_ref)

        def each(fn):
            def last_block(e, _):
                @pl.when(has_rows(e))
                def _():
                    fn(zero_block(ends_ref[e] - 1))
                return 0

            def tail_block(blk, _):
                fn(zero_block(blk))
                return 0

            lax.fori_loop(0, N_EXPERTS, last_block, 0)
            lax.fori_loop(n_used, nb, tail_block, 0)

        each(lambda cp: cp.start())
        each(lambda cp: cp.wait())

    def start_rows(t, _):
        for k in range(TOP_K):
            _row_copy(h2p_ref, t, xs_ref, dest_ref[k, t], sem).start()
        return 0

    def wait_rows(t, _):
        for k in range(TOP_K):
            _row_copy(h2p_ref, t, xs_ref, dest_ref[k, t], sem).wait()
        return 0

    lax.fori_loop(0, td, start_rows, 0)
    lax.fori_loop(0, td, wait_rows, 0)


def _dispatch(ends, dest_t, h2p, nb):
    t = h2p.shape[0]
    td = DISPATCH_TOK
    grid_spec = pltpu.PrefetchScalarGridSpec(
        num_scalar_prefetch=1,
        grid=(t // td,),
        in_specs=[pl.BlockSpec((SUBLANES, td), lambda i, ends: (0, i), memory_space=pltpu.SMEM),
                  pl.BlockSpec((td, D_MODEL // 2), lambda i, ends: (i, 0))],
        out_specs=pl.BlockSpec(memory_space=pl.ANY),
        scratch_shapes=[pltpu.VMEM((MOE_BLK, D_MODEL // 2), U32),
                        pltpu.SemaphoreType.DMA(()), pltpu.SemaphoreType.DMA(())],
    )
    return pl.pallas_call(
        _dispatch_kernel,
        out_shape=jax.ShapeDtypeStruct((nb * MOE_BLK, D_MODEL // 2), U32),
        grid_spec=grid_spec,
        compiler_params=_cparams(("arbitrary",), 32),
        name="dispatch",
    )(ends, dest_t, h2p)


def _experts_kernel(blk_e_ref, nused_ref, xs_ref, wgu_ref, bgu_ref, wd_ref, bd_ref, ys_ref, wgu_bf, wd_bf):
    j = pl.program_id(0)
    active = j < nused_ref[0]
    e = blk_e_ref[j]
    prev = blk_e_ref[jnp.maximum(j - 1, 0)]
    fresh = (j == 0) | (e != prev)

    @pl.when(active & fresh)
    def _():
        step = 256
        for r in range(0, D_MODEL, step):
            wgu_bf[r:r + step, :] = wgu_ref[r:r + step, :].astype(BF16)
        for r in range(0, D_FF, step):
            wd_bf[r:r + step, :] = wd_ref[r:r + step, :].astype(BF16)

    @pl.when(active)
    def _():
        x = _unpack_bf16_pair(xs_ref[...])
        gu = jnp.dot(x, wgu_bf[...], preferred_element_type=F32) + bgu_ref[...]
        gate = jnp.minimum(gu[:, :D_FF], SWIGLU_LIMIT)
        up = jnp.clip(gu[:, D_FF:], -SWIGLU_LIMIT, SWIGLU_LIMIT)
        glu = gate * jax.nn.sigmoid(SWIGLU_ALPHA * gate)
        act = ((up + 1.0) * glu).astype(BF16)
        ys_ref[...] = jnp.dot(act, wd_bf[...], preferred_element_type=F32) + bd_ref[...]

    @pl.when(jnp.logical_not(active))
    def _():
        ys_ref[...] = jnp.zeros_like(ys_ref)


def _experts(blk_e, nused, xs, w_gate_up, b_gate_up, w_down, b_down, nb):
    def blk(j, blk_e, nused):
        return jnp.minimum(j, nused[0] - 1)

    def exp_of(j, blk_e, nused):
        return blk_e[jnp.minimum(j, nused[0] - 1)]

    grid_spec = pltpu.PrefetchScalarGridSpec(
        num_scalar_prefetch=2,
        grid=(nb,),
        in_specs=[pl.BlockSpec((MOE_BLK, D_MODEL // 2), lambda j, be, nu: (blk(j, be, nu), 0)),
                  pl.BlockSpec((None, D_MODEL, 2 * D_FF), lambda j, be, nu: (exp_of(j, be, nu), 0, 0)),
                  pl.BlockSpec((None, 1, 2 * D_FF), lambda j, be, nu: (exp_of(j, be, nu), 0, 0)),
                  pl.BlockSpec((None, D_FF, D_MODEL), lambda j, be, nu: (exp_of(j, be, nu), 0, 0)),
                  pl.BlockSpec((None, 1, D_MODEL), lambda j, be, nu: (exp_of(j, be, nu), 0, 0))],
        out_specs=pl.BlockSpec((MOE_BLK, D_MODEL), lambda j, be, nu: (j, 0)),
        scratch_shapes=[pltpu.VMEM((D_MODEL, 2 * D_FF), BF16), pltpu.VMEM((D_FF, D_MODEL), BF16)],
    )
    return pl.pallas_call(
        _experts_kernel,
        out_shape=jax.ShapeDtypeStruct((nb * MOE_BLK, D_MODEL), F32),
        grid_spec=grid_spec,
        compiler_params=_cparams(("arbitrary",), 52),
        name="experts",
    )(blk_e, nused, xs, w_gate_up, b_gate_up, w_down, b_down)


def _combine_kernel(dest_ref, route_ref, h2_ref, g_ref, b_ref, ys_ref, o_ref, buf_ref, sem):
    tc = h2_ref.shape[0]

    def copy(t, k):
        return pltpu.make_async_copy(ys_ref.at[pl.ds(dest_ref[k, t], 1), :],
                                     buf_ref.at[k, pl.ds(t, 1), :], sem)

    def start_rows(t, _):
        for k in range(TOP_K):
            copy(t, k).start()
        return 0

    def wait_rows(t, _):
        for k in range(TOP_K):
            copy(t, k).wait()
        return 0

    lax.fori_loop(0, tc, start_rows, 0)
    lax.fori_loop(0, tc, wait_rows, 0)
    route = route_ref[...]
    ffn = jnp.zeros((tc, D_MODEL), F32)
    for k in range(TOP_K):
        ffn = ffn + route[:, k:k + 1] * buf_ref[k]
    o_ref[...] = _layer_norm(DEEPNORM_ALPHA * h2_ref[...] + ffn, g_ref[...], b_ref[...])


def _combine(dest_t, route, h2, g, b, ys):
    t = h2.shape[0]
    tc = COMBINE_TOK
    return pl.pallas_call(
        _combine_kernel,
        out_shape=jax.ShapeDtypeStruct((t, D_MODEL), F32),
        grid=(t // tc,),
        in_specs=[pl.BlockSpec((SUBLANES, tc), lambda i: (0, i), memory_space=pltpu.SMEM),
                  pl.BlockSpec((tc, LANES), lambda i: (i, 0)),
                  pl.BlockSpec((tc, D_MODEL), lambda i: (i, 0)),
                  pl.BlockSpec((1, D_MODEL), lambda i: (0, 0)),
                  pl.BlockSpec((1, D_MODEL), lambda i: (0, 0)),
                  pl.BlockSpec(memory_space=pl.ANY)],
        out_specs=pl.BlockSpec((tc, D_MODEL), lambda i: (i, 0)),
        scratch_shapes=[pltpu.VMEM((TOP_K, tc, D_MODEL), F32), pltpu.SemaphoreType.DMA(())],
        compiler_params=_cparams(("arbitrary",), 32),
        name="combine",
    )(dest_t, route, h2, g, b, ys)


def _split_cols(w):
    out, off = [], 0
    for n in IN_SPLITS:
        out.append(w[:, off:off + n])
        off += n
    return out


def kernel(x, ln_in_g, ln_in_b, w_in, w_gla_gate_up, b_gla_gate, g_gla_norm, b_forget, w_gla_proj, w_fox_proj, w_out, ln_mix_g, ln_mix_b, w_router, b_router, w_gate_up, b_gate_up, w_down, b_down, ln_ffn_g, ln_ffn_b):
    batch, seq, d = x.shape
    assert d == D_MODEL and w_in.shape[0] == DEPTH == 1
    assert seq % max(GLA_ROWS, FOX_TQ, FOX_TK, CUM_ROWS) == 0
    t = batch * seq
    assert t % max(TM_PROJ, RANK_TW, DISPATCH_TOK, COMBINE_TOK) == 0 and t // MOE_BLK <= 256
    row = lambda v: v.reshape(1, -1).astype(F32)

    wq, wk, wv, w_low, wr, wfq, wfk, wfv, wff, wzg, wzf = _split_cols(w_in[0])
    w_wide = jnp.concatenate([wq, wk, wv, wr, wfq, wfk, wfv, wzg, wzf], axis=1).astype(BF16)
    n_forget = FOX_HEADS * FORGET_PIECES
    w_small = jnp.concatenate(
        [w_low, jnp.repeat(wff, FORGET_PIECES, axis=1),
         jnp.zeros((D_MODEL, LANES - GLA_RANK - n_forget), F32)], axis=1).astype(BF16)
    lane = jnp.arange(LANES)
    in_forget = (lane >= FORGET_COL0) & (lane < FORGET_COL0 + n_forget)
    piece = (lane - FORGET_COL0) % FORGET_PIECES
    forget_bias = jnp.zeros((1, LANES), F32).at[0, FORGET_COL0:FORGET_COL0 + n_forget].set(
        jnp.repeat(b_forget[0].astype(F32), FORGET_PIECES))
    piece_sel = jnp.stack([(in_forget & (piece == p)).astype(F32) for p in range(FORGET_PIECES)]
                          + [jnp.zeros((LANES,), F32)] * (SUBLANES - FORGET_PIECES))
    head_of_lane = (lane - FORGET_COL0) // FORGET_PIECES
    qsel = jnp.repeat(jnp.stack([(in_forget & (head_of_lane == h)) for h in range(FOX_HEADS)]), SUBLANES,
                      axis=0).astype(BF16)
    wgu_pad = jnp.concatenate([w_gla_gate_up[0], jnp.zeros((LANES - GLA_RANK, GLA_QK), F32)], axis=0).astype(BF16)

    x2 = x.reshape(t, d)
    h, gq, gk, gv, gr, fq, fk, fv, zg, zf, small = _ln_inproj(x2, row(ln_in_g), row(ln_in_b), w_wide, w_small)
    kf = _forget_cum(small, forget_bias, piece_sel, batch, seq)
    gla_o = _gla(gq, gk, gv, gr, small, wgu_pad, row(b_gla_gate[0]), row(g_gla_norm[0]), batch, seq)
    fox_o = _fox(fq, fk, fv, kf, qsel, batch, seq)

    br_rep = jnp.broadcast_to(b_router[0].astype(F32)[:, None], (N_EXPERTS, LANES))
    h2, h2p, idx_t, route = _post_mix(
        gla_o, fox_o, zg, zf, h, w_gla_proj[0].astype(BF16), w_fox_proj[0].astype(BF16), w_out[0].astype(BF16),
        row(ln_mix_g[0]), row(ln_mix_b[0]), w_router[0].T.astype(BF16), br_rep)

    nb = (t * TOP_K + N_EXPERTS * (MOE_BLK - 1) + MOE_BLK - 1) // MOE_BLK
    nb_pad = (nb + LANES - 1) // LANES * LANES
    rank_t, counts = _rank(idx_t)
    dest_t, blk_e8, ends = _dest(idx_t, rank_t, counts, nb_pad)
    ends1 = ends[:, 0]
    xs = _dispatch(ends1, dest_t, h2p, nb)
    ys = _experts(blk_e8[0, :nb], ends1[N_EXPERTS - 1:], xs, w_gate_up[0], b_gate_up[0][:, None, :],
                  w_down[0], b_down[0][:, None, :], nb)
    out = _combine(dest_t, route, h2, row(ln_ffn_g[0]), row(ln_ffn_b[0]), ys)
    return out.reshape(batch, seq, d)
```

```python
import functools

import jax
import jax.numpy as jnp
from jax import lax
from jax.experimental import pallas as pl
from jax.experimental.pallas import tpu as pltpu

F32 = jnp.float32
BF16 = jnp.bfloat16
I32 = jnp.int32

D_MODEL = 1024
CHUNK = 64
GLA_HEADS, GLA_DK, GLA_DV, GLA_RANK, GLA_TAU = 4, 128, 256, 16, 16.0
GLA_QK = GLA_HEADS * GLA_DK
GLA_V = GLA_HEADS * GLA_DV
FOX_HEADS, FOX_HD = 8, 128
FOX_W = FOX_HEADS * FOX_HD
N_EXPERTS, TOP_K, D_FF = 32, 4, 1024
SWIGLU_LIMIT, SWIGLU_ALPHA = 7.0, 1.702
LN_EPS = 1e-5
DEPTH = 1
DEEPNORM_ALPHA = (2 * DEPTH) ** 0.25
IN_SPLITS = (GLA_QK, GLA_QK, GLA_V, GLA_RANK, GLA_V, FOX_W, FOX_W, FOX_W, FOX_HEADS, D_MODEL, D_MODEL)

LANES = 128
SUBLANES = 8
VMEM_BYTES_V7X = 64 * 1024 * 1024

FORGET_COL0 = GLA_RANK
FORGET_PIECES = 3

TM_PROJ = 512
GLA_ROWS = 512
FOX_TQ = 512
FOX_TK = 512
FOX_HEADS_PER_STEP = 2
FOX_STRIP = 64
LOG2E = 1.4426950408889634
CUM_ROWS = 256
RANK_TW = 512
MOE_BLK = 512
DISPATCH_TOK = 256
COMBINE_TOK = 256
ROW_DMA_UNROLL = 8


def _cparams(sem, vmem_mib):
    return pltpu.CompilerParams(dimension_semantics=sem, vmem_limit_bytes=vmem_mib * 1024 * 1024)


def _log_sigmoid(z):
    return jnp.minimum(z, 0.0) - jnp.log1p(jnp.exp(-jnp.abs(z)))


def _layer_norm(x, g, b):
    mu = jnp.mean(x, axis=-1, keepdims=True)
    xc = x - mu
    var = jnp.mean(xc * xc, axis=-1, keepdims=True)
    return xc * lax.rsqrt(var + LN_EPS) * g + b


def _split_bf16(x, pieces):
    out = []
    for _ in range(pieces):
        p = x.astype(BF16)
        out.append(p)
        x = x - p.astype(F32)
    return out


_PROJ_WIDTHS = (GLA_QK, GLA_QK, GLA_V, GLA_V, FOX_W, FOX_W, FOX_W, D_MODEL, D_MODEL)


def _ln_inproj_kernel(x_ref, g_ref, b_ref, w_ref, ws_ref, h_ref, *out_refs):
    wide_refs, small_ref = out_refs[:-1], out_refs[-1]
    h = _layer_norm(x_ref[...], g_ref[...], b_ref[...])
    h_ref[...] = h
    hb = h.astype(BF16)
    off = 0
    for o_ref in wide_refs:
        n = o_ref.shape[1]
        o_ref[...] = jnp.dot(hb, w_ref[:, off:off + n], preferred_element_type=F32).astype(o_ref.dtype)
        off += n
    small_ref[...] = jnp.dot(hb, ws_ref[...], preferred_element_type=F32)


def _ln_inproj(x2, g, b, w_wide, w_small):
    t = x2.shape[0]
    tm = TM_PROJ
    row = lambda w: pl.BlockSpec((tm, w), lambda i: (i, 0))
    const = lambda a: pl.BlockSpec(a.shape, lambda i: (0, 0), pipeline_mode=pl.Buffered(1))
    out_shape = ([jax.ShapeDtypeStruct((t, D_MODEL), F32)]
                 + [jax.ShapeDtypeStruct((t, w), BF16) for w in _PROJ_WIDTHS]
                 + [jax.ShapeDtypeStruct((t, LANES), F32)])
    out_specs = [row(D_MODEL)] + [row(w) for w in _PROJ_WIDTHS] + [row(LANES)]
    return pl.pallas_call(
        _ln_inproj_kernel,
        out_shape=out_shape,
        grid=(t // tm,),
        in_specs=[row(D_MODEL), const(g), const(b), const(w_wide), const(w_small)],
        out_specs=out_specs,
        compiler_params=_cparams(("parallel",), 52),
        name="ln_inproj",
    )(x2, g, b, w_wide, w_small)


def _forget_cum_kernel(small_ref, bias_ref, sel_ref, kf_ref):
    s = small_ref.shape[0]
    r = CUM_ROWS
    ri = lax.broadcasted_iota(I32, (r, r), 0)
    ci = lax.broadcasted_iota(I32, (r, r), 1)
    tri = jnp.where(ci <= ri, 1.0, 0.0).astype(BF16)
    sel = sel_ref[...]
    carry = jnp.zeros((1, LANES), F32)
    for blk in range(s // r):
        rows = pl.ds(blk * r, r)
        ls = _log_sigmoid(small_ref[rows, :] + bias_ref[...])
        cum = carry
        for p in _split_bf16(ls, 3):
            cum = cum + jnp.dot(tri, p, preferred_element_type=F32)
        carry = cum[r - 1:r, :]
        neg = cum * (-LOG2E)
        p1, p2, p3 = _split_bf16(neg, FORGET_PIECES)
        kf = (p1.astype(F32) * sel[0:1, :] + p2.astype(F32) * sel[1:2, :] + p3.astype(F32) * sel[2:3, :])
        kf_ref[rows, :] = kf.astype(BF16)


def _forget_cum(small, bias_row, sel, batch, seq):
    return pl.pallas_call(
        _forget_cum_kernel,
        out_shape=jax.ShapeDtypeStruct((batch * seq, LANES), BF16),
        grid=(batch,),
        in_specs=[pl.BlockSpec((seq, LANES), lambda b: (b, 0)),
                  pl.BlockSpec((1, LANES), lambda b: (0, 0)),
                  pl.BlockSpec((SUBLANES, LANES), lambda b: (0, 0))],
        out_specs=pl.BlockSpec((seq, LANES), lambda b: (b, 0)),
        compiler_params=_cparams(("parallel",), 32),
        name="forget_cum",
    )(small, bias_row, sel)


def _gla_kernel(q_ref, k_ref, v_ref, r_ref, small_ref, wgu_ref, bg_ref, gn_ref, o_ref,
                state_ref, ds_ref, sb_ref, of_ref):
    rows = q_ref.shape[0]
    nchunk = rows // CHUNK
    kcol = lambda h: slice(h * GLA_DK, (h + 1) * GLA_DK)
    vcol = lambda h: slice(h * GLA_DV, (h + 1) * GLA_DV)
    crow = lambda c: slice(c * CHUNK, (c + 1) * CHUNK)

    @pl.when(pl.program_id(1) == 0)
    def _():
        state_ref[...] = jnp.zeros_like(state_ref)

    z = jnp.dot(small_ref[...].astype(BF16), wgu_ref[...], preferred_element_type=F32) + bg_ref[...]
    la = _log_sigmoid(z) * (1.0 / GLA_TAU)
    ri = lax.broadcasted_iota(I32, (rows, rows), 0)
    ci = lax.broadcasted_iota(I32, (rows, rows), 1)
    shift = CHUNK.bit_length() - 1
    same = lax.shift_right_logical(ri, shift) == lax.shift_right_logical(ci, shift)
    tri = jnp.where(same & (ci <= ri), 1.0, 0.0).astype(BF16)
    cum = jnp.zeros((rows, GLA_QK), F32)
    for p in _split_bf16(la, 2):
        cum = cum + jnp.dot(tri, p, preferred_element_type=F32)
    last = [cum[(c + 1) * CHUNK - 1:(c + 1) * CHUNK, :] for c in range(nchunk)]
    tot = jnp.concatenate([jnp.broadcast_to(t, (CHUNK, GLA_QK)) for t in last], axis=0)
    kd = (k_ref[...].astype(F32) * jnp.exp(tot - cum)).astype(BF16)
    dec_rows = jnp.exp(jnp.concatenate(last + [jnp.zeros((LANES - nchunk, GLA_QK), F32)], axis=0))
    dec_cols = dec_rows.T

    for c in range(nchunk):
        for h in range(GLA_HEADS):
            ds_ref[c, h] = lax.dot_general(kd[crow(c), kcol(h)], v_ref[crow(c), vcol(h)],
                                           (((0,), (0,)), ((), ())), preferred_element_type=F32)
    for h in range(GLA_HEADS):
        st = state_ref[h]
        for c in range(nchunk):
            st = st * jnp.broadcast_to(dec_cols[kcol(h), c:c + 1], (GLA_DK, GLA_DV)) + ds_ref[c, h]
            sb_ref[c, h] = st.astype(BF16)
        state_ref[h] = st
    scale = float(GLA_DK) ** -0.5
    for c in range(nchunk):
        for h in range(GLA_HEADS):
            of_ref[crow(c), vcol(h)] = jnp.dot(q_ref[crow(c), kcol(h)], sb_ref[c, h],
                                               preferred_element_type=F32) * scale
    for h in range(GLA_HEADS):
        o = of_ref[:, vcol(h)]
        o = o * lax.rsqrt(jnp.mean(o * o, axis=-1, keepdims=True) + LN_EPS) * gn_ref[...]
        r = r_ref[:, vcol(h)].astype(F32)
        o_ref[:, vcol(h)] = (o * (r * jax.nn.sigmoid(r))).astype(o_ref.dtype)


def _gla(gq, gk, gv, gr, small, wgu_pad, b_gate, g_norm, batch, seq):
    rows = GLA_ROWS
    nsteps = seq // rows
    row = lambda w: pl.BlockSpec((rows, w), lambda b, i: (b * nsteps + i, 0))
    const = lambda a: pl.BlockSpec(a.shape, lambda b, i: (0, 0))
    return pl.pallas_call(
        _gla_kernel,
        out_shape=jax.ShapeDtypeStruct((batch * seq, GLA_V), BF16),
        grid=(batch, nsteps),
        in_specs=[row(GLA_QK), row(GLA_QK), row(GLA_V), row(GLA_V), row(LANES),
                  const(wgu_pad), const(b_gate), const(g_norm)],
        out_specs=row(GLA_V),
        scratch_shapes=[pltpu.VMEM((GLA_HEADS, GLA_DK, GLA_DV), F32),
                        pltpu.VMEM((rows // CHUNK, GLA_HEADS, GLA_DK, GLA_DV), F32),
                        pltpu.VMEM((rows // CHUNK, GLA_HEADS, GLA_DK, GLA_DV), BF16),
                        pltpu.VMEM((rows, GLA_V), F32)],
        compiler_params=_cparams(("parallel", "arbitrary"), 40),
        name="gla",
    )(gq, gk, gv, gr, small, wgu_pad, b_gate, g_norm)


def _fox_kernel(q_ref, k_ref, v_ref, kf_ref, qsel_ref, o_ref, s_ref, acc_ref, m_ref, l_ref):
    tq, tk = FOX_TQ, FOX_TK
    i = pl.program_id(2)
    c = (float(FOX_HD) ** -0.5) * LOG2E
    hd = FOX_HD
    heads = range(FOX_HEADS_PER_STEP)
    qas = [jnp.concatenate([(q_ref[:, h * hd:(h + 1) * hd].astype(F32) * c).astype(BF16),
                            jnp.broadcast_to(qsel_ref[h * SUBLANES:h * SUBLANES + 1, :], (tq, LANES))], axis=1)
           for h in heads]

    def scores(h, j):
        rows = pl.ds(pl.multiple_of(j * tk, tk), tk)
        ka = jnp.concatenate([k_ref[rows, h * hd:(h + 1) * hd], kf_ref[rows, :]], axis=1)
        return lax.dot_general(ka, qas[h], (((1,), (1,)), ((), ())), preferred_element_type=F32)

    def values(h, j, p):
        rows = pl.ds(pl.multiple_of(j * tk, tk), tk)
        return lax.dot_general(v_ref[rows, h * hd:(h + 1) * hd], p, (((0,), (0,)), ((), ())),
                               preferred_element_type=F32)

    def step(j, slot, masked, prefetch):
        if prefetch:
            for h in heads:
                s_ref[1 - slot, h] = scores(h, j + 1)
        for h in heads:
            s = s_ref[slot, h]
            if masked:
                causal = lax.broadcasted_iota(I32, (tk, tq), 0) <= lax.broadcasted_iota(I32, (tk, tq), 1)
                s = jnp.where(causal, s, -jnp.inf)
            m_old = m_ref[h]
            m_new = jnp.maximum(m_old, jnp.broadcast_to(jnp.max(s, axis=0, keepdims=True), (SUBLANES, tq)))
            alpha = jnp.exp2(m_old - m_new)
            p = jnp.exp2(s - m_new[0:1, :])
            m_ref[h] = m_new
            l_ref[h] = alpha * l_ref[h] + jnp.sum(p, axis=0, keepdims=True)
            acc_ref[h] = alpha[0:1, :] * acc_ref[h] + values(h, j, p.astype(BF16))

    for h in heads:
        s_ref[0, h] = scores(h, 0)
        acc_ref[h] = jnp.zeros((hd, tq), F32)
        m_ref[h] = jnp.full((SUBLANES, tq), -jnp.inf, F32)
        l_ref[h] = jnp.zeros((SUBLANES, tq), F32)

    def pair(jj, _):
        step(2 * jj, 0, False, True)
        step(2 * jj + 1, 1, False, True)
        return 0

    lax.fori_loop(0, i // 2, pair, 0)
    odd = lax.rem(i, 2) == 1

    @pl.when(odd)
    def _():
        step(i - 1, 0, False, True)
        step(i, 1, True, False)

    @pl.when(jnp.logical_not(odd))
    def _():
        step(i, 0, True, False)

    for h in heads:
        o_ref[:, h * hd:(h + 1) * hd] = (acc_ref[h] / l_ref[h][0:1, :]).T.astype(o_ref.dtype)


def _fox(fq, fk, fv, kf, qsel, batch, seq):
    nq = seq // FOX_TQ
    hps = FOX_HEADS_PER_STEP
    w = hps * FOX_HD
    return pl.pallas_call(
        _fox_kernel,
        out_shape=jax.ShapeDtypeStruct((batch * seq, FOX_W), BF16),
        grid=(batch, FOX_HEADS // hps, nq),
        in_specs=[pl.BlockSpec((FOX_TQ, w), lambda b, h, i: (b * nq + i, h)),
                  pl.BlockSpec((seq, w), lambda b, h, i: (b, h)),
                  pl.BlockSpec((seq, w), lambda b, h, i: (b, h)),
                  pl.BlockSpec((seq, LANES), lambda b, h, i: (b, 0)),
                  pl.BlockSpec((hps * SUBLANES, LANES), lambda b, h, i: (h, 0))],
        out_specs=pl.BlockSpec((FOX_TQ, w), lambda b, h, i: (b * nq + i, h)),
        scratch_shapes=[pltpu.VMEM((2, hps, FOX_TK, FOX_TQ), F32), pltpu.VMEM((hps, FOX_HD, FOX_TQ), F32),
                        pltpu.VMEM((hps, SUBLANES, FOX_TQ), F32), pltpu.VMEM((hps, SUBLANES, FOX_TQ), F32)],
        compiler_params=_cparams(("parallel", "parallel", "arbitrary"), 40),
        name="fox",
    )(fq, fk, fv, kf, qsel)


def _post_mix_kernel(gla_ref, fox_ref, zg_ref, zf_ref, h_ref, wg_ref, wf_ref, wo_ref, g_ref, b_ref,
                     wr_ref, br_ref, h2_ref, idx_ref, route_ref):
    tm = h_ref.shape[0]
    y_gla = jnp.dot(gla_ref[...], wg_ref[...], preferred_element_type=F32)
    y_fox = jnp.dot(fox_ref[...], wf_ref[...], preferred_element_type=F32)
    mixed = (jax.nn.sigmoid(zg_ref[...].astype(F32)) * y_gla
             + jax.nn.sigmoid(zf_ref[...].astype(F32)) * y_fox)
    pre = DEEPNORM_ALPHA * h_ref[...] + jnp.dot(mixed.astype(BF16), wo_ref[...], preferred_element_type=F32)
    h2 = _layer_norm(pre, g_ref[...], b_ref[...])
    h2_ref[...] = h2

    logits = lax.dot_general(wr_ref[...], h2.astype(BF16), (((1,), (1,)), ((), ())),
                             preferred_element_type=F32)
    logits = logits + jnp.tile(br_ref[...], (1, tm // LANES))
    eidx = lax.broadcasted_iota(I32, (N_EXPERTS, tm), 0)
    vals, idxs = [], []
    for _ in range(TOP_K):
        mx = jnp.max(logits, axis=0, keepdims=True)
        ik = jnp.min(jnp.where(logits == mx, eidx, N_EXPERTS), axis=0, keepdims=True)
        vals.append(mx)
        idxs.append(ik)
        logits = jnp.where(eidx == ik, -jnp.inf, logits)
    exps = [jnp.exp(v - vals[0]) for v in vals]
    denom = exps[0] + exps[1] + exps[2] + exps[3]
    rid = lax.broadcasted_iota(I32, (SUBLANES, tm), 0)
    idx8 = jnp.zeros((SUBLANES, tm), I32)
    gate8 = jnp.zeros((SUBLANES, tm), F32)
    for k in range(TOP_K):
        idx8 = jnp.where(rid == k, idxs[k], idx8)
        gate8 = jnp.where(rid == k, exps[k] / denom, gate8)
    idx_ref[...] = idx8
    gates = jnp.concatenate([gate8, jnp.zeros((LANES - SUBLANES, tm), F32)], axis=0)
    route_ref[...] = gates.T


def _post_mix(gla_o, fox_o, zg, zf, h, wg, wf, wo, g, b, wr_t, br_rep):
    t = h.shape[0]
    tm = TM_PROJ
    row = lambda w: pl.BlockSpec((tm, w), lambda i: (i, 0))
    const = lambda a: pl.BlockSpec(a.shape, lambda i: (0, 0))
    return pl.pallas_call(
        _post_mix_kernel,
        out_shape=[jax.ShapeDtypeStruct((t, D_MODEL), F32),
                   jax.ShapeDtypeStruct((SUBLANES, t), I32),
                   jax.ShapeDtypeStruct((t, LANES), F32)],
        grid=(t // tm,),
        in_specs=[row(GLA_V), row(FOX_W), row(D_MODEL), row(D_MODEL), row(D_MODEL),
                  const(wg), const(wf), const(wo), const(g), const(b), const(wr_t), const(br_rep)],
        out_specs=[row(D_MODEL),
                   pl.BlockSpec((SUBLANES, tm), lambda i: (0, i)), row(LANES)],
        compiler_params=_cparams(("parallel",), 48),
        name="post_mix",
    )(gla_o, fox_o, zg, zf, h, wg, wf, wo, g, b, wr_t, br_rep)


def _onehot_rows(idx8, tw):
    eidx = lax.broadcasted_iota(I32, (N_EXPERTS, tw), 0)
    hit = eidx == idx8[0:1, :]
    for k in range(1, TOP_K):
        hit = hit | (eidx == idx8[k:k + 1, :])
    return eidx, hit


def _rank_kernel(idx_ref, rank_ref, counts_ref, carry_ref):
    tw = idx_ref.shape[1]

    @pl.when(pl.program_id(0) == 0)
    def _():
        carry_ref[...] = jnp.zeros_like(carry_ref)

    idx8 = idx_ref[...]
    eidx, hit = _onehot_rows(idx8, tw)
    onehot = jnp.where(hit, 1.0, 0.0).astype(BF16)
    ri = lax.broadcasted_iota(I32, (tw, tw + LANES), 0)
    ci = lax.broadcasted_iota(I32, (tw, tw + LANES), 1)
    upper = jnp.where((ri < ci) | (ci >= tw), 1.0, 0.0).astype(BF16)
    cnt = jnp.dot(onehot, upper, preferred_element_type=F32)
    before = cnt[:, :tw] + jnp.tile(carry_ref[...], (1, tw // LANES))
    rid = lax.broadcasted_iota(I32, (SUBLANES, tw), 0)
    rank8 = jnp.zeros((SUBLANES, tw), I32)
    for k in range(TOP_K):
        rk = jnp.sum(jnp.where(eidx == idx8[k:k + 1, :], before, 0.0), axis=0, keepdims=True)
        rank8 = jnp.where(rid == k, rk.astype(I32), rank8)
    rank_ref[...] = rank8
    carry_ref[...] = carry_ref[...] + cnt[:, tw:]
    counts_ref[...] = carry_ref[...]


def _rank(idx_t):
    t = idx_t.shape[1]
    tw = RANK_TW
    return pl.pallas_call(
        _rank_kernel,
        out_shape=[jax.ShapeDtypeStruct((SUBLANES, t), I32),
                   jax.ShapeDtypeStruct((N_EXPERTS, LANES), F32)],
        grid=(t // tw,),
        in_specs=[pl.BlockSpec((SUBLANES, tw), lambda i: (0, i))],
        out_specs=[pl.BlockSpec((SUBLANES, tw), lambda i: (0, i)),
                   pl.BlockSpec((N_EXPERTS, LANES), lambda i: (0, 0))],
        scratch_shapes=[pltpu.VMEM((N_EXPERTS, LANES), F32)],
        compiler_params=_cparams(("arbitrary",), 32),
        name="rank",
    )(idx_t)


def _dest_kernel(idx_ref, rank_ref, counts_ref, dest_ref, blk_e_ref, ends_ref, *, nb_pad):
    tw = idx_ref.shape[1]
    nblk = jnp.floor((counts_ref[...] + (MOE_BLK - 1)) * (1.0 / MOE_BLK))
    ri = lax.broadcasted_iota(I32, (N_EXPERTS, N_EXPERTS), 0)
    ci = lax.broadcasted_iota(I32, (N_EXPERTS, N_EXPERTS), 1)
    tri = jnp.where(ci <= ri, 1.0, 0.0).astype(BF16)
    end_blk = jnp.dot(tri, nblk.astype(BF16), preferred_element_type=F32)
    start_row = (end_blk - nblk) * float(MOE_BLK)
    idx8 = idx_ref[...]
    eidx = lax.broadcasted_iota(I32, (N_EXPERTS, tw), 0)
    start_t = jnp.tile(start_row, (1, tw // LANES))
    rid = lax.broadcasted_iota(I32, (SUBLANES, tw), 0)
    dest8 = jnp.zeros((SUBLANES, tw), I32)
    for k in range(TOP_K):
        st = jnp.sum(jnp.where(eidx == idx8[k:k + 1, :], start_t, 0.0), axis=0, keepdims=True)
        dest8 = jnp.where(rid == k, st.astype(I32), dest8)
    dest_ref[...] = dest8 + rank_ref[...]
    bid = lax.broadcasted_iota(I32, (N_EXPERTS, nb_pad), 1).astype(F32)
    ends_t = jnp.tile(end_blk, (1, nb_pad // LANES))
    be = jnp.sum(jnp.where(ends_t <= bid, 1.0, 0.0), axis=0, keepdims=True)
    blk_e_ref[...] = jnp.broadcast_to(jnp.minimum(be, N_EXPERTS - 1.0), (SUBLANES, nb_pad)).astype(I32)
    ends_ref[...] = end_blk.astype(I32)


def _dest(idx_t, rank_t, counts, nb_pad):
    t = idx_t.shape[1]
    tw = RANK_TW
    tok = pl.BlockSpec((SUBLANES, tw), lambda i: (0, i))
    return pl.pallas_call(
        functools.partial(_dest_kernel, nb_pad=nb_pad),
        out_shape=[jax.ShapeDtypeStruct((SUBLANES, t), I32),
                   jax.ShapeDtypeStruct((SUBLANES, nb_pad), I32),
                   jax.ShapeDtypeStruct((N_EXPERTS, LANES), I32)],
        grid=(t // tw,),
        in_specs=[tok, tok, pl.BlockSpec((N_EXPERTS, LANES), lambda i: (0, 0))],
        out_specs=[tok, pl.BlockSpec((SUBLANES, nb_pad), lambda i: (0, 0)),
                   pl.BlockSpec((N_EXPERTS, LANES), lambda i: (0, 0))],
        compiler_params=_cparams(("arbitrary",), 32),
        name="dest",
    )(idx_t, rank_t, counts)


def _row_copy(src_ref, src_row, dst_ref, dst_row, sem):
    return pltpu.make_async_copy(src_ref.at[pl.ds(src_row, 1), :], dst_ref.at[pl.ds(dst_row, 1), :], sem)


def _dispatch_kernel(ends_ref, dest_ref, h2_ref, xs_ref, zero_ref, sem, zsem):
    td = h2_ref.shape[0]

    nb = xs_ref.shape[0] // MOE_BLK
    n_used = ends_ref[N_EXPERTS - 1]

    def zero_block(blk):
        start = pl.multiple_of(blk * MOE_BLK, MOE_BLK)
        return pltpu.make_async_copy(zero_ref, xs_ref.at[pl.ds(start, MOE_BLK), :], zsem)

    def has_rows(e):
        prev = jnp.where(e == 0, 0, ends_ref[jnp.maximum(e - 1, 0)])
        return ends_ref[e] > prev

    @pl.when(pl.program_id(0) == 0)
    def _():
        zero_ref[...] = jnp.zeros_like(zero_ref)

        def each(fn):
            def last_block(e, _):
                @pl.when(has_rows(e))
                def _():
                    fn(zero_block(ends_ref[e] - 1))
                return 0

            def tail_block(blk, _):
                fn(zero_block(blk))
                return 0

            lax.fori_loop(0, N_EXPERTS, last_block, 0)
            lax.fori_loop(n_used, nb, tail_block, 0)

        each(lambda cp: cp.start())
        each(lambda cp: cp.wait())

    def start_rows(t, _):
        for k in range(TOP_K):
            _row_copy(h2_ref, t, xs_ref, dest_ref[k, t], sem).start(priority=k % 2)
        return 0

    lax.fori_loop(0, td, start_rows, 0, unroll=ROW_DMA_UNROLL)
    for k in range(TOP_K):
        pltpu.make_async_copy(h2_ref, xs_ref.at[pl.ds(0, td), :], sem).wait()


def _dispatch(ends, dest_t, h2, nb):
    t = h2.shape[0]
    td = DISPATCH_TOK
    grid_spec = pltpu.PrefetchScalarGridSpec(
        num_scalar_prefetch=1,
        grid=(t // td,),
        in_specs=[pl.BlockSpec((SUBLANES, td), lambda i, ends: (0, i), memory_space=pltpu.SMEM),
                  pl.BlockSpec((td, D_MODEL), lambda i, ends: (i, 0))],
        out_specs=pl.BlockSpec(memory_space=pl.ANY),
        scratch_shapes=[pltpu.VMEM((MOE_BLK, D_MODEL), F32),
                        pltpu.SemaphoreType.DMA(()), pltpu.SemaphoreType.DMA(())],
    )
    return pl.pallas_call(
        _dispatch_kernel,
        out_shape=jax.ShapeDtypeStruct((nb * MOE_BLK, D_MODEL), F32),
        grid_spec=grid_spec,
        compiler_params=_cparams(("arbitrary",), 32),
        name="dispatch",
    )(ends, dest_t, h2)


def _experts_kernel(blk_e_ref, nused_ref, xs_ref, wgu_ref, bgu_ref, wd_ref, bd_ref, ys_ref, wgu_bf, wd_bf):
    j = pl.program_id(0)
    active = j < nused_ref[0]
    e = blk_e_ref[j]
    prev = blk_e_ref[jnp.maximum(j - 1, 0)]
    fresh = (j == 0) | (e != prev)

    @pl.when(active & fresh)
    def _():
        step = 256
        for r in range(0, D_MODEL, step):
            wgu_bf[r:r + step, :] = wgu_ref[r:r + step, :].astype(BF16)
        for r in range(0, D_FF, step):
            wd_bf[r:r + step, :] = wd_ref[r:r + step, :].astype(BF16)

    @pl.when(active)
    def _():
        x = xs_ref[...].astype(BF16)
        gu = jnp.dot(x, wgu_bf[...], preferred_element_type=F32) + bgu_ref[...]
        gate = jnp.minimum(gu[:, :D_FF], SWIGLU_LIMIT)
        up = jnp.clip(gu[:, D_FF:], -SWIGLU_LIMIT, SWIGLU_LIMIT)
        glu = gate * jax.nn.sigmoid(SWIGLU_ALPHA * gate)
        act = ((up + 1.0) * glu).astype(BF16)
        ys_ref[...] = jnp.dot(act, wd_bf[...], preferred_element_type=F32) + bd_ref[...]

    @pl.when(jnp.logical_not(active))
    def _():
        ys_ref[...] = jnp.zeros_like(ys_ref)


def _experts(blk_e, nused, xs, w_gate_up, b_gate_up, w_down, b_down, nb):
    def blk(j, blk_e, nused):
        return jnp.minimum(j, nused[0] - 1)

    def exp_of(j, blk_e, nused):
        return blk_e[jnp.minimum(j, nused[0] - 1)]

    grid_spec = pltpu.PrefetchScalarGridSpec(
        num_scalar_prefetch=2,
        grid=(nb,),
        in_specs=[pl.BlockSpec((MOE_BLK, D_MODEL), lambda j, be, nu: (blk(j, be, nu), 0)),
                  pl.BlockSpec((None, D_MODEL, 2 * D_FF), lambda j, be, nu: (exp_of(j, be, nu), 0, 0)),
                  pl.BlockSpec((None, 1, 2 * D_FF), lambda j, be, nu: (exp_of(j, be, nu), 0, 0)),
                  pl.BlockSpec((None, D_FF, D_MODEL), lambda j, be, nu: (exp_of(j, be, nu), 0, 0)),
                  pl.BlockSpec((None, 1, D_MODEL), lambda j, be, nu: (exp_of(j, be, nu), 0, 0))],
        out_specs=pl.BlockSpec((MOE_BLK, D_MODEL), lambda j, be, nu: (j, 0)),
        scratch_shapes=[pltpu.VMEM((D_MODEL, 2 * D_FF), BF16), pltpu.VMEM((D_FF, D_MODEL), BF16)],
    )
    return pl.pallas_call(
        _experts_kernel,
        out_shape=jax.ShapeDtypeStruct((nb * MOE_BLK, D_MODEL), F32),
        grid_spec=grid_spec,
        compiler_params=_cparams(("arbitrary",), 52),
        name="experts",
    )(blk_e, nused, xs, w_gate_up, b_gate_up, w_down, b_down)


def _combine_kernel(dest_ref, dest_next_ref, route_ref, h2_ref, g_ref, b_ref, ys_ref, o_ref, buf_ref, sems):
    tc = h2_ref.shape[0]
    i = pl.program_id(0)
    slot = lax.rem(i, 2)

    def gather(d_ref, s):
        def start_rows(t, _):
            for k in range(TOP_K):
                pltpu.make_async_copy(ys_ref.at[pl.ds(d_ref[k, t], 1), :],
                                      buf_ref.at[s, k, pl.ds(t, 1), :], sems.at[s]).start(priority=k % 2)
            return 0

        lax.fori_loop(0, tc, start_rows, 0, unroll=ROW_DMA_UNROLL)

    @pl.when(i == 0)
    def _():
        gather(dest_ref, 0)

    @pl.when(i + 1 < pl.num_programs(0))
    def _():
        gather(dest_next_ref, 1 - slot)

    for k in range(TOP_K):
        pltpu.make_async_copy(ys_ref.at[pl.ds(0, tc), :], buf_ref.at[slot, k], sems.at[slot]).wait()
    route = route_ref[...]
    ffn = jnp.zeros((tc, D_MODEL), F32)
    for k in range(TOP_K):
        ffn = ffn + route[:, k:k + 1] * buf_ref[slot, k]
    o_ref[...] = _layer_norm(DEEPNORM_ALPHA * h2_ref[...] + ffn, g_ref[...], b_ref[...])


def _combine(dest_t, route, h2, g, b, ys):
    t = h2.shape[0]
    tc = COMBINE_TOK
    last = t // tc - 1
    return pl.pallas_call(
        _combine_kernel,
        out_shape=jax.ShapeDtypeStruct((t, D_MODEL), F32),
        grid=(t // tc,),
        in_specs=[pl.BlockSpec((SUBLANES, tc), lambda i: (0, i), memory_space=pltpu.SMEM),
                  pl.BlockSpec((SUBLANES, tc), lambda i: (0, jnp.minimum(i + 1, last)), memory_space=pltpu.SMEM),
                  pl.BlockSpec((tc, LANES), lambda i: (i, 0)),
                  pl.BlockSpec((tc, D_MODEL), lambda i: (i, 0)),
                  pl.BlockSpec((1, D_MODEL), lambda i: (0, 0)),
                  pl.BlockSpec((1, D_MODEL), lambda i: (0, 0)),
                  pl.BlockSpec(memory_space=pl.ANY)],
        out_specs=pl.BlockSpec((tc, D_MODEL), lambda i: (i, 0)),
        scratch_shapes=[pltpu.VMEM((2, TOP_K, tc, D_MODEL), F32), pltpu.SemaphoreType.DMA((2,))],
        compiler_params=_cparams(("arbitrary",), 32),
        name="combine",
    )(dest_t, dest_t, route, h2, g, b, ys)


def _split_cols(w):
    out, off = [], 0
    for n in IN_SPLITS:
        out.append(w[:, off:off + n])
        off += n
    return out


def kernel(x, ln_in_g, ln_in_b, w_in, w_gla_gate_up, b_gla_gate, g_gla_norm, b_forget, w_gla_proj, w_fox_proj, w_out, ln_mix_g, ln_mix_b, w_router, b_router, w_gate_up, b_gate_up, w_down, b_down, ln_ffn_g, ln_ffn_b):
    batch, seq, d = x.shape
    assert d == D_MODEL and w_in.shape[0] == DEPTH == 1
    assert seq % max(GLA_ROWS, FOX_TQ, FOX_TK, CUM_ROWS) == 0
    t = batch * seq
    assert t % max(TM_PROJ, RANK_TW, DISPATCH_TOK, COMBINE_TOK) == 0 and t // MOE_BLK <= 256
    row = lambda v: v.reshape(1, -1).astype(F32)

    wq, wk, wv, w_low, wr, wfq, wfk, wfv, wff, wzg, wzf = _split_cols(w_in[0])
    w_wide = jnp.concatenate([wq, wk, wv, wr, wfq, wfk, wfv, wzg, wzf], axis=1).astype(BF16)
    n_forget = FOX_HEADS * FORGET_PIECES
    w_small = jnp.concatenate(
        [w_low, jnp.repeat(wff, FORGET_PIECES, axis=1),
         jnp.zeros((D_MODEL, LANES - GLA_RANK - n_forget), F32)], axis=1).astype(BF16)
    lane = jnp.arange(LANES)
    in_forget = (lane >= FORGET_COL0) & (lane < FORGET_COL0 + n_forget)
    piece = (lane - FORGET_COL0) % FORGET_PIECES
    forget_bias = jnp.zeros((1, LANES), F32).at[0, FORGET_COL0:FORGET_COL0 + n_forget].set(
        jnp.repeat(b_forget[0].astype(F32), FORGET_PIECES))
    piece_sel = jnp.stack([(in_forget & (piece == p)).astype(F32) for p in range(FORGET_PIECES)]
                          + [jnp.zeros((LANES,), F32)] * (SUBLANES - FORGET_PIECES))
    head_of_lane = (lane - FORGET_COL0) // FORGET_PIECES
    qsel = jnp.repeat(jnp.stack([(in_forget & (head_of_lane == h)) for h in range(FOX_HEADS)]), SUBLANES,
                      axis=0).astype(BF16)
    wgu_pad = jnp.concatenate([w_gla_gate_up[0], jnp.zeros((LANES - GLA_RANK, GLA_QK), F32)], axis=0).astype(BF16)

    x2 = x.reshape(t, d)
    h, gq, gk, gv, gr, fq, fk, fv, zg, zf, small = _ln_inproj(x2, row(ln_in_g), row(ln_in_b), w_wide, w_small)
    kf = _forget_cum(small, forget_bias, piece_sel, batch, seq)
    gla_o = _gla(gq, gk, gv, gr, small, wgu_pad, row(b_gla_gate[0]), row(g_gla_norm[0]), batch, seq)
    fox_o = _fox(fq, fk, fv, kf, qsel, batch, seq)

    br_rep = jnp.broadcast_to(b_router[0].astype(F32)[:, None], (N_EXPERTS, LANES))
    h2, idx_t, route = _post_mix(
        gla_o, fox_o, zg, zf, h, w_gla_proj[0].astype(BF16), w_fox_proj[0].astype(BF16), w_out[0].astype(BF16),
        row(ln_mix_g[0]), row(ln_mix_b[0]), w_router[0].T.astype(BF16), br_rep)

    nb = (t * TOP_K + N_EXPERTS * (MOE_BLK - 1) + MOE_BLK - 1) // MOE_BLK
    nb_pad = (nb + LANES - 1) // LANES * LANES
    rank_t, counts = _rank(idx_t)
    dest_t, blk_e8, ends = _dest(idx_t, rank_t, counts, nb_pad)
    ends1 = ends[:, 0]
    xs = _dispatch(ends1, dest_t, h2, nb)
    ys = _experts(blk_e8[0, :nb], ends1[N_EXPERTS - 1:], xs, w_gate_up[0], b_gate_up[0][:, None, :],
                  w_down[0], b_down[0][:, None, :], nb)
    out = _combine(dest_t, route, h2, row(ln_ffn_g[0]), row(ln_ffn_b[0]), ys)
    return out.reshape(batch, seq, d)
```

```python
import functools

import jax
import jax.numpy as jnp
from jax import lax
from jax.experimental import pallas as pl
from jax.experimental.pallas import tpu as pltpu

F32 = jnp.float32
BF16 = jnp.bfloat16
I32 = jnp.int32

D_MODEL = 1024
CHUNK = 64
GLA_HEADS, GLA_DK, GLA_DV, GLA_RANK, GLA_TAU = 4, 128, 256, 16, 16.0
GLA_QK = GLA_HEADS * GLA_DK
GLA_V = GLA_HEADS * GLA_DV
FOX_HEADS, FOX_HD = 8, 128
FOX_W = FOX_HEADS * FOX_HD
N_EXPERTS, TOP_K, D_FF = 32, 4, 1024
SWIGLU_LIMIT, SWIGLU_ALPHA = 7.0, 1.702
LN_EPS = 1e-5
DEPTH = 1
DEEPNORM_ALPHA = (2 * DEPTH) ** 0.25
IN_SPLITS = (GLA_QK, GLA_QK, GLA_V, GLA_RANK, GLA_V, FOX_W, FOX_W, FOX_W, FOX_HEADS, D_MODEL, D_MODEL)

LANES = 128
SUBLANES = 8
VMEM_BYTES_V7X = 64 * 1024 * 1024

FORGET_COL0 = GLA_RANK
FORGET_PIECES = 3

TM_PROJ = 512
GLA_ROWS = 512
FOX_TQ = 512
FOX_TK = 512
FOX_HEADS_PER_STEP = 2
FOX_STRIP = 64
LOG2E = 1.4426950408889634
CUM_ROWS = 256
RANK_TW = 512
MOE_BLK = 512
DISPATCH_TOK = 256
COMBINE_TOK = 256
ROW_DMA_UNROLL = 8


def _cparams(sem, vmem_mib):
    return pltpu.CompilerParams(dimension_semantics=sem, vmem_limit_bytes=vmem_mib * 1024 * 1024)


def _log_sigmoid(z):
    return jnp.minimum(z, 0.0) - jnp.log1p(jnp.exp(-jnp.abs(z)))


def _layer_norm(x, g, b):
    mu = jnp.mean(x, axis=-1, keepdims=True)
    xc = x - mu
    var = jnp.mean(xc * xc, axis=-1, keepdims=True)
    return xc * lax.rsqrt(var + LN_EPS) * g + b


def _split_bf16(x, pieces):
    out = []
    for _ in range(pieces):
        p = x.astype(BF16)
        out.append(p)
        x = x - p.astype(F32)
    return out


_PROJ_WIDTHS = (GLA_QK, GLA_QK, GLA_V, GLA_V, FOX_W, FOX_W, FOX_W, D_MODEL, D_MODEL)


def _ln_inproj_kernel(x_ref, g_ref, b_ref, w_ref, ws_ref, h_ref, *out_refs):
    wide_refs, small_ref = out_refs[:-1], out_refs[-1]
    h = _layer_norm(x_ref[...], g_ref[...], b_ref[...])
    h_ref[...] = h
    hb = h.astype(BF16)
    off = 0
    for o_ref in wide_refs:
        n = o_ref.shape[1]
        o_ref[...] = jnp.dot(hb, w_ref[:, off:off + n], preferred_element_type=F32).astype(o_ref.dtype)
        off += n
    small_ref[...] = jnp.dot(hb, ws_ref[...], preferred_element_type=F32)


def _ln_inproj(x2, g, b, w_wide, w_small):
    t = x2.shape[0]
    tm = TM_PROJ
    row = lambda w: pl.BlockSpec((tm, w), lambda i: (i, 0))
    const = lambda a: pl.BlockSpec(a.shape, lambda i: (0, 0), pipeline_mode=pl.Buffered(1))
    out_shape = ([jax.ShapeDtypeStruct((t, D_MODEL), F32)]
                 + [jax.ShapeDtypeStruct((t, w), BF16) for w in _PROJ_WIDTHS]
                 + [jax.ShapeDtypeStruct((t, LANES), F32)])
    out_specs = [row(D_MODEL)] + [row(w) for w in _PROJ_WIDTHS] + [row(LANES)]
    return pl.pallas_call(
        _ln_inproj_kernel,
        out_shape=out_shape,
        grid=(t // tm,),
        in_specs=[row(D_MODEL), const(g), const(b), const(w_wide), const(w_small)],
        out_specs=out_specs,
        compiler_params=_cparams(("parallel",), 52),
        name="ln_inproj",
    )(x2, g, b, w_wide, w_small)


def _forget_cum_kernel(small_ref, bias_ref, sel_ref, kf_ref):
    s = small_ref.shape[0]
    r = CUM_ROWS
    ri = lax.broadcasted_iota(I32, (r, r), 0)
    ci = lax.broadcasted_iota(I32, (r, r), 1)
    tri = jnp.where(ci <= ri, 1.0, 0.0).astype(BF16)
    sel = sel_ref[...]
    carry = jnp.zeros((1, LANES), F32)
    for blk in range(s // r):
        rows = pl.ds(blk * r, r)
        ls = _log_sigmoid(small_ref[rows, :] + bias_ref[...])
        cum = carry
        for p in _split_bf16(ls, 3):
            cum = cum + jnp.dot(tri, p, preferred_element_type=F32)
        carry = cum[r - 1:r, :]
        neg = cum * (-LOG2E)
        p1, p2, p3 = _split_bf16(neg, FORGET_PIECES)
        kf = (p1.astype(F32) * sel[0:1, :] + p2.astype(F32) * sel[1:2, :] + p3.astype(F32) * sel[2:3, :])
        kf_ref[rows, :] = kf.astype(BF16)


def _forget_cum(small, bias_row, sel, batch, seq):
    return pl.pallas_call(
        _forget_cum_kernel,
        out_shape=jax.ShapeDtypeStruct((batch * seq, LANES), BF16),
        grid=(batch,),
        in_specs=[pl.BlockSpec((seq, LANES), lambda b: (b, 0)),
                  pl.BlockSpec((1, LANES), lambda b: (0, 0)),
                  pl.BlockSpec((SUBLANES, LANES), lambda b: (0, 0))],
        out_specs=pl.BlockSpec((seq, LANES), lambda b: (b, 0)),
        compiler_params=_cparams(("parallel",), 32),
        name="forget_cum",
    )(small, bias_row, sel)


def _gla_kernel(q_ref, k_ref, v_ref, r_ref, small_ref, wgu_ref, bg_ref, gn_ref, o_ref,
                state_ref, ds_ref, sb_ref, of_ref):
    rows = q_ref.shape[0]
    nchunk = rows // CHUNK
    kcol = lambda h: slice(h * GLA_DK, (h + 1) * GLA_DK)
    vcol = lambda h: slice(h * GLA_DV, (h + 1) * GLA_DV)
    crow = lambda c: slice(c * CHUNK, (c + 1) * CHUNK)

    @pl.when(pl.program_id(1) == 0)
    def _():
        state_ref[...] = jnp.zeros_like(state_ref)

    z = jnp.dot(small_ref[...].astype(BF16), wgu_ref[...], preferred_element_type=F32) + bg_ref[...]
    la = _log_sigmoid(z) * (1.0 / GLA_TAU)
    ri = lax.broadcasted_iota(I32, (rows, rows), 0)
    ci = lax.broadcasted_iota(I32, (rows, rows), 1)
    shift = CHUNK.bit_length() - 1
    same = lax.shift_right_logical(ri, shift) == lax.shift_right_logical(ci, shift)
    tri = jnp.where(same & (ci <= ri), 1.0, 0.0).astype(BF16)
    cum = jnp.zeros((rows, GLA_QK), F32)
    for p in _split_bf16(la, 2):
        cum = cum + jnp.dot(tri, p, preferred_element_type=F32)
    last = [cum[(c + 1) * CHUNK - 1:(c + 1) * CHUNK, :] for c in range(nchunk)]
    tot = jnp.concatenate([jnp.broadcast_to(t, (CHUNK, GLA_QK)) for t in last], axis=0)
    kd = (k_ref[...].astype(F32) * jnp.exp(tot - cum)).astype(BF16)
    dec_rows = jnp.exp(jnp.concatenate(last + [jnp.zeros((LANES - nchunk, GLA_QK), F32)], axis=0))
    dec_cols = dec_rows.T

    for c in range(nchunk):
        for h in range(GLA_HEADS):
            ds_ref[c, h] = lax.dot_general(kd[crow(c), kcol(h)], v_ref[crow(c), vcol(h)],
                                           (((0,), (0,)), ((), ())), preferred_element_type=F32)
    for h in range(GLA_HEADS):
        st = state_ref[h]
        for c in range(nchunk):
            st = st * jnp.broadcast_to(dec_cols[kcol(h), c:c + 1], (GLA_DK, GLA_DV)) + ds_ref[c, h]
            sb_ref[c, h] = st.astype(BF16)
        state_ref[h] = st
    scale = float(GLA_DK) ** -0.5
    for c in range(nchunk):
        for h in range(GLA_HEADS):
            of_ref[crow(c), vcol(h)] = jnp.dot(q_ref[crow(c), kcol(h)], sb_ref[c, h],
                                               preferred_element_type=F32) * scale
    for h in range(GLA_HEADS):
        o = of_ref[:, vcol(h)]
        o = o * lax.rsqrt(jnp.mean(o * o, axis=-1, keepdims=True) + LN_EPS) * gn_ref[...]
        r = r_ref[:, vcol(h)].astype(F32)
        o_ref[:, vcol(h)] = (o * (r * jax.nn.sigmoid(r))).astype(o_ref.dtype)


def _gla(gq, gk, gv, gr, small, wgu_pad, b_gate, g_norm, batch, seq):
    rows = GLA_ROWS
    nsteps = seq // rows
    row = lambda w: pl.BlockSpec((rows, w), lambda b, i: (b * nsteps + i, 0))
    const = lambda a: pl.BlockSpec(a.shape, lambda b, i: (0, 0))
    return pl.pallas_call(
        _gla_kernel,
        out_shape=jax.ShapeDtypeStruct((batch * seq, GLA_V), BF16),
        grid=(batch, nsteps),
        in_specs=[row(GLA_QK), row(GLA_QK), row(GLA_V), row(GLA_V), row(LANES),
                  const(wgu_pad), const(b_gate), const(g_norm)],
        out_specs=row(GLA_V),
        scratch_shapes=[pltpu.VMEM((GLA_HEADS, GLA_DK, GLA_DV), F32),
                        pltpu.VMEM((rows // CHUNK, GLA_HEADS, GLA_DK, GLA_DV), F32),
                        pltpu.VMEM((rows // CHUNK, GLA_HEADS, GLA_DK, GLA_DV), BF16),
                        pltpu.VMEM((rows, GLA_V), F32)],
        compiler_params=_cparams(("parallel", "arbitrary"), 40),
        name="gla",
    )(gq, gk, gv, gr, small, wgu_pad, b_gate, g_norm)


def _fox_kernel(q_ref, k_ref, v_ref, kf_ref, qsel_ref, o_ref, qa_ref, s_ref, acc_ref, m_ref, l_ref):
    t = FOX_TQ
    nq = q_ref.shape[0] // t
    c = (float(FOX_HD) ** -0.5) * LOG2E
    hd = FOX_HD
    heads = range(FOX_HEADS_PER_STEP)
    for h in heads:
        qa_ref[h, :, 0:hd] = (q_ref[:, h * hd:(h + 1) * hd].astype(F32) * c).astype(BF16)
        qa_ref[h, :, hd:hd + LANES] = jnp.broadcast_to(qsel_ref[h * SUBLANES:h * SUBLANES + 1, :],
                                                       (q_ref.shape[0], LANES))

    def key_rows(j):
        return pl.ds(pl.multiple_of(j * t, t), t)

    def scores(h, i, j):
        ka = jnp.concatenate([k_ref[key_rows(j), h * hd:(h + 1) * hd], kf_ref[key_rows(j), :]], axis=1)
        return lax.dot_general(ka, qa_ref[h, i * t:(i + 1) * t, :], (((1,), (1,)), ((), ())),
                               preferred_element_type=F32)

    def values(h, j, p):
        return lax.dot_general(v_ref[key_rows(j), h * hd:(h + 1) * hd], p, (((0,), (0,)), ((), ())),
                               preferred_element_type=F32)

    def step(j, slot, masked, nxt):
        if nxt is not None:
            for h in heads:
                s_ref[1 - slot, h] = scores(h, *nxt)
        for h in heads:
            s = s_ref[slot, h]
            if masked:
                causal = lax.broadcasted_iota(I32, (t, t), 0) <= lax.broadcasted_iota(I32, (t, t), 1)
                s = jnp.where(causal, s, -jnp.inf)
            m_old = m_ref[h]
            m_new = jnp.maximum(m_old, jnp.broadcast_to(jnp.max(s, axis=0, keepdims=True), (SUBLANES, t)))
            alpha = jnp.exp2(m_old - m_new)
            p = jnp.exp2(s - m_new[0:1, :])
            m_ref[h] = m_new
            l_ref[h] = alpha * l_ref[h] + jnp.sum(p, axis=0, keepdims=True)
            acc_ref[h] = alpha[0:1, :] * acc_ref[h] + values(h, j, p.astype(BF16))

    for h in heads:
        s_ref[0, h] = scores(h, 0, 0)
    slot = 0
    for i in range(nq):
        for h in heads:
            acc_ref[h] = jnp.zeros((hd, t), F32)
            m_ref[h] = jnp.full((SUBLANES, t), -jnp.inf, F32)
            l_ref[h] = jnp.zeros((SUBLANES, t), F32)
        if i >= 2:
            def pair(jj, _, i=i, slot=slot):
                step(2 * jj, slot, False, (i, 2 * jj + 1))
                step(2 * jj + 1, 1 - slot, False, (i, 2 * jj + 2))
                return 0

            lax.fori_loop(0, i // 2, pair, 0)
        if i % 2 == 1:
            step(i - 1, slot, False, (i, i))
            slot = 1 - slot
        step(i, slot, True, (i + 1, 0) if i + 1 < nq else None)
        slot = 1 - slot
        for h in heads:
            o_ref[i * t:(i + 1) * t, h * hd:(h + 1) * hd] = (acc_ref[h] / l_ref[h][0:1, :]).T.astype(o_ref.dtype)


def _fox(fq, fk, fv, kf, qsel, batch, seq):
    hps = FOX_HEADS_PER_STEP
    w = hps * FOX_HD
    t = FOX_TQ
    head_cols = pl.BlockSpec((seq, w), lambda b, h: (b, h))
    return pl.pallas_call(
        _fox_kernel,
        out_shape=jax.ShapeDtypeStruct((batch * seq, FOX_W), BF16),
        grid=(batch, FOX_HEADS // hps),
        in_specs=[head_cols, head_cols, head_cols,
                  pl.BlockSpec((seq, LANES), lambda b, h: (b, 0)),
                  pl.BlockSpec((hps * SUBLANES, LANES), lambda b, h: (h, 0))],
        out_specs=head_cols,
        scratch_shapes=[pltpu.VMEM((hps, seq, FOX_HD + LANES), BF16),
                        pltpu.VMEM((2, hps, t, t), F32), pltpu.VMEM((hps, FOX_HD, t), F32),
                        pltpu.VMEM((hps, SUBLANES, t), F32), pltpu.VMEM((hps, SUBLANES, t), F32)],
        compiler_params=_cparams(("parallel", "parallel"), 48),
        name="fox",
    )(fq, fk, fv, kf, qsel)


def _post_mix_kernel(gla_ref, fox_ref, zg_ref, zf_ref, h_ref, wg_ref, wf_ref, wo_ref, g_ref, b_ref,
                     wr_ref, br_ref, h2_ref, idx_ref, route_ref):
    tm = h_ref.shape[0]
    y_gla = jnp.dot(gla_ref[...], wg_ref[...], preferred_element_type=F32)
    y_fox = jnp.dot(fox_ref[...], wf_ref[...], preferred_element_type=F32)
    mixed = (jax.nn.sigmoid(zg_ref[...].astype(F32)) * y_gla
             + jax.nn.sigmoid(zf_ref[...].astype(F32)) * y_fox)
    pre = DEEPNORM_ALPHA * h_ref[...] + jnp.dot(mixed.astype(BF16), wo_ref[...], preferred_element_type=F32)
    h2 = _layer_norm(pre, g_ref[...], b_ref[...])
    h2_ref[...] = h2

    logits = lax.dot_general(wr_ref[...], h2.astype(BF16), (((1,), (1,)), ((), ())),
                             preferred_element_type=F32)
    logits = logits + jnp.tile(br_ref[...], (1, tm // LANES))
    eidx = lax.broadcasted_iota(I32, (N_EXPERTS, tm), 0)
    vals, idxs = [], []
    for _ in range(TOP_K):
        mx = jnp.max(logits, axis=0, keepdims=True)
        ik = jnp.min(jnp.where(logits == mx, eidx, N_EXPERTS), axis=0, keepdims=True)
        vals.append(mx)
        idxs.append(ik)
        logits = jnp.where(eidx == ik, -jnp.inf, logits)
    exps = [jnp.exp(v - vals[0]) for v in vals]
    denom = exps[0] + exps[1] + exps[2] + exps[3]
    rid = lax.broadcasted_iota(I32, (SUBLANES, tm), 0)
    idx8 = jnp.zeros((SUBLANES, tm), I32)
    gate8 = jnp.zeros((SUBLANES, tm), F32)
    for k in range(TOP_K):
        idx8 = jnp.where(rid == k, idxs[k], idx8)
        gate8 = jnp.where(rid == k, exps[k] / denom, gate8)
    idx_ref[...] = idx8
    gates = jnp.concatenate([gate8, jnp.zeros((LANES - SUBLANES, tm), F32)], axis=0)
    route_ref[...] = gates.T


def _post_mix(gla_o, fox_o, zg, zf, h, wg, wf, wo, g, b, wr_t, br_rep):
    t = h.shape[0]
    tm = TM_PROJ
    row = lambda w: pl.BlockSpec((tm, w), lambda i: (i, 0))
    const = lambda a: pl.BlockSpec(a.shape, lambda i: (0, 0))
    return pl.pallas_call(
        _post_mix_kernel,
        out_shape=[jax.ShapeDtypeStruct((t, D_MODEL), F32),
                   jax.ShapeDtypeStruct((SUBLANES, t), I32),
                   jax.ShapeDtypeStruct((t, LANES), F32)],
        grid=(t // tm,),
        in_specs=[row(GLA_V), row(FOX_W), row(D_MODEL), row(D_MODEL), row(D_MODEL),
                  const(wg), const(wf), const(wo), const(g), const(b), const(wr_t), const(br_rep)],
        out_specs=[row(D_MODEL),
                   pl.BlockSpec((SUBLANES, tm), lambda i: (0, i)), row(LANES)],
        compiler_params=_cparams(("parallel",), 48),
        name="post_mix",
    )(gla_o, fox_o, zg, zf, h, wg, wf, wo, g, b, wr_t, br_rep)


def _onehot_rows(idx8, tw):
    eidx = lax.broadcasted_iota(I32, (N_EXPERTS, tw), 0)
    hit = eidx == idx8[0:1, :]
    for k in range(1, TOP_K):
        hit = hit | (eidx == idx8[k:k + 1, :])
    return eidx, hit


def _rank_kernel(idx_ref, rank_ref, counts_ref, carry_ref):
    tw = idx_ref.shape[1]

    @pl.when(pl.program_id(0) == 0)
    def _():
        carry_ref[...] = jnp.zeros_like(carry_ref)

    idx8 = idx_ref[...]
    eidx, hit = _onehot_rows(idx8, tw)
    onehot = jnp.where(hit, 1.0, 0.0).astype(BF16)
    ri = lax.broadcasted_iota(I32, (tw, tw + LANES), 0)
    ci = lax.broadcasted_iota(I32, (tw, tw + LANES), 1)
    upper = jnp.where((ri < ci) | (ci >= tw), 1.0, 0.0).astype(BF16)
    cnt = jnp.dot(onehot, upper, preferred_element_type=F32)
    before = cnt[:, :tw] + jnp.tile(carry_ref[...], (1, tw // LANES))
    rid = lax.broadcasted_iota(I32, (SUBLANES, tw), 0)
    rank8 = jnp.zeros((SUBLANES, tw), I32)
    for k in range(TOP_K):
        rk = jnp.sum(jnp.where(eidx == idx8[k:k + 1, :], before, 0.0), axis=0, keepdims=True)
        rank8 = jnp.where(rid == k, rk.astype(I32), rank8)
    rank_ref[...] = rank8
    carry_ref[...] = carry_ref[...] + cnt[:, tw:]
    counts_ref[...] = carry_ref[...]


def _rank(idx_t):
    t = idx_t.shape[1]
    tw = RANK_TW
    return pl.pallas_call(
        _rank_kernel,
        out_shape=[jax.ShapeDtypeStruct((SUBLANES, t), I32),
                   jax.ShapeDtypeStruct((N_EXPERTS, LANES), F32)],
        grid=(t // tw,),
        in_specs=[pl.BlockSpec((SUBLANES, tw), lambda i: (0, i))],
        out_specs=[pl.BlockSpec((SUBLANES, tw), lambda i: (0, i)),
                   pl.BlockSpec((N_EXPERTS, LANES), lambda i: (0, 0))],
        scratch_shapes=[pltpu.VMEM((N_EXPERTS, LANES), F32)],
        compiler_params=_cparams(("arbitrary",), 32),
        name="rank",
    )(idx_t)


def _dest_kernel(idx_ref, rank_ref, counts_ref, dest_ref, blk_e_ref, ends_ref, *, nb_pad):
    tw = idx_ref.shape[1]
    nblk = jnp.floor((counts_ref[...] + (MOE_BLK - 1)) * (1.0 / MOE_BLK))
    ri = lax.broadcasted_iota(I32, (N_EXPERTS, N_EXPERTS), 0)
    ci = lax.broadcasted_iota(I32, (N_EXPERTS, N_EXPERTS), 1)
    tri = jnp.where(ci <= ri, 1.0, 0.0).astype(BF16)
    end_blk = jnp.dot(tri, nblk.astype(BF16), preferred_element_type=F32)
    start_row = (end_blk - nblk) * float(MOE_BLK)
    idx8 = idx_ref[...]
    eidx = lax.broadcasted_iota(I32, (N_EXPERTS, tw), 0)
    start_t = jnp.tile(start_row, (1, tw // LANES))
    rid = lax.broadcasted_iota(I32, (SUBLANES, tw), 0)
    dest8 = jnp.zeros((SUBLANES, tw), I32)
    for k in range(TOP_K):
        st = jnp.sum(jnp.where(eidx == idx8[k:k + 1, :], start_t, 0.0), axis=0, keepdims=True)
        dest8 = jnp.where(rid == k, st.astype(I32), dest8)
    dest_ref[...] = dest8 + rank_ref[...]
    bid = lax.broadcasted_iota(I32, (N_EXPERTS, nb_pad), 1).astype(F32)
    ends_t = jnp.tile(end_blk, (1, nb_pad // LANES))
    be = jnp.sum(jnp.where(ends_t <= bid, 1.0, 0.0), axis=0, keepdims=True)
    blk_e_ref[...] = jnp.broadcast_to(jnp.minimum(be, N_EXPERTS - 1.0), (SUBLANES, nb_pad)).astype(I32)
    ends_ref[...] = end_blk.astype(I32)


def _dest(idx_t, rank_t, counts, nb_pad):
    t = idx_t.shape[1]
    tw = RANK_TW
    tok = pl.BlockSpec((SUBLANES, tw), lambda i: (0, i))
    return pl.pallas_call(
        functools.partial(_dest_kernel, nb_pad=nb_pad),
        out_shape=[jax.ShapeDtypeStruct((SUBLANES, t), I32),
                   jax.ShapeDtypeStruct((SUBLANES, nb_pad), I32),
                   jax.ShapeDtypeStruct((N_EXPERTS, LANES), I32)],
        grid=(t // tw,),
        in_specs=[tok, tok, pl.BlockSpec((N_EXPERTS, LANES), lambda i: (0, 0))],
        out_specs=[tok, pl.BlockSpec((SUBLANES, nb_pad), lambda i: (0, 0)),
                   pl.BlockSpec((N_EXPERTS, LANES), lambda i: (0, 0))],
        compiler_params=_cparams(("arbitrary",), 32),
        name="dest",
    )(idx_t, rank_t, counts)


def _row_copy(src_ref, src_row, dst_ref, dst_row, sem):
    return pltpu.make_async_copy(src_ref.at[pl.ds(src_row, 1), :], dst_ref.at[pl.ds(dst_row, 1), :], sem)


def _dispatch_kernel(ends_ref, dest_ref, h2_ref, xs_ref, zero_ref, sem, zsem):
    td = h2_ref.shape[0]

    nb = xs_ref.shape[0] // MOE_BLK
    n_used = ends_ref[N_EXPERTS - 1]

    def zero_block(blk):
        start = pl.multiple_of(blk * MOE_BLK, MOE_BLK)
        return pltpu.make_async_copy(zero_ref, xs_ref.at[pl.ds(start, MOE_BLK), :], zsem)

    def has_rows(e):
        prev = jnp.where(e == 0, 0, ends_ref[jnp.maximum(e - 1, 0)])
        return ends_ref[e] > prev

    @pl.when(pl.program_id(0) == 0)
    def _():
        zero_ref[...] = jnp.zeros_like(zero_ref)

        def each(fn):
            def last_block(e, _):
                @pl.when(has_rows(e))
                def _():
                    fn(zero_block(ends_ref[e] - 1))
                return 0

            def tail_block(blk, _):
                fn(zero_block(blk))
                return 0

            lax.fori_loop(0, N_EXPERTS, last_block, 0)
            lax.fori_loop(n_used, nb, tail_block, 0)

        each(lambda cp: cp.start())
        each(lambda cp: cp.wait())

    def start_rows(t, _):
        for k in range(TOP_K):
            _row_copy(h2_ref, t, xs_ref, dest_ref[k, t], sem).start(priority=k % 2)
        return 0

    lax.fori_loop(0, td, start_rows, 0, unroll=ROW_DMA_UNROLL)
    for k in range(TOP_K):
        pltpu.make_async_copy(h2_ref, xs_ref.at[pl.ds(0, td), :], sem).wait()


def _dispatch(ends, dest_t, h2, nb):
    t = h2.shape[0]
    td = DISPATCH_TOK
    grid_spec = pltpu.PrefetchScalarGridSpec(
        num_scalar_prefetch=1,
        grid=(t // td,),
        in_specs=[pl.BlockSpec((SUBLANES, td), lambda i, ends: (0, i), memory_space=pltpu.SMEM),
                  pl.BlockSpec((td, D_MODEL), lambda i, ends: (i, 0))],
        out_specs=pl.BlockSpec(memory_space=pl.ANY),
        scratch_shapes=[pltpu.VMEM((MOE_BLK, D_MODEL), F32),
                        pltpu.SemaphoreType.DMA(()), pltpu.SemaphoreType.DMA(())],
    )
    return pl.pallas_call(
        _dispatch_kernel,
        out_shape=jax.ShapeDtypeStruct((nb * MOE_BLK, D_MODEL), F32),
        grid_spec=grid_spec,
        compiler_params=_cparams(("arbitrary",), 32),
        name="dispatch",
    )(ends, dest_t, h2)


def _experts_kernel(blk_e_ref, nused_ref, ends_ref, xs_ref, wgu_hbm, bgu_ref, wd_hbm, bd_ref, ys_ref,
                    wgu_f32, wd_f32, wgu_bf, wd_bf, slot_ref, sems):
    j = pl.program_id(0)
    n_used = nused_ref[0]
    active = j < n_used
    e = blk_e_ref[j]
    prev = blk_e_ref[jnp.maximum(j - 1, 0)]
    fresh = (j == 0) | (e != prev)

    def fetch(expert, slot):
        return (pltpu.make_async_copy(wgu_hbm.at[expert], wgu_f32.at[slot], sems.at[slot, 0]),
                pltpu.make_async_copy(wd_hbm.at[expert], wd_f32.at[slot], sems.at[slot, 1]))

    @pl.when(j == 0)
    def _():
        slot_ref[0] = 0
        for cp in fetch(e, 0):
            cp.start()

    @pl.when(active & fresh)
    def _():
        slot = slot_ref[0]
        for cp in fetch(e, slot):
            cp.wait()
        step = 256
        for r in range(0, D_MODEL, step):
            wgu_bf[r:r + step, :] = wgu_f32[slot, r:r + step, :].astype(BF16)
        for r in range(0, D_FF, step):
            wd_bf[r:r + step, :] = wd_f32[slot, r:r + step, :].astype(BF16)
        next_blk = ends_ref[e]

        @pl.when(next_blk < n_used)
        def _():
            for cp in fetch(blk_e_ref[jnp.minimum(next_blk, n_used - 1)], 1 - slot):
                cp.start()

        slot_ref[0] = 1 - slot

    @pl.when(active)
    def _():
        x = xs_ref[...].astype(BF16)
        gu = jnp.dot(x, wgu_bf[...], preferred_element_type=F32) + bgu_ref[...]
        gate = jnp.minimum(gu[:, :D_FF], SWIGLU_LIMIT)
        up = jnp.clip(gu[:, D_FF:], -SWIGLU_LIMIT, SWIGLU_LIMIT)
        glu = gate * jax.nn.sigmoid(SWIGLU_ALPHA * gate)
        act = ((up + 1.0) * glu).astype(BF16)
        ys_ref[...] = jnp.dot(act, wd_bf[...], preferred_element_type=F32) + bd_ref[...]

    @pl.when(jnp.logical_not(active))
    def _():
        ys_ref[...] = jnp.zeros_like(ys_ref)


def _experts(blk_e, nused, ends, xs, w_gate_up, b_gate_up, w_down, b_down, nb):
    def blk(j, blk_e, nused):
        return jnp.minimum(j, nused[0] - 1)

    def exp_of(j, blk_e, nused):
        return blk_e[jnp.minimum(j, nused[0] - 1)]

    grid_spec = pltpu.PrefetchScalarGridSpec(
        num_scalar_prefetch=3,
        grid=(nb,),
        in_specs=[pl.BlockSpec((MOE_BLK, D_MODEL), lambda j, be, nu, en: (blk(j, be, nu), 0)),
                  pl.BlockSpec(memory_space=pl.ANY),
                  pl.BlockSpec((None, 1, 2 * D_FF), lambda j, be, nu, en: (exp_of(j, be, nu), 0, 0)),
                  pl.BlockSpec(memory_space=pl.ANY),
                  pl.BlockSpec((None, 1, D_MODEL), lambda j, be, nu, en: (exp_of(j, be, nu), 0, 0))],
        out_specs=pl.BlockSpec((MOE_BLK, D_MODEL), lambda j, be, nu, en: (j, 0)),
        scratch_shapes=[pltpu.VMEM((2, D_MODEL, 2 * D_FF), F32), pltpu.VMEM((2, D_FF, D_MODEL), F32),
                        pltpu.VMEM((D_MODEL, 2 * D_FF), BF16), pltpu.VMEM((D_FF, D_MODEL), BF16),
                        pltpu.SMEM((1,), I32), pltpu.SemaphoreType.DMA((2, 2))],
    )
    return pl.pallas_call(
        _experts_kernel,
        out_shape=jax.ShapeDtypeStruct((nb * MOE_BLK, D_MODEL), F32),
        grid_spec=grid_spec,
        compiler_params=_cparams(("arbitrary",), 56),
        name="experts",
    )(blk_e, nused, ends, xs, w_gate_up, b_gate_up, w_down, b_down)


def _combine_kernel(dest_ref, dest_next_ref, route_ref, h2_ref, g_ref, b_ref, ys_ref, o_ref, buf_ref, sems):
    tc = h2_ref.shape[0]
    i = pl.program_id(0)
    slot = lax.rem(i, 2)

    def gather(d_ref, s):
        def start_rows(t, _):
            for k in range(TOP_K):
                pltpu.make_async_copy(ys_ref.at[pl.ds(d_ref[k, t], 1), :],
                                      buf_ref.at[s, k, pl.ds(t, 1), :], sems.at[s]).start(priority=k % 2)
            return 0

        lax.fori_loop(0, tc, start_rows, 0, unroll=ROW_DMA_UNROLL)

    @pl.when(i == 0)
    def _():
        gather(dest_ref, 0)

    @pl.when(i + 1 < pl.num_programs(0))
    def _():
        gather(dest_next_ref, 1 - slot)

    for k in range(TOP_K):
        pltpu.make_async_copy(ys_ref.at[pl.ds(0, tc), :], buf_ref.at[slot, k], sems.at[slot]).wait()
    route = route_ref[...]
    ffn = jnp.zeros((tc, D_MODEL), F32)
    for k in range(TOP_K):
        ffn = ffn + route[:, k:k + 1] * buf_ref[slot, k]
    o_ref[...] = _layer_norm(DEEPNORM_ALPHA * h2_ref[...] + ffn, g_ref[...], b_ref[...])


def _combine(dest_t, route, h2, g, b, ys):
    t = h2.shape[0]
    tc = COMBINE_TOK
    last = t // tc - 1
    return pl.pallas_call(
        _combine_kernel,
        out_shape=jax.ShapeDtypeStruct((t, D_MODEL), F32),
        grid=(t // tc,),
        in_specs=[pl.BlockSpec((SUBLANES, tc), lambda i: (0, i), memory_space=pltpu.SMEM),
                  pl.BlockSpec((SUBLANES, tc), lambda i: (0, jnp.minimum(i + 1, last)), memory_space=pltpu.SMEM),
                  pl.BlockSpec((tc, LANES), lambda i: (i, 0)),
                  pl.BlockSpec((tc, D_MODEL), lambda i: (i, 0)),
                  pl.BlockSpec((1, D_MODEL), lambda i: (0, 0)),
                  pl.BlockSpec((1, D_MODEL), lambda i: (0, 0)),
                  pl.BlockSpec(memory_space=pl.ANY)],
        out_specs=pl.BlockSpec((tc, D_MODEL), lambda i: (i, 0)),
        scratch_shapes=[pltpu.VMEM((2, TOP_K, tc, D_MODEL), F32), pltpu.SemaphoreType.DMA((2,))],
        compiler_params=_cparams(("arbitrary",), 32),
        name="combine",
    )(dest_t, dest_t, route, h2, g, b, ys)


def _split_cols(w):
    out, off = [], 0
    for n in IN_SPLITS:
        out.append(w[:, off:off + n])
        off += n
    return out


def kernel(x, ln_in_g, ln_in_b, w_in, w_gla_gate_up, b_gla_gate, g_gla_norm, b_forget, w_gla_proj, w_fox_proj, w_out, ln_mix_g, ln_mix_b, w_router, b_router, w_gate_up, b_gate_up, w_down, b_down, ln_ffn_g, ln_ffn_b):
    batch, seq, d = x.shape
    assert d == D_MODEL and w_in.shape[0] == DEPTH == 1
    assert seq % max(GLA_ROWS, FOX_TQ, FOX_TK, CUM_ROWS) == 0
    t = batch * seq
    assert t % max(TM_PROJ, RANK_TW, DISPATCH_TOK, COMBINE_TOK) == 0 and t // MOE_BLK <= 256
    row = lambda v: v.reshape(1, -1).astype(F32)

    wq, wk, wv, w_low, wr, wfq, wfk, wfv, wff, wzg, wzf = _split_cols(w_in[0])
    w_wide = jnp.concatenate([wq, wk, wv, wr, wfq, wfk, wfv, wzg, wzf], axis=1).astype(BF16)
    n_forget = FOX_HEADS * FORGET_PIECES
    w_small = jnp.concatenate(
        [w_low, jnp.repeat(wff, FORGET_PIECES, axis=1),
         jnp.zeros((D_MODEL, LANES - GLA_RANK - n_forget), F32)], axis=1).astype(BF16)
    lane = jnp.arange(LANES)
    in_forget = (lane >= FORGET_COL0) & (lane < FORGET_COL0 + n_forget)
    piece = (lane - FORGET_COL0) % FORGET_PIECES
    forget_bias = jnp.zeros((1, LANES), F32).at[0, FORGET_COL0:FORGET_COL0 + n_forget].set(
        jnp.repeat(b_forget[0].astype(F32), FORGET_PIECES))
    piece_sel = jnp.stack([(in_forget & (piece == p)).astype(F32) for p in range(FORGET_PIECES)]
                          + [jnp.zeros((LANES,), F32)] * (SUBLANES - FORGET_PIECES))
    head_of_lane = (lane - FORGET_COL0) // FORGET_PIECES
    qsel = jnp.repeat(jnp.stack([(in_forget & (head_of_lane == h)) for h in range(FOX_HEADS)]), SUBLANES,
                      axis=0).astype(BF16)
    wgu_pad = jnp.concatenate([w_gla_gate_up[0], jnp.zeros((LANES - GLA_RANK, GLA_QK), F32)], axis=0).astype(BF16)

    x2 = x.reshape(t, d)
    h, gq, gk, gv, gr, fq, fk, fv, zg, zf, small = _ln_inproj(x2, row(ln_in_g), row(ln_in_b), w_wide, w_small)
    kf = _forget_cum(small, forget_bias, piece_sel, batch, seq)
    gla_o = _gla(gq, gk, gv, gr, small, wgu_pad, row(b_gla_gate[0]), row(g_gla_norm[0]), batch, seq)
    fox_o = _fox(fq, fk, fv, kf, qsel, batch, seq)

    br_rep = jnp.broadcast_to(b_router[0].astype(F32)[:, None], (N_EXPERTS, LANES))
    h2, idx_t, route = _post_mix(
        gla_o, fox_o, zg, zf, h, w_gla_proj[0].astype(BF16), w_fox_proj[0].astype(BF16), w_out[0].astype(BF16),
        row(ln_mix_g[0]), row(ln_mix_b[0]), w_router[0].T.astype(BF16), br_rep)

    nb = (t * TOP_K + N_EXPERTS * (MOE_BLK - 1) + MOE_BLK - 1) // MOE_BLK
    nb_pad = (nb + LANES - 1) // LANES * LANES
    rank_t, counts = _rank(idx_t)
    dest_t, blk_e8, ends = _dest(idx_t, rank_t, counts, nb_pad)
    ends1 = ends[:, 0]
    xs = _dispatch(ends1, dest_t, h2, nb)
    ys = _experts(blk_e8[0, :nb], ends1[N_EXPERTS - 1:], ends1, xs, w_gate_up[0], b_gate_up[0][:, None, :],
                  w_down[0], b_down[0][:, None, :], nb)
    out = _combine(dest_t, route, h2, row(ln_ffn_g[0]), row(ln_ffn_b[0]), ys)
    return out.reshape(batch, seq, d)
```

```python
import functools

import jax
import jax.numpy as jnp
from jax import lax
from jax.experimental import pallas as pl
from jax.experimental.pallas import tpu as pltpu

F32 = jnp.float32
BF16 = jnp.bfloat16
I32 = jnp.int32

D_MODEL = 1024
CHUNK = 64
GLA_HEADS, GLA_DK, GLA_DV, GLA_RANK, GLA_TAU = 4, 128, 256, 16, 16.0
GLA_QK = GLA_HEADS * GLA_DK
GLA_V = GLA_HEADS * GLA_DV
FOX_HEADS, FOX_HD = 8, 128
FOX_W = FOX_HEADS * FOX_HD
N_EXPERTS, TOP_K, D_FF = 32, 4, 1024
SWIGLU_LIMIT, SWIGLU_ALPHA = 7.0, 1.702
LN_EPS = 1e-5
DEPTH = 1
DEEPNORM_ALPHA = (2 * DEPTH) ** 0.25
IN_SPLITS = (GLA_QK, GLA_QK, GLA_V, GLA_RANK, GLA_V, FOX_W, FOX_W, FOX_W, FOX_HEADS, D_MODEL, D_MODEL)

LANES = 128
SUBLANES = 8
VMEM_BYTES_V7X = 64 * 1024 * 1024
SLAB_ROWS = D_MODEL // LANES

FORGET_COL0 = GLA_RANK
FORGET_PIECES = 3

TM_PROJ = 512
GLA_ROWS = 512
FOX_TQ = 512
FOX_TK = 512
FOX_HEADS_PER_STEP = 2
FOX_STRIP = 64
LOG2E = 1.4426950408889634
CUM_ROWS = 256
RANK_TW = 512
MOE_BLK = 512
DISPATCH_TOK = 256
COMBINE_TOK = 256
ROW_DMA_UNROLL = 8


def _cparams(sem, vmem_mib):
    return pltpu.CompilerParams(dimension_semantics=sem, vmem_limit_bytes=vmem_mib * 1024 * 1024)


def _log_sigmoid(z):
    return jnp.minimum(z, 0.0) - jnp.log1p(jnp.exp(-jnp.abs(z)))


def _layer_norm(x, g, b):
    mu = jnp.mean(x, axis=-1, keepdims=True)
    xc = x - mu
    var = jnp.mean(xc * xc, axis=-1, keepdims=True)
    return xc * lax.rsqrt(var + LN_EPS) * g + b


def _split_bf16(x, pieces):
    out = []
    for _ in range(pieces):
        p = x.astype(BF16)
        out.append(p)
        x = x - p.astype(F32)
    return out


_PROJ_WIDTHS = (GLA_QK, GLA_QK, GLA_V, GLA_V, FOX_W, FOX_W, FOX_W, D_MODEL, D_MODEL)


def _ln_inproj_kernel(x_ref, g_ref, b_ref, w_ref, ws_ref, h_ref, *out_refs):
    wide_refs, small_ref = out_refs[:-1], out_refs[-1]
    h = _layer_norm(x_ref[...], g_ref[...], b_ref[...])
    h_ref[...] = h
    hb = h.astype(BF16)
    off = 0
    for o_ref in wide_refs:
        n = o_ref.shape[1]
        o_ref[...] = jnp.dot(hb, w_ref[:, off:off + n], preferred_element_type=F32).astype(o_ref.dtype)
        off += n
    small_ref[...] = jnp.dot(hb, ws_ref[...], preferred_element_type=F32)


def _ln_inproj(x2, g, b, w_wide, w_small):
    t = x2.shape[0]
    tm = TM_PROJ
    row = lambda w: pl.BlockSpec((tm, w), lambda i: (i, 0))
    const = lambda a: pl.BlockSpec(a.shape, lambda i: (0, 0), pipeline_mode=pl.Buffered(1))
    out_shape = ([jax.ShapeDtypeStruct((t, D_MODEL), F32)]
                 + [jax.ShapeDtypeStruct((t, w), BF16) for w in _PROJ_WIDTHS]
                 + [jax.ShapeDtypeStruct((t, LANES), F32)])
    out_specs = [row(D_MODEL)] + [row(w) for w in _PROJ_WIDTHS] + [row(LANES)]
    return pl.pallas_call(
        _ln_inproj_kernel,
        out_shape=out_shape,
        grid=(t // tm,),
        in_specs=[row(D_MODEL), const(g), const(b), const(w_wide), const(w_small)],
        out_specs=out_specs,
        compiler_params=_cparams(("parallel",), 52),
        name="ln_inproj",
    )(x2, g, b, w_wide, w_small)


def _forget_cum_kernel(small_ref, bias_ref, sel_ref, kf_ref):
    s = small_ref.shape[0]
    r = CUM_ROWS
    ri = lax.broadcasted_iota(I32, (r, r), 0)
    ci = lax.broadcasted_iota(I32, (r, r), 1)
    tri = jnp.where(ci <= ri, 1.0, 0.0).astype(BF16)
    sel = sel_ref[...]
    carry = jnp.zeros((1, LANES), F32)
    for blk in range(s // r):
        rows = pl.ds(blk * r, r)
        ls = _log_sigmoid(small_ref[rows, :] + bias_ref[...])
        cum = carry
        for p in _split_bf16(ls, 3):
            cum = cum + jnp.dot(tri, p, preferred_element_type=F32)
        carry = cum[r - 1:r, :]
        neg = cum * (-LOG2E)
        p1, p2, p3 = _split_bf16(neg, FORGET_PIECES)
        kf = (p1.astype(F32) * sel[0:1, :] + p2.astype(F32) * sel[1:2, :] + p3.astype(F32) * sel[2:3, :])
        kf_ref[rows, :] = kf.astype(BF16)


def _forget_cum(small, bias_row, sel, batch, seq):
    return pl.pallas_call(
        _forget_cum_kernel,
        out_shape=jax.ShapeDtypeStruct((batch * seq, LANES), BF16),
        grid=(batch,),
        in_specs=[pl.BlockSpec((seq, LANES), lambda b: (b, 0)),
                  pl.BlockSpec((1, LANES), lambda b: (0, 0)),
                  pl.BlockSpec((SUBLANES, LANES), lambda b: (0, 0))],
        out_specs=pl.BlockSpec((seq, LANES), lambda b: (b, 0)),
        compiler_params=_cparams(("parallel",), 32),
        name="forget_cum",
    )(small, bias_row, sel)


def _gla_kernel(q_ref, k_ref, v_ref, r_ref, small_ref, wgu_ref, bg_ref, gn_ref, o_ref,
                state_ref, ds_ref, sb_ref, of_ref):
    rows = q_ref.shape[0]
    nchunk = rows // CHUNK
    kcol = lambda h: slice(h * GLA_DK, (h + 1) * GLA_DK)
    vcol = lambda h: slice(h * GLA_DV, (h + 1) * GLA_DV)
    crow = lambda c: slice(c * CHUNK, (c + 1) * CHUNK)

    @pl.when(pl.program_id(1) == 0)
    def _():
        state_ref[...] = jnp.zeros_like(state_ref)

    z = jnp.dot(small_ref[...].astype(BF16), wgu_ref[...], preferred_element_type=F32) + bg_ref[...]
    la = _log_sigmoid(z) * (1.0 / GLA_TAU)
    ri = lax.broadcasted_iota(I32, (rows, rows), 0)
    ci = lax.broadcasted_iota(I32, (rows, rows), 1)
    shift = CHUNK.bit_length() - 1
    same = lax.shift_right_logical(ri, shift) == lax.shift_right_logical(ci, shift)
    tri = jnp.where(same & (ci <= ri), 1.0, 0.0).astype(BF16)
    cum = jnp.zeros((rows, GLA_QK), F32)
    for p in _split_bf16(la, 2):
        cum = cum + jnp.dot(tri, p, preferred_element_type=F32)
    last = [cum[(c + 1) * CHUNK - 1:(c + 1) * CHUNK, :] for c in range(nchunk)]
    tot = jnp.concatenate([jnp.broadcast_to(t, (CHUNK, GLA_QK)) for t in last], axis=0)
    kd = (k_ref[...].astype(F32) * jnp.exp(tot - cum)).astype(BF16)
    dec_rows = jnp.exp(jnp.concatenate(last + [jnp.zeros((LANES - nchunk, GLA_QK), F32)], axis=0))
    dec_cols = dec_rows.T

    for c in range(nchunk):
        for h in range(GLA_HEADS):
            ds_ref[c, h] = lax.dot_general(kd[crow(c), kcol(h)], v_ref[crow(c), vcol(h)],
                                           (((0,), (0,)), ((), ())), preferred_element_type=F32)
    for h in range(GLA_HEADS):
        st = state_ref[h]
        for c in range(nchunk):
            st = st * jnp.broadcast_to(dec_cols[kcol(h), c:c + 1], (GLA_DK, GLA_DV)) + ds_ref[c, h]
            sb_ref[c, h] = st.astype(BF16)
        state_ref[h] = st
    scale = float(GLA_DK) ** -0.5
    for c in range(nchunk):
        for h in range(GLA_HEADS):
            of_ref[crow(c), vcol(h)] = jnp.dot(q_ref[crow(c), kcol(h)], sb_ref[c, h],
                                               preferred_element_type=F32) * scale
    for h in range(GLA_HEADS):
        o = of_ref[:, vcol(h)]
        o = o * lax.rsqrt(jnp.mean(o * o, axis=-1, keepdims=True) + LN_EPS) * gn_ref[...]
        r = r_ref[:, vcol(h)].astype(F32)
        o_ref[:, vcol(h)] = (o * (r * jax.nn.sigmoid(r))).astype(o_ref.dtype)


def _gla(gq, gk, gv, gr, small, wgu_pad, b_gate, g_norm, batch, seq):
    rows = GLA_ROWS
    nsteps = seq // rows
    row = lambda w: pl.BlockSpec((rows, w), lambda b, i: (b * nsteps + i, 0))
    const = lambda a: pl.BlockSpec(a.shape, lambda b, i: (0, 0))
    return pl.pallas_call(
        _gla_kernel,
        out_shape=jax.ShapeDtypeStruct((batch * seq, GLA_V), BF16),
        grid=(batch, nsteps),
        in_specs=[row(GLA_QK), row(GLA_QK), row(GLA_V), row(GLA_V), row(LANES),
                  const(wgu_pad), const(b_gate), const(g_norm)],
        out_specs=row(GLA_V),
        scratch_shapes=[pltpu.VMEM((GLA_HEADS, GLA_DK, GLA_DV), F32),
                        pltpu.VMEM((rows // CHUNK, GLA_HEADS, GLA_DK, GLA_DV), F32),
                        pltpu.VMEM((rows // CHUNK, GLA_HEADS, GLA_DK, GLA_DV), BF16),
                        pltpu.VMEM((rows, GLA_V), F32)],
        compiler_params=_cparams(("parallel", "arbitrary"), 40),
        name="gla",
    )(gq, gk, gv, gr, small, wgu_pad, b_gate, g_norm)


def _fox_kernel(q_ref, k_ref, v_ref, kf_ref, qsel_ref, o_ref, qa_ref, s_ref, acc_ref, m_ref, l_ref):
    t = FOX_TQ
    nq = q_ref.shape[0] // t
    c = (float(FOX_HD) ** -0.5) * LOG2E
    hd = FOX_HD
    heads = range(FOX_HEADS_PER_STEP)
    for h in heads:
        qa_ref[h, :, 0:hd] = (q_ref[:, h * hd:(h + 1) * hd].astype(F32) * c).astype(BF16)
        qa_ref[h, :, hd:hd + LANES] = jnp.broadcast_to(qsel_ref[h * SUBLANES:h * SUBLANES + 1, :],
                                                       (q_ref.shape[0], LANES))

    def key_rows(j):
        return pl.ds(pl.multiple_of(j * t, t), t)

    def scores(h, i, j):
        ka = jnp.concatenate([k_ref[key_rows(j), h * hd:(h + 1) * hd], kf_ref[key_rows(j), :]], axis=1)
        return lax.dot_general(ka, qa_ref[h, i * t:(i + 1) * t, :], (((1,), (1,)), ((), ())),
                               preferred_element_type=F32)

    def values(h, j, p):
        return lax.dot_general(v_ref[key_rows(j), h * hd:(h + 1) * hd], p, (((0,), (0,)), ((), ())),
                               preferred_element_type=F32)

    def step(j, slot, masked, nxt):
        if nxt is not None:
            for h in heads:
                s_ref[1 - slot, h] = scores(h, *nxt)
        for h in heads:
            s = s_ref[slot, h]
            if masked:
                causal = lax.broadcasted_iota(I32, (t, t), 0) <= lax.broadcasted_iota(I32, (t, t), 1)
                s = jnp.where(causal, s, -jnp.inf)
            m_old = m_ref[h]
            m_new = jnp.maximum(m_old, jnp.broadcast_to(jnp.max(s, axis=0, keepdims=True), (SUBLANES, t)))
            alpha = jnp.exp2(m_old - m_new)
            p = jnp.exp2(s - m_new[0:1, :])
            m_ref[h] = m_new
            l_ref[h] = alpha * l_ref[h] + jnp.sum(p, axis=0, keepdims=True)
            acc_ref[h] = alpha[0:1, :] * acc_ref[h] + values(h, j, p.astype(BF16))

    for h in heads:
        s_ref[0, h] = scores(h, 0, 0)
    slot = 0
    for i in range(nq):
        for h in heads:
            acc_ref[h] = jnp.zeros((hd, t), F32)
            m_ref[h] = jnp.full((SUBLANES, t), -jnp.inf, F32)
            l_ref[h] = jnp.zeros((SUBLANES, t), F32)
        if i >= 2:
            def pair(jj, _, i=i, slot=slot):
                step(2 * jj, slot, False, (i, 2 * jj + 1))
                step(2 * jj + 1, 1 - slot, False, (i, 2 * jj + 2))
                return 0

            lax.fori_loop(0, i // 2, pair, 0)
        if i % 2 == 1:
            step(i - 1, slot, False, (i, i))
            slot = 1 - slot
        step(i, slot, True, (i + 1, 0) if i + 1 < nq else None)
        slot = 1 - slot
        for h in heads:
            o_ref[i * t:(i + 1) * t, h * hd:(h + 1) * hd] = (acc_ref[h] / l_ref[h][0:1, :]).T.astype(o_ref.dtype)


def _fox(fq, fk, fv, kf, qsel, batch, seq):
    hps = FOX_HEADS_PER_STEP
    w = hps * FOX_HD
    t = FOX_TQ
    head_cols = pl.BlockSpec((seq, w), lambda b, h: (b, h))
    return pl.pallas_call(
        _fox_kernel,
        out_shape=jax.ShapeDtypeStruct((batch * seq, FOX_W), BF16),
        grid=(batch, FOX_HEADS // hps),
        in_specs=[head_cols, head_cols, head_cols,
                  pl.BlockSpec((seq, LANES), lambda b, h: (b, 0)),
                  pl.BlockSpec((hps * SUBLANES, LANES), lambda b, h: (h, 0))],
        out_specs=head_cols,
        scratch_shapes=[pltpu.VMEM((hps, seq, FOX_HD + LANES), BF16),
                        pltpu.VMEM((2, hps, t, t), F32), pltpu.VMEM((hps, FOX_HD, t), F32),
                        pltpu.VMEM((hps, SUBLANES, t), F32), pltpu.VMEM((hps, SUBLANES, t), F32)],
        compiler_params=_cparams(("parallel", "parallel"), 48),
        name="fox",
    )(fq, fk, fv, kf, qsel)


def _post_mix_kernel(gla_ref, fox_ref, zg_ref, zf_ref, h_ref, wg_ref, wf_ref, wo_ref, g_ref, b_ref,
                     wr_ref, br_ref, h2_ref, idx_ref, gate_ref):
    tm = h_ref.shape[0]
    y_gla = jnp.dot(gla_ref[...], wg_ref[...], preferred_element_type=F32)
    y_fox = jnp.dot(fox_ref[...], wf_ref[...], preferred_element_type=F32)
    mixed = (jax.nn.sigmoid(zg_ref[...].astype(F32)) * y_gla
             + jax.nn.sigmoid(zf_ref[...].astype(F32)) * y_fox)
    pre = DEEPNORM_ALPHA * h_ref[...] + jnp.dot(mixed.astype(BF16), wo_ref[...], preferred_element_type=F32)
    h2 = _layer_norm(pre, g_ref[...], b_ref[...])
    h2_ref[...] = h2

    logits = lax.dot_general(wr_ref[...], h2.astype(BF16), (((1,), (1,)), ((), ())),
                             preferred_element_type=F32)
    logits = logits + jnp.tile(br_ref[...], (1, tm // LANES))
    eidx = lax.broadcasted_iota(I32, (N_EXPERTS, tm), 0)
    vals, idxs = [], []
    for _ in range(TOP_K):
        mx = jnp.max(logits, axis=0, keepdims=True)
        ik = jnp.min(jnp.where(logits == mx, eidx, N_EXPERTS), axis=0, keepdims=True)
        vals.append(mx)
        idxs.append(ik)
        logits = jnp.where(eidx == ik, -jnp.inf, logits)
    exps = [jnp.exp(v - vals[0]) for v in vals]
    denom = exps[0] + exps[1] + exps[2] + exps[3]
    rid = lax.broadcasted_iota(I32, (SUBLANES, tm), 0)
    idx8 = jnp.zeros((SUBLANES, tm), I32)
    gate8 = jnp.zeros((SUBLANES, tm), F32)
    for k in range(TOP_K):
        idx8 = jnp.where(rid == k, idxs[k], idx8)
        gate8 = jnp.where(rid == k, exps[k] / denom, gate8)
    idx_ref[...] = idx8
    gate_ref[...] = gate8


def _post_mix(gla_o, fox_o, zg, zf, h, wg, wf, wo, g, b, wr_t, br_rep):
    t = h.shape[0]
    tm = TM_PROJ
    row = lambda w: pl.BlockSpec((tm, w), lambda i: (i, 0))
    const = lambda a: pl.BlockSpec(a.shape, lambda i: (0, 0))
    return pl.pallas_call(
        _post_mix_kernel,
        out_shape=[jax.ShapeDtypeStruct((t, D_MODEL), F32),
                   jax.ShapeDtypeStruct((SUBLANES, t), I32),
                   jax.ShapeDtypeStruct((SUBLANES, t), F32)],
        grid=(t // tm,),
        in_specs=[row(GLA_V), row(FOX_W), row(D_MODEL), row(D_MODEL), row(D_MODEL),
                  const(wg), const(wf), const(wo), const(g), const(b), const(wr_t), const(br_rep)],
        out_specs=[row(D_MODEL),
                   pl.BlockSpec((SUBLANES, tm), lambda i: (0, i)), pl.BlockSpec((SUBLANES, tm), lambda i: (0, i))],
        compiler_params=_cparams(("parallel",), 48),
        name="post_mix",
    )(gla_o, fox_o, zg, zf, h, wg, wf, wo, g, b, wr_t, br_rep)


def _onehot_rows(idx8, tw):
    eidx = lax.broadcasted_iota(I32, (N_EXPERTS, tw), 0)
    hit = eidx == idx8[0:1, :]
    for k in range(1, TOP_K):
        hit = hit | (eidx == idx8[k:k + 1, :])
    return eidx, hit


def _rank_kernel(idx_ref, rank_ref, counts_ref, carry_ref):
    tw = idx_ref.shape[1]

    @pl.when(pl.program_id(0) == 0)
    def _():
        carry_ref[...] = jnp.zeros_like(carry_ref)

    idx8 = idx_ref[...]
    eidx, hit = _onehot_rows(idx8, tw)
    onehot = jnp.where(hit, 1.0, 0.0).astype(BF16)
    ri = lax.broadcasted_iota(I32, (tw, tw + LANES), 0)
    ci = lax.broadcasted_iota(I32, (tw, tw + LANES), 1)
    upper = jnp.where((ri < ci) | (ci >= tw), 1.0, 0.0).astype(BF16)
    cnt = jnp.dot(onehot, upper, preferred_element_type=F32)
    before = cnt[:, :tw] + jnp.tile(carry_ref[...], (1, tw // LANES))
    rid = lax.broadcasted_iota(I32, (SUBLANES, tw), 0)
    rank8 = jnp.zeros((SUBLANES, tw), I32)
    for k in range(TOP_K):
        rk = jnp.sum(jnp.where(eidx == idx8[k:k + 1, :], before, 0.0), axis=0, keepdims=True)
        rank8 = jnp.where(rid == k, rk.astype(I32), rank8)
    rank_ref[...] = rank8
    carry_ref[...] = carry_ref[...] + cnt[:, tw:]
    counts_ref[...] = carry_ref[...]


def _rank(idx_t):
    t = idx_t.shape[1]
    tw = RANK_TW
    return pl.pallas_call(
        _rank_kernel,
        out_shape=[jax.ShapeDtypeStruct((SUBLANES, t), I32),
                   jax.ShapeDtypeStruct((N_EXPERTS, LANES), F32)],
        grid=(t // tw,),
        in_specs=[pl.BlockSpec((SUBLANES, tw), lambda i: (0, i))],
        out_specs=[pl.BlockSpec((SUBLANES, tw), lambda i: (0, i)),
                   pl.BlockSpec((N_EXPERTS, LANES), lambda i: (0, 0))],
        scratch_shapes=[pltpu.VMEM((N_EXPERTS, LANES), F32)],
        compiler_params=_cparams(("arbitrary",), 32),
        name="rank",
    )(idx_t)


def _dest_kernel(idx_ref, rank_ref, counts_ref, dest_ref, blk_e_ref, ends_ref, *, nb_pad):
    tw = idx_ref.shape[1]
    nblk = jnp.floor((counts_ref[...] + (MOE_BLK - 1)) * (1.0 / MOE_BLK))
    ri = lax.broadcasted_iota(I32, (N_EXPERTS, N_EXPERTS), 0)
    ci = lax.broadcasted_iota(I32, (N_EXPERTS, N_EXPERTS), 1)
    tri = jnp.where(ci <= ri, 1.0, 0.0).astype(BF16)
    end_blk = jnp.dot(tri, nblk.astype(BF16), preferred_element_type=F32)
    start_row = (end_blk - nblk) * float(MOE_BLK)
    idx8 = idx_ref[...]
    eidx = lax.broadcasted_iota(I32, (N_EXPERTS, tw), 0)
    start_t = jnp.tile(start_row, (1, tw // LANES))
    rid = lax.broadcasted_iota(I32, (SUBLANES, tw), 0)
    dest8 = jnp.zeros((SUBLANES, tw), I32)
    for k in range(TOP_K):
        st = jnp.sum(jnp.where(eidx == idx8[k:k + 1, :], start_t, 0.0), axis=0, keepdims=True)
        dest8 = jnp.where(rid == k, st.astype(I32), dest8)
    dest_ref[...] = dest8 + rank_ref[...]
    bid = lax.broadcasted_iota(I32, (N_EXPERTS, nb_pad), 1).astype(F32)
    ends_t = jnp.tile(end_blk, (1, nb_pad // LANES))
    be = jnp.sum(jnp.where(ends_t <= bid, 1.0, 0.0), axis=0, keepdims=True)
    blk_e_ref[...] = jnp.broadcast_to(jnp.minimum(be, N_EXPERTS - 1.0), (SUBLANES, nb_pad)).astype(I32)
    ends_ref[...] = end_blk.astype(I32)


def _dest(idx_t, rank_t, counts, nb_pad):
    t = idx_t.shape[1]
    tw = RANK_TW
    tok = pl.BlockSpec((SUBLANES, tw), lambda i: (0, i))
    return pl.pallas_call(
        functools.partial(_dest_kernel, nb_pad=nb_pad),
        out_shape=[jax.ShapeDtypeStruct((SUBLANES, t), I32),
                   jax.ShapeDtypeStruct((SUBLANES, nb_pad), I32),
                   jax.ShapeDtypeStruct((N_EXPERTS, LANES), I32)],
        grid=(t // tw,),
        in_specs=[tok, tok, pl.BlockSpec((N_EXPERTS, LANES), lambda i: (0, 0))],
        out_specs=[tok, pl.BlockSpec((SUBLANES, nb_pad), lambda i: (0, 0)),
                   pl.BlockSpec((N_EXPERTS, LANES), lambda i: (0, 0))],
        compiler_params=_cparams(("arbitrary",), 32),
        name="dest",
    )(idx_t, rank_t, counts)


def _dispatch_kernel(ends_ref, dest_ref, h2_ref, xs_ref, slab_ref, zero_ref, sem, zsem):
    td = h2_ref.shape[0]

    nb = xs_ref.shape[0] // MOE_BLK
    n_used = ends_ref[N_EXPERTS - 1]

    def zero_block(blk):
        start = pl.multiple_of(blk * MOE_BLK, MOE_BLK)
        return pltpu.make_async_copy(zero_ref, xs_ref.at[pl.ds(start, MOE_BLK)], zsem)

    def has_rows(e):
        prev = jnp.where(e == 0, 0, ends_ref[jnp.maximum(e - 1, 0)])
        return ends_ref[e] > prev

    @pl.when(pl.program_id(0) == 0)
    def _():
        zero_ref[...] = jnp.zeros_like(zero_ref)

        def each(fn):
            def last_block(e, _):
                @pl.when(has_rows(e))
                def _():
                    fn(zero_block(ends_ref[e] - 1))
                return 0

            def tail_block(blk, _):
                fn(zero_block(blk))
                return 0

            lax.fori_loop(0, N_EXPERTS, last_block, 0)
            lax.fori_loop(n_used, nb, tail_block, 0)

        each(lambda cp: cp.start())
        each(lambda cp: cp.wait())

    for c in range(SLAB_ROWS):
        slab_ref[:, c, :] = h2_ref[:, c * LANES:(c + 1) * LANES]

    def start_rows(t, _):
        for k in range(TOP_K):
            pltpu.make_async_copy(slab_ref.at[t], xs_ref.at[dest_ref[k, t]], sem).start(priority=k % 2)
        return 0

    lax.fori_loop(0, td, start_rows, 0, unroll=ROW_DMA_UNROLL)
    for k in range(TOP_K):
        pltpu.make_async_copy(slab_ref, xs_ref.at[pl.ds(0, td)], sem).wait()


def _dispatch(ends, dest_t, h2, nb):
    t = h2.shape[0]
    td = DISPATCH_TOK
    grid_spec = pltpu.PrefetchScalarGridSpec(
        num_scalar_prefetch=1,
        grid=(t // td,),
        in_specs=[pl.BlockSpec((SUBLANES, td), lambda i, ends: (0, i), memory_space=pltpu.SMEM),
                  pl.BlockSpec((td, D_MODEL), lambda i, ends: (i, 0))],
        out_specs=pl.BlockSpec(memory_space=pl.ANY),
        scratch_shapes=[pltpu.VMEM((td, SLAB_ROWS, LANES), F32), pltpu.VMEM((MOE_BLK, SLAB_ROWS, LANES), F32),
                        pltpu.SemaphoreType.DMA(()), pltpu.SemaphoreType.DMA(())],
    )
    return pl.pallas_call(
        _dispatch_kernel,
        out_shape=jax.ShapeDtypeStruct((nb * MOE_BLK, SLAB_ROWS, LANES), F32),
        grid_spec=grid_spec,
        compiler_params=_cparams(("arbitrary",), 32),
        name="dispatch",
    )(ends, dest_t, h2)


def _experts_kernel(blk_e_ref, nused_ref, ends_ref, xs_hbm, wgu_hbm, bgu_ref, wd_hbm, bd_ref, ys_hbm,
                    xbuf, ybuf, wgu_f32, wd_f32, wgu_bf, wd_bf, slot_ref, sems, xsems, ysems):
    j = pl.program_id(0)
    nb = pl.num_programs(0)
    par = lax.rem(j, 2)
    n_used = nused_ref[0]
    active = j < n_used
    e = blk_e_ref[j]
    prev = blk_e_ref[jnp.maximum(j - 1, 0)]
    fresh = (j == 0) | (e != prev)

    def fetch(expert, slot):
        return (pltpu.make_async_copy(wgu_hbm.at[expert], wgu_f32.at[slot], sems.at[slot, 0]),
                pltpu.make_async_copy(wd_hbm.at[expert], wd_f32.at[slot], sems.at[slot, 1]))

    def x_load(blk, half):
        rows = pl.ds(pl.multiple_of(blk * MOE_BLK, MOE_BLK), MOE_BLK)
        return [pltpu.make_async_copy(xs_hbm.at[rows, c, :], xbuf.at[half, :, pl.ds(c * LANES, LANES)], xsems.at[half])
                for c in range(SLAB_ROWS)]

    def y_store(blk, half):
        rows = pl.ds(pl.multiple_of(blk * MOE_BLK, MOE_BLK), MOE_BLK)
        return [pltpu.make_async_copy(ybuf.at[half, :, pl.ds(c * LANES, LANES)], ys_hbm.at[rows, c, :], ysems.at[half])
                for c in range(SLAB_ROWS)]

    @pl.when(j == 0)
    def _():
        for cp in x_load(0, 0):
            cp.start()

    @pl.when(j + 1 < n_used)
    def _():
        for cp in x_load(j + 1, 1 - par):
            cp.start()

    @pl.when(j >= 2)
    def _():
        for cp in y_store(j - 2, par):
            cp.wait()

    @pl.when(j == 0)
    def _():
        slot_ref[0] = 0
        for cp in fetch(e, 0):
            cp.start()

    @pl.when(active & fresh)
    def _():
        slot = slot_ref[0]
        for cp in fetch(e, slot):
            cp.wait()
        step = 256
        for r in range(0, D_MODEL, step):
            wgu_bf[r:r + step, :] = wgu_f32[slot, r:r + step, :].astype(BF16)
        for r in range(0, D_FF, step):
            wd_bf[r:r + step, :] = wd_f32[slot, r:r + step, :].astype(BF16)
        next_blk = ends_ref[e]

        @pl.when(next_blk < n_used)
        def _():
            for cp in fetch(blk_e_ref[jnp.minimum(next_blk, n_used - 1)], 1 - slot):
                cp.start()

        slot_ref[0] = 1 - slot

    @pl.when(active)
    def _():
        for cp in x_load(j, par):
            cp.wait()
        x = xbuf[par].astype(BF16)
        gu = jnp.dot(x, wgu_bf[...], preferred_element_type=F32) + bgu_ref[...]
        gate = jnp.minimum(gu[:, :D_FF], SWIGLU_LIMIT)
        up = jnp.clip(gu[:, D_FF:], -SWIGLU_LIMIT, SWIGLU_LIMIT)
        glu = gate * jax.nn.sigmoid(SWIGLU_ALPHA * gate)
        act = ((up + 1.0) * glu).astype(BF16)
        ybuf[par] = jnp.dot(act, wd_bf[...], preferred_element_type=F32) + bd_ref[...]

    @pl.when(jnp.logical_not(active))
    def _():
        ybuf[par] = jnp.zeros((MOE_BLK, D_MODEL), F32)

    for cp in y_store(j, par):
        cp.start()

    @pl.when(j == nb - 1)
    def _():
        for cp in y_store(j, par) + y_store(j - 1, 1 - par):
            cp.wait()


def _experts(blk_e, nused, ends, xs, w_gate_up, b_gate_up, w_down, b_down, nb):
    assert nb >= 2

    def exp_of(j, blk_e, nused):
        return blk_e[jnp.minimum(j, nused[0] - 1)]

    grid_spec = pltpu.PrefetchScalarGridSpec(
        num_scalar_prefetch=3,
        grid=(nb,),
        in_specs=[pl.BlockSpec(memory_space=pl.ANY),
                  pl.BlockSpec(memory_space=pl.ANY),
                  pl.BlockSpec((None, 1, 2 * D_FF), lambda j, be, nu, en: (exp_of(j, be, nu), 0, 0)),
                  pl.BlockSpec(memory_space=pl.ANY),
                  pl.BlockSpec((None, 1, D_MODEL), lambda j, be, nu, en: (exp_of(j, be, nu), 0, 0))],
        out_specs=pl.BlockSpec(memory_space=pl.ANY),
        scratch_shapes=[pltpu.VMEM((2, MOE_BLK, D_MODEL), F32), pltpu.VMEM((2, MOE_BLK, D_MODEL), F32),
                        pltpu.VMEM((2, D_MODEL, 2 * D_FF), F32), pltpu.VMEM((2, D_FF, D_MODEL), F32),
                        pltpu.VMEM((D_MODEL, 2 * D_FF), BF16), pltpu.VMEM((D_FF, D_MODEL), BF16),
                        pltpu.SMEM((1,), I32), pltpu.SemaphoreType.DMA((2, 2)),
                        pltpu.SemaphoreType.DMA((2,)), pltpu.SemaphoreType.DMA((2,))],
    )
    return pl.pallas_call(
        _experts_kernel,
        out_shape=jax.ShapeDtypeStruct((nb * MOE_BLK, SLAB_ROWS, LANES), F32),
        grid_spec=grid_spec,
        compiler_params=_cparams(("arbitrary",), 56),
        name="experts",
    )(blk_e, nused, ends, xs, w_gate_up, b_gate_up, w_down, b_down)


def _combine_kernel(dest_ref, dest_next_ref, gate_ref, h2_ref, g_ref, b_ref, ys_ref, o_ref, buf_ref, ffn_ref, sems):
    tc = h2_ref.shape[0]
    i = pl.program_id(0)
    slot = lax.rem(i, 2)

    def gather(d_ref, s):
        def start_rows(t, _):
            for k in range(TOP_K):
                pltpu.make_async_copy(ys_ref.at[d_ref[k, t]], buf_ref.at[s, k, t], sems.at[s]).start(priority=k % 2)
            return 0

        lax.fori_loop(0, tc, start_rows, 0, unroll=ROW_DMA_UNROLL)

    @pl.when(i == 0)
    def _():
        gather(dest_ref, 0)

    @pl.when(i + 1 < pl.num_programs(0))
    def _():
        gather(dest_next_ref, 1 - slot)

    for k in range(TOP_K):
        pltpu.make_async_copy(ys_ref.at[pl.ds(0, tc)], buf_ref.at[slot, k], sems.at[slot]).wait()

    def weigh(t, _):
        acc = gate_ref[0, t] * buf_ref[slot, 0, t]
        for k in range(1, TOP_K):
            acc = acc + gate_ref[k, t] * buf_ref[slot, k, t]
        ffn_ref[t] = acc
        return 0

    lax.fori_loop(0, tc, weigh, 0, unroll=ROW_DMA_UNROLL)
    ffn = jnp.concatenate([ffn_ref[:, c, :] for c in range(SLAB_ROWS)], axis=1)
    o_ref[...] = _layer_norm(DEEPNORM_ALPHA * h2_ref[...] + ffn, g_ref[...], b_ref[...])


def _combine(dest_t, gate_t, h2, g, b, ys):
    t = h2.shape[0]
    tc = COMBINE_TOK
    last = t // tc - 1
    return pl.pallas_call(
        _combine_kernel,
        out_shape=jax.ShapeDtypeStruct((t, D_MODEL), F32),
        grid=(t // tc,),
        in_specs=[pl.BlockSpec((SUBLANES, tc), lambda i: (0, i), memory_space=pltpu.SMEM),
                  pl.BlockSpec((SUBLANES, tc), lambda i: (0, jnp.minimum(i + 1, last)), memory_space=pltpu.SMEM),
                  pl.BlockSpec((SUBLANES, tc), lambda i: (0, i), memory_space=pltpu.SMEM),
                  pl.BlockSpec((tc, D_MODEL), lambda i: (i, 0)),
                  pl.BlockSpec((1, D_MODEL), lambda i: (0, 0)),
                  pl.BlockSpec((1, D_MODEL), lambda i: (0, 0)),
                  pl.BlockSpec(memory_space=pl.ANY)],
        out_specs=pl.BlockSpec((tc, D_MODEL), lambda i: (i, 0)),
        scratch_shapes=[pltpu.VMEM((2, TOP_K, tc, SLAB_ROWS, LANES), F32), pltpu.VMEM((tc, SLAB_ROWS, LANES), F32),
                        pltpu.SemaphoreType.DMA((2,))],
        compiler_params=_cparams(("arbitrary",), 32),
        name="combine",
    )(dest_t, dest_t, gate_t, h2, g, b, ys)


def _split_cols(w):
    out, off = [], 0
    for n in IN_SPLITS:
        out.append(w[:, off:off + n])
        off += n
    return out


def kernel(x, ln_in_g, ln_in_b, w_in, w_gla_gate_up, b_gla_gate, g_gla_norm, b_forget, w_gla_proj, w_fox_proj, w_out, ln_mix_g, ln_mix_b, w_router, b_router, w_gate_up, b_gate_up, w_down, b_down, ln_ffn_g, ln_ffn_b):
    batch, seq, d = x.shape
    assert d == D_MODEL and w_in.shape[0] == DEPTH == 1
    assert seq % max(GLA_ROWS, FOX_TQ, FOX_TK, CUM_ROWS) == 0
    t = batch * seq
    assert t % max(TM_PROJ, RANK_TW, DISPATCH_TOK, COMBINE_TOK) == 0 and t // MOE_BLK <= 256
    row = lambda v: v.reshape(1, -1).astype(F32)

    wq, wk, wv, w_low, wr, wfq, wfk, wfv, wff, wzg, wzf = _split_cols(w_in[0])
    w_wide = jnp.concatenate([wq, wk, wv, wr, wfq, wfk, wfv, wzg, wzf], axis=1).astype(BF16)
    n_forget = FOX_HEADS * FORGET_PIECES
    w_small = jnp.concatenate(
        [w_low, jnp.repeat(wff, FORGET_PIECES, axis=1),
         jnp.zeros((D_MODEL, LANES - GLA_RANK - n_forget), F32)], axis=1).astype(BF16)
    lane = jnp.arange(LANES)
    in_forget = (lane >= FORGET_COL0) & (lane < FORGET_COL0 + n_forget)
    piece = (lane - FORGET_COL0) % FORGET_PIECES
    forget_bias = jnp.zeros((1, LANES), F32).at[0, FORGET_COL0:FORGET_COL0 + n_forget].set(
        jnp.repeat(b_forget[0].astype(F32), FORGET_PIECES))
    piece_sel = jnp.stack([(in_forget & (piece == p)).astype(F32) for p in range(FORGET_PIECES)]
                          + [jnp.zeros((LANES,), F32)] * (SUBLANES - FORGET_PIECES))
    head_of_lane = (lane - FORGET_COL0) // FORGET_PIECES
    qsel = jnp.repeat(jnp.stack([(in_forget & (head_of_lane == h)) for h in range(FOX_HEADS)]), SUBLANES,
                      axis=0).astype(BF16)
    wgu_pad = jnp.concatenate([w_gla_gate_up[0], jnp.zeros((LANES - GLA_RANK, GLA_QK), F32)], axis=0).astype(BF16)

    x2 = x.reshape(t, d)
    h, gq, gk, gv, gr, fq, fk, fv, zg, zf, small = _ln_inproj(x2, row(ln_in_g), row(ln_in_b), w_wide, w_small)
    kf = _forget_cum(small, forget_bias, piece_sel, batch, seq)
    gla_o = _gla(gq, gk, gv, gr, small, wgu_pad, row(b_gla_gate[0]), row(g_gla_norm[0]), batch, seq)
    fox_o = _fox(fq, fk, fv, kf, qsel, batch, seq)

    br_rep = jnp.broadcast_to(b_router[0].astype(F32)[:, None], (N_EXPERTS, LANES))
    h2, idx_t, gate_t = _post_mix(
        gla_o, fox_o, zg, zf, h, w_gla_proj[0].astype(BF16), w_fox_proj[0].astype(BF16), w_out[0].astype(BF16),
        row(ln_mix_g[0]), row(ln_mix_b[0]), w_router[0].T.astype(BF16), br_rep)

    nb = (t * TOP_K + N_EXPERTS * (MOE_BLK - 1) + MOE_BLK - 1) // MOE_BLK
    nb_pad = (nb + LANES - 1) // LANES * LANES
    rank_t, counts = _rank(idx_t)
    dest_t, blk_e8, ends = _dest(idx_t, rank_t, counts, nb_pad)
    ends1 = ends[:, 0]
    xs = _dispatch(ends1, dest_t, h2, nb)
    ys = _experts(blk_e8[0, :nb], ends1[N_EXPERTS - 1:], ends1, xs, w_gate_up[0], b_gate_up[0][:, None, :],
                  w_down[0], b_down[0][:, None, :], nb)
    out = _combine(dest_t, gate_t, h2, row(ln_ffn_g[0]), row(ln_ffn_b[0]), ys)
    return out.reshape(batch, seq, d)
```

```python
import functools

import jax
import jax.numpy as jnp
from jax import lax
from jax.experimental import pallas as pl
from jax.experimental.pallas import tpu as pltpu

F32 = jnp.float32
BF16 = jnp.bfloat16
I32 = jnp.int32

D_MODEL = 1024
CHUNK = 64
GLA_HEADS, GLA_DK, GLA_DV, GLA_RANK, GLA_TAU = 4, 128, 256, 16, 16.0
GLA_QK = GLA_HEADS * GLA_DK
GLA_V = GLA_HEADS * GLA_DV
FOX_HEADS, FOX_HD = 8, 128
FOX_W = FOX_HEADS * FOX_HD
N_EXPERTS, TOP_K, D_FF = 32, 4, 1024
SWIGLU_LIMIT, SWIGLU_ALPHA = 7.0, 1.702
LN_EPS = 1e-5
DEPTH = 1
DEEPNORM_ALPHA = (2 * DEPTH) ** 0.25
IN_SPLITS = (GLA_QK, GLA_QK, GLA_V, GLA_RANK, GLA_V, FOX_W, FOX_W, FOX_W, FOX_HEADS, D_MODEL, D_MODEL)

LANES = 128
SUBLANES = 8
VMEM_BYTES_V7X = 64 * 1024 * 1024
SLAB_ROWS = D_MODEL // LANES

FORGET_COL0 = GLA_RANK
FORGET_PIECES = 3

TM_PROJ = 512
GLA_ROWS = 512
FOX_TQ = 512
FOX_TK = 512
FOX_HEADS_PER_STEP = 2
FOX_STRIP = 64
LOG2E = 1.4426950408889634
CUM_ROWS = 256
RANK_TW = 512
MOE_BLK = 512
DISPATCH_TOK = 256
COMBINE_TOK = 256
ROW_DMA_UNROLL = 8


def _cparams(sem, vmem_mib):
    return pltpu.CompilerParams(dimension_semantics=sem, vmem_limit_bytes=vmem_mib * 1024 * 1024)


def _log_sigmoid(z):
    return jnp.minimum(z, 0.0) - jnp.log1p(jnp.exp(-jnp.abs(z)))


def _layer_norm(x, g, b):
    mu = jnp.mean(x, axis=-1, keepdims=True)
    xc = x - mu
    var = jnp.mean(xc * xc, axis=-1, keepdims=True)
    return xc * lax.rsqrt(var + LN_EPS) * g + b


def _split_bf16(x, pieces):
    out = []
    for _ in range(pieces):
        p = x.astype(BF16)
        out.append(p)
        x = x - p.astype(F32)
    return out


_NARROW_A = sum(IN_SPLITS[:3])
_NARROW_B = sum(IN_SPLITS[:8])


def _regroup_kernel(w_ref, sel_a_ref, sel_b_ref, wide_ref, small_ref):
    a, b = _NARROW_A, _NARROW_B
    w = w_ref[...]
    wide_ref[...] = jnp.concatenate(
        [w[:, :a], w[:, a + GLA_RANK:b], w[:, b + FOX_HEADS:]], axis=1).astype(wide_ref.dtype)
    b0 = b // LANES * LANES
    small = (jnp.dot(w[:, a:a + LANES].astype(BF16), sel_a_ref[...], preferred_element_type=F32)
             + jnp.dot(w[:, b0:b0 + LANES].astype(BF16), sel_b_ref[...], preferred_element_type=F32))
    small_ref[...] = small.astype(small_ref.dtype)


def _regroup_w_in(w, sel_a, sel_b):
    rows = 128
    n_wide = w.shape[1] - GLA_RANK - FOX_HEADS
    const = lambda a: pl.BlockSpec(a.shape, lambda i: (0, 0))
    return pl.pallas_call(
        _regroup_kernel,
        out_shape=[jax.ShapeDtypeStruct((D_MODEL, n_wide), BF16), jax.ShapeDtypeStruct((D_MODEL, LANES), BF16)],
        grid=(D_MODEL // rows,),
        in_specs=[pl.BlockSpec((rows, w.shape[1]), lambda i: (i, 0)), const(sel_a), const(sel_b)],
        out_specs=[pl.BlockSpec((rows, n_wide), lambda i: (i, 0)), pl.BlockSpec((rows, LANES), lambda i: (i, 0))],
        compiler_params=_cparams(("parallel",), 32),
        name="regroup_w_in",
    )(w, sel_a, sel_b)


_PROJ_WIDTHS = (GLA_QK, GLA_QK, GLA_V, GLA_V, FOX_W, FOX_W, FOX_W, D_MODEL, D_MODEL)


def _ln_inproj_kernel(x_ref, g_ref, b_ref, w_ref, ws_ref, h_ref, *out_refs):
    wide_refs, small_ref = out_refs[:-1], out_refs[-1]
    h = _layer_norm(x_ref[...], g_ref[...], b_ref[...])
    h_ref[...] = h
    hb = h.astype(BF16)
    off = 0
    for o_ref in wide_refs:
        n = o_ref.shape[1]
        o_ref[...] = jnp.dot(hb, w_ref[:, off:off + n], preferred_element_type=F32).astype(o_ref.dtype)
        off += n
    small_ref[...] = jnp.dot(hb, ws_ref[...], preferred_element_type=F32)


def _ln_inproj(x2, g, b, w_wide, w_small):
    t = x2.shape[0]
    tm = TM_PROJ
    row = lambda w: pl.BlockSpec((tm, w), lambda i: (i, 0))
    const = lambda a: pl.BlockSpec(a.shape, lambda i: (0, 0), pipeline_mode=pl.Buffered(1))
    out_shape = ([jax.ShapeDtypeStruct((t, D_MODEL), F32)]
                 + [jax.ShapeDtypeStruct((t, w), BF16) for w in _PROJ_WIDTHS]
                 + [jax.ShapeDtypeStruct((t, LANES), F32)])
    out_specs = [row(D_MODEL)] + [row(w) for w in _PROJ_WIDTHS] + [row(LANES)]
    return pl.pallas_call(
        _ln_inproj_kernel,
        out_shape=out_shape,
        grid=(t // tm,),
        in_specs=[row(D_MODEL), const(g), const(b), const(w_wide), const(w_small)],
        out_specs=out_specs,
        compiler_params=_cparams(("parallel",), 52),
        name="ln_inproj",
    )(x2, g, b, w_wide, w_small)


def _forget_cum_kernel(small_ref, bias_ref, sel_ref, kf_ref):
    s = small_ref.shape[0]
    r = CUM_ROWS
    ri = lax.broadcasted_iota(I32, (r, r), 0)
    ci = lax.broadcasted_iota(I32, (r, r), 1)
    tri = jnp.where(ci <= ri, 1.0, 0.0).astype(BF16)
    sel = sel_ref[...]
    carry = jnp.zeros((1, LANES), F32)
    for blk in range(s // r):
        rows = pl.ds(blk * r, r)
        ls = _log_sigmoid(small_ref[rows, :] + bias_ref[...])
        cum = carry
        for p in _split_bf16(ls, 3):
            cum = cum + jnp.dot(tri, p, preferred_element_type=F32)
        carry = cum[r - 1:r, :]
        neg = cum * (-LOG2E)
        p1, p2, p3 = _split_bf16(neg, FORGET_PIECES)
        kf = (p1.astype(F32) * sel[0:1, :] + p2.astype(F32) * sel[1:2, :] + p3.astype(F32) * sel[2:3, :])
        kf_ref[rows, :] = kf.astype(BF16)


def _forget_cum(small, bias_row, sel, batch, seq):
    return pl.pallas_call(
        _forget_cum_kernel,
        out_shape=jax.ShapeDtypeStruct((batch * seq, LANES), BF16),
        grid=(batch,),
        in_specs=[pl.BlockSpec((seq, LANES), lambda b: (b, 0)),
                  pl.BlockSpec((1, LANES), lambda b: (0, 0)),
                  pl.BlockSpec((SUBLANES, LANES), lambda b: (0, 0))],
        out_specs=pl.BlockSpec((seq, LANES), lambda b: (b, 0)),
        compiler_params=_cparams(("parallel",), 32),
        name="forget_cum",
    )(small, bias_row, sel)


def _gla_kernel(q_ref, k_ref, v_ref, r_ref, small_ref, wgu_ref, bg_ref, gn_ref, o_ref,
                state_ref, ds_ref, sb_ref, of_ref):
    rows = q_ref.shape[0]
    nchunk = rows // CHUNK
    kcol = lambda h: slice(h * GLA_DK, (h + 1) * GLA_DK)
    vcol = lambda h: slice(h * GLA_DV, (h + 1) * GLA_DV)
    crow = lambda c: slice(c * CHUNK, (c + 1) * CHUNK)

    @pl.when(pl.program_id(1) == 0)
    def _():
        state_ref[...] = jnp.zeros_like(state_ref)

    z = jnp.dot(small_ref[...].astype(BF16), wgu_ref[...], preferred_element_type=F32) + bg_ref[...]
    la = _log_sigmoid(z) * (1.0 / GLA_TAU)
    ri = lax.broadcasted_iota(I32, (rows, rows), 0)
    ci = lax.broadcasted_iota(I32, (rows, rows), 1)
    shift = CHUNK.bit_length() - 1
    same = lax.shift_right_logical(ri, shift) == lax.shift_right_logical(ci, shift)
    tri = jnp.where(same & (ci <= ri), 1.0, 0.0).astype(BF16)
    cum = jnp.zeros((rows, GLA_QK), F32)
    for p in _split_bf16(la, 2):
        cum = cum + jnp.dot(tri, p, preferred_element_type=F32)
    last = [cum[(c + 1) * CHUNK - 1:(c + 1) * CHUNK, :] for c in range(nchunk)]
    tot = jnp.concatenate([jnp.broadcast_to(t, (CHUNK, GLA_QK)) for t in last], axis=0)
    kd = (k_ref[...].astype(F32) * jnp.exp(tot - cum)).astype(BF16)
    dec_rows = jnp.exp(jnp.concatenate(last + [jnp.zeros((LANES - nchunk, GLA_QK), F32)], axis=0))
    dec_cols = dec_rows.T

    for c in range(nchunk):
        for h in range(GLA_HEADS):
            ds_ref[c, h] = lax.dot_general(kd[crow(c), kcol(h)], v_ref[crow(c), vcol(h)],
                                           (((0,), (0,)), ((), ())), preferred_element_type=F32)
    for h in range(GLA_HEADS):
        st = state_ref[h]
        for c in range(nchunk):
            st = st * jnp.broadcast_to(dec_cols[kcol(h), c:c + 1], (GLA_DK, GLA_DV)) + ds_ref[c, h]
            sb_ref[c, h] = st.astype(BF16)
        state_ref[h] = st
    scale = float(GLA_DK) ** -0.5
    for c in range(nchunk):
        for h in range(GLA_HEADS):
            of_ref[crow(c), vcol(h)] = jnp.dot(q_ref[crow(c), kcol(h)], sb_ref[c, h],
                                               preferred_element_type=F32) * scale
    for h in range(GLA_HEADS):
        o = of_ref[:, vcol(h)]
        o = o * lax.rsqrt(jnp.mean(o * o, axis=-1, keepdims=True) + LN_EPS) * gn_ref[...]
        r = r_ref[:, vcol(h)].astype(F32)
        o_ref[:, vcol(h)] = (o * (r * jax.nn.sigmoid(r))).astype(o_ref.dtype)


def _gla(gq, gk, gv, gr, small, wgu_pad, b_gate, g_norm, batch, seq):
    rows = GLA_ROWS
    nsteps = seq // rows
    row = lambda w: pl.BlockSpec((rows, w), lambda b, i: (b * nsteps + i, 0))
    const = lambda a: pl.BlockSpec(a.shape, lambda b, i: (0, 0))
    return pl.pallas_call(
        _gla_kernel,
        out_shape=jax.ShapeDtypeStruct((batch * seq, GLA_V), BF16),
        grid=(batch, nsteps),
        in_specs=[row(GLA_QK), row(GLA_QK), row(GLA_V), row(GLA_V), row(LANES),
                  const(wgu_pad), const(b_gate), const(g_norm)],
        out_specs=row(GLA_V),
        scratch_shapes=[pltpu.VMEM((GLA_HEADS, GLA_DK, GLA_DV), F32),
                        pltpu.VMEM((rows // CHUNK, GLA_HEADS, GLA_DK, GLA_DV), F32),
                        pltpu.VMEM((rows // CHUNK, GLA_HEADS, GLA_DK, GLA_DV), BF16),
                        pltpu.VMEM((rows, GLA_V), F32)],
        compiler_params=_cparams(("parallel", "arbitrary"), 40),
        name="gla",
    )(gq, gk, gv, gr, small, wgu_pad, b_gate, g_norm)


def _fox_kernel(q_ref, k_ref, v_ref, kf_ref, qsel_ref, o_ref, qa_ref, s_ref, acc_ref, m_ref, l_ref):
    t = FOX_TQ
    nq = q_ref.shape[0] // t
    c = (float(FOX_HD) ** -0.5) * LOG2E
    hd = FOX_HD
    heads = range(FOX_HEADS_PER_STEP)
    for h in heads:
        qa_ref[h, :, 0:hd] = (q_ref[:, h * hd:(h + 1) * hd].astype(F32) * c).astype(BF16)
        qa_ref[h, :, hd:hd + LANES] = jnp.broadcast_to(qsel_ref[h * SUBLANES:h * SUBLANES + 1, :],
                                                       (q_ref.shape[0], LANES))

    def key_rows(j):
        return pl.ds(pl.multiple_of(j * t, t), t)

    def scores(h, i, j):
        ka = jnp.concatenate([k_ref[key_rows(j), h * hd:(h + 1) * hd], kf_ref[key_rows(j), :]], axis=1)
        return lax.dot_general(ka, qa_ref[h, i * t:(i + 1) * t, :], (((1,), (1,)), ((), ())),
                               preferred_element_type=F32)

    def values(h, j, p):
        return lax.dot_general(v_ref[key_rows(j), h * hd:(h + 1) * hd], p, (((0,), (0,)), ((), ())),
                               preferred_element_type=F32)

    def step(j, slot, masked, nxt):
        if nxt is not None:
            for h in heads:
                s_ref[1 - slot, h] = scores(h, *nxt)
        for h in heads:
            s = s_ref[slot, h]
            if masked:
                causal = lax.broadcasted_iota(I32, (t, t), 0) <= lax.broadcasted_iota(I32, (t, t), 1)
                s = jnp.where(causal, s, -jnp.inf)
            m_old = m_ref[h]
            m_new = jnp.maximum(m_old, jnp.broadcast_to(jnp.max(s, axis=0, keepdims=True), (SUBLANES, t)))
            alpha = jnp.exp2(m_old - m_new)
            p = jnp.exp2(s - m_new[0:1, :])
            m_ref[h] = m_new
            l_ref[h] = alpha * l_ref[h] + jnp.sum(p, axis=0, keepdims=True)
            acc_ref[h] = alpha[0:1, :] * acc_ref[h] + values(h, j, p.astype(BF16))

    for h in heads:
        s_ref[0, h] = scores(h, 0, 0)
    slot = 0
    for i in range(nq):
        for h in heads:
            acc_ref[h] = jnp.zeros((hd, t), F32)
            m_ref[h] = jnp.full((SUBLANES, t), -jnp.inf, F32)
            l_ref[h] = jnp.zeros((SUBLANES, t), F32)
        if i >= 2:
            def pair(jj, _, i=i, slot=slot):
                step(2 * jj, slot, False, (i, 2 * jj + 1))
                step(2 * jj + 1, 1 - slot, False, (i, 2 * jj + 2))
                return 0

            lax.fori_loop(0, i // 2, pair, 0)
        if i % 2 == 1:
            step(i - 1, slot, False, (i, i))
            slot = 1 - slot
        step(i, slot, True, (i + 1, 0) if i + 1 < nq else None)
        slot = 1 - slot
        for h in heads:
            o_ref[i * t:(i + 1) * t, h * hd:(h + 1) * hd] = (acc_ref[h] / l_ref[h][0:1, :]).T.astype(o_ref.dtype)


def _fox(fq, fk, fv, kf, qsel, batch, seq):
    hps = FOX_HEADS_PER_STEP
    w = hps * FOX_HD
    t = FOX_TQ
    head_cols = pl.BlockSpec((seq, w), lambda b, h: (b, h))
    return pl.pallas_call(
        _fox_kernel,
        out_shape=jax.ShapeDtypeStruct((batch * seq, FOX_W), BF16),
        grid=(batch, FOX_HEADS // hps),
        in_specs=[head_cols, head_cols, head_cols,
                  pl.BlockSpec((seq, LANES), lambda b, h: (b, 0)),
                  pl.BlockSpec((hps * SUBLANES, LANES), lambda b, h: (h, 0))],
        out_specs=head_cols,
        scratch_shapes=[pltpu.VMEM((hps, seq, FOX_HD + LANES), BF16),
                        pltpu.VMEM((2, hps, t, t), F32), pltpu.VMEM((hps, FOX_HD, t), F32),
                        pltpu.VMEM((hps, SUBLANES, t), F32), pltpu.VMEM((hps, SUBLANES, t), F32)],
        compiler_params=_cparams(("parallel", "parallel"), 48),
        name="fox",
    )(fq, fk, fv, kf, qsel)


def _post_mix_kernel(gla_ref, fox_ref, zg_ref, zf_ref, h_ref, wg_ref, wf_ref, wo_ref, g_ref, b_ref,
                     wr_ref, br_ref, h2_ref, idx_ref, gate_ref):
    tm = h_ref.shape[0]
    y_gla = jnp.dot(gla_ref[...], wg_ref[...], preferred_element_type=F32)
    y_fox = jnp.dot(fox_ref[...], wf_ref[...], preferred_element_type=F32)
    mixed = (jax.nn.sigmoid(zg_ref[...].astype(F32)) * y_gla
             + jax.nn.sigmoid(zf_ref[...].astype(F32)) * y_fox)
    pre = DEEPNORM_ALPHA * h_ref[...] + jnp.dot(mixed.astype(BF16), wo_ref[...], preferred_element_type=F32)
    h2 = _layer_norm(pre, g_ref[...], b_ref[...])
    h2_ref[...] = h2

    logits = lax.dot_general(wr_ref[...], h2.astype(BF16), (((1,), (1,)), ((), ())),
                             preferred_element_type=F32)
    logits = logits + jnp.tile(br_ref[...], (1, tm // LANES))
    eidx = lax.broadcasted_iota(I32, (N_EXPERTS, tm), 0)
    vals, idxs = [], []
    for _ in range(TOP_K):
        mx = jnp.max(logits, axis=0, keepdims=True)
        ik = jnp.min(jnp.where(logits == mx, eidx, N_EXPERTS), axis=0, keepdims=True)
        vals.append(mx)
        idxs.append(ik)
        logits = jnp.where(eidx == ik, -jnp.inf, logits)
    exps = [jnp.exp(v - vals[0]) for v in vals]
    denom = exps[0] + exps[1] + exps[2] + exps[3]
    rid = lax.broadcasted_iota(I32, (SUBLANES, tm), 0)
    idx8 = jnp.zeros((SUBLANES, tm), I32)
    gate8 = jnp.zeros((SUBLANES, tm), F32)
    for k in range(TOP_K):
        idx8 = jnp.where(rid == k, idxs[k], idx8)
        gate8 = jnp.where(rid == k, exps[k] / denom, gate8)
    idx_ref[...] = idx8
    gate_ref[...] = gate8


def _post_mix(gla_o, fox_o, zg, zf, h, wg, wf, wo, g, b, wr_t, br_rep):
    t = h.shape[0]
    tm = TM_PROJ
    row = lambda w: pl.BlockSpec((tm, w), lambda i: (i, 0))
    const = lambda a: pl.BlockSpec(a.shape, lambda i: (0, 0))
    return pl.pallas_call(
        _post_mix_kernel,
        out_shape=[jax.ShapeDtypeStruct((t, D_MODEL), F32),
                   jax.ShapeDtypeStruct((SUBLANES, t), I32),
                   jax.ShapeDtypeStruct((SUBLANES, t), F32)],
        grid=(t // tm,),
        in_specs=[row(GLA_V), row(FOX_W), row(D_MODEL), row(D_MODEL), row(D_MODEL),
                  const(wg), const(wf), const(wo), const(g), const(b), const(wr_t), const(br_rep)],
        out_specs=[row(D_MODEL),
                   pl.BlockSpec((SUBLANES, tm), lambda i: (0, i)), pl.BlockSpec((SUBLANES, tm), lambda i: (0, i))],
        compiler_params=_cparams(("parallel",), 48),
        name="post_mix",
    )(gla_o, fox_o, zg, zf, h, wg, wf, wo, g, b, wr_t, br_rep)


def _onehot_rows(idx8, tw):
    eidx = lax.broadcasted_iota(I32, (N_EXPERTS, tw), 0)
    hit = eidx == idx8[0:1, :]
    for k in range(1, TOP_K):
        hit = hit | (eidx == idx8[k:k + 1, :])
    return eidx, hit


def _rank_kernel(idx_ref, rank_ref, counts_ref, carry_ref):
    tw = idx_ref.shape[1]

    @pl.when(pl.program_id(0) == 0)
    def _():
        carry_ref[...] = jnp.zeros_like(carry_ref)

    idx8 = idx_ref[...]
    eidx, hit = _onehot_rows(idx8, tw)
    onehot = jnp.where(hit, 1.0, 0.0).astype(BF16)
    ri = lax.broadcasted_iota(I32, (tw, tw + LANES), 0)
    ci = lax.broadcasted_iota(I32, (tw, tw + LANES), 1)
    upper = jnp.where((ri < ci) | (ci >= tw), 1.0, 0.0).astype(BF16)
    cnt = jnp.dot(onehot, upper, preferred_element_type=F32)
    before = cnt[:, :tw] + jnp.tile(carry_ref[...], (1, tw // LANES))
    rid = lax.broadcasted_iota(I32, (SUBLANES, tw), 0)
    rank8 = jnp.zeros((SUBLANES, tw), I32)
    for k in range(TOP_K):
        rk = jnp.sum(jnp.where(eidx == idx8[k:k + 1, :], before, 0.0), axis=0, keepdims=True)
        rank8 = jnp.where(rid == k, rk.astype(I32), rank8)
    rank_ref[...] = rank8
    carry_ref[...] = carry_ref[...] + cnt[:, tw:]
    counts_ref[...] = carry_ref[...]


def _rank(idx_t):
    t = idx_t.shape[1]
    tw = RANK_TW
    return pl.pallas_call(
        _rank_kernel,
        out_shape=[jax.ShapeDtypeStruct((SUBLANES, t), I32),
                   jax.ShapeDtypeStruct((N_EXPERTS, LANES), F32)],
        grid=(t // tw,),
        in_specs=[pl.BlockSpec((SUBLANES, tw), lambda i: (0, i))],
        out_specs=[pl.BlockSpec((SUBLANES, tw), lambda i: (0, i)),
                   pl.BlockSpec((N_EXPERTS, LANES), lambda i: (0, 0))],
        scratch_shapes=[pltpu.VMEM((N_EXPERTS, LANES), F32)],
        compiler_params=_cparams(("arbitrary",), 32),
        name="rank",
    )(idx_t)


def _dest_kernel(idx_ref, rank_ref, counts_ref, dest_ref, blk_e_ref, ends_ref, *, nb_pad):
    tw = idx_ref.shape[1]
    nblk = jnp.floor((counts_ref[...] + (MOE_BLK - 1)) * (1.0 / MOE_BLK))
    ri = lax.broadcasted_iota(I32, (N_EXPERTS, N_EXPERTS), 0)
    ci = lax.broadcasted_iota(I32, (N_EXPERTS, N_EXPERTS), 1)
    tri = jnp.where(ci <= ri, 1.0, 0.0).astype(BF16)
    end_blk = jnp.dot(tri, nblk.astype(BF16), preferred_element_type=F32)
    start_row = (end_blk - nblk) * float(MOE_BLK)
    idx8 = idx_ref[...]
    eidx = lax.broadcasted_iota(I32, (N_EXPERTS, tw), 0)
    start_t = jnp.tile(start_row, (1, tw // LANES))
    rid = lax.broadcasted_iota(I32, (SUBLANES, tw), 0)
    dest8 = jnp.zeros((SUBLANES, tw), I32)
    for k in range(TOP_K):
        st = jnp.sum(jnp.where(eidx == idx8[k:k + 1, :], start_t, 0.0), axis=0, keepdims=True)
        dest8 = jnp.where(rid == k, st.astype(I32), dest8)
    dest_ref[...] = dest8 + rank_ref[...]
    bid = lax.broadcasted_iota(I32, (N_EXPERTS, nb_pad), 1).astype(F32)
    ends_t = jnp.tile(end_blk, (1, nb_pad // LANES))
    be = jnp.sum(jnp.where(ends_t <= bid, 1.0, 0.0), axis=0, keepdims=True)
    blk_e_ref[...] = jnp.broadcast_to(jnp.minimum(be, N_EXPERTS - 1.0), (SUBLANES, nb_pad)).astype(I32)
    ends_ref[...] = end_blk.astype(I32)


def _dest(idx_t, rank_t, counts, nb_pad):
    t = idx_t.shape[1]
    tw = RANK_TW
    tok = pl.BlockSpec((SUBLANES, tw), lambda i: (0, i))
    return pl.pallas_call(
        functools.partial(_dest_kernel, nb_pad=nb_pad),
        out_shape=[jax.ShapeDtypeStruct((SUBLANES, t), I32),
                   jax.ShapeDtypeStruct((SUBLANES, nb_pad), I32),
                   jax.ShapeDtypeStruct((N_EXPERTS, LANES), I32)],
        grid=(t // tw,),
        in_specs=[tok, tok, pl.BlockSpec((N_EXPERTS, LANES), lambda i: (0, 0))],
        out_specs=[tok, pl.BlockSpec((SUBLANES, nb_pad), lambda i: (0, 0)),
                   pl.BlockSpec((N_EXPERTS, LANES), lambda i: (0, 0))],
        compiler_params=_cparams(("arbitrary",), 32),
        name="dest",
    )(idx_t, rank_t, counts)


def _dispatch_kernel(ends_ref, dest_ref, h2_ref, xs_ref, slab_ref, zero_ref, sem, zsem):
    td = h2_ref.shape[0]

    nb = xs_ref.shape[0] // MOE_BLK
    n_used = ends_ref[N_EXPERTS - 1]

    def zero_block(blk):
        start = pl.multiple_of(blk * MOE_BLK, MOE_BLK)
        return pltpu.make_async_copy(zero_ref, xs_ref.at[pl.ds(start, MOE_BLK)], zsem)

    def has_rows(e):
        prev = jnp.where(e == 0, 0, ends_ref[jnp.maximum(e - 1, 0)])
        return ends_ref[e] > prev

    @pl.when(pl.program_id(0) == 0)
    def _():
        zero_ref[...] = jnp.zeros_like(zero_ref)

        def each(fn):
            def last_block(e, _):
                @pl.when(has_rows(e))
                def _():
                    fn(zero_block(ends_ref[e] - 1))
                return 0

            def tail_block(blk, _):
                fn(zero_block(blk))
                return 0

            lax.fori_loop(0, N_EXPERTS, last_block, 0)
            lax.fori_loop(n_used, nb, tail_block, 0)

        each(lambda cp: cp.start())
        each(lambda cp: cp.wait())

    for c in range(SLAB_ROWS):
        slab_ref[:, c, :] = h2_ref[:, c * LANES:(c + 1) * LANES]

    def start_rows(t, _):
        for k in range(TOP_K):
            pltpu.make_async_copy(slab_ref.at[t], xs_ref.at[dest_ref[k, t]], sem).start(priority=k % 2)
        return 0

    lax.fori_loop(0, td, start_rows, 0, unroll=ROW_DMA_UNROLL)
    for k in range(TOP_K):
        pltpu.make_async_copy(slab_ref, xs_ref.at[pl.ds(0, td)], sem).wait()


def _dispatch(ends, dest_t, h2, nb):
    t = h2.shape[0]
    td = DISPATCH_TOK
    grid_spec = pltpu.PrefetchScalarGridSpec(
        num_scalar_prefetch=1,
        grid=(t // td,),
        in_specs=[pl.BlockSpec((SUBLANES, td), lambda i, ends: (0, i), memory_space=pltpu.SMEM),
                  pl.BlockSpec((td, D_MODEL), lambda i, ends: (i, 0))],
        out_specs=pl.BlockSpec(memory_space=pl.ANY),
        scratch_shapes=[pltpu.VMEM((td, SLAB_ROWS, LANES), F32), pltpu.VMEM((MOE_BLK, SLAB_ROWS, LANES), F32),
                        pltpu.SemaphoreType.DMA(()), pltpu.SemaphoreType.DMA(())],
    )
    return pl.pallas_call(
        _dispatch_kernel,
        out_shape=jax.ShapeDtypeStruct((nb * MOE_BLK, SLAB_ROWS, LANES), F32),
        grid_spec=grid_spec,
        compiler_params=_cparams(("arbitrary",), 32),
        name="dispatch",
    )(ends, dest_t, h2)


def _experts_kernel(blk_e_ref, nused_ref, ends_ref, xs_hbm, wgu_hbm, bgu_ref, wd_hbm, bd_ref, ys_hbm,
                    xbuf, ybuf, wgu_f32, wd_f32, wgu_bf, wd_bf, slot_ref, sems, xsems, ysems):
    j = pl.program_id(0)
    nb = pl.num_programs(0)
    par = lax.rem(j, 2)
    n_used = nused_ref[0]
    active = j < n_used
    e = blk_e_ref[j]
    prev = blk_e_ref[jnp.maximum(j - 1, 0)]
    fresh = (j == 0) | (e != prev)

    def fetch(expert, slot):
        return (pltpu.make_async_copy(wgu_hbm.at[expert], wgu_f32.at[slot], sems.at[slot, 0]),
                pltpu.make_async_copy(wd_hbm.at[expert], wd_f32.at[slot], sems.at[slot, 1]))

    def x_load(blk, half):
        rows = pl.ds(pl.multiple_of(blk * MOE_BLK, MOE_BLK), MOE_BLK)
        return [pltpu.make_async_copy(xs_hbm.at[rows, c, :], xbuf.at[half, :, pl.ds(c * LANES, LANES)], xsems.at[half])
                for c in range(SLAB_ROWS)]

    def y_store(blk, half):
        rows = pl.ds(pl.multiple_of(blk * MOE_BLK, MOE_BLK), MOE_BLK)
        return [pltpu.make_async_copy(ybuf.at[half, :, pl.ds(c * LANES, LANES)], ys_hbm.at[rows, c, :], ysems.at[half])
                for c in range(SLAB_ROWS)]

    @pl.when(j == 0)
    def _():
        for cp in x_load(0, 0):
            cp.start()

    @pl.when(j + 1 < n_used)
    def _():
        for cp in x_load(j + 1, 1 - par):
            cp.start()

    @pl.when(j >= 2)
    def _():
        for cp in y_store(j - 2, par):
            cp.wait()

    @pl.when(j == 0)
    def _():
        slot_ref[0] = 0
        for cp in fetch(e, 0):
            cp.start()

    @pl.when(active & fresh)
    def _():
        slot = slot_ref[0]
        for cp in fetch(e, slot):
            cp.wait()
        step = 256
        for r in range(0, D_MODEL, step):
            wgu_bf[r:r + step, :] = wgu_f32[slot, r:r + step, :].astype(BF16)
        for r in range(0, D_FF, step):
            wd_bf[r:r + step, :] = wd_f32[slot, r:r + step, :].astype(BF16)
        next_blk = ends_ref[e]

        @pl.when(next_blk < n_used)
        def _():
            for cp in fetch(blk_e_ref[jnp.minimum(next_blk, n_used - 1)], 1 - slot):
                cp.start()

        slot_ref[0] = 1 - slot

    @pl.when(active)
    def _():
        for cp in x_load(j, par):
            cp.wait()
        x = xbuf[par].astype(BF16)
        gu = jnp.dot(x, wgu_bf[...], preferred_element_type=F32) + bgu_ref[...]
        gate = jnp.minimum(gu[:, :D_FF], SWIGLU_LIMIT)
        up = jnp.clip(gu[:, D_FF:], -SWIGLU_LIMIT, SWIGLU_LIMIT)
        glu = gate * jax.nn.sigmoid(SWIGLU_ALPHA * gate)
        act = ((up + 1.0) * glu).astype(BF16)
        ybuf[par] = jnp.dot(act, wd_bf[...], preferred_element_type=F32) + bd_ref[...]

    @pl.when(jnp.logical_not(active))
    def _():
        ybuf[par] = jnp.zeros((MOE_BLK, D_MODEL), F32)

    for cp in y_store(j, par):
        cp.start()

    @pl.when(j == nb - 1)
    def _():
        for cp in y_store(j, par) + y_store(j - 1, 1 - par):
            cp.wait()


def _experts(blk_e, nused, ends, xs, w_gate_up, b_gate_up, w_down, b_down, nb):
    assert nb >= 2

    def exp_of(j, blk_e, nused):
        return blk_e[jnp.minimum(j, nused[0] - 1)]

    grid_spec = pltpu.PrefetchScalarGridSpec(
        num_scalar_prefetch=3,
        grid=(nb,),
        in_specs=[pl.BlockSpec(memory_space=pl.ANY),
                  pl.BlockSpec(memory_space=pl.ANY),
                  pl.BlockSpec((None, 1, 2 * D_FF), lambda j, be, nu, en: (exp_of(j, be, nu), 0, 0)),
                  pl.BlockSpec(memory_space=pl.ANY),
                  pl.BlockSpec((None, 1, D_MODEL), lambda j, be, nu, en: (exp_of(j, be, nu), 0, 0))],
        out_specs=pl.BlockSpec(memory_space=pl.ANY),
        scratch_shapes=[pltpu.VMEM((2, MOE_BLK, D_MODEL), F32), pltpu.VMEM((2, MOE_BLK, D_MODEL), F32),
                        pltpu.VMEM((2, D_MODEL, 2 * D_FF), F32), pltpu.VMEM((2, D_FF, D_MODEL), F32),
                        pltpu.VMEM((D_MODEL, 2 * D_FF), BF16), pltpu.VMEM((D_FF, D_MODEL), BF16),
                        pltpu.SMEM((1,), I32), pltpu.SemaphoreType.DMA((2, 2)),
                        pltpu.SemaphoreType.DMA((2,)), pltpu.SemaphoreType.DMA((2,))],
    )
    return pl.pallas_call(
        _experts_kernel,
        out_shape=jax.ShapeDtypeStruct((nb * MOE_BLK, SLAB_ROWS, LANES), F32),
        grid_spec=grid_spec,
        compiler_params=_cparams(("arbitrary",), 56),
        name="experts",
    )(blk_e, nused, ends, xs, w_gate_up, b_gate_up, w_down, b_down)


def _combine_kernel(dest_ref, dest_next_ref, gate_ref, h2_ref, g_ref, b_ref, ys_ref, o_ref, buf_ref, ffn_ref, sems):
    tc = h2_ref.shape[0]
    i = pl.program_id(0)
    slot = lax.rem(i, 2)

    def gather(d_ref, s):
        def start_rows(t, _):
            for k in range(TOP_K):
                pltpu.make_async_copy(ys_ref.at[d_ref[k, t]], buf_ref.at[s, k, t], sems.at[s]).start(priority=k % 2)
            return 0

        lax.fori_loop(0, tc, start_rows, 0, unroll=ROW_DMA_UNROLL)

    @pl.when(i == 0)
    def _():
        gather(dest_ref, 0)

    @pl.when(i + 1 < pl.num_programs(0))
    def _():
        gather(dest_next_ref, 1 - slot)

    for k in range(TOP_K):
        pltpu.make_async_copy(ys_ref.at[pl.ds(0, tc)], buf_ref.at[slot, k], sems.at[slot]).wait()

    def weigh(t, _):
        acc = gate_ref[0, t] * buf_ref[slot, 0, t]
        for k in range(1, TOP_K):
            acc = acc + gate_ref[k, t] * buf_ref[slot, k, t]
        ffn_ref[t] = acc
        return 0

    lax.fori_loop(0, tc, weigh, 0, unroll=ROW_DMA_UNROLL)
    ffn = jnp.concatenate([ffn_ref[:, c, :] for c in range(SLAB_ROWS)], axis=1)
    o_ref[...] = _layer_norm(DEEPNORM_ALPHA * h2_ref[...] + ffn, g_ref[...], b_ref[...])


def _combine(dest_t, gate_t, h2, g, b, ys):
    t = h2.shape[0]
    tc = COMBINE_TOK
    last = t // tc - 1
    return pl.pallas_call(
        _combine_kernel,
        out_shape=jax.ShapeDtypeStruct((t, D_MODEL), F32),
        grid=(t // tc,),
        in_specs=[pl.BlockSpec((SUBLANES, tc), lambda i: (0, i), memory_space=pltpu.SMEM),
                  pl.BlockSpec((SUBLANES, tc), lambda i: (0, jnp.minimum(i + 1, last)), memory_space=pltpu.SMEM),
                  pl.BlockSpec((SUBLANES, tc), lambda i: (0, i), memory_space=pltpu.SMEM),
                  pl.BlockSpec((tc, D_MODEL), lambda i: (i, 0)),
                  pl.BlockSpec((1, D_MODEL), lambda i: (0, 0)),
                  pl.BlockSpec((1, D_MODEL), lambda i: (0, 0)),
                  pl.BlockSpec(memory_space=pl.ANY)],
        out_specs=pl.BlockSpec((tc, D_MODEL), lambda i: (i, 0)),
        scratch_shapes=[pltpu.VMEM((2, TOP_K, tc, SLAB_ROWS, LANES), F32), pltpu.VMEM((tc, SLAB_ROWS, LANES), F32),
                        pltpu.SemaphoreType.DMA((2,))],
        compiler_params=_cparams(("arbitrary",), 32),
        name="combine",
    )(dest_t, dest_t, gate_t, h2, g, b, ys)


def kernel(x, ln_in_g, ln_in_b, w_in, w_gla_gate_up, b_gla_gate, g_gla_norm, b_forget, w_gla_proj, w_fox_proj, w_out, ln_mix_g, ln_mix_b, w_router, b_router, w_gate_up, b_gate_up, w_down, b_down, ln_ffn_g, ln_ffn_b):
    batch, seq, d = x.shape
    assert d == D_MODEL and w_in.shape[0] == DEPTH == 1
    assert seq % max(GLA_ROWS, FOX_TQ, FOX_TK, CUM_ROWS) == 0
    t = batch * seq
    assert t % max(TM_PROJ, RANK_TW, DISPATCH_TOK, COMBINE_TOK) == 0 and t // MOE_BLK <= 256
    row = lambda v: v.reshape(1, -1).astype(F32)

    n_forget = FOX_HEADS * FORGET_PIECES
    lane = jnp.arange(LANES)
    sel_a = ((lane[:, None] == lane[None, :]) & (lane[None, :] < GLA_RANK)).astype(BF16)
    src_b = _NARROW_B % LANES + (lane[None, :] - FORGET_COL0) // FORGET_PIECES
    sel_b = ((lane[:, None] == src_b) & (lane[None, :] >= FORGET_COL0)
             & (lane[None, :] < FORGET_COL0 + n_forget)).astype(BF16)
    w_wide, w_small = _regroup_w_in(w_in[0], sel_a, sel_b)
    in_forget = (lane >= FORGET_COL0) & (lane < FORGET_COL0 + n_forget)
    piece = (lane - FORGET_COL0) % FORGET_PIECES
    forget_bias = jnp.zeros((1, LANES), F32).at[0, FORGET_COL0:FORGET_COL0 + n_forget].set(
        jnp.repeat(b_forget[0].astype(F32), FORGET_PIECES))
    piece_sel = jnp.stack([(in_forget & (piece == p)).astype(F32) for p in range(FORGET_PIECES)]
                          + [jnp.zeros((LANES,), F32)] * (SUBLANES - FORGET_PIECES))
    head_of_lane = (lane - FORGET_COL0) // FORGET_PIECES
    qsel = jnp.repeat(jnp.stack([(in_forget & (head_of_lane == h)) for h in range(FOX_HEADS)]), SUBLANES,
                      axis=0).astype(BF16)
    wgu_pad = jnp.concatenate([w_gla_gate_up[0], jnp.zeros((LANES - GLA_RANK, GLA_QK), F32)], axis=0).astype(BF16)

    x2 = x.reshape(t, d)
    h, gq, gk, gv, gr, fq, fk, fv, zg, zf, small = _ln_inproj(x2, row(ln_in_g), row(ln_in_b), w_wide, w_small)
    kf = _forget_cum(small, forget_bias, piece_sel, batch, seq)
    gla_o = _gla(gq, gk, gv, gr, small, wgu_pad, row(b_gla_gate[0]), row(g_gla_norm[0]), batch, seq)
    fox_o = _fox(fq, fk, fv, kf, qsel, batch, seq)

    br_rep = jnp.broadcast_to(b_router[0].astype(F32)[:, None], (N_EXPERTS, LANES))
    h2, idx_t, gate_t = _post_mix(
        gla_o, fox_o, zg, zf, h, w_gla_proj[0].astype(BF16), w_fox_proj[0].astype(BF16), w_out[0].astype(BF16),
        row(ln_mix_g[0]), row(ln_mix_b[0]), w_router[0].T.astype(BF16), br_rep)

    nb = (t * TOP_K + N_EXPERTS * (MOE_BLK - 1) + MOE_BLK - 1) // MOE_BLK
    nb_pad = (nb + LANES - 1) // LANES * LANES
    rank_t, counts = _rank(idx_t)
    dest_t, blk_e8, ends = _dest(idx_t, rank_t, counts, nb_pad)
    ends1 = ends[:, 0]
    xs = _dispatch(ends1, dest_t, h2, nb)
    ys = _experts(blk_e8[0, :nb], ends1[N_EXPERTS - 1:], ends1, xs, w_gate_up[0], b_gate_up[0][:, None, :],
                  w_down[0], b_down[0][:, None, :], nb)
    out = _combine(dest_t, gate_t, h2, row(ln_ffn_g[0]), row(ln_ffn_b[0]), ys)
    return out.reshape(batch, seq, d)
```

```python
import functools

import jax
import jax.numpy as jnp
from jax import lax
from jax.experimental import pallas as pl
from jax.experimental.pallas import tpu as pltpu

F32 = jnp.float32
BF16 = jnp.bfloat16
I32 = jnp.int32

D_MODEL = 1024
CHUNK = 64
GLA_HEADS, GLA_DK, GLA_DV, GLA_RANK, GLA_TAU = 4, 128, 256, 16, 16.0
GLA_QK = GLA_HEADS * GLA_DK
GLA_V = GLA_HEADS * GLA_DV
FOX_HEADS, FOX_HD = 8, 128
FOX_W = FOX_HEADS * FOX_HD
N_EXPERTS, TOP_K, D_FF = 32, 4, 1024
SWIGLU_LIMIT, SWIGLU_ALPHA = 7.0, 1.702
LN_EPS = 1e-5
DEPTH = 1
DEEPNORM_ALPHA = (2 * DEPTH) ** 0.25
IN_SPLITS = (GLA_QK, GLA_QK, GLA_V, GLA_RANK, GLA_V, FOX_W, FOX_W, FOX_W, FOX_HEADS, D_MODEL, D_MODEL)

LANES = 128
SUBLANES = 8
VMEM_BYTES_V7X = 64 * 1024 * 1024
SLAB_ROWS = D_MODEL // LANES

FORGET_COL0 = GLA_RANK
FORGET_PIECES = 3

TM_PROJ = 512
GLA_ROWS = 512
FOX_TQ = 512
FOX_TK = 512
FOX_HEADS_PER_STEP = 2
FOX_ONES_ROWS = 16
LOG2E = 1.4426950408889634
CUM_ROWS = 256
RANK_TW = 512
MOE_BLK = 512
DISPATCH_TOK = 256
COMBINE_TOK = 256
ROW_DMA_UNROLL = 8


def _cparams(sem, vmem_mib, flags=None):
    return pltpu.CompilerParams(dimension_semantics=sem, vmem_limit_bytes=vmem_mib * 1024 * 1024, flags=flags)


def _log_sigmoid(z):
    return jnp.minimum(z, 0.0) - jnp.log1p(jnp.exp(-jnp.abs(z)))


def _layer_norm(x, g, b):
    mu = jnp.mean(x, axis=-1, keepdims=True)
    xc = x - mu
    var = jnp.mean(xc * xc, axis=-1, keepdims=True)
    return xc * lax.rsqrt(var + LN_EPS) * g + b


def _split_bf16(x, pieces):
    out = []
    for _ in range(pieces):
        p = x.astype(BF16)
        out.append(p)
        x = x - p.astype(F32)
    return out


_NARROW_A = sum(IN_SPLITS[:3])
_NARROW_B = sum(IN_SPLITS[:8])


_REGROUP_COLS = 512


def _regroup_kernel(wt_hbm, wide_ref, small_ref, buf, nbuf, sems, nsem):
    j = pl.program_id(0)
    cols = _REGROUP_COLS
    half = lax.rem(j, 2)

    def load(step, s):
        c0 = step * cols
        src = c0 + jnp.where(c0 >= _NARROW_A, GLA_RANK, 0) + jnp.where(c0 >= _NARROW_B - GLA_RANK, FOX_HEADS, 0)
        return pltpu.make_async_copy(wt_hbm.at[pl.ds(pl.multiple_of(src, SUBLANES), cols), :], buf.at[s], sems.at[s])

    @pl.when(j == 0)
    def _():
        load(0, 0).start()
        nbuf[...] = jnp.zeros_like(nbuf)
        copies = [pltpu.make_async_copy(wt_hbm.at[pl.ds(_NARROW_A, GLA_RANK), :], nbuf.at[pl.ds(0, GLA_RANK), :], nsem)]
        for h in range(FOX_HEADS):
            for p in range(FORGET_PIECES):
                copies.append(pltpu.make_async_copy(
                    wt_hbm.at[pl.ds(_NARROW_B + h, 1), :],
                    nbuf.at[pl.ds(FORGET_COL0 + FORGET_PIECES * h + p, 1), :], nsem))
        for cp in copies:
            cp.start()
        for cp in copies:
            cp.wait()
        small_ref[...] = nbuf[...].T.astype(small_ref.dtype)

    @pl.when(j + 1 < pl.num_programs(0))
    def _():
        load(j + 1, 1 - half).start()

    load(j, half).wait()
    wide_ref[...] = buf[half].T.astype(wide_ref.dtype)


def _regroup_w_in(wt):
    n_wide = wt.shape[0] - GLA_RANK - FOX_HEADS
    assert _NARROW_A % _REGROUP_COLS == 0 and (_NARROW_B - GLA_RANK) % _REGROUP_COLS == 0
    return pl.pallas_call(
        _regroup_kernel,
        out_shape=[jax.ShapeDtypeStruct((D_MODEL, n_wide), BF16), jax.ShapeDtypeStruct((D_MODEL, LANES), BF16)],
        grid=(n_wide // _REGROUP_COLS,),
        in_specs=[pl.BlockSpec(memory_space=pl.ANY)],
        out_specs=[pl.BlockSpec((D_MODEL, _REGROUP_COLS), lambda j: (0, j)),
                   pl.BlockSpec((D_MODEL, LANES), lambda j: (0, 0))],
        scratch_shapes=[pltpu.VMEM((2, _REGROUP_COLS, D_MODEL), F32), pltpu.VMEM((LANES, D_MODEL), F32),
                        pltpu.SemaphoreType.DMA((2,)), pltpu.SemaphoreType.DMA(())],
        compiler_params=_cparams(("arbitrary",), 32),
        name="regroup_w_in",
    )(wt)


_PROJ_WIDTHS = (GLA_QK, GLA_QK, GLA_V, GLA_V, FOX_W, FOX_W, FOX_W, D_MODEL, D_MODEL)


def _ln_inproj_kernel(x0_ref, xn_ref, g_ref, b_ref, w_ref, ws_ref, *refs):
    wide_refs, small_ref, hb_refs = refs[:-3], refs[-3], refs[-2:]
    i = pl.program_id(0)

    @pl.when(i == 0)
    def _():
        hb_refs[0][...] = _layer_norm(x0_ref[...], g_ref[...], b_ref[...]).astype(BF16)

    def project(cur_ref, nxt_ref):
        hb = cur_ref[...]
        off = 0
        for o_ref in wide_refs:
            n = o_ref.shape[1]
            o_ref[...] = jnp.dot(hb, w_ref[:, off:off + n], preferred_element_type=F32).astype(o_ref.dtype)
            off += n
        small_ref[...] = jnp.dot(hb, ws_ref[...], preferred_element_type=F32)
        nxt_ref[...] = _layer_norm(xn_ref[...], g_ref[...], b_ref[...]).astype(BF16)

    for parity in range(2):
        pl.when(lax.rem(i, 2) == parity)(functools.partial(project, hb_refs[parity], hb_refs[1 - parity]))


def _ln_inproj(x2, g, b, w_wide, w_small):
    t = x2.shape[0]
    tm = TM_PROJ
    last = t // tm - 1
    row = lambda w: pl.BlockSpec((tm, w), lambda i: (i, 0))
    const = lambda a: pl.BlockSpec(a.shape, lambda i: (0, 0), pipeline_mode=pl.Buffered(1))
    out_shape = ([jax.ShapeDtypeStruct((t, w), BF16) for w in _PROJ_WIDTHS]
                 + [jax.ShapeDtypeStruct((t, LANES), F32)])
    out_specs = [row(w) for w in _PROJ_WIDTHS] + [row(LANES)]
    return pl.pallas_call(
        _ln_inproj_kernel,
        out_shape=out_shape,
        grid=(t // tm,),
        in_specs=[pl.BlockSpec((tm, D_MODEL), lambda i: (0, 0), pipeline_mode=pl.Buffered(1)),
                  pl.BlockSpec((tm, D_MODEL), lambda i: (jnp.minimum(i + 1, last), 0)),
                  const(g), const(b), const(w_wide), const(w_small)],
        out_specs=out_specs,
        scratch_shapes=[pltpu.VMEM((tm, D_MODEL), BF16), pltpu.VMEM((tm, D_MODEL), BF16)],
        compiler_params=_cparams(("arbitrary",), 52),
        name="ln_inproj",
    )(x2, x2, g, b, w_wide, w_small)


def _forget_cum_kernel(small_ref, bias_ref, sel_ref, kf_ref):
    s = small_ref.shape[0]
    r = CUM_ROWS
    ri = lax.broadcasted_iota(I32, (r, r), 0)
    ci = lax.broadcasted_iota(I32, (r, r), 1)
    tri = jnp.where(ci <= ri, 1.0, 0.0).astype(BF16)
    sel = sel_ref[...]
    carry = jnp.zeros((1, LANES), F32)
    for blk in range(s // r):
        rows = pl.ds(blk * r, r)
        ls = _log_sigmoid(small_ref[rows, :] + bias_ref[...])
        cum = carry
        for p in _split_bf16(ls, 3):
            cum = cum + jnp.dot(tri, p, preferred_element_type=F32)
        carry = cum[r - 1:r, :]
        neg = cum * (-LOG2E)
        p1, p2, p3 = _split_bf16(neg, FORGET_PIECES)
        kf = (p1.astype(F32) * sel[0:1, :] + p2.astype(F32) * sel[1:2, :] + p3.astype(F32) * sel[2:3, :])
        kf_ref[rows, :] = kf.astype(BF16)


def _forget_cum(small, bias_row, sel, batch, seq):
    return pl.pallas_call(
        _forget_cum_kernel,
        out_shape=jax.ShapeDtypeStruct((batch * seq, LANES), BF16),
        grid=(batch,),
        in_specs=[pl.BlockSpec((seq, LANES), lambda b: (b, 0)),
                  pl.BlockSpec((1, LANES), lambda b: (0, 0)),
                  pl.BlockSpec((SUBLANES, LANES), lambda b: (0, 0))],
        out_specs=pl.BlockSpec((seq, LANES), lambda b: (b, 0)),
        compiler_params=_cparams(("parallel",), 32),
        name="forget_cum",
    )(small, bias_row, sel)


def _gla_kernel(q_ref, k_ref, v_ref, r_ref, small_ref, wgu_ref, bg_ref, gn_ref, o_ref,
                state_ref, ds_ref, sb_ref, of_ref):
    rows = q_ref.shape[0]
    nchunk = rows // CHUNK
    kcol = lambda h: slice(h * GLA_DK, (h + 1) * GLA_DK)
    vcol = lambda h: slice(h * GLA_DV, (h + 1) * GLA_DV)
    crow = lambda c: slice(c * CHUNK, (c + 1) * CHUNK)

    @pl.when(pl.program_id(1) == 0)
    def _():
        state_ref[...] = jnp.zeros_like(state_ref)

    z = jnp.dot(small_ref[...].astype(BF16), wgu_ref[...], preferred_element_type=F32) + bg_ref[...]
    la = _log_sigmoid(z) * (1.0 / GLA_TAU)
    ri = lax.broadcasted_iota(I32, (rows, rows), 0)
    ci = lax.broadcasted_iota(I32, (rows, rows), 1)
    shift = CHUNK.bit_length() - 1
    same = lax.shift_right_logical(ri, shift) == lax.shift_right_logical(ci, shift)
    tri = jnp.where(same & (ci <= ri), 1.0, 0.0).astype(BF16)
    cum = jnp.zeros((rows, GLA_QK), F32)
    for p in _split_bf16(la, 2):
        cum = cum + jnp.dot(tri, p, preferred_element_type=F32)
    last = [cum[(c + 1) * CHUNK - 1:(c + 1) * CHUNK, :] for c in range(nchunk)]
    tot = jnp.concatenate([jnp.broadcast_to(t, (CHUNK, GLA_QK)) for t in last], axis=0)
    kd = (k_ref[...].astype(F32) * jnp.exp(tot - cum)).astype(BF16)
    dec_rows = jnp.exp(jnp.concatenate(last + [jnp.zeros((LANES - nchunk, GLA_QK), F32)], axis=0))
    dec_cols = dec_rows.T

    for c in range(nchunk):
        for h in range(GLA_HEADS):
            ds_ref[c, h] = lax.dot_general(kd[crow(c), kcol(h)], v_ref[crow(c), vcol(h)],
                                           (((0,), (0,)), ((), ())), preferred_element_type=F32)
    for h in range(GLA_HEADS):
        st = state_ref[h]
        for c in range(nchunk):
            st = st * jnp.broadcast_to(dec_cols[kcol(h), c:c + 1], (GLA_DK, GLA_DV)) + ds_ref[c, h]
            sb_ref[c, h] = st.astype(BF16)
        state_ref[h] = st
    scale = float(GLA_DK) ** -0.5
    for c in range(nchunk):
        for h in range(GLA_HEADS):
            of_ref[crow(c), vcol(h)] = jnp.dot(q_ref[crow(c), kcol(h)], sb_ref[c, h],
                                               preferred_element_type=F32) * scale
    for h in range(GLA_HEADS):
        o = of_ref[:, vcol(h)]
        o = o * lax.rsqrt(jnp.mean(o * o, axis=-1, keepdims=True) + LN_EPS) * gn_ref[...]
        r = r_ref[:, vcol(h)].astype(F32)
        o_ref[:, vcol(h)] = (o * (r * jax.nn.sigmoid(r))).astype(o_ref.dtype)


def _gla(gq, gk, gv, gr, small, wgu_pad, b_gate, g_norm, batch, seq):
    rows = GLA_ROWS
    nsteps = seq // rows
    row = lambda w: pl.BlockSpec((rows, w), lambda b, i: (b * nsteps + i, 0))
    const = lambda a: pl.BlockSpec(a.shape, lambda b, i: (0, 0))
    return pl.pallas_call(
        _gla_kernel,
        out_shape=jax.ShapeDtypeStruct((batch * seq, GLA_V), BF16),
        grid=(batch, nsteps),
        in_specs=[row(GLA_QK), row(GLA_QK), row(GLA_V), row(GLA_V), row(LANES),
                  const(wgu_pad), const(b_gate), const(g_norm)],
        out_specs=row(GLA_V),
        scratch_shapes=[pltpu.VMEM((GLA_HEADS, GLA_DK, GLA_DV), F32),
                        pltpu.VMEM((rows // CHUNK, GLA_HEADS, GLA_DK, GLA_DV), F32),
                        pltpu.VMEM((rows // CHUNK, GLA_HEADS, GLA_DK, GLA_DV), BF16),
                        pltpu.VMEM((rows, GLA_V), F32)],
        compiler_params=_cparams(("parallel", "arbitrary"), 40),
        name="gla",
    )(gq, gk, gv, gr, small, wgu_pad, b_gate, g_norm)


def _fox_kernel(q_ref, k_ref, v_ref, kf_ref, qsel_ref, o_ref, qa_ref, vt_ref, s_ref, acc_ref, m_ref):
    t = FOX_TQ
    seq = q_ref.shape[0]
    nq = seq // t
    c = (float(FOX_HD) ** -0.5) * LOG2E
    hd = FOX_HD
    heads = range(FOX_HEADS_PER_STEP)
    for h in heads:
        for r in range(0, seq, t):
            qa_ref[h, 0:hd, r:r + t] = (q_ref[r:r + t, h * hd:(h + 1) * hd].astype(F32) * c).T.astype(BF16)
            vt_ref[h, 0:hd, r:r + t] = v_ref[r:r + t, h * hd:(h + 1) * hd].astype(F32).T.astype(BF16)
        vt_ref[h, hd:hd + FOX_ONES_ROWS, :] = jnp.ones((FOX_ONES_ROWS, seq), BF16)
        qa_ref[h, hd:hd + LANES, :] = jnp.tile(qsel_ref[h], (1, seq // LANES))

    def key_rows(j):
        return pl.ds(pl.multiple_of(j * t, t), t)

    def scores(h, i, j):
        ka = jnp.concatenate([k_ref[key_rows(j), h * hd:(h + 1) * hd], kf_ref[key_rows(j), :]], axis=1)
        return jnp.dot(ka, qa_ref[h, :, i * t:(i + 1) * t], preferred_element_type=F32)

    def values(h, j, p):
        return jnp.dot(vt_ref[h, :, key_rows(j)], p, preferred_element_type=F32)

    def step(j, slot, masked, nxt):
        if nxt is not None:
            for h in heads:
                s_ref[1 - slot, h] = scores(h, *nxt)
        for h in heads:
            s = s_ref[slot, h]
            if masked:
                causal = lax.broadcasted_iota(I32, (t, t), 0) <= lax.broadcasted_iota(I32, (t, t), 1)
                s = jnp.where(causal, s, -jnp.inf)
            m_old = m_ref[h]
            m_new = jnp.maximum(m_old, jnp.broadcast_to(jnp.max(s, axis=0, keepdims=True), (SUBLANES, t)))
            alpha = jnp.exp2(m_old - m_new)
            p = jnp.exp2(s - m_new[0:1, :])
            m_ref[h] = m_new
            acc_ref[h] = alpha[0:1, :] * acc_ref[h] + values(h, j, p.astype(BF16))

    for h in heads:
        s_ref[0, h] = scores(h, 0, 0)
    slot = 0
    for i in range(nq):
        for h in heads:
            acc_ref[h] = jnp.zeros((hd + FOX_ONES_ROWS, t), F32)
            m_ref[h] = jnp.full((SUBLANES, t), -jnp.inf, F32)
        if i >= 2:
            def pair(jj, _, i=i, slot=slot):
                step(2 * jj, slot, False, (i, 2 * jj + 1))
                step(2 * jj + 1, 1 - slot, False, (i, 2 * jj + 2))
                return 0

            lax.fori_loop(0, i // 2, pair, 0)
        if i % 2 == 1:
            step(i - 1, slot, False, (i, i))
            slot = 1 - slot
        step(i, slot, True, (i + 1, 0) if i + 1 < nq else None)
        slot = 1 - slot
        for h in heads:
            acc = acc_ref[h]
            o_ref[i * t:(i + 1) * t, h * hd:(h + 1) * hd] = (acc[0:hd] / acc[hd:hd + 1]).T.astype(o_ref.dtype)


def _fox(fq, fk, fv, kf, qsel, batch, seq):
    hps = FOX_HEADS_PER_STEP
    w = hps * FOX_HD
    t = FOX_TQ
    head_cols = pl.BlockSpec((seq, w), lambda b, h: (b, h))
    return pl.pallas_call(
        _fox_kernel,
        out_shape=jax.ShapeDtypeStruct((batch * seq, FOX_W), BF16),
        grid=(batch, FOX_HEADS // hps),
        in_specs=[head_cols, head_cols, head_cols,
                  pl.BlockSpec((seq, LANES), lambda b, h: (b, 0)),
                  pl.BlockSpec((hps, LANES, LANES), lambda b, h: (h, 0, 0))],
        out_specs=head_cols,
        scratch_shapes=[pltpu.VMEM((hps, FOX_HD + LANES, seq), BF16),
                        pltpu.VMEM((hps, FOX_HD + FOX_ONES_ROWS, seq), BF16),
                        pltpu.VMEM((2, hps, t, t), F32), pltpu.VMEM((hps, FOX_HD + FOX_ONES_ROWS, t), F32),
                        pltpu.VMEM((hps, SUBLANES, t), F32)],
        compiler_params=_cparams(("parallel", "parallel"), 48),
        name="fox",
    )(fq, fk, fv, kf, qsel)


def _post_mix_kernel(gla_ref, fox_ref, zg_ref, zf_ref, x_ref, gin_ref, bin_ref, wg_ref, wf_ref, wo_ref, g_ref, b_ref,
                     wr_ref, br_ref, h2_ref, idx_ref, gate_ref):
    tm = x_ref.shape[0]
    y_gla = jnp.dot(gla_ref[...], wg_ref[...], preferred_element_type=F32)
    y_fox = jnp.dot(fox_ref[...], wf_ref[...], preferred_element_type=F32)
    mixed = (jax.nn.sigmoid(zg_ref[...].astype(F32)) * y_gla
             + jax.nn.sigmoid(zf_ref[...].astype(F32)) * y_fox)
    h = _layer_norm(x_ref[...], gin_ref[...], bin_ref[...])
    pre = DEEPNORM_ALPHA * h + jnp.dot(mixed.astype(BF16), wo_ref[...], preferred_element_type=F32)
    h2 = _layer_norm(pre, g_ref[...], b_ref[...])
    h2_ref[...] = h2

    logits = lax.dot_general(wr_ref[...], h2.astype(BF16), (((1,), (1,)), ((), ())),
                             preferred_element_type=F32)
    logits = logits + jnp.tile(br_ref[...], (1, tm // LANES))
    eidx = lax.broadcasted_iota(I32, (N_EXPERTS, tm), 0)
    vals, idxs = [], []
    for _ in range(TOP_K):
        mx = jnp.max(logits, axis=0, keepdims=True)
        ik = jnp.min(jnp.where(logits == mx, eidx, N_EXPERTS), axis=0, keepdims=True)
        vals.append(mx)
        idxs.append(ik)
        logits = jnp.where(eidx == ik, -jnp.inf, logits)
    exps = [jnp.exp(v - vals[0]) for v in vals]
    denom = exps[0] + exps[1] + exps[2] + exps[3]
    rid = lax.broadcasted_iota(I32, (SUBLANES, tm), 0)
    idx8 = jnp.zeros((SUBLANES, tm), I32)
    gate8 = jnp.zeros((SUBLANES, tm), F32)
    for k in range(TOP_K):
        idx8 = jnp.where(rid == k, idxs[k], idx8)
        gate8 = jnp.where(rid == k, exps[k] / denom, gate8)
    idx_ref[...] = idx8
    gate_ref[...] = gate8


def _post_mix(gla_o, fox_o, zg, zf, x2, g_in, b_in, wg, wf, wo, g, b, wr_t, br_rep):
    t = x2.shape[0]
    tm = TM_PROJ
    row = lambda w: pl.BlockSpec((tm, w), lambda i: (i, 0))
    const = lambda a: pl.BlockSpec(a.shape, lambda i: (0, 0))
    return pl.pallas_call(
        _post_mix_kernel,
        out_shape=[jax.ShapeDtypeStruct((t, D_MODEL), F32),
                   jax.ShapeDtypeStruct((SUBLANES, t), I32),
                   jax.ShapeDtypeStruct((SUBLANES, t), F32)],
        grid=(t // tm,),
        in_specs=[row(GLA_V), row(FOX_W), row(D_MODEL), row(D_MODEL), row(D_MODEL), const(g_in), const(b_in),
                  const(wg), const(wf), const(wo), const(g), const(b), const(wr_t), const(br_rep)],
        out_specs=[row(D_MODEL),
                   pl.BlockSpec((SUBLANES, tm), lambda i: (0, i)), pl.BlockSpec((SUBLANES, tm), lambda i: (0, i))],
        compiler_params=_cparams(("parallel",), 48),
        name="post_mix",
    )(gla_o, fox_o, zg, zf, x2, g_in, b_in, wg, wf, wo, g, b, wr_t, br_rep)


def _onehot_rows(idx8, tw):
    eidx = lax.broadcasted_iota(I32, (N_EXPERTS, tw), 0)
    hit = eidx == idx8[0:1, :]
    for k in range(1, TOP_K):
        hit = hit | (eidx == idx8[k:k + 1, :])
    return eidx, hit


def _rank_kernel(idx_ref, rank_ref, counts_ref, carry_ref):
    tw = idx_ref.shape[1]

    @pl.when(pl.program_id(0) == 0)
    def _():
        carry_ref[...] = jnp.zeros_like(carry_ref)

    idx8 = idx_ref[...]
    eidx, hit = _onehot_rows(idx8, tw)
    onehot = jnp.where(hit, 1.0, 0.0).astype(BF16)
    ri = lax.broadcasted_iota(I32, (tw, tw + LANES), 0)
    ci = lax.broadcasted_iota(I32, (tw, tw + LANES), 1)
    upper = jnp.where((ri < ci) | (ci >= tw), 1.0, 0.0).astype(BF16)
    cnt = jnp.dot(onehot, upper, preferred_element_type=F32)
    before = cnt[:, :tw] + jnp.tile(carry_ref[...], (1, tw // LANES))
    rid = lax.broadcasted_iota(I32, (SUBLANES, tw), 0)
    rank8 = jnp.zeros((SUBLANES, tw), I32)
    for k in range(TOP_K):
        rk = jnp.sum(jnp.where(eidx == idx8[k:k + 1, :], before, 0.0), axis=0, keepdims=True)
        rank8 = jnp.where(rid == k, rk.astype(I32), rank8)
    rank_ref[...] = rank8
    carry_ref[...] = carry_ref[...] + cnt[:, tw:]
    counts_ref[...] = carry_ref[...]


def _rank(idx_t):
    t = idx_t.shape[1]
    tw = RANK_TW
    return pl.pallas_call(
        _rank_kernel,
        out_shape=[jax.ShapeDtypeStruct((SUBLANES, t), I32),
                   jax.ShapeDtypeStruct((N_EXPERTS, LANES), F32)],
        grid=(t // tw,),
        in_specs=[pl.BlockSpec((SUBLANES, tw), lambda i: (0, i))],
        out_specs=[pl.BlockSpec((SUBLANES, tw), lambda i: (0, i)),
                   pl.BlockSpec((N_EXPERTS, LANES), lambda i: (0, 0))],
        scratch_shapes=[pltpu.VMEM((N_EXPERTS, LANES), F32)],
        compiler_params=_cparams(("arbitrary",), 32),
        name="rank",
    )(idx_t)


def _dest_kernel(idx_ref, rank_ref, counts_ref, dest_ref, blk_e_ref, ends_ref, *, nb_pad):
    tw = idx_ref.shape[1]
    nblk = jnp.floor((counts_ref[...] + (MOE_BLK - 1)) * (1.0 / MOE_BLK))
    ri = lax.broadcasted_iota(I32, (N_EXPERTS, N_EXPERTS), 0)
    ci = lax.broadcasted_iota(I32, (N_EXPERTS, N_EXPERTS), 1)
    tri = jnp.where(ci <= ri, 1.0, 0.0).astype(BF16)
    end_blk = jnp.dot(tri, nblk.astype(BF16), preferred_element_type=F32)
    start_row = (end_blk - nblk) * float(MOE_BLK)
    idx8 = idx_ref[...]
    eidx = lax.broadcasted_iota(I32, (N_EXPERTS, tw), 0)
    start_t = jnp.tile(start_row, (1, tw // LANES))
    rid = lax.broadcasted_iota(I32, (SUBLANES, tw), 0)
    dest8 = jnp.zeros((SUBLANES, tw), I32)
    for k in range(TOP_K):
        st = jnp.sum(jnp.where(eidx == idx8[k:k + 1, :], start_t, 0.0), axis=0, keepdims=True)
        dest8 = jnp.where(rid == k, st.astype(I32), dest8)
    dest_ref[...] = dest8 + rank_ref[...]
    bid = lax.broadcasted_iota(I32, (N_EXPERTS, nb_pad), 1).astype(F32)
    ends_t = jnp.tile(end_blk, (1, nb_pad // LANES))
    be = jnp.sum(jnp.where(ends_t <= bid, 1.0, 0.0), axis=0, keepdims=True)
    blk_e_ref[...] = jnp.broadcast_to(jnp.minimum(be, N_EXPERTS - 1.0), (SUBLANES, nb_pad)).astype(I32)
    ends_ref[...] = end_blk.astype(I32)


def _dest(idx_t, rank_t, counts, nb_pad):
    t = idx_t.shape[1]
    tw = RANK_TW
    tok = pl.BlockSpec((SUBLANES, tw), lambda i: (0, i))
    return pl.pallas_call(
        functools.partial(_dest_kernel, nb_pad=nb_pad),
        out_shape=[jax.ShapeDtypeStruct((SUBLANES, t), I32),
                   jax.ShapeDtypeStruct((SUBLANES, nb_pad), I32),
                   jax.ShapeDtypeStruct((N_EXPERTS, LANES), I32)],
        grid=(t // tw,),
        in_specs=[tok, tok, pl.BlockSpec((N_EXPERTS, LANES), lambda i: (0, 0))],
        out_specs=[tok, pl.BlockSpec((SUBLANES, nb_pad), lambda i: (0, 0)),
                   pl.BlockSpec((N_EXPERTS, LANES), lambda i: (0, 0))],
        compiler_params=_cparams(("arbitrary",), 32),
        name="dest",
    )(idx_t, rank_t, counts)


def _dispatch_kernel(ends_ref, dest_ref, h2_ref, xs_ref, slab_ref, zero_ref, sem, zsem):
    td = h2_ref.shape[0]

    nb = xs_ref.shape[0] // MOE_BLK
    n_used = ends_ref[N_EXPERTS - 1]

    def zero_block(blk):
        start = pl.multiple_of(blk * MOE_BLK, MOE_BLK)
        return pltpu.make_async_copy(zero_ref, xs_ref.at[pl.ds(start, MOE_BLK)], zsem)

    def has_rows(e):
        prev = jnp.where(e == 0, 0, ends_ref[jnp.maximum(e - 1, 0)])
        return ends_ref[e] > prev

    @pl.when(pl.program_id(0) == 0)
    def _():
        zero_ref[...] = jnp.zeros_like(zero_ref)

        def each(fn):
            def last_block(e, _):
                @pl.when(has_rows(e))
                def _():
                    fn(zero_block(ends_ref[e] - 1))
                return 0

            def tail_block(blk, _):
                fn(zero_block(blk))
                return 0

            lax.fori_loop(0, N_EXPERTS, last_block, 0)
            lax.fori_loop(n_used, nb, tail_block, 0)

        each(lambda cp: cp.start())
        each(lambda cp: cp.wait())

    for c in range(SLAB_ROWS):
        slab_ref[:, c, :] = h2_ref[:, c * LANES:(c + 1) * LANES]

    def start_rows(t, _):
        for k in range(TOP_K):
            pltpu.make_async_copy(slab_ref.at[t], xs_ref.at[dest_ref[k, t]], sem).start(priority=k % 2)
        return 0

    lax.fori_loop(0, td, start_rows, 0, unroll=ROW_DMA_UNROLL)
    for k in range(TOP_K):
        pltpu.make_async_copy(slab_ref, xs_ref.at[pl.ds(0, td)], sem).wait()


def _dispatch(ends, dest_t, h2, nb):
    t = h2.shape[0]
    td = DISPATCH_TOK
    grid_spec = pltpu.PrefetchScalarGridSpec(
        num_scalar_prefetch=1,
        grid=(t // td,),
        in_specs=[pl.BlockSpec((SUBLANES, td), lambda i, ends: (0, i), memory_space=pltpu.SMEM),
                  pl.BlockSpec((td, D_MODEL), lambda i, ends: (i, 0))],
        out_specs=pl.BlockSpec(memory_space=pl.ANY),
        scratch_shapes=[pltpu.VMEM((td, SLAB_ROWS, LANES), F32), pltpu.VMEM((MOE_BLK, SLAB_ROWS, LANES), F32),
                        pltpu.SemaphoreType.DMA(()), pltpu.SemaphoreType.DMA(())],
    )
    return pl.pallas_call(
        _dispatch_kernel,
        out_shape=jax.ShapeDtypeStruct((nb * MOE_BLK, SLAB_ROWS, LANES), F32),
        grid_spec=grid_spec,
        compiler_params=_cparams(("arbitrary",), 32),
        name="dispatch",
    )(ends, dest_t, h2)


def _experts_kernel(blk_e_ref, nused_ref, ends_ref, xs_hbm, wgu_hbm, bgu_ref, wd_hbm, bd_ref, ys_hbm,
                    xbuf, ybuf, wgu_f32, wd_f32, wgu_bf, wd_bf, slot_ref, sems, xsems, ysems):
    nb = xs_hbm.shape[0] // MOE_BLK

    def block(j, _):
        _experts_block(j, nb, blk_e_ref, nused_ref, ends_ref, xs_hbm, wgu_hbm, bgu_ref, wd_hbm, bd_ref, ys_hbm,
                       xbuf, ybuf, wgu_f32, wd_f32, wgu_bf, wd_bf, slot_ref, sems, xsems, ysems)
        return 0

    lax.fori_loop(0, nb, block, 0)


def _experts_block(j, nb, blk_e_ref, nused_ref, ends_ref, xs_hbm, wgu_hbm, bgu_ref, wd_hbm, bd_ref, ys_hbm,
                   xbuf, ybuf, wgu_f32, wd_f32, wgu_bf, wd_bf, slot_ref, sems, xsems, ysems):
    par = lax.rem(j, 2)
    n_used = nused_ref[0]
    active = j < n_used
    e = blk_e_ref[j]
    prev = blk_e_ref[jnp.maximum(j - 1, 0)]
    fresh = (j == 0) | (e != prev)

    def fetch(expert, slot):
        return (pltpu.make_async_copy(wgu_hbm.at[expert], wgu_f32.at[slot], sems.at[slot, 0]),
                pltpu.make_async_copy(wd_hbm.at[expert], wd_f32.at[slot], sems.at[slot, 1]))

    def x_load(blk, half):
        rows = pl.ds(pl.multiple_of(blk * MOE_BLK, MOE_BLK), MOE_BLK)
        return [pltpu.make_async_copy(xs_hbm.at[rows, c, :], xbuf.at[half, :, pl.ds(c * LANES, LANES)], xsems.at[half])
                for c in range(SLAB_ROWS)]

    def y_store(blk, half):
        rows = pl.ds(pl.multiple_of(blk * MOE_BLK, MOE_BLK), MOE_BLK)
        return [pltpu.make_async_copy(ybuf.at[half, :, pl.ds(c * LANES, LANES)], ys_hbm.at[rows, c, :], ysems.at[half])
                for c in range(SLAB_ROWS)]

    @pl.when(j == 0)
    def _():
        for cp in x_load(0, 0):
            cp.start()

    @pl.when(j + 1 < n_used)
    def _():
        for cp in x_load(j + 1, 1 - par):
            cp.start()

    @pl.when(j >= 2)
    def _():
        for cp in y_store(j - 2, par):
            cp.wait()

    @pl.when(j == 0)
    def _():
        slot_ref[0] = 0
        for cp in fetch(e, 0):
            cp.start()

    @pl.when(active & fresh)
    def _():
        slot = slot_ref[0]
        for cp in fetch(e, slot):
            cp.wait()
        step = 256
        for r in range(0, D_MODEL, step):
            wgu_bf[r:r + step, :] = wgu_f32[slot, r:r + step, :].astype(BF16)
        for r in range(0, D_FF, step):
            wd_bf[r:r + step, :] = wd_f32[slot, r:r + step, :].astype(BF16)
        next_blk = ends_ref[e]

        @pl.when(next_blk < n_used)
        def _():
            for cp in fetch(blk_e_ref[jnp.minimum(next_blk, n_used - 1)], 1 - slot):
                cp.start()

        slot_ref[0] = 1 - slot

    @pl.when(active)
    def _():
        for cp in x_load(j, par):
            cp.wait()
        x = xbuf[par].astype(BF16)
        gu = jnp.dot(x, wgu_bf[...], preferred_element_type=F32) + bgu_ref[e]
        gate = jnp.minimum(gu[:, :D_FF], SWIGLU_LIMIT)
        up = jnp.clip(gu[:, D_FF:], -SWIGLU_LIMIT, SWIGLU_LIMIT)
        glu = gate * jax.nn.sigmoid(SWIGLU_ALPHA * gate)
        act = ((up + 1.0) * glu).astype(BF16)
        ybuf[par] = jnp.dot(act, wd_bf[...], preferred_element_type=F32) + bd_ref[e]

    @pl.when(jnp.logical_not(active))
    def _():
        ybuf[par] = jnp.zeros((MOE_BLK, D_MODEL), F32)

    for cp in y_store(j, par):
        cp.start()

    @pl.when(j == nb - 1)
    def _():
        for cp in y_store(j, par) + y_store(j - 1, 1 - par):
            cp.wait()


def _experts(blk_e, nused, ends, xs, w_gate_up, b_gate_up, w_down, b_down, nb):
    assert nb >= 2
    whole = lambda a: pl.BlockSpec(a.shape, lambda i, be, nu, en: (0,) * a.ndim)
    grid_spec = pltpu.PrefetchScalarGridSpec(
        num_scalar_prefetch=3,
        grid=(1,),
        in_specs=[pl.BlockSpec(memory_space=pl.ANY),
                  pl.BlockSpec(memory_space=pl.ANY),
                  whole(b_gate_up),
                  pl.BlockSpec(memory_space=pl.ANY),
                  whole(b_down)],
        out_specs=pl.BlockSpec(memory_space=pl.ANY),
        scratch_shapes=[pltpu.VMEM((2, MOE_BLK, D_MODEL), F32), pltpu.VMEM((2, MOE_BLK, D_MODEL), F32),
                        pltpu.VMEM((2, D_MODEL, 2 * D_FF), F32), pltpu.VMEM((2, D_FF, D_MODEL), F32),
                        pltpu.VMEM((D_MODEL, 2 * D_FF), BF16), pltpu.VMEM((D_FF, D_MODEL), BF16),
                        pltpu.SMEM((1,), I32), pltpu.SemaphoreType.DMA((2, 2)),
                        pltpu.SemaphoreType.DMA((2,)), pltpu.SemaphoreType.DMA((2,))],
    )
    return pl.pallas_call(
        _experts_kernel,
        out_shape=jax.ShapeDtypeStruct((nb * MOE_BLK, SLAB_ROWS, LANES), F32),
        grid_spec=grid_spec,
        compiler_params=_cparams(("arbitrary",), 56),
        name="experts",
    )(blk_e, nused, ends, xs, w_gate_up, b_gate_up, w_down, b_down)


def _combine_kernel(dest_ref, dest_next_ref, gate_ref, h2_ref, g_ref, b_ref, ys_ref, o_ref, buf_ref, ffn_ref, sems):
    tc = h2_ref.shape[0]
    i = pl.program_id(0)
    slot = lax.rem(i, 2)

    def gather(d_ref, s):
        def start_rows(t, _):
            for k in range(TOP_K):
                pltpu.make_async_copy(ys_ref.at[d_ref[k, t]], buf_ref.at[s, k, t], sems.at[s]).start(priority=k % 2)
            return 0

        lax.fori_loop(0, tc, start_rows, 0, unroll=ROW_DMA_UNROLL)

    @pl.when(i == 0)
    def _():
        gather(dest_ref, 0)

    @pl.when(i + 1 < pl.num_programs(0))
    def _():
        gather(dest_next_ref, 1 - slot)

    for k in range(TOP_K):
        pltpu.make_async_copy(ys_ref.at[pl.ds(0, tc)], buf_ref.at[slot, k], sems.at[slot]).wait()

    def weigh(t, _):
        acc = gate_ref[0, t] * buf_ref[slot, 0, t]
        for k in range(1, TOP_K):
            acc = acc + gate_ref[k, t] * buf_ref[slot, k, t]
        ffn_ref[t] = acc
        return 0

    lax.fori_loop(0, tc, weigh, 0, unroll=ROW_DMA_UNROLL)
    ffn = jnp.concatenate([ffn_ref[:, c, :] for c in range(SLAB_ROWS)], axis=1)
    o_ref[...] = _layer_norm(DEEPNORM_ALPHA * h2_ref[...] + ffn, g_ref[...], b_ref[...])


def _combine(dest_t, gate_t, h2, g, b, ys):
    t = h2.shape[0]
    tc = COMBINE_TOK
    last = t // tc - 1
    return pl.pallas_call(
        _combine_kernel,
        out_shape=jax.ShapeDtypeStruct((t, D_MODEL), F32),
        grid=(t // tc,),
        in_specs=[pl.BlockSpec((SUBLANES, tc), lambda i: (0, i), memory_space=pltpu.SMEM),
                  pl.BlockSpec((SUBLANES, tc), lambda i: (0, jnp.minimum(i + 1, last)), memory_space=pltpu.SMEM),
                  pl.BlockSpec((SUBLANES, tc), lambda i: (0, i), memory_space=pltpu.SMEM),
                  pl.BlockSpec((tc, D_MODEL), lambda i: (i, 0)),
                  pl.BlockSpec((1, D_MODEL), lambda i: (0, 0)),
                  pl.BlockSpec((1, D_MODEL), lambda i: (0, 0)),
                  pl.BlockSpec(memory_space=pl.ANY)],
        out_specs=pl.BlockSpec((tc, D_MODEL), lambda i: (i, 0)),
        scratch_shapes=[pltpu.VMEM((2, TOP_K, tc, SLAB_ROWS, LANES), F32), pltpu.VMEM((tc, SLAB_ROWS, LANES), F32),
                        pltpu.SemaphoreType.DMA((2,))],
        compiler_params=_cparams(("arbitrary",), 32),
        name="combine",
    )(dest_t, dest_t, gate_t, h2, g, b, ys)


def kernel(x, ln_in_g, ln_in_b, w_in, w_gla_gate_up, b_gla_gate, g_gla_norm, b_forget, w_gla_proj, w_fox_proj, w_out, ln_mix_g, ln_mix_b, w_router, b_router, w_gate_up, b_gate_up, w_down, b_down, ln_ffn_g, ln_ffn_b):
    batch, seq, d = x.shape
    assert d == D_MODEL and w_in.shape[0] == DEPTH == 1
    assert seq % max(GLA_ROWS, FOX_TQ, FOX_TK, CUM_ROWS) == 0
    t = batch * seq
    assert t % max(TM_PROJ, RANK_TW, DISPATCH_TOK, COMBINE_TOK) == 0 and t // MOE_BLK <= 256
    row = lambda v: v.reshape(1, -1).astype(F32)

    n_forget = FOX_HEADS * FORGET_PIECES
    lane = jnp.arange(LANES)
    w_wide, w_small = _regroup_w_in(jnp.transpose(w_in[0]))
    in_forget = (lane >= FORGET_COL0) & (lane < FORGET_COL0 + n_forget)
    piece = (lane - FORGET_COL0) % FORGET_PIECES
    forget_bias = jnp.zeros((1, LANES), F32).at[0, FORGET_COL0:FORGET_COL0 + n_forget].set(
        jnp.repeat(b_forget[0].astype(F32), FORGET_PIECES))
    piece_sel = jnp.stack([(in_forget & (piece == p)).astype(F32) for p in range(FORGET_PIECES)]
                          + [jnp.zeros((LANES,), F32)] * (SUBLANES - FORGET_PIECES))
    head_of_lane = (lane - FORGET_COL0) // FORGET_PIECES
    qsel = jnp.broadcast_to(jnp.stack([(in_forget & (head_of_lane == h)) for h in range(FOX_HEADS)])[:, :, None],
                            (FOX_HEADS, LANES, LANES)).astype(BF16)
    wgu_pad = jnp.concatenate([w_gla_gate_up[0], jnp.zeros((LANES - GLA_RANK, GLA_QK), F32)], axis=0).astype(BF16)

    x2 = x.reshape(t, d)
    gq, gk, gv, gr, fq, fk, fv, zg, zf, small = _ln_inproj(x2, row(ln_in_g), row(ln_in_b), w_wide, w_small)
    kf = _forget_cum(small, forget_bias, piece_sel, batch, seq)
    gla_o = _gla(gq, gk, gv, gr, small, wgu_pad, row(b_gla_gate[0]), row(g_gla_norm[0]), batch, seq)
    fox_o = _fox(fq, fk, fv, kf, qsel, batch, seq)

    br_rep = jnp.broadcast_to(b_router[0].astype(F32)[:, None], (N_EXPERTS, LANES))
    h2, idx_t, gate_t = _post_mix(
        gla_o, fox_o, zg, zf, x2, row(ln_in_g), row(ln_in_b),
        w_gla_proj[0].astype(BF16), w_fox_proj[0].astype(BF16), w_out[0].astype(BF16),
        row(ln_mix_g[0]), row(ln_mix_b[0]), w_router[0].T.astype(BF16), br_rep)

    nb = (t * TOP_K + N_EXPERTS * (MOE_BLK - 1) + MOE_BLK - 1) // MOE_BLK
    nb_pad = (nb + LANES - 1) // LANES * LANES
    rank_t, counts = _rank(idx_t)
    dest_t, blk_e8, ends = _dest(idx_t, rank_t, counts, nb_pad)
    ends1 = ends[:, 0]
    xs = _dispatch(ends1, dest_t, h2, nb)
    ys = _experts(blk_e8[0, :nb], ends1[N_EXPERTS - 1:], ends1, xs, w_gate_up[0], b_gate_up[0][:, None, :],
                  w_down[0], b_down[0][:, None, :], nb)
    out = _combine(dest_t, gate_t, h2, row(ln_ffn_g[0]), row(ln_ffn_b[0]), ys)
    return out.reshape(batch, seq, d)
```

```python
import functools

import jax
import jax.numpy as jnp
from jax import lax
from jax.experimental import pallas as pl
from jax.experimental.pallas import tpu as pltpu

F32 = jnp.float32
BF16 = jnp.bfloat16
I32 = jnp.int32

D_MODEL = 1024
CHUNK = 64
GLA_HEADS, GLA_DK, GLA_DV, GLA_RANK, GLA_TAU = 4, 128, 256, 16, 16.0
GLA_QK = GLA_HEADS * GLA_DK
GLA_V = GLA_HEADS * GLA_DV
FOX_HEADS, FOX_HD = 8, 128
FOX_W = FOX_HEADS * FOX_HD
N_EXPERTS, TOP_K, D_FF = 32, 4, 1024
SWIGLU_LIMIT, SWIGLU_ALPHA = 7.0, 1.702
LN_EPS = 1e-5
DEPTH = 1
DEEPNORM_ALPHA = (2 * DEPTH) ** 0.25
IN_SPLITS = (GLA_QK, GLA_QK, GLA_V, GLA_RANK, GLA_V, FOX_W, FOX_W, FOX_W, FOX_HEADS, D_MODEL, D_MODEL)

LANES = 128
SUBLANES = 8
VMEM_BYTES_V7X = 64 * 1024 * 1024
SLAB_ROWS = D_MODEL // LANES

FORGET_COL0 = GLA_RANK
FORGET_PIECES = 3

TM_PROJ = 512
GLA_ROWS = 512
FOX_TQ = 512
FOX_TK = 512
FOX_HEADS_PER_STEP = 2
LOG2E = 1.4426950408889634
CUM_ROWS = 256
RANK_TW = 512
MOE_BLK = 512
EXPERT_ROW_GROUP = 128
DISPATCH_TOK = 256
COMBINE_TOK = 256
ROW_DMA_UNROLL = 8


def _cparams(sem, vmem_mib, flags=None):
    return pltpu.CompilerParams(dimension_semantics=sem, vmem_limit_bytes=vmem_mib * 1024 * 1024, flags=flags)


def _log_sigmoid(z):
    return jnp.minimum(z, 0.0) - jnp.log1p(jnp.exp(-jnp.abs(z)))


def _layer_norm(x, g, b):
    mu = jnp.mean(x, axis=-1, keepdims=True)
    xc = x - mu
    var = jnp.mean(xc * xc, axis=-1, keepdims=True)
    return xc * lax.rsqrt(var + LN_EPS) * g + b


def _split_bf16(x, pieces):
    out = []
    for _ in range(pieces):
        p = x.astype(BF16)
        out.append(p)
        x = x - p.astype(F32)
    return out


_NARROW_A = sum(IN_SPLITS[:3])
_NARROW_B = sum(IN_SPLITS[:8])


_REGROUP_COLS = 512


def _regroup_kernel(wt_hbm, wide_ref, small_ref, buf, nbuf, sems, nsem):
    j = pl.program_id(0)
    cols = _REGROUP_COLS
    half = lax.rem(j, 2)

    def load(step, s):
        c0 = step * cols
        src = c0 + jnp.where(c0 >= _NARROW_A, GLA_RANK, 0) + jnp.where(c0 >= _NARROW_B - GLA_RANK, FOX_HEADS, 0)
        return pltpu.make_async_copy(wt_hbm.at[pl.ds(pl.multiple_of(src, SUBLANES), cols), :], buf.at[s], sems.at[s])

    @pl.when(j == 0)
    def _():
        load(0, 0).start()
        nbuf[...] = jnp.zeros_like(nbuf)
        copies = [pltpu.make_async_copy(wt_hbm.at[pl.ds(_NARROW_A, GLA_RANK), :], nbuf.at[pl.ds(0, GLA_RANK), :], nsem)]
        for h in range(FOX_HEADS):
            for p in range(FORGET_PIECES):
                copies.append(pltpu.make_async_copy(
                    wt_hbm.at[pl.ds(_NARROW_B + h, 1), :],
                    nbuf.at[pl.ds(FORGET_COL0 + FORGET_PIECES * h + p, 1), :], nsem))
        for cp in copies:
            cp.start()
        for cp in copies:
            cp.wait()
        small_ref[...] = nbuf[...].T.astype(small_ref.dtype)

    @pl.when(j + 1 < pl.num_programs(0))
    def _():
        load(j + 1, 1 - half).start()

    load(j, half).wait()
    wide_ref[...] = buf[half].T.astype(wide_ref.dtype)


def _regroup_w_in(wt):
    n_wide = wt.shape[0] - GLA_RANK - FOX_HEADS
    assert _NARROW_A % _REGROUP_COLS == 0 and (_NARROW_B - GLA_RANK) % _REGROUP_COLS == 0
    return pl.pallas_call(
        _regroup_kernel,
        out_shape=[jax.ShapeDtypeStruct((D_MODEL, n_wide), BF16), jax.ShapeDtypeStruct((D_MODEL, LANES), BF16)],
        grid=(n_wide // _REGROUP_COLS,),
        in_specs=[pl.BlockSpec(memory_space=pl.ANY)],
        out_specs=[pl.BlockSpec((D_MODEL, _REGROUP_COLS), lambda j: (0, j)),
                   pl.BlockSpec((D_MODEL, LANES), lambda j: (0, 0))],
        scratch_shapes=[pltpu.VMEM((2, _REGROUP_COLS, D_MODEL), F32), pltpu.VMEM((LANES, D_MODEL), F32),
                        pltpu.SemaphoreType.DMA((2,)), pltpu.SemaphoreType.DMA(())],
        compiler_params=_cparams(("arbitrary",), 32),
        name="regroup_w_in",
    )(wt)


_PROJ_WIDTHS = (GLA_QK, GLA_QK, GLA_V, GLA_V, FOX_W, FOX_W, FOX_W, D_MODEL, D_MODEL)


def _ln_inproj_kernel(x0_ref, xn_ref, g_ref, b_ref, w_ref, ws_ref, *refs):
    wide_refs, small_ref, hb_refs = refs[:-3], refs[-3], refs[-2:]
    i = pl.program_id(0)

    @pl.when(i == 0)
    def _():
        hb_refs[0][...] = _layer_norm(x0_ref[...], g_ref[...], b_ref[...]).astype(BF16)

    def project(cur_ref, nxt_ref):
        hb = cur_ref[...]
        off = 0
        for o_ref in wide_refs:
            n = o_ref.shape[1]
            o_ref[...] = jnp.dot(hb, w_ref[:, off:off + n], preferred_element_type=F32).astype(o_ref.dtype)
            off += n
        small_ref[...] = jnp.dot(hb, ws_ref[...], preferred_element_type=F32)
        nxt_ref[...] = _layer_norm(xn_ref[...], g_ref[...], b_ref[...]).astype(BF16)

    for parity in range(2):
        pl.when(lax.rem(i, 2) == parity)(functools.partial(project, hb_refs[parity], hb_refs[1 - parity]))


def _ln_inproj(x2, g, b, w_wide, w_small):
    t = x2.shape[0]
    tm = TM_PROJ
    last = t // tm - 1
    row = lambda w: pl.BlockSpec((tm, w), lambda i: (i, 0))
    const = lambda a: pl.BlockSpec(a.shape, lambda i: (0, 0), pipeline_mode=pl.Buffered(1))
    out_shape = ([jax.ShapeDtypeStruct((t, w), BF16) for w in _PROJ_WIDTHS]
                 + [jax.ShapeDtypeStruct((t, LANES), F32)])
    out_specs = [row(w) for w in _PROJ_WIDTHS] + [row(LANES)]
    return pl.pallas_call(
        _ln_inproj_kernel,
        out_shape=out_shape,
        grid=(t // tm,),
        in_specs=[pl.BlockSpec((tm, D_MODEL), lambda i: (0, 0), pipeline_mode=pl.Buffered(1)),
                  pl.BlockSpec((tm, D_MODEL), lambda i: (jnp.minimum(i + 1, last), 0)),
                  const(g), const(b), const(w_wide), const(w_small)],
        out_specs=out_specs,
        scratch_shapes=[pltpu.VMEM((tm, D_MODEL), BF16), pltpu.VMEM((tm, D_MODEL), BF16)],
        compiler_params=_cparams(("arbitrary",), 52),
        name="ln_inproj",
    )(x2, x2, g, b, w_wide, w_small)


def _forget_cum_kernel(small_ref, bias_ref, sel_ref, kf_ref):
    s = small_ref.shape[0]
    r = CUM_ROWS
    ri = lax.broadcasted_iota(I32, (r, r), 0)
    ci = lax.broadcasted_iota(I32, (r, r), 1)
    tri = jnp.where(ci <= ri, 1.0, 0.0).astype(BF16)
    sel = sel_ref[...]
    carry = jnp.zeros((1, LANES), F32)
    for blk in range(s // r):
        rows = pl.ds(blk * r, r)
        ls = _log_sigmoid(small_ref[rows, :] + bias_ref[...])
        cum = carry
        for p in _split_bf16(ls, 3):
            cum = cum + jnp.dot(tri, p, preferred_element_type=F32)
        carry = cum[r - 1:r, :]
        neg = cum * (-LOG2E)
        p1, p2, p3 = _split_bf16(neg, FORGET_PIECES)
        kf = (p1.astype(F32) * sel[0:1, :] + p2.astype(F32) * sel[1:2, :] + p3.astype(F32) * sel[2:3, :])
        kf_ref[rows, :] = kf.astype(BF16)


def _forget_cum(small, bias_row, sel, batch, seq):
    return pl.pallas_call(
        _forget_cum_kernel,
        out_shape=jax.ShapeDtypeStruct((batch * seq, LANES), BF16),
        grid=(batch,),
        in_specs=[pl.BlockSpec((seq, LANES), lambda b: (b, 0)),
                  pl.BlockSpec((1, LANES), lambda b: (0, 0)),
                  pl.BlockSpec((SUBLANES, LANES), lambda b: (0, 0))],
        out_specs=pl.BlockSpec((seq, LANES), lambda b: (b, 0)),
        compiler_params=_cparams(("parallel",), 32),
        name="forget_cum",
    )(small, bias_row, sel)


def _gla_kernel(q_ref, k_ref, v_ref, r_ref, small_ref, wgu_ref, bg_ref, gn_ref, o_ref,
                state_ref, ds_ref, sb_ref, of_ref):
    rows = q_ref.shape[0]
    nchunk = rows // CHUNK
    kcol = lambda h: slice(h * GLA_DK, (h + 1) * GLA_DK)
    vcol = lambda h: slice(h * GLA_DV, (h + 1) * GLA_DV)
    crow = lambda c: slice(c * CHUNK, (c + 1) * CHUNK)

    @pl.when(pl.program_id(1) == 0)
    def _():
        state_ref[...] = jnp.zeros_like(state_ref)

    z = jnp.dot(small_ref[...].astype(BF16), wgu_ref[...], preferred_element_type=F32) + bg_ref[...]
    la = _log_sigmoid(z) * (1.0 / GLA_TAU)
    ri = lax.broadcasted_iota(I32, (rows, rows), 0)
    ci = lax.broadcasted_iota(I32, (rows, rows), 1)
    shift = CHUNK.bit_length() - 1
    same = lax.shift_right_logical(ri, shift) == lax.shift_right_logical(ci, shift)
    tri = jnp.where(same & (ci <= ri), 1.0, 0.0).astype(BF16)
    cum = jnp.zeros((rows, GLA_QK), F32)
    for p in _split_bf16(la, 2):
        cum = cum + jnp.dot(tri, p, preferred_element_type=F32)
    last = [cum[(c + 1) * CHUNK - 1:(c + 1) * CHUNK, :] for c in range(nchunk)]
    tot = jnp.concatenate([jnp.broadcast_to(t, (CHUNK, GLA_QK)) for t in last], axis=0)
    kd = (k_ref[...].astype(F32) * jnp.exp(tot - cum)).astype(BF16)
    dec_rows = jnp.exp(jnp.concatenate(last + [jnp.zeros((LANES - nchunk, GLA_QK), F32)], axis=0))
    dec_cols = dec_rows.T

    for c in range(nchunk):
        for h in range(GLA_HEADS):
            ds_ref[c, h] = lax.dot_general(kd[crow(c), kcol(h)], v_ref[crow(c), vcol(h)],
                                           (((0,), (0,)), ((), ())), preferred_element_type=F32)
    for h in range(GLA_HEADS):
        st = state_ref[h]
        for c in range(nchunk):
            st = st * jnp.broadcast_to(dec_cols[kcol(h), c:c + 1], (GLA_DK, GLA_DV)) + ds_ref[c, h]
            sb_ref[c, h] = st.astype(BF16)
        state_ref[h] = st
    scale = float(GLA_DK) ** -0.5
    for c in range(nchunk):
        for h in range(GLA_HEADS):
            of_ref[crow(c), vcol(h)] = jnp.dot(q_ref[crow(c), kcol(h)], sb_ref[c, h],
                                               preferred_element_type=F32) * scale
    for h in range(GLA_HEADS):
        o = of_ref[:, vcol(h)]
        o = o * lax.rsqrt(jnp.mean(o * o, axis=-1, keepdims=True) + LN_EPS) * gn_ref[...]
        r = r_ref[:, vcol(h)].astype(F32)
        o_ref[:, vcol(h)] = (o * (r * jax.nn.sigmoid(r))).astype(o_ref.dtype)


def _gla(gq, gk, gv, gr, small, wgu_pad, b_gate, g_norm, batch, seq):
    rows = GLA_ROWS
    nsteps = seq // rows
    row = lambda w: pl.BlockSpec((rows, w), lambda b, i: (b * nsteps + i, 0))
    const = lambda a: pl.BlockSpec(a.shape, lambda b, i: (0, 0))
    return pl.pallas_call(
        _gla_kernel,
        out_shape=jax.ShapeDtypeStruct((batch * seq, GLA_V), BF16),
        grid=(batch, nsteps),
        in_specs=[row(GLA_QK), row(GLA_QK), row(GLA_V), row(GLA_V), row(LANES),
                  const(wgu_pad), const(b_gate), const(g_norm)],
        out_specs=row(GLA_V),
        scratch_shapes=[pltpu.VMEM((GLA_HEADS, GLA_DK, GLA_DV), F32),
                        pltpu.VMEM((rows // CHUNK, GLA_HEADS, GLA_DK, GLA_DV), F32),
                        pltpu.VMEM((rows // CHUNK, GLA_HEADS, GLA_DK, GLA_DV), BF16),
                        pltpu.VMEM((rows, GLA_V), F32)],
        compiler_params=_cparams(("parallel", "arbitrary"), 40),
        name="gla",
    )(gq, gk, gv, gr, small, wgu_pad, b_gate, g_norm)


def _fox_kernel(q_ref, k_ref, v_ref, kf_ref, qsel_ref, o_ref, qa_ref, vt_ref, s_ref, acc_ref, m_ref, l_ref):
    t = FOX_TQ
    seq = q_ref.shape[0]
    nq = seq // t
    c = (float(FOX_HD) ** -0.5) * LOG2E
    hd = FOX_HD
    heads = range(FOX_HEADS_PER_STEP)
    for h in heads:
        for r in range(0, seq, t):
            qa_ref[h, 0:hd, r:r + t] = (q_ref[r:r + t, h * hd:(h + 1) * hd].astype(F32) * c).T.astype(BF16)
            vt_ref[h, :, r:r + t] = v_ref[r:r + t, h * hd:(h + 1) * hd].astype(F32).T.astype(BF16)
        qa_ref[h, hd:hd + LANES, :] = jnp.tile(qsel_ref[h], (1, seq // LANES))

    def key_rows(j):
        return pl.ds(pl.multiple_of(j * t, t), t)

    def scores(h, i, j):
        ka = jnp.concatenate([k_ref[key_rows(j), h * hd:(h + 1) * hd], kf_ref[key_rows(j), :]], axis=1)
        return jnp.dot(ka, qa_ref[h, :, i * t:(i + 1) * t], preferred_element_type=F32)

    def values(h, j, p):
        return jnp.dot(vt_ref[h, :, key_rows(j)], p, preferred_element_type=F32)

    def step(j, slot, masked, nxt):
        if nxt is not None:
            for h in heads:
                s_ref[1 - slot, h] = scores(h, *nxt)
        for h in heads:
            s = s_ref[slot, h]
            if masked:
                causal = lax.broadcasted_iota(I32, (t, t), 0) <= lax.broadcasted_iota(I32, (t, t), 1)
                s = jnp.where(causal, s, -jnp.inf)
            m_old = m_ref[h]
            m_new = jnp.maximum(m_old, jnp.broadcast_to(jnp.max(s, axis=0, keepdims=True), (SUBLANES, t)))
            alpha = jnp.exp2(m_old - m_new)
            p = jnp.exp2(s - m_new[0:1, :])
            m_ref[h] = m_new
            l_ref[h] = alpha * l_ref[h] + jnp.sum(p, axis=0, keepdims=True)
            acc_ref[h] = alpha[0:1, :] * acc_ref[h] + values(h, j, p.astype(BF16))

    for h in heads:
        s_ref[0, h] = scores(h, 0, 0)
    slot = 0
    for i in range(nq):
        for h in heads:
            acc_ref[h] = jnp.zeros((hd, t), F32)
            m_ref[h] = jnp.full((SUBLANES, t), -jnp.inf, F32)
            l_ref[h] = jnp.zeros((SUBLANES, t), F32)
        if i >= 2:
            def pair(jj, _, i=i, slot=slot):
                step(2 * jj, slot, False, (i, 2 * jj + 1))
                step(2 * jj + 1, 1 - slot, False, (i, 2 * jj + 2))
                return 0

            lax.fori_loop(0, i // 2, pair, 0)
        if i % 2 == 1:
            step(i - 1, slot, False, (i, i))
            slot = 1 - slot
        step(i, slot, True, (i + 1, 0) if i + 1 < nq else None)
        slot = 1 - slot
        for h in heads:
            o_ref[i * t:(i + 1) * t, h * hd:(h + 1) * hd] = (acc_ref[h] / l_ref[h][0:1, :]).T.astype(o_ref.dtype)


def _fox(fq, fk, fv, kf, qsel, batch, seq):
    hps = FOX_HEADS_PER_STEP
    w = hps * FOX_HD
    t = FOX_TQ
    head_cols = pl.BlockSpec((seq, w), lambda b, h: (b, h))
    return pl.pallas_call(
        _fox_kernel,
        out_shape=jax.ShapeDtypeStruct((batch * seq, FOX_W), BF16),
        grid=(batch, FOX_HEADS // hps),
        in_specs=[head_cols, head_cols, head_cols,
                  pl.BlockSpec((seq, LANES), lambda b, h: (b, 0)),
                  pl.BlockSpec((hps, LANES, LANES), lambda b, h: (h, 0, 0))],
        out_specs=head_cols,
        scratch_shapes=[pltpu.VMEM((hps, FOX_HD + LANES, seq), BF16), pltpu.VMEM((hps, FOX_HD, seq), BF16),
                        pltpu.VMEM((2, hps, t, t), F32), pltpu.VMEM((hps, FOX_HD, t), F32),
                        pltpu.VMEM((hps, SUBLANES, t), F32), pltpu.VMEM((hps, SUBLANES, t), F32)],
        compiler_params=_cparams(("parallel", "parallel"), 48),
        name="fox",
    )(fq, fk, fv, kf, qsel)


def _post_mix_kernel(gla_ref, fox_ref, zg_ref, zf_ref, x_ref, gin_ref, bin_ref, wg_ref, wf_ref, wo_ref, g_ref, b_ref,
                     wr_ref, br_ref, h2_ref, idx_ref, gate_ref):
    tm = x_ref.shape[0]
    y_gla = jnp.dot(gla_ref[...], wg_ref[...], preferred_element_type=F32)
    y_fox = jnp.dot(fox_ref[...], wf_ref[...], preferred_element_type=F32)
    mixed = (jax.nn.sigmoid(zg_ref[...].astype(F32)) * y_gla
             + jax.nn.sigmoid(zf_ref[...].astype(F32)) * y_fox)
    h = _layer_norm(x_ref[...], gin_ref[...], bin_ref[...])
    pre = DEEPNORM_ALPHA * h + jnp.dot(mixed.astype(BF16), wo_ref[...], preferred_element_type=F32)
    h2 = _layer_norm(pre, g_ref[...], b_ref[...])
    h2_ref[...] = h2

    logits = lax.dot_general(wr_ref[...], h2.astype(BF16), (((1,), (1,)), ((), ())),
                             preferred_element_type=F32)
    logits = logits + jnp.tile(br_ref[...], (1, tm // LANES))
    eidx = lax.broadcasted_iota(I32, (N_EXPERTS, tm), 0)
    vals, idxs = [], []
    for _ in range(TOP_K):
        mx = jnp.max(logits, axis=0, keepdims=True)
        ik = jnp.min(jnp.where(logits == mx, eidx, N_EXPERTS), axis=0, keepdims=True)
        vals.append(mx)
        idxs.append(ik)
        logits = jnp.where(eidx == ik, -jnp.inf, logits)
    exps = [jnp.exp(v - vals[0]) for v in vals]
    denom = exps[0] + exps[1] + exps[2] + exps[3]
    rid = lax.broadcasted_iota(I32, (SUBLANES, tm), 0)
    idx8 = jnp.zeros((SUBLANES, tm), I32)
    gate8 = jnp.zeros((SUBLANES, tm), F32)
    for k in range(TOP_K):
        idx8 = jnp.where(rid == k, idxs[k], idx8)
        gate8 = jnp.where(rid == k, exps[k] / denom, gate8)
    idx_ref[...] = idx8
    gate_ref[...] = gate8


def _post_mix(gla_o, fox_o, zg, zf, x2, g_in, b_in, wg, wf, wo, g, b, wr_t, br_rep):
    t = x2.shape[0]
    tm = TM_PROJ
    row = lambda w: pl.BlockSpec((tm, w), lambda i: (i, 0))
    const = lambda a: pl.BlockSpec(a.shape, lambda i: (0, 0))
    return pl.pallas_call(
        _post_mix_kernel,
        out_shape=[jax.ShapeDtypeStruct((t, D_MODEL), F32),
                   jax.ShapeDtypeStruct((SUBLANES, t), I32),
                   jax.ShapeDtypeStruct((SUBLANES, t), F32)],
        grid=(t // tm,),
        in_specs=[row(GLA_V), row(FOX_W), row(D_MODEL), row(D_MODEL), row(D_MODEL), const(g_in), const(b_in),
                  const(wg), const(wf), const(wo), const(g), const(b), const(wr_t), const(br_rep)],
        out_specs=[row(D_MODEL),
                   pl.BlockSpec((SUBLANES, tm), lambda i: (0, i)), pl.BlockSpec((SUBLANES, tm), lambda i: (0, i))],
        compiler_params=_cparams(("parallel",), 48),
        name="post_mix",
    )(gla_o, fox_o, zg, zf, x2, g_in, b_in, wg, wf, wo, g, b, wr_t, br_rep)


def _onehot_rows(idx8, tw):
    eidx = lax.broadcasted_iota(I32, (N_EXPERTS, tw), 0)
    hit = eidx == idx8[0:1, :]
    for k in range(1, TOP_K):
        hit = hit | (eidx == idx8[k:k + 1, :])
    return eidx, hit


def _rank_kernel(idx_ref, rank_ref, counts_ref, carry_ref):
    tw = idx_ref.shape[1]

    @pl.when(pl.program_id(0) == 0)
    def _():
        carry_ref[...] = jnp.zeros_like(carry_ref)

    idx8 = idx_ref[...]
    eidx, hit = _onehot_rows(idx8, tw)
    onehot = jnp.where(hit, 1.0, 0.0).astype(BF16)
    ri = lax.broadcasted_iota(I32, (tw, tw + LANES), 0)
    ci = lax.broadcasted_iota(I32, (tw, tw + LANES), 1)
    upper = jnp.where((ri < ci) | (ci >= tw), 1.0, 0.0).astype(BF16)
    cnt = jnp.dot(onehot, upper, preferred_element_type=F32)
    before = cnt[:, :tw] + jnp.tile(carry_ref[...], (1, tw // LANES))
    rid = lax.broadcasted_iota(I32, (SUBLANES, tw), 0)
    rank8 = jnp.zeros((SUBLANES, tw), I32)
    for k in range(TOP_K):
        rk = jnp.sum(jnp.where(eidx == idx8[k:k + 1, :], before, 0.0), axis=0, keepdims=True)
        rank8 = jnp.where(rid == k, rk.astype(I32), rank8)
    rank_ref[...] = rank8
    carry_ref[...] = carry_ref[...] + cnt[:, tw:]
    counts_ref[...] = carry_ref[...]


def _rank(idx_t):
    t = idx_t.shape[1]
    tw = RANK_TW
    return pl.pallas_call(
        _rank_kernel,
        out_shape=[jax.ShapeDtypeStruct((SUBLANES, t), I32),
                   jax.ShapeDtypeStruct((N_EXPERTS, LANES), F32)],
        grid=(t // tw,),
        in_specs=[pl.BlockSpec((SUBLANES, tw), lambda i: (0, i))],
        out_specs=[pl.BlockSpec((SUBLANES, tw), lambda i: (0, i)),
                   pl.BlockSpec((N_EXPERTS, LANES), lambda i: (0, 0))],
        scratch_shapes=[pltpu.VMEM((N_EXPERTS, LANES), F32)],
        compiler_params=_cparams(("arbitrary",), 32),
        name="rank",
    )(idx_t)


def _dest_kernel(idx_ref, rank_ref, counts_ref, dest_ref, blk_e_ref, ends_ref, *, nb_pad):
    tw = idx_ref.shape[1]
    nblk = jnp.floor((counts_ref[...] + (MOE_BLK - 1)) * (1.0 / MOE_BLK))
    ri = lax.broadcasted_iota(I32, (N_EXPERTS, N_EXPERTS), 0)
    ci = lax.broadcasted_iota(I32, (N_EXPERTS, N_EXPERTS), 1)
    tri = jnp.where(ci <= ri, 1.0, 0.0).astype(BF16)
    end_blk = jnp.dot(tri, nblk.astype(BF16), preferred_element_type=F32)
    start_row = (end_blk - nblk) * float(MOE_BLK)
    idx8 = idx_ref[...]
    eidx = lax.broadcasted_iota(I32, (N_EXPERTS, tw), 0)
    start_t = jnp.tile(start_row, (1, tw // LANES))
    rid = lax.broadcasted_iota(I32, (SUBLANES, tw), 0)
    dest8 = jnp.zeros((SUBLANES, tw), I32)
    for k in range(TOP_K):
        st = jnp.sum(jnp.where(eidx == idx8[k:k + 1, :], start_t, 0.0), axis=0, keepdims=True)
        dest8 = jnp.where(rid == k, st.astype(I32), dest8)
    dest_ref[...] = dest8 + rank_ref[...]
    bid = lax.broadcasted_iota(I32, (N_EXPERTS, nb_pad), 1).astype(F32)
    ends_t = jnp.tile(end_blk, (1, nb_pad // LANES))
    be = jnp.sum(jnp.where(ends_t <= bid, 1.0, 0.0), axis=0, keepdims=True)
    blk_e_ref[...] = jnp.broadcast_to(jnp.minimum(be, N_EXPERTS - 1.0), (SUBLANES, nb_pad)).astype(I32)
    ends_ref[...] = end_blk.astype(I32)


def _dest(idx_t, rank_t, counts, nb_pad):
    t = idx_t.shape[1]
    tw = RANK_TW
    tok = pl.BlockSpec((SUBLANES, tw), lambda i: (0, i))
    return pl.pallas_call(
        functools.partial(_dest_kernel, nb_pad=nb_pad),
        out_shape=[jax.ShapeDtypeStruct((SUBLANES, t), I32),
                   jax.ShapeDtypeStruct((SUBLANES, nb_pad), I32),
                   jax.ShapeDtypeStruct((N_EXPERTS, LANES), I32)],
        grid=(t // tw,),
        in_specs=[tok, tok, pl.BlockSpec((N_EXPERTS, LANES), lambda i: (0, 0))],
        out_specs=[tok, pl.BlockSpec((SUBLANES, nb_pad), lambda i: (0, 0)),
                   pl.BlockSpec((N_EXPERTS, LANES), lambda i: (0, 0))],
        compiler_params=_cparams(("arbitrary",), 32),
        name="dest",
    )(idx_t, rank_t, counts)


def _dispatch_kernel(ends_ref, dest_ref, h2_ref, xs_ref, slab_ref, zero_ref, sem, zsem):
    td = h2_ref.shape[0]

    nb = xs_ref.shape[0] // MOE_BLK
    n_used = ends_ref[N_EXPERTS - 1]

    def zero_block(blk):
        start = pl.multiple_of(blk * MOE_BLK, MOE_BLK)
        return pltpu.make_async_copy(zero_ref, xs_ref.at[pl.ds(start, MOE_BLK)], zsem)

    def has_rows(e):
        prev = jnp.where(e == 0, 0, ends_ref[jnp.maximum(e - 1, 0)])
        return ends_ref[e] > prev

    @pl.when(pl.program_id(0) == 0)
    def _():
        zero_ref[...] = jnp.zeros_like(zero_ref)

        def each(fn):
            def last_block(e, _):
                @pl.when(has_rows(e))
                def _():
                    fn(zero_block(ends_ref[e] - 1))
                return 0

            def tail_block(blk, _):
                fn(zero_block(blk))
                return 0

            lax.fori_loop(0, N_EXPERTS, last_block, 0)
            lax.fori_loop(n_used, nb, tail_block, 0)

        each(lambda cp: cp.start())
        each(lambda cp: cp.wait())

    for c in range(SLAB_ROWS):
        slab_ref[:, c, :] = h2_ref[:, c * LANES:(c + 1) * LANES]

    def start_rows(t, _):
        for k in range(TOP_K):
            pltpu.make_async_copy(slab_ref.at[t], xs_ref.at[dest_ref[k, t]], sem).start(priority=k % 2)
        return 0

    lax.fori_loop(0, td, start_rows, 0, unroll=ROW_DMA_UNROLL)
    for k in range(TOP_K):
        pltpu.make_async_copy(slab_ref, xs_ref.at[pl.ds(0, td)], sem).wait()


def _dispatch(ends, dest_t, h2, nb):
    t = h2.shape[0]
    td = DISPATCH_TOK
    grid_spec = pltpu.PrefetchScalarGridSpec(
        num_scalar_prefetch=1,
        grid=(t // td,),
        in_specs=[pl.BlockSpec((SUBLANES, td), lambda i, ends: (0, i), memory_space=pltpu.SMEM),
                  pl.BlockSpec((td, D_MODEL), lambda i, ends: (i, 0))],
        out_specs=pl.BlockSpec(memory_space=pl.ANY),
        scratch_shapes=[pltpu.VMEM((td, SLAB_ROWS, LANES), F32), pltpu.VMEM((MOE_BLK, SLAB_ROWS, LANES), F32),
                        pltpu.SemaphoreType.DMA(()), pltpu.SemaphoreType.DMA(())],
    )
    return pl.pallas_call(
        _dispatch_kernel,
        out_shape=jax.ShapeDtypeStruct((nb * MOE_BLK, SLAB_ROWS, LANES), F32),
        grid_spec=grid_spec,
        compiler_params=_cparams(("arbitrary",), 32),
        name="dispatch",
    )(ends, dest_t, h2)


def _experts_kernel(blk_e_ref, nused_ref, ends_ref, cnt_ref, xs_hbm, wgu_hbm, bgu_ref, wd_hbm, bd_ref, ys_hbm,
                    xbuf, ybuf, wgu_f32, wd_f32, wgu_bf, wd_bf, slot_ref, sems, xsems, ysems):
    nb = xs_hbm.shape[0] // MOE_BLK

    def block(j, _):
        _experts_block(j, nb, blk_e_ref, nused_ref, ends_ref, cnt_ref, xs_hbm, wgu_hbm, bgu_ref, wd_hbm, bd_ref, ys_hbm,
                       xbuf, ybuf, wgu_f32, wd_f32, wgu_bf, wd_bf, slot_ref, sems, xsems, ysems)
        return 0

    lax.fori_loop(0, nb, block, 0)


def _experts_block(j, nb, blk_e_ref, nused_ref, ends_ref, cnt_ref, xs_hbm, wgu_hbm, bgu_ref, wd_hbm, bd_ref, ys_hbm,
                   xbuf, ybuf, wgu_f32, wd_f32, wgu_bf, wd_bf, slot_ref, sems, xsems, ysems):
    par = lax.rem(j, 2)
    n_used = nused_ref[0]
    active = j < n_used
    e = blk_e_ref[j]
    prev = blk_e_ref[jnp.maximum(j - 1, 0)]
    fresh = (j == 0) | (e != prev)

    def fetch(expert, slot):
        return (pltpu.make_async_copy(wgu_hbm.at[expert], wgu_f32.at[slot], sems.at[slot, 0]),
                pltpu.make_async_copy(wd_hbm.at[expert], wd_f32.at[slot], sems.at[slot, 1]))

    def x_load(blk, half):
        rows = pl.ds(pl.multiple_of(blk * MOE_BLK, MOE_BLK), MOE_BLK)
        return [pltpu.make_async_copy(xs_hbm.at[rows, c, :], xbuf.at[half, :, pl.ds(c * LANES, LANES)], xsems.at[half])
                for c in range(SLAB_ROWS)]

    def y_store(blk, half):
        rows = pl.ds(pl.multiple_of(blk * MOE_BLK, MOE_BLK), MOE_BLK)
        return [pltpu.make_async_copy(ybuf.at[half, :, pl.ds(c * LANES, LANES)], ys_hbm.at[rows, c, :], ysems.at[half])
                for c in range(SLAB_ROWS)]

    @pl.when(j == 0)
    def _():
        ybuf[...] = jnp.zeros_like(ybuf)
        for cp in x_load(0, 0):
            cp.start()

    @pl.when(j + 1 < n_used)
    def _():
        for cp in x_load(j + 1, 1 - par):
            cp.start()

    @pl.when(j >= 2)
    def _():
        for cp in y_store(j - 2, par):
            cp.wait()

    @pl.when(j == 0)
    def _():
        slot_ref[0] = 0
        for cp in fetch(e, 0):
            cp.start()

    @pl.when(active & fresh)
    def _():
        slot = slot_ref[0]
        for cp in fetch(e, slot):
            cp.wait()
        step = 256
        for r in range(0, D_MODEL, step):
            wgu_bf[r:r + step, :] = wgu_f32[slot, r:r + step, :].astype(BF16)
        for r in range(0, D_FF, step):
            wd_bf[r:r + step, :] = wd_f32[slot, r:r + step, :].astype(BF16)
        next_blk = ends_ref[e]

        @pl.when(next_blk < n_used)
        def _():
            for cp in fetch(blk_e_ref[jnp.minimum(next_blk, n_used - 1)], 1 - slot):
                cp.start()

        slot_ref[0] = 1 - slot

    @pl.when(active)
    def _():
        for cp in x_load(j, par):
            cp.wait()

    first_blk = jnp.where(e == 0, 0, ends_ref[jnp.maximum(e - 1, 0)])
    valid = jnp.clip(cnt_ref[e] - (j - first_blk) * MOE_BLK, 0, MOE_BLK)
    groups = lax.shift_right_logical(valid + (EXPERT_ROW_GROUP - 1), EXPERT_ROW_GROUP.bit_length() - 1)

    def compute(m):
        x = xbuf[par, 0:m].astype(BF16)
        gu = jnp.dot(x, wgu_bf[...], preferred_element_type=F32) + bgu_ref[e]
        gate = jnp.minimum(gu[:, :D_FF], SWIGLU_LIMIT)
        up = jnp.clip(gu[:, D_FF:], -SWIGLU_LIMIT, SWIGLU_LIMIT)
        glu = gate * jax.nn.sigmoid(SWIGLU_ALPHA * gate)
        act = ((up + 1.0) * glu).astype(BF16)
        ybuf[par, 0:m] = jnp.dot(act, wd_bf[...], preferred_element_type=F32) + bd_ref[e]

    for g in range(1, MOE_BLK // EXPERT_ROW_GROUP + 1):
        pl.when(active & (groups == g))(functools.partial(compute, g * EXPERT_ROW_GROUP))

    @pl.when(jnp.logical_not(active))
    def _():
        ybuf[par] = jnp.zeros((MOE_BLK, D_MODEL), F32)

    for cp in y_store(j, par):
        cp.start()

    @pl.when(j == nb - 1)
    def _():
        for cp in y_store(j, par) + y_store(j - 1, 1 - par):
            cp.wait()


def _experts(blk_e, nused, ends, cnt, xs, w_gate_up, b_gate_up, w_down, b_down, nb):
    assert nb >= 2
    whole = lambda a: pl.BlockSpec(a.shape, lambda i, be, nu, en, cn: (0,) * a.ndim)
    grid_spec = pltpu.PrefetchScalarGridSpec(
        num_scalar_prefetch=4,
        grid=(1,),
        in_specs=[pl.BlockSpec(memory_space=pl.ANY),
                  pl.BlockSpec(memory_space=pl.ANY),
                  whole(b_gate_up),
                  pl.BlockSpec(memory_space=pl.ANY),
                  whole(b_down)],
        out_specs=pl.BlockSpec(memory_space=pl.ANY),
        scratch_shapes=[pltpu.VMEM((2, MOE_BLK, D_MODEL), F32), pltpu.VMEM((2, MOE_BLK, D_MODEL), F32),
                        pltpu.VMEM((2, D_MODEL, 2 * D_FF), F32), pltpu.VMEM((2, D_FF, D_MODEL), F32),
                        pltpu.VMEM((D_MODEL, 2 * D_FF), BF16), pltpu.VMEM((D_FF, D_MODEL), BF16),
                        pltpu.SMEM((1,), I32), pltpu.SemaphoreType.DMA((2, 2)),
                        pltpu.SemaphoreType.DMA((2,)), pltpu.SemaphoreType.DMA((2,))],
    )
    return pl.pallas_call(
        _experts_kernel,
        out_shape=jax.ShapeDtypeStruct((nb * MOE_BLK, SLAB_ROWS, LANES), F32),
        grid_spec=grid_spec,
        compiler_params=_cparams(("arbitrary",), 56),
        name="experts",
    )(blk_e, nused, ends, cnt, xs, w_gate_up, b_gate_up, w_down, b_down)


def _combine_kernel(dest_ref, dest_next_ref, gate_ref, h2_ref, g_ref, b_ref, ys_ref, o_ref, buf_ref, ffn_ref, sems):
    tc = h2_ref.shape[0]
    i = pl.program_id(0)
    slot = lax.rem(i, 2)

    def gather(d_ref, s):
        def start_rows(t, _):
            for k in range(TOP_K):
                pltpu.make_async_copy(ys_ref.at[d_ref[k, t]], buf_ref.at[s, k, t], sems.at[s]).start(priority=k % 2)
            return 0

        lax.fori_loop(0, tc, start_rows, 0, unroll=ROW_DMA_UNROLL)

    @pl.when(i == 0)
    def _():
        gather(dest_ref, 0)

    @pl.when(i + 1 < pl.num_programs(0))
    def _():
        gather(dest_next_ref, 1 - slot)

    for k in range(TOP_K):
        pltpu.make_async_copy(ys_ref.at[pl.ds(0, tc)], buf_ref.at[slot, k], sems.at[slot]).wait()

    def weigh(t, _):
        acc = gate_ref[0, t] * buf_ref[slot, 0, t]
        for k in range(1, TOP_K):
            acc = acc + gate_ref[k, t] * buf_ref[slot, k, t]
        ffn_ref[t] = acc
        return 0

    lax.fori_loop(0, tc, weigh, 0, unroll=ROW_DMA_UNROLL)
    ffn = jnp.concatenate([ffn_ref[:, c, :] for c in range(SLAB_ROWS)], axis=1)
    o_ref[...] = _layer_norm(DEEPNORM_ALPHA * h2_ref[...] + ffn, g_ref[...], b_ref[...])


def _combine(dest_t, gate_t, h2, g, b, ys):
    t = h2.shape[0]
    tc = COMBINE_TOK
    last = t // tc - 1
    return pl.pallas_call(
        _combine_kernel,
        out_shape=jax.ShapeDtypeStruct((t, D_MODEL), F32),
        grid=(t // tc,),
        in_specs=[pl.BlockSpec((SUBLANES, tc), lambda i: (0, i), memory_space=pltpu.SMEM),
                  pl.BlockSpec((SUBLANES, tc), lambda i: (0, jnp.minimum(i + 1, last)), memory_space=pltpu.SMEM),
                  pl.BlockSpec((SUBLANES, tc), lambda i: (0, i), memory_space=pltpu.SMEM),
                  pl.BlockSpec((tc, D_MODEL), lambda i: (i, 0)),
                  pl.BlockSpec((1, D_MODEL), lambda i: (0, 0)),
                  pl.BlockSpec((1, D_MODEL), lambda i: (0, 0)),
                  pl.BlockSpec(memory_space=pl.ANY)],
        out_specs=pl.BlockSpec((tc, D_MODEL), lambda i: (i, 0)),
        scratch_shapes=[pltpu.VMEM((2, TOP_K, tc, SLAB_ROWS, LANES), F32), pltpu.VMEM((tc, SLAB_ROWS, LANES), F32),
                        pltpu.SemaphoreType.DMA((2,))],
        compiler_params=_cparams(("arbitrary",), 32),
        name="combine",
    )(dest_t, dest_t, gate_t, h2, g, b, ys)


def kernel(x, ln_in_g, ln_in_b, w_in, w_gla_gate_up, b_gla_gate, g_gla_norm, b_forget, w_gla_proj, w_fox_proj, w_out, ln_mix_g, ln_mix_b, w_router, b_router, w_gate_up, b_gate_up, w_down, b_down, ln_ffn_g, ln_ffn_b):
    batch, seq, d = x.shape
    assert d == D_MODEL and w_in.shape[0] == DEPTH == 1
    assert seq % max(GLA_ROWS, FOX_TQ, FOX_TK, CUM_ROWS) == 0
    t = batch * seq
    assert t % max(TM_PROJ, RANK_TW, DISPATCH_TOK, COMBINE_TOK) == 0 and t // MOE_BLK <= 256
    row = lambda v: v.reshape(1, -1).astype(F32)

    n_forget = FOX_HEADS * FORGET_PIECES
    lane = jnp.arange(LANES)
    w_wide, w_small = _regroup_w_in(jnp.transpose(w_in[0]))
    in_forget = (lane >= FORGET_COL0) & (lane < FORGET_COL0 + n_forget)
    piece = (lane - FORGET_COL0) % FORGET_PIECES
    forget_bias = jnp.zeros((1, LANES), F32).at[0, FORGET_COL0:FORGET_COL0 + n_forget].set(
        jnp.repeat(b_forget[0].astype(F32), FORGET_PIECES))
    piece_sel = jnp.stack([(in_forget & (piece == p)).astype(F32) for p in range(FORGET_PIECES)]
                          + [jnp.zeros((LANES,), F32)] * (SUBLANES - FORGET_PIECES))
    head_of_lane = (lane - FORGET_COL0) // FORGET_PIECES
    qsel = jnp.broadcast_to(jnp.stack([(in_forget & (head_of_lane == h)) for h in range(FOX_HEADS)])[:, :, None],
                            (FOX_HEADS, LANES, LANES)).astype(BF16)
    wgu_pad = jnp.concatenate([w_gla_gate_up[0], jnp.zeros((LANES - GLA_RANK, GLA_QK), F32)], axis=0).astype(BF16)

    x2 = x.reshape(t, d)
    gq, gk, gv, gr, fq, fk, fv, zg, zf, small = _ln_inproj(x2, row(ln_in_g), row(ln_in_b), w_wide, w_small)
    kf = _forget_cum(small, forget_bias, piece_sel, batch, seq)
    gla_o = _gla(gq, gk, gv, gr, small, wgu_pad, row(b_gla_gate[0]), row(g_gla_norm[0]), batch, seq)
    fox_o = _fox(fq, fk, fv, kf, qsel, batch, seq)

    br_rep = jnp.broadcast_to(b_router[0].astype(F32)[:, None], (N_EXPERTS, LANES))
    h2, idx_t, gate_t = _post_mix(
        gla_o, fox_o, zg, zf, x2, row(ln_in_g), row(ln_in_b),
        w_gla_proj[0].astype(BF16), w_fox_proj[0].astype(BF16), w_out[0].astype(BF16),
        row(ln_mix_g[0]), row(ln_mix_b[0]), w_router[0].T.astype(BF16), br_rep)

    nb = (t * TOP_K + N_EXPERTS * (MOE_BLK - 1) + MOE_BLK - 1) // MOE_BLK
    nb_pad = (nb + LANES - 1) // LANES * LANES
    rank_t, counts = _rank(idx_t)
    dest_t, blk_e8, ends = _dest(idx_t, rank_t, counts, nb_pad)
    ends1 = ends[:, 0]
    xs = _dispatch(ends1, dest_t, h2, nb)
    ys = _experts(blk_e8[0, :nb], ends1[N_EXPERTS - 1:], ends1, counts[:, 0].astype(I32), xs, w_gate_up[0], b_gate_up[0][:, None, :],
                  w_down[0], b_down[0][:, None, :], nb)
    out = _combine(dest_t, gate_t, h2, row(ln_ffn_g[0]), row(ln_ffn_b[0]), ys)
    return out.reshape(batch, seq, d)
```

```python
import functools

import jax
import jax.numpy as jnp
from jax import lax
from jax.experimental import pallas as pl
from jax.experimental.pallas import tpu as pltpu

F32 = jnp.float32
BF16 = jnp.bfloat16
I32 = jnp.int32

D_MODEL = 1024
CHUNK = 64
GLA_HEADS, GLA_DK, GLA_DV, GLA_RANK, GLA_TAU = 4, 128, 256, 16, 16.0
GLA_QK = GLA_HEADS * GLA_DK
GLA_V = GLA_HEADS * GLA_DV
FOX_HEADS, FOX_HD = 8, 128
FOX_W = FOX_HEADS * FOX_HD
N_EXPERTS, TOP_K, D_FF = 32, 4, 1024
SWIGLU_LIMIT, SWIGLU_ALPHA = 7.0, 1.702
LN_EPS = 1e-5
DEPTH = 1
DEEPNORM_ALPHA = (2 * DEPTH) ** 0.25
IN_SPLITS = (GLA_QK, GLA_QK, GLA_V, GLA_RANK, GLA_V, FOX_W, FOX_W, FOX_W, FOX_HEADS, D_MODEL, D_MODEL)

LANES = 128
SUBLANES = 8
VMEM_BYTES_V7X = 64 * 1024 * 1024
SLAB_ROWS = D_MODEL // LANES

FORGET_COL0 = GLA_RANK
FORGET_PIECES = 3

TM_PROJ = 512
GLA_ROWS = 512
FOX_TQ = 512
FOX_TK = 512
FOX_HEADS_PER_STEP = 2
LOG2E = 1.4426950408889634
CUM_ROWS = 256
RANK_TW = 512
MOE_BLK = 512
EXPERT_ROW_GROUP = 128
DISPATCH_TOK = 256
COMBINE_TOK = 256
ROW_DMA_UNROLL = 8


def _cparams(sem, vmem_mib, flags=None):
    return pltpu.CompilerParams(dimension_semantics=sem, vmem_limit_bytes=vmem_mib * 1024 * 1024, flags=flags)


def _log_sigmoid(z):
    return jnp.minimum(z, 0.0) - jnp.log1p(jnp.exp(-jnp.abs(z)))


def _layer_norm(x, g, b):
    mu = jnp.mean(x, axis=-1, keepdims=True)
    xc = x - mu
    var = jnp.mean(xc * xc, axis=-1, keepdims=True)
    return xc * lax.rsqrt(var + LN_EPS) * g + b


def _split_bf16(x, pieces):
    out = []
    for _ in range(pieces):
        p = x.astype(BF16)
        out.append(p)
        x = x - p.astype(F32)
    return out


_NARROW_A = sum(IN_SPLITS[:3])
_NARROW_B = sum(IN_SPLITS[:8])


_REGROUP_COLS = 512


def _regroup_kernel(wt_hbm, wide_ref, small_ref, buf, nbuf, sems, nsem):
    j = pl.program_id(0)
    cols = _REGROUP_COLS
    half = lax.rem(j, 2)

    def load(step, s):
        c0 = step * cols
        src = c0 + jnp.where(c0 >= _NARROW_A, GLA_RANK, 0) + jnp.where(c0 >= _NARROW_B - GLA_RANK, FOX_HEADS, 0)
        return pltpu.make_async_copy(wt_hbm.at[pl.ds(pl.multiple_of(src, SUBLANES), cols), :], buf.at[s], sems.at[s])

    @pl.when(j == 0)
    def _():
        load(0, 0).start()
        nbuf[...] = jnp.zeros_like(nbuf)
        copies = [pltpu.make_async_copy(wt_hbm.at[pl.ds(_NARROW_A, GLA_RANK), :], nbuf.at[pl.ds(0, GLA_RANK), :], nsem)]
        for h in range(FOX_HEADS):
            for p in range(FORGET_PIECES):
                copies.append(pltpu.make_async_copy(
                    wt_hbm.at[pl.ds(_NARROW_B + h, 1), :],
                    nbuf.at[pl.ds(FORGET_COL0 + FORGET_PIECES * h + p, 1), :], nsem))
        for cp in copies:
            cp.start()
        for cp in copies:
            cp.wait()
        small_ref[...] = nbuf[...].T.astype(small_ref.dtype)

    @pl.when(j + 1 < pl.num_programs(0))
    def _():
        load(j + 1, 1 - half).start()

    load(j, half).wait()
    wide_ref[...] = buf[half].T.astype(wide_ref.dtype)


def _regroup_w_in(wt):
    n_wide = wt.shape[0] - GLA_RANK - FOX_HEADS
    assert _NARROW_A % _REGROUP_COLS == 0 and (_NARROW_B - GLA_RANK) % _REGROUP_COLS == 0
    return pl.pallas_call(
        _regroup_kernel,
        out_shape=[jax.ShapeDtypeStruct((D_MODEL, n_wide), BF16), jax.ShapeDtypeStruct((D_MODEL, LANES), BF16)],
        grid=(n_wide // _REGROUP_COLS,),
        in_specs=[pl.BlockSpec(memory_space=pl.ANY)],
        out_specs=[pl.BlockSpec((D_MODEL, _REGROUP_COLS), lambda j: (0, j)),
                   pl.BlockSpec((D_MODEL, LANES), lambda j: (0, 0))],
        scratch_shapes=[pltpu.VMEM((2, _REGROUP_COLS, D_MODEL), F32), pltpu.VMEM((LANES, D_MODEL), F32),
                        pltpu.SemaphoreType.DMA((2,)), pltpu.SemaphoreType.DMA(())],
        compiler_params=_cparams(("arbitrary",), 32),
        name="regroup_w_in",
    )(wt)


_PROJ_WIDTHS = (GLA_QK, GLA_QK, GLA_V, GLA_V, FOX_W, FOX_W, FOX_W, D_MODEL, D_MODEL)


def _ln_inproj_kernel(x0_ref, xn_ref, g_ref, b_ref, w_ref, ws_ref, *refs):
    wide_refs, small_ref, hb_refs = refs[:-3], refs[-3], refs[-2:]
    i = pl.program_id(0)

    @pl.when(i == 0)
    def _():
        hb_refs[0][...] = _layer_norm(x0_ref[...], g_ref[...], b_ref[...]).astype(BF16)

    def project(cur_ref, nxt_ref):
        hb = cur_ref[...]
        off = 0
        for o_ref in wide_refs:
            n = o_ref.shape[1]
            o_ref[...] = jnp.dot(hb, w_ref[:, off:off + n], preferred_element_type=F32).astype(o_ref.dtype)
            off += n
        small_ref[...] = jnp.dot(hb, ws_ref[...], preferred_element_type=F32)
        nxt_ref[...] = _layer_norm(xn_ref[...], g_ref[...], b_ref[...]).astype(BF16)

    for parity in range(2):
        pl.when(lax.rem(i, 2) == parity)(functools.partial(project, hb_refs[parity], hb_refs[1 - parity]))


def _ln_inproj(x2, g, b, w_wide, w_small):
    t = x2.shape[0]
    tm = TM_PROJ
    last = t // tm - 1
    row = lambda w: pl.BlockSpec((tm, w), lambda i: (i, 0))
    const = lambda a: pl.BlockSpec(a.shape, lambda i: (0, 0), pipeline_mode=pl.Buffered(1))
    out_shape = ([jax.ShapeDtypeStruct((t, w), BF16) for w in _PROJ_WIDTHS]
                 + [jax.ShapeDtypeStruct((t, LANES), F32)])
    out_specs = [row(w) for w in _PROJ_WIDTHS] + [row(LANES)]
    return pl.pallas_call(
        _ln_inproj_kernel,
        out_shape=out_shape,
        grid=(t // tm,),
        in_specs=[pl.BlockSpec((tm, D_MODEL), lambda i: (0, 0), pipeline_mode=pl.Buffered(1)),
                  pl.BlockSpec((tm, D_MODEL), lambda i: (jnp.minimum(i + 1, last), 0)),
                  const(g), const(b), const(w_wide), const(w_small)],
        out_specs=out_specs,
        scratch_shapes=[pltpu.VMEM((tm, D_MODEL), BF16), pltpu.VMEM((tm, D_MODEL), BF16)],
        compiler_params=_cparams(("arbitrary",), 52),
        name="ln_inproj",
    )(x2, x2, g, b, w_wide, w_small)


def _forget_cum_kernel(small_ref, bias_ref, sel_ref, kf_ref):
    s = small_ref.shape[0]
    r = CUM_ROWS
    ri = lax.broadcasted_iota(I32, (r, r), 0)
    ci = lax.broadcasted_iota(I32, (r, r), 1)
    tri = jnp.where(ci <= ri, 1.0, 0.0).astype(BF16)
    sel = sel_ref[...]
    carry = jnp.zeros((1, LANES), F32)
    for blk in range(s // r):
        rows = pl.ds(blk * r, r)
        ls = _log_sigmoid(small_ref[rows, :] + bias_ref[...])
        cum = carry
        for p in _split_bf16(ls, 3):
            cum = cum + jnp.dot(tri, p, preferred_element_type=F32)
        carry = cum[r - 1:r, :]
        neg = cum * (-LOG2E)
        p1, p2, p3 = _split_bf16(neg, FORGET_PIECES)
        kf = (p1.astype(F32) * sel[0:1, :] + p2.astype(F32) * sel[1:2, :] + p3.astype(F32) * sel[2:3, :])
        kf_ref[rows, :] = kf.astype(BF16)


def _forget_cum(small, bias_row, sel, batch, seq):
    return pl.pallas_call(
        _forget_cum_kernel,
        out_shape=jax.ShapeDtypeStruct((batch * seq, LANES), BF16),
        grid=(batch,),
        in_specs=[pl.BlockSpec((seq, LANES), lambda b: (b, 0)),
                  pl.BlockSpec((1, LANES), lambda b: (0, 0)),
                  pl.BlockSpec((SUBLANES, LANES), lambda b: (0, 0))],
        out_specs=pl.BlockSpec((seq, LANES), lambda b: (b, 0)),
        compiler_params=_cparams(("parallel",), 32),
        name="forget_cum",
    )(small, bias_row, sel)


def _gla_kernel(q_ref, k_ref, v_ref, r_ref, small_ref, wgu_ref, bg_ref, gn_ref, o_ref,
                state_ref, ds_ref, sb_ref, of_ref):
    rows = q_ref.shape[0]
    nchunk = rows // CHUNK
    kcol = lambda h: slice(h * GLA_DK, (h + 1) * GLA_DK)
    vcol = lambda h: slice(h * GLA_DV, (h + 1) * GLA_DV)
    crow = lambda c: slice(c * CHUNK, (c + 1) * CHUNK)

    @pl.when(pl.program_id(1) == 0)
    def _():
        state_ref[...] = jnp.zeros_like(state_ref)

    z = jnp.dot(small_ref[...].astype(BF16), wgu_ref[...], preferred_element_type=F32) + bg_ref[...]
    la = _log_sigmoid(z) * (1.0 / GLA_TAU)
    ri = lax.broadcasted_iota(I32, (rows, rows), 0)
    ci = lax.broadcasted_iota(I32, (rows, rows), 1)
    shift = CHUNK.bit_length() - 1
    same = lax.shift_right_logical(ri, shift) == lax.shift_right_logical(ci, shift)
    tri = jnp.where(same & (ci <= ri), 1.0, 0.0).astype(BF16)
    cum = jnp.zeros((rows, GLA_QK), F32)
    for p in _split_bf16(la, 2):
        cum = cum + jnp.dot(tri, p, preferred_element_type=F32)
    last = [cum[(c + 1) * CHUNK - 1:(c + 1) * CHUNK, :] for c in range(nchunk)]
    tot = jnp.concatenate([jnp.broadcast_to(t, (CHUNK, GLA_QK)) for t in last], axis=0)
    kd = (k_ref[...].astype(F32) * jnp.exp(tot - cum)).astype(BF16)
    dec_rows = jnp.exp(jnp.concatenate(last + [jnp.zeros((LANES - nchunk, GLA_QK), F32)], axis=0))
    dec_cols = dec_rows.T

    for c in range(nchunk):
        for h in range(GLA_HEADS):
            ds_ref[c, h] = lax.dot_general(kd[crow(c), kcol(h)], v_ref[crow(c), vcol(h)],
                                           (((0,), (0,)), ((), ())), preferred_element_type=F32)
    for h in range(GLA_HEADS):
        st = state_ref[h]
        for c in range(nchunk):
            st = st * jnp.broadcast_to(dec_cols[kcol(h), c:c + 1], (GLA_DK, GLA_DV)) + ds_ref[c, h]
            sb_ref[c, h] = st.astype(BF16)
        state_ref[h] = st
    scale = float(GLA_DK) ** -0.5
    for c in range(nchunk):
        for h in range(GLA_HEADS):
            of_ref[crow(c), vcol(h)] = jnp.dot(q_ref[crow(c), kcol(h)], sb_ref[c, h],
                                               preferred_element_type=F32) * scale
    for h in range(GLA_HEADS):
        o = of_ref[:, vcol(h)]
        o = o * lax.rsqrt(jnp.mean(o * o, axis=-1, keepdims=True) + LN_EPS) * gn_ref[...]
        r = r_ref[:, vcol(h)].astype(F32)
        o_ref[:, vcol(h)] = (o * (r * jax.nn.sigmoid(r))).astype(o_ref.dtype)


def _gla(gq, gk, gv, gr, small, wgu_pad, b_gate, g_norm, batch, seq):
    rows = GLA_ROWS
    nsteps = seq // rows
    row = lambda w: pl.BlockSpec((rows, w), lambda b, i: (b * nsteps + i, 0))
    const = lambda a: pl.BlockSpec(a.shape, lambda b, i: (0, 0))
    return pl.pallas_call(
        _gla_kernel,
        out_shape=jax.ShapeDtypeStruct((batch * seq, GLA_V), BF16),
        grid=(batch, nsteps),
        in_specs=[row(GLA_QK), row(GLA_QK), row(GLA_V), row(GLA_V), row(LANES),
                  const(wgu_pad), const(b_gate), const(g_norm)],
        out_specs=row(GLA_V),
        scratch_shapes=[pltpu.VMEM((GLA_HEADS, GLA_DK, GLA_DV), F32),
                        pltpu.VMEM((rows // CHUNK, GLA_HEADS, GLA_DK, GLA_DV), F32),
                        pltpu.VMEM((rows // CHUNK, GLA_HEADS, GLA_DK, GLA_DV), BF16),
                        pltpu.VMEM((rows, GLA_V), F32)],
        compiler_params=_cparams(("parallel", "arbitrary"), 40),
        name="gla",
    )(gq, gk, gv, gr, small, wgu_pad, b_gate, g_norm)


def _fox_kernel(q_ref, k_ref, v_ref, kf_ref, qsel_ref, o_ref, qa_ref, vt_ref, s_ref, acc_ref, m_ref, l_ref):
    t = FOX_TQ
    seq = q_ref.shape[0]
    nq = seq // t
    c = (float(FOX_HD) ** -0.5) * LOG2E
    hd = FOX_HD
    heads = range(FOX_HEADS_PER_STEP)
    for h in heads:
        for r in range(0, seq, t):
            qa_ref[h, 0:hd, r:r + t] = (q_ref[r:r + t, h * hd:(h + 1) * hd].astype(F32) * c).T.astype(BF16)
            vt_ref[h, :, r:r + t] = v_ref[r:r + t, h * hd:(h + 1) * hd].astype(F32).T.astype(BF16)
        qa_ref[h, hd:hd + LANES, :] = jnp.tile(qsel_ref[h], (1, seq // LANES))

    def key_rows(j):
        return pl.ds(pl.multiple_of(j * t, t), t)

    def scores(h, i, j):
        ka = jnp.concatenate([k_ref[key_rows(j), h * hd:(h + 1) * hd], kf_ref[key_rows(j), :]], axis=1)
        return jnp.dot(ka, qa_ref[h, :, i * t:(i + 1) * t], preferred_element_type=F32)

    def values(h, j, p):
        return jnp.dot(vt_ref[h, :, key_rows(j)], p, preferred_element_type=F32)

    def step(j, slot, masked, nxt):
        if nxt is not None:
            for h in heads:
                s_ref[1 - slot, h] = scores(h, *nxt)
        for h in heads:
            s = s_ref[slot, h]
            if masked:
                causal = lax.broadcasted_iota(I32, (t, t), 0) <= lax.broadcasted_iota(I32, (t, t), 1)
                s = jnp.where(causal, s, -jnp.inf)
            m_old = m_ref[h]
            m_new = jnp.maximum(m_old, jnp.broadcast_to(jnp.max(s, axis=0, keepdims=True), (SUBLANES, t)))
            alpha = jnp.exp2(m_old - m_new)
            p = jnp.exp2(s - m_new[0:1, :])
            m_ref[h] = m_new
            l_ref[h] = alpha * l_ref[h] + jnp.sum(p, axis=0, keepdims=True)
            acc_ref[h] = alpha[0:1, :] * acc_ref[h] + values(h, j, p.astype(BF16))

    for h in heads:
        s_ref[0, h] = scores(h, 0, 0)
    slot = 0
    for i in range(nq):
        for h in heads:
            acc_ref[h] = jnp.zeros((hd, t), F32)
            m_ref[h] = jnp.full((SUBLANES, t), -jnp.inf, F32)
            l_ref[h] = jnp.zeros((SUBLANES, t), F32)
        if i >= 2:
            def pair(jj, _, i=i, slot=slot):
                step(2 * jj, slot, False, (i, 2 * jj + 1))
                step(2 * jj + 1, 1 - slot, False, (i, 2 * jj + 2))
                return 0

            lax.fori_loop(0, i // 2, pair, 0)
        if i % 2 == 1:
            step(i - 1, slot, False, (i, i))
            slot = 1 - slot
        step(i, slot, True, (i + 1, 0) if i + 1 < nq else None)
        slot = 1 - slot
        for h in heads:
            o_ref[i * t:(i + 1) * t, h * hd:(h + 1) * hd] = (acc_ref[h] / l_ref[h][0:1, :]).T.astype(o_ref.dtype)


def _fox(fq, fk, fv, kf, qsel, batch, seq):
    hps = FOX_HEADS_PER_STEP
    w = hps * FOX_HD
    t = FOX_TQ
    head_cols = pl.BlockSpec((seq, w), lambda b, h: (b, h))
    return pl.pallas_call(
        _fox_kernel,
        out_shape=jax.ShapeDtypeStruct((batch * seq, FOX_W), BF16),
        grid=(batch, FOX_HEADS // hps),
        in_specs=[head_cols, head_cols, head_cols,
                  pl.BlockSpec((seq, LANES), lambda b, h: (b, 0)),
                  pl.BlockSpec((hps, LANES, LANES), lambda b, h: (h, 0, 0))],
        out_specs=head_cols,
        scratch_shapes=[pltpu.VMEM((hps, FOX_HD + LANES, seq), BF16), pltpu.VMEM((hps, FOX_HD, seq), BF16),
                        pltpu.VMEM((2, hps, t, t), F32), pltpu.VMEM((hps, FOX_HD, t), F32),
                        pltpu.VMEM((hps, SUBLANES, t), F32), pltpu.VMEM((hps, SUBLANES, t), F32)],
        compiler_params=_cparams(("parallel", "parallel"), 48),
        name="fox",
    )(fq, fk, fv, kf, qsel)


def _post_mix_kernel(gla_ref, fox_ref, zg_ref, zf_ref, x_ref, gin_ref, bin_ref, wg_ref, wf_ref, wo_ref, g_ref, b_ref,
                     wr_ref, br_ref, h2_ref, idx_ref, gate_ref):
    tm = x_ref.shape[0]
    y_gla = jnp.dot(gla_ref[...], wg_ref[...], preferred_element_type=F32)
    y_fox = jnp.dot(fox_ref[...], wf_ref[...], preferred_element_type=F32)
    mixed = (jax.nn.sigmoid(zg_ref[...].astype(F32)) * y_gla
             + jax.nn.sigmoid(zf_ref[...].astype(F32)) * y_fox)
    h = _layer_norm(x_ref[...], gin_ref[...], bin_ref[...])
    pre = DEEPNORM_ALPHA * h + jnp.dot(mixed.astype(BF16), wo_ref[...], preferred_element_type=F32)
    h2 = _layer_norm(pre, g_ref[...], b_ref[...])
    h2_ref[...] = h2

    logits = lax.dot_general(wr_ref[...], h2.astype(BF16), (((1,), (1,)), ((), ())),
                             preferred_element_type=F32)
    logits = logits + jnp.tile(br_ref[...], (1, tm // LANES))
    eidx = lax.broadcasted_iota(I32, (N_EXPERTS, tm), 0)
    vals, idxs = [], []
    for _ in range(TOP_K):
        mx = jnp.max(logits, axis=0, keepdims=True)
        ik = jnp.min(jnp.where(logits == mx, eidx, N_EXPERTS), axis=0, keepdims=True)
        vals.append(mx)
        idxs.append(ik)
        logits = jnp.where(eidx == ik, -jnp.inf, logits)
    exps = [jnp.exp(v - vals[0]) for v in vals]
    denom = exps[0] + exps[1] + exps[2] + exps[3]
    rid = lax.broadcasted_iota(I32, (SUBLANES, tm), 0)
    idx8 = jnp.zeros((SUBLANES, tm), I32)
    gate8 = jnp.zeros((SUBLANES, tm), F32)
    for k in range(TOP_K):
        idx8 = jnp.where(rid == k, idxs[k], idx8)
        gate8 = jnp.where(rid == k, exps[k] / denom, gate8)
    idx_ref[...] = idx8
    gate_ref[...] = gate8


def _post_mix(gla_o, fox_o, zg, zf, x2, g_in, b_in, wg, wf, wo, g, b, wr_t, br_rep):
    t = x2.shape[0]
    tm = TM_PROJ
    row = lambda w: pl.BlockSpec((tm, w), lambda i: (i, 0))
    const = lambda a: pl.BlockSpec(a.shape, lambda i: (0, 0))
    return pl.pallas_call(
        _post_mix_kernel,
        out_shape=[jax.ShapeDtypeStruct((t, D_MODEL), F32),
                   jax.ShapeDtypeStruct((SUBLANES, t), I32),
                   jax.ShapeDtypeStruct((SUBLANES, t), F32)],
        grid=(t // tm,),
        in_specs=[row(GLA_V), row(FOX_W), row(D_MODEL), row(D_MODEL), row(D_MODEL), const(g_in), const(b_in),
                  const(wg), const(wf), const(wo), const(g), const(b), const(wr_t), const(br_rep)],
        out_specs=[row(D_MODEL),
                   pl.BlockSpec((SUBLANES, tm), lambda i: (0, i)), pl.BlockSpec((SUBLANES, tm), lambda i: (0, i))],
        compiler_params=_cparams(("parallel",), 48),
        name="post_mix",
    )(gla_o, fox_o, zg, zf, x2, g_in, b_in, wg, wf, wo, g, b, wr_t, br_rep)


def _onehot_rows(idx8, tw):
    eidx = lax.broadcasted_iota(I32, (N_EXPERTS, tw), 0)
    hit = eidx == idx8[0:1, :]
    for k in range(1, TOP_K):
        hit = hit | (eidx == idx8[k:k + 1, :])
    return eidx, hit


def _rank_kernel(idx_ref, rank_ref, counts_ref, carry_ref):
    tw = idx_ref.shape[1]

    @pl.when(pl.program_id(0) == 0)
    def _():
        carry_ref[...] = jnp.zeros_like(carry_ref)

    idx8 = idx_ref[...]
    eidx, hit = _onehot_rows(idx8, tw)
    onehot = jnp.where(hit, 1.0, 0.0).astype(BF16)
    ri = lax.broadcasted_iota(I32, (tw, tw + LANES), 0)
    ci = lax.broadcasted_iota(I32, (tw, tw + LANES), 1)
    upper = jnp.where((ri < ci) | (ci >= tw), 1.0, 0.0).astype(BF16)
    cnt = jnp.dot(onehot, upper, preferred_element_type=F32)
    before = cnt[:, :tw] + jnp.tile(carry_ref[...], (1, tw // LANES))
    rid = lax.broadcasted_iota(I32, (SUBLANES, tw), 0)
    rank8 = jnp.zeros((SUBLANES, tw), I32)
    for k in range(TOP_K):
        rk = jnp.sum(jnp.where(eidx == idx8[k:k + 1, :], before, 0.0), axis=0, keepdims=True)
        rank8 = jnp.where(rid == k, rk.astype(I32), rank8)
    rank_ref[...] = rank8
    carry_ref[...] = carry_ref[...] + cnt[:, tw:]
    counts_ref[...] = carry_ref[...]


def _rank(idx_t):
    t = idx_t.shape[1]
    tw = RANK_TW
    return pl.pallas_call(
        _rank_kernel,
        out_shape=[jax.ShapeDtypeStruct((SUBLANES, t), I32),
                   jax.ShapeDtypeStruct((N_EXPERTS, LANES), F32)],
        grid=(t // tw,),
        in_specs=[pl.BlockSpec((SUBLANES, tw), lambda i: (0, i))],
        out_specs=[pl.BlockSpec((SUBLANES, tw), lambda i: (0, i)),
                   pl.BlockSpec((N_EXPERTS, LANES), lambda i: (0, 0))],
        scratch_shapes=[pltpu.VMEM((N_EXPERTS, LANES), F32)],
        compiler_params=_cparams(("arbitrary",), 32),
        name="rank",
    )(idx_t)


def _dest_kernel(idx_ref, rank_ref, counts_ref, dest_ref, blk_e_ref, ends_ref, *, nb_pad):
    tw = idx_ref.shape[1]
    nblk = jnp.floor((counts_ref[...] + (MOE_BLK - 1)) * (1.0 / MOE_BLK))
    ri = lax.broadcasted_iota(I32, (N_EXPERTS, N_EXPERTS), 0)
    ci = lax.broadcasted_iota(I32, (N_EXPERTS, N_EXPERTS), 1)
    tri = jnp.where(ci <= ri, 1.0, 0.0).astype(BF16)
    end_blk = jnp.dot(tri, nblk.astype(BF16), preferred_element_type=F32)
    start_row = (end_blk - nblk) * float(MOE_BLK)
    idx8 = idx_ref[...]
    eidx = lax.broadcasted_iota(I32, (N_EXPERTS, tw), 0)
    start_t = jnp.tile(start_row, (1, tw // LANES))
    rid = lax.broadcasted_iota(I32, (SUBLANES, tw), 0)
    dest8 = jnp.zeros((SUBLANES, tw), I32)
    for k in range(TOP_K):
        st = jnp.sum(jnp.where(eidx == idx8[k:k + 1, :], start_t, 0.0), axis=0, keepdims=True)
        dest8 = jnp.where(rid == k, st.astype(I32), dest8)
    dest_ref[...] = dest8 + rank_ref[...]
    bid = lax.broadcasted_iota(I32, (N_EXPERTS, nb_pad), 1).astype(F32)
    ends_t = jnp.tile(end_blk, (1, nb_pad // LANES))
    be = jnp.sum(jnp.where(ends_t <= bid, 1.0, 0.0), axis=0, keepdims=True)
    blk_e_ref[...] = jnp.broadcast_to(jnp.minimum(be, N_EXPERTS - 1.0), (SUBLANES, nb_pad)).astype(I32)
    ends_ref[...] = end_blk.astype(I32)


def _dest(idx_t, rank_t, counts, nb_pad):
    t = idx_t.shape[1]
    tw = RANK_TW
    tok = pl.BlockSpec((SUBLANES, tw), lambda i: (0, i))
    return pl.pallas_call(
        functools.partial(_dest_kernel, nb_pad=nb_pad),
        out_shape=[jax.ShapeDtypeStruct((SUBLANES, t), I32),
                   jax.ShapeDtypeStruct((SUBLANES, nb_pad), I32),
                   jax.ShapeDtypeStruct((N_EXPERTS, LANES), I32)],
        grid=(t // tw,),
        in_specs=[tok, tok, pl.BlockSpec((N_EXPERTS, LANES), lambda i: (0, 0))],
        out_specs=[tok, pl.BlockSpec((SUBLANES, nb_pad), lambda i: (0, 0)),
                   pl.BlockSpec((N_EXPERTS, LANES), lambda i: (0, 0))],
        compiler_params=_cparams(("arbitrary",), 32),
        name="dest",
    )(idx_t, rank_t, counts)


def _dispatch_kernel(ends_ref, dest_ref, h2_ref, xs_ref, h2s_ref, slab_ref, zero_ref, sem, zsem, hsem):
    td = h2_ref.shape[0]

    nb = xs_ref.shape[0] // MOE_BLK
    n_used = ends_ref[N_EXPERTS - 1]

    def zero_block(blk):
        start = pl.multiple_of(blk * MOE_BLK, MOE_BLK)
        return pltpu.make_async_copy(zero_ref, xs_ref.at[pl.ds(start, MOE_BLK)], zsem)

    def has_rows(e):
        prev = jnp.where(e == 0, 0, ends_ref[jnp.maximum(e - 1, 0)])
        return ends_ref[e] > prev

    @pl.when(pl.program_id(0) == 0)
    def _():
        zero_ref[...] = jnp.zeros_like(zero_ref)

        def each(fn):
            def last_block(e, _):
                @pl.when(has_rows(e))
                def _():
                    fn(zero_block(ends_ref[e] - 1))
                return 0

            def tail_block(blk, _):
                fn(zero_block(blk))
                return 0

            lax.fori_loop(0, N_EXPERTS, last_block, 0)
            lax.fori_loop(n_used, nb, tail_block, 0)

        each(lambda cp: cp.start())
        each(lambda cp: cp.wait())

    for c in range(SLAB_ROWS):
        slab_ref[:, c, :] = h2_ref[:, c * LANES:(c + 1) * LANES]

    tile0 = pl.multiple_of(pl.program_id(0) * td, td)
    stage = pltpu.make_async_copy(slab_ref, h2s_ref.at[pl.ds(tile0, td)], hsem)
    stage.start()
    stage.wait()

    def start_rows(t, _):
        for k in range(TOP_K):
            if k < TOP_K // 2:
                cp = pltpu.make_async_copy(slab_ref.at[t], xs_ref.at[dest_ref[k, t]], sem)
            else:
                cp = pltpu.make_async_copy(h2s_ref.at[tile0 + t], xs_ref.at[dest_ref[k, t]], hsem)
            cp.start(priority=k % 2)
        return 0

    lax.fori_loop(0, td, start_rows, 0, unroll=ROW_DMA_UNROLL)
    for k in range(TOP_K // 2):
        pltpu.make_async_copy(slab_ref, xs_ref.at[pl.ds(0, td)], sem).wait()
        pltpu.make_async_copy(h2s_ref.at[pl.ds(0, td)], xs_ref.at[pl.ds(0, td)], hsem).wait()


def _dispatch(ends, dest_t, h2, nb):
    t = h2.shape[0]
    td = DISPATCH_TOK
    grid_spec = pltpu.PrefetchScalarGridSpec(
        num_scalar_prefetch=1,
        grid=(t // td,),
        in_specs=[pl.BlockSpec((SUBLANES, td), lambda i, ends: (0, i), memory_space=pltpu.SMEM),
                  pl.BlockSpec((td, D_MODEL), lambda i, ends: (i, 0))],
        out_specs=[pl.BlockSpec(memory_space=pl.ANY), pl.BlockSpec(memory_space=pl.ANY)],
        scratch_shapes=[pltpu.VMEM((td, SLAB_ROWS, LANES), F32), pltpu.VMEM((MOE_BLK, SLAB_ROWS, LANES), F32),
                        pltpu.SemaphoreType.DMA(()), pltpu.SemaphoreType.DMA(()), pltpu.SemaphoreType.DMA(())],
    )
    xs, _ = pl.pallas_call(
        _dispatch_kernel,
        out_shape=[jax.ShapeDtypeStruct((nb * MOE_BLK, SLAB_ROWS, LANES), F32),
                   jax.ShapeDtypeStruct((t, SLAB_ROWS, LANES), F32)],
        grid_spec=grid_spec,
        compiler_params=_cparams(("arbitrary",), 32),
        name="dispatch",
    )(ends, dest_t, h2)
    return xs


def _experts_kernel(blk_e_ref, nused_ref, ends_ref, cnt_ref, xs_hbm, wgu_hbm, bgu_ref, wd_hbm, bd_ref, ys_hbm,
                    xbuf, ybuf, wgu_f32, wd_f32, wgu_bf, wd_bf, slot_ref, sems, xsems, ysems):
    nb = xs_hbm.shape[0] // MOE_BLK

    def block(j, _):
        _experts_block(j, nb, blk_e_ref, nused_ref, ends_ref, cnt_ref, xs_hbm, wgu_hbm, bgu_ref, wd_hbm, bd_ref, ys_hbm,
                       xbuf, ybuf, wgu_f32, wd_f32, wgu_bf, wd_bf, slot_ref, sems, xsems, ysems)
        return 0

    lax.fori_loop(0, nb, block, 0)


def _experts_block(j, nb, blk_e_ref, nused_ref, ends_ref, cnt_ref, xs_hbm, wgu_hbm, bgu_ref, wd_hbm, bd_ref, ys_hbm,
                   xbuf, ybuf, wgu_f32, wd_f32, wgu_bf, wd_bf, slot_ref, sems, xsems, ysems):
    par = lax.rem(j, 2)
    n_used = nused_ref[0]
    active = j < n_used
    e = blk_e_ref[j]
    prev = blk_e_ref[jnp.maximum(j - 1, 0)]
    fresh = (j == 0) | (e != prev)

    def fetch(expert, slot):
        return (pltpu.make_async_copy(wgu_hbm.at[expert], wgu_f32.at[slot], sems.at[slot, 0]),
                pltpu.make_async_copy(wd_hbm.at[expert], wd_f32.at[slot], sems.at[slot, 1]))

    def x_load(blk, half):
        rows = pl.ds(pl.multiple_of(blk * MOE_BLK, MOE_BLK), MOE_BLK)
        return [pltpu.make_async_copy(xs_hbm.at[rows, c, :], xbuf.at[half, :, pl.ds(c * LANES, LANES)], xsems.at[half])
                for c in range(SLAB_ROWS)]

    def y_store(blk, half):
        rows = pl.ds(pl.multiple_of(blk * MOE_BLK, MOE_BLK), MOE_BLK)
        return [pltpu.make_async_copy(ybuf.at[half, :, pl.ds(c * LANES, LANES)], ys_hbm.at[rows, c, :], ysems.at[half])
                for c in range(SLAB_ROWS)]

    @pl.when(j == 0)
    def _():
        ybuf[...] = jnp.zeros_like(ybuf)
        for cp in x_load(0, 0):
            cp.start()

    @pl.when(j + 1 < n_used)
    def _():
        for cp in x_load(j + 1, 1 - par):
            cp.start()

    @pl.when(j >= 2)
    def _():
        for cp in y_store(j - 2, par):
            cp.wait()

    @pl.when(j == 0)
    def _():
        slot_ref[0] = 0
        for cp in fetch(e, 0):
            cp.start()

    @pl.when(active & fresh)
    def _():
        slot = slot_ref[0]
        for cp in fetch(e, slot):
            cp.wait()
        step = 256
        for r in range(0, D_MODEL, step):
            wgu_bf[r:r + step, :] = wgu_f32[slot, r:r + step, :].astype(BF16)
        for r in range(0, D_FF, step):
            wd_bf[r:r + step, :] = wd_f32[slot, r:r + step, :].astype(BF16)
        next_blk = ends_ref[e]

        @pl.when(next_blk < n_used)
        def _():
            for cp in fetch(blk_e_ref[jnp.minimum(next_blk, n_used - 1)], 1 - slot):
                cp.start()

        slot_ref[0] = 1 - slot

    @pl.when(active)
    def _():
        for cp in x_load(j, par):
            cp.wait()

    first_blk = jnp.where(e == 0, 0, ends_ref[jnp.maximum(e - 1, 0)])
    valid = jnp.clip(cnt_ref[e] - (j - first_blk) * MOE_BLK, 0, MOE_BLK)
    groups = lax.shift_right_logical(valid + (EXPERT_ROW_GROUP - 1), EXPERT_ROW_GROUP.bit_length() - 1)

    def compute(m):
        x = xbuf[par, 0:m].astype(BF16)
        gu = jnp.dot(x, wgu_bf[...], preferred_element_type=F32) + bgu_ref[e]
        gate = jnp.minimum(gu[:, :D_FF], SWIGLU_LIMIT)
        up = jnp.clip(gu[:, D_FF:], -SWIGLU_LIMIT, SWIGLU_LIMIT)
        glu = gate * jax.nn.sigmoid(SWIGLU_ALPHA * gate)
        act = ((up + 1.0) * glu).astype(BF16)
        ybuf[par, 0:m] = jnp.dot(act, wd_bf[...], preferred_element_type=F32) + bd_ref[e]

    for g in range(1, MOE_BLK // EXPERT_ROW_GROUP + 1):
        pl.when(active & (groups == g))(functools.partial(compute, g * EXPERT_ROW_GROUP))

    @pl.when(jnp.logical_not(active))
    def _():
        ybuf[par] = jnp.zeros((MOE_BLK, D_MODEL), F32)

    for cp in y_store(j, par):
        cp.start()

    @pl.when(j == nb - 1)
    def _():
        for cp in y_store(j, par) + y_store(j - 1, 1 - par):
            cp.wait()


def _experts(blk_e, nused, ends, cnt, xs, w_gate_up, b_gate_up, w_down, b_down, nb):
    assert nb >= 2
    whole = lambda a: pl.BlockSpec(a.shape, lambda i, be, nu, en, cn: (0,) * a.ndim)
    grid_spec = pltpu.PrefetchScalarGridSpec(
        num_scalar_prefetch=4,
        grid=(1,),
        in_specs=[pl.BlockSpec(memory_space=pl.ANY),
                  pl.BlockSpec(memory_space=pl.ANY),
                  whole(b_gate_up),
                  pl.BlockSpec(memory_space=pl.ANY),
                  whole(b_down)],
        out_specs=pl.BlockSpec(memory_space=pl.ANY),
        scratch_shapes=[pltpu.VMEM((2, MOE_BLK, D_MODEL), F32), pltpu.VMEM((2, MOE_BLK, D_MODEL), F32),
                        pltpu.VMEM((2, D_MODEL, 2 * D_FF), F32), pltpu.VMEM((2, D_FF, D_MODEL), F32),
                        pltpu.VMEM((D_MODEL, 2 * D_FF), BF16), pltpu.VMEM((D_FF, D_MODEL), BF16),
                        pltpu.SMEM((1,), I32), pltpu.SemaphoreType.DMA((2, 2)),
                        pltpu.SemaphoreType.DMA((2,)), pltpu.SemaphoreType.DMA((2,))],
    )
    return pl.pallas_call(
        _experts_kernel,
        out_shape=jax.ShapeDtypeStruct((nb * MOE_BLK, SLAB_ROWS, LANES), F32),
        grid_spec=grid_spec,
        compiler_params=_cparams(("arbitrary",), 56),
        name="experts",
    )(blk_e, nused, ends, cnt, xs, w_gate_up, b_gate_up, w_down, b_down)


def _combine_kernel(dest_ref, dest_next_ref, gate_ref, h2_ref, g_ref, b_ref, ys_ref, o_ref, buf_ref, ffn_ref, sems):
    tc = h2_ref.shape[0]
    i = pl.program_id(0)
    slot = lax.rem(i, 2)

    def gather(d_ref, s):
        def start_rows(t, _):
            for k in range(TOP_K):
                pltpu.make_async_copy(ys_ref.at[d_ref[k, t]], buf_ref.at[s, k, t], sems.at[s]).start(priority=k % 2)
            return 0

        lax.fori_loop(0, tc, start_rows, 0, unroll=ROW_DMA_UNROLL)

    @pl.when(i == 0)
    def _():
        gather(dest_ref, 0)

    @pl.when(i + 1 < pl.num_programs(0))
    def _():
        gather(dest_next_ref, 1 - slot)

    for k in range(TOP_K):
        pltpu.make_async_copy(ys_ref.at[pl.ds(0, tc)], buf_ref.at[slot, k], sems.at[slot]).wait()

    def weigh(t, _):
        acc = gate_ref[0, t] * buf_ref[slot, 0, t]
        for k in range(1, TOP_K):
            acc = acc + gate_ref[k, t] * buf_ref[slot, k, t]
        ffn_ref[t] = acc
        return 0

    lax.fori_loop(0, tc, weigh, 0, unroll=ROW_DMA_UNROLL)
    ffn = jnp.concatenate([ffn_ref[:, c, :] for c in range(SLAB_ROWS)], axis=1)
    o_ref[...] = _layer_norm(DEEPNORM_ALPHA * h2_ref[...] + ffn, g_ref[...], b_ref[...])


def _combine(dest_t, gate_t, h2, g, b, ys):
    t = h2.shape[0]
    tc = COMBINE_TOK
    last = t // tc - 1
    return pl.pallas_call(
        _combine_kernel,
        out_shape=jax.ShapeDtypeStruct((t, D_MODEL), F32),
        grid=(t // tc,),
        in_specs=[pl.BlockSpec((SUBLANES, tc), lambda i: (0, i), memory_space=pltpu.SMEM),
                  pl.BlockSpec((SUBLANES, tc), lambda i: (0, jnp.minimum(i + 1, last)), memory_space=pltpu.SMEM),
                  pl.BlockSpec((SUBLANES, tc), lambda i: (0, i), memory_space=pltpu.SMEM),
                  pl.BlockSpec((tc, D_MODEL), lambda i: (i, 0)),
                  pl.BlockSpec((1, D_MODEL), lambda i: (0, 0)),
                  pl.BlockSpec((1, D_MODEL), lambda i: (0, 0)),
                  pl.BlockSpec(memory_space=pl.ANY)],
        out_specs=pl.BlockSpec((tc, D_MODEL), lambda i: (i, 0)),
        scratch_shapes=[pltpu.VMEM((2, TOP_K, tc, SLAB_ROWS, LANES), F32), pltpu.VMEM((tc, SLAB_ROWS, LANES), F32),
                        pltpu.SemaphoreType.DMA((2,))],
        compiler_params=_cparams(("arbitrary",), 32),
        name="combine",
    )(dest_t, dest_t, gate_t, h2, g, b, ys)


def kernel(x, ln_in_g, ln_in_b, w_in, w_gla_gate_up, b_gla_gate, g_gla_norm, b_forget, w_gla_proj, w_fox_proj, w_out, ln_mix_g, ln_mix_b, w_router, b_router, w_gate_up, b_gate_up, w_down, b_down, ln_ffn_g, ln_ffn_b):
    batch, seq, d = x.shape
    assert d == D_MODEL and w_in.shape[0] == DEPTH == 1
    assert seq % max(GLA_ROWS, FOX_TQ, FOX_TK, CUM_ROWS) == 0
    t = batch * seq
    assert t % max(TM_PROJ, RANK_TW, DISPATCH_TOK, COMBINE_TOK) == 0 and t // MOE_BLK <= 256
    row = lambda v: v.reshape(1, -1).astype(F32)

    n_forget = FOX_HEADS * FORGET_PIECES
    lane = jnp.arange(LANES)
    w_wide, w_small = _regroup_w_in(jnp.transpose(w_in[0]))
    in_forget = (lane >= FORGET_COL0) & (lane < FORGET_COL0 + n_forget)
    piece = (lane - FORGET_COL0) % FORGET_PIECES
    forget_bias = jnp.zeros((1, LANES), F32).at[0, FORGET_COL0:FORGET_COL0 + n_forget].set(
        jnp.repeat(b_forget[0].astype(F32), FORGET_PIECES))
    piece_sel = jnp.stack([(in_forget & (piece == p)).astype(F32) for p in range(FORGET_PIECES)]
                          + [jnp.zeros((LANES,), F32)] * (SUBLANES - FORGET_PIECES))
    head_of_lane = (lane - FORGET_COL0) // FORGET_PIECES
    qsel = jnp.broadcast_to(jnp.stack([(in_forget & (head_of_lane == h)) for h in range(FOX_HEADS)])[:, :, None],
                            (FOX_HEADS, LANES, LANES)).astype(BF16)
    wgu_pad = jnp.concatenate([w_gla_gate_up[0], jnp.zeros((LANES - GLA_RANK, GLA_QK), F32)], axis=0).astype(BF16)

    x2 = x.reshape(t, d)
    gq, gk, gv, gr, fq, fk, fv, zg, zf, small = _ln_inproj(x2, row(ln_in_g), row(ln_in_b), w_wide, w_small)
    kf = _forget_cum(small, forget_bias, piece_sel, batch, seq)
    gla_o = _gla(gq, gk, gv, gr, small, wgu_pad, row(b_gla_gate[0]), row(g_gla_norm[0]), batch, seq)
    fox_o = _fox(fq, fk, fv, kf, qsel, batch, seq)

    br_rep = jnp.broadcast_to(b_router[0].astype(F32)[:, None], (N_EXPERTS, LANES))
    h2, idx_t, gate_t = _post_mix(
        gla_o, fox_o, zg, zf, x2, row(ln_in_g), row(ln_in_b),
        w_gla_proj[0].astype(BF16), w_fox_proj[0].astype(BF16), w_out[0].astype(BF16),
        row(ln_mix_g[0]), row(ln_mix_b[0]), w_router[0].T.astype(BF16), br_rep)

    nb = (t * TOP_K + N_EXPERTS * (MOE_BLK - 1) + MOE_BLK - 1) // MOE_BLK
    nb_pad = (nb + LANES - 1) // LANES * LANES
    rank_t, counts = _rank(idx_t)
    dest_t, blk_e8, ends = _dest(idx_t, rank_t, counts, nb_pad)
    ends1 = ends[:, 0]
    xs = _dispatch(ends1, dest_t, h2, nb)
    ys = _experts(blk_e8[0, :nb], ends1[N_EXPERTS - 1:], ends1, counts[:, 0].astype(I32), xs, w_gate_up[0], b_gate_up[0][:, None, :],
                  w_down[0], b_down[0][:, None, :], nb)
    out = _combine(dest_t, gate_t, h2, row(ln_ffn_g[0]), row(ln_ffn_b[0]), ys)
    return out.reshape(batch, seq, d)
```

```python
import functools

import jax
import jax.numpy as jnp
from jax import lax
from jax.experimental import pallas as pl
from jax.experimental.pallas import tpu as pltpu

F32 = jnp.float32
BF16 = jnp.bfloat16
I32 = jnp.int32

D_MODEL = 1024
CHUNK = 64
GLA_HEADS, GLA_DK, GLA_DV, GLA_RANK, GLA_TAU = 4, 128, 256, 16, 16.0
GLA_QK = GLA_HEADS * GLA_DK
GLA_V = GLA_HEADS * GLA_DV
FOX_HEADS, FOX_HD = 8, 128
FOX_W = FOX_HEADS * FOX_HD
N_EXPERTS, TOP_K, D_FF = 32, 4, 1024
SWIGLU_LIMIT, SWIGLU_ALPHA = 7.0, 1.702
LN_EPS = 1e-5
DEPTH = 1
DEEPNORM_ALPHA = (2 * DEPTH) ** 0.25
IN_SPLITS = (GLA_QK, GLA_QK, GLA_V, GLA_RANK, GLA_V, FOX_W, FOX_W, FOX_W, FOX_HEADS, D_MODEL, D_MODEL)

LANES = 128
SUBLANES = 8
VMEM_BYTES_V7X = 64 * 1024 * 1024
SLAB_ROWS = D_MODEL // LANES

FORGET_COL0 = GLA_RANK
FORGET_PIECES = 3

TM_PROJ = 512
GLA_ROWS = 512
FOX_TQ = 512
FOX_TK = 512
FOX_HEADS_PER_STEP = 2
LOG2E = 1.4426950408889634
CUM_ROWS = 256
RANK_TW = 512
MOE_BLK = 512
EXPERT_ROW_GROUP = 128
DISPATCH_TOK = 256
COMBINE_TOK = 256
ROW_DMA_UNROLL = 8


def _cparams(sem, vmem_mib, flags=None):
    return pltpu.CompilerParams(dimension_semantics=sem, vmem_limit_bytes=vmem_mib * 1024 * 1024, flags=flags)


def _log_sigmoid(z):
    return jnp.minimum(z, 0.0) - jnp.log1p(jnp.exp(-jnp.abs(z)))


def _layer_norm(x, g, b):
    mu = jnp.mean(x, axis=-1, keepdims=True)
    xc = x - mu
    var = jnp.mean(xc * xc, axis=-1, keepdims=True)
    return xc * lax.rsqrt(var + LN_EPS) * g + b


def _split_bf16(x, pieces):
    out = []
    for _ in range(pieces):
        p = x.astype(BF16)
        out.append(p)
        x = x - p.astype(F32)
    return out


_NARROW_A = sum(IN_SPLITS[:3])
_NARROW_B = sum(IN_SPLITS[:8])


_REGROUP_COLS = 512


def _regroup_kernel(wt_hbm, wide_ref, small_ref, buf, nbuf, sems, nsem):
    j = pl.program_id(0)
    cols = _REGROUP_COLS
    half = lax.rem(j, 2)

    def load(step, s):
        c0 = step * cols
        src = c0 + jnp.where(c0 >= _NARROW_A, GLA_RANK, 0) + jnp.where(c0 >= _NARROW_B - GLA_RANK, FOX_HEADS, 0)
        return pltpu.make_async_copy(wt_hbm.at[pl.ds(pl.multiple_of(src, SUBLANES), cols), :], buf.at[s], sems.at[s])

    @pl.when(j == 0)
    def _():
        load(0, 0).start()
        nbuf[...] = jnp.zeros_like(nbuf)
        copies = [pltpu.make_async_copy(wt_hbm.at[pl.ds(_NARROW_A, GLA_RANK), :], nbuf.at[pl.ds(0, GLA_RANK), :], nsem)]
        for h in range(FOX_HEADS):
            for p in range(FORGET_PIECES):
                copies.append(pltpu.make_async_copy(
                    wt_hbm.at[pl.ds(_NARROW_B + h, 1), :],
                    nbuf.at[pl.ds(FORGET_COL0 + FORGET_PIECES * h + p, 1), :], nsem))
        for cp in copies:
            cp.start()
        for cp in copies:
            cp.wait()
        small_ref[...] = nbuf[...].T.astype(small_ref.dtype)

    @pl.when(j + 1 < pl.num_programs(0))
    def _():
        load(j + 1, 1 - half).start()

    load(j, half).wait()
    wide_ref[...] = buf[half].T.astype(wide_ref.dtype)


def _regroup_w_in(wt):
    n_wide = wt.shape[0] - GLA_RANK - FOX_HEADS
    assert _NARROW_A % _REGROUP_COLS == 0 and (_NARROW_B - GLA_RANK) % _REGROUP_COLS == 0
    return pl.pallas_call(
        _regroup_kernel,
        out_shape=[jax.ShapeDtypeStruct((D_MODEL, n_wide), BF16), jax.ShapeDtypeStruct((D_MODEL, LANES), BF16)],
        grid=(n_wide // _REGROUP_COLS,),
        in_specs=[pl.BlockSpec(memory_space=pl.ANY)],
        out_specs=[pl.BlockSpec((D_MODEL, _REGROUP_COLS), lambda j: (0, j)),
                   pl.BlockSpec((D_MODEL, LANES), lambda j: (0, 0))],
        scratch_shapes=[pltpu.VMEM((2, _REGROUP_COLS, D_MODEL), F32), pltpu.VMEM((LANES, D_MODEL), F32),
                        pltpu.SemaphoreType.DMA((2,)), pltpu.SemaphoreType.DMA(())],
        compiler_params=_cparams(("arbitrary",), 32),
        name="regroup_w_in",
    )(wt)


_PROJ_WIDTHS = (GLA_QK, GLA_QK, GLA_V, GLA_V, FOX_W, FOX_W, FOX_W, D_MODEL, D_MODEL)


def _ln_inproj_kernel(x0_ref, xn_ref, g_ref, b_ref, w_ref, ws_ref, *refs):
    wide_refs, small_ref, hb_refs = refs[:-3], refs[-3], refs[-2:]
    i = pl.program_id(0)

    @pl.when(i == 0)
    def _():
        hb_refs[0][...] = _layer_norm(x0_ref[...], g_ref[...], b_ref[...]).astype(BF16)

    def project(cur_ref, nxt_ref):
        hb = cur_ref[...]
        off = 0
        for o_ref in wide_refs:
            n = o_ref.shape[1]
            o_ref[...] = jnp.dot(hb, w_ref[:, off:off + n], preferred_element_type=F32).astype(o_ref.dtype)
            off += n
        small_ref[...] = jnp.dot(hb, ws_ref[...], preferred_element_type=F32)
        nxt_ref[...] = _layer_norm(xn_ref[...], g_ref[...], b_ref[...]).astype(BF16)

    for parity in range(2):
        pl.when(lax.rem(i, 2) == parity)(functools.partial(project, hb_refs[parity], hb_refs[1 - parity]))


def _ln_inproj(x2, g, b, w_wide, w_small):
    t = x2.shape[0]
    tm = TM_PROJ
    last = t // tm - 1
    row = lambda w: pl.BlockSpec((tm, w), lambda i: (i, 0))
    const = lambda a: pl.BlockSpec(a.shape, lambda i: (0, 0), pipeline_mode=pl.Buffered(1))
    out_shape = ([jax.ShapeDtypeStruct((t, w), BF16) for w in _PROJ_WIDTHS]
                 + [jax.ShapeDtypeStruct((t, LANES), F32)])
    out_specs = [row(w) for w in _PROJ_WIDTHS] + [row(LANES)]
    return pl.pallas_call(
        _ln_inproj_kernel,
        out_shape=out_shape,
        grid=(t // tm,),
        in_specs=[pl.BlockSpec((tm, D_MODEL), lambda i: (0, 0), pipeline_mode=pl.Buffered(1)),
                  pl.BlockSpec((tm, D_MODEL), lambda i: (jnp.minimum(i + 1, last), 0)),
                  const(g), const(b), const(w_wide), const(w_small)],
        out_specs=out_specs,
        scratch_shapes=[pltpu.VMEM((tm, D_MODEL), BF16), pltpu.VMEM((tm, D_MODEL), BF16)],
        compiler_params=_cparams(("arbitrary",), 52),
        name="ln_inproj",
    )(x2, x2, g, b, w_wide, w_small)


def _forget_cum_kernel(small_ref, bias_ref, sel_ref, kf_ref):
    s = small_ref.shape[0]
    r = CUM_ROWS
    ri = lax.broadcasted_iota(I32, (r, r), 0)
    ci = lax.broadcasted_iota(I32, (r, r), 1)
    tri = jnp.where(ci <= ri, 1.0, 0.0).astype(BF16)
    sel = sel_ref[...]
    carry = jnp.zeros((1, LANES), F32)
    for blk in range(s // r):
        rows = pl.ds(blk * r, r)
        ls = _log_sigmoid(small_ref[rows, :] + bias_ref[...])
        cum = carry
        for p in _split_bf16(ls, 3):
            cum = cum + jnp.dot(tri, p, preferred_element_type=F32)
        carry = cum[r - 1:r, :]
        neg = cum * (-LOG2E)
        p1, p2, p3 = _split_bf16(neg, FORGET_PIECES)
        kf = (p1.astype(F32) * sel[0:1, :] + p2.astype(F32) * sel[1:2, :] + p3.astype(F32) * sel[2:3, :])
        kf_ref[rows, :] = kf.astype(BF16)


def _forget_cum(small, bias_row, sel, batch, seq):
    return pl.pallas_call(
        _forget_cum_kernel,
        out_shape=jax.ShapeDtypeStruct((batch * seq, LANES), BF16),
        grid=(batch,),
        in_specs=[pl.BlockSpec((seq, LANES), lambda b: (b, 0)),
                  pl.BlockSpec((1, LANES), lambda b: (0, 0)),
                  pl.BlockSpec((SUBLANES, LANES), lambda b: (0, 0))],
        out_specs=pl.BlockSpec((seq, LANES), lambda b: (b, 0)),
        compiler_params=_cparams(("parallel",), 32),
        name="forget_cum",
    )(small, bias_row, sel)


def _gla_kernel(q_ref, k_ref, v_ref, r_ref, small_ref, wgu_ref, bg_ref, gn_ref, o_ref,
                state_ref, ds_ref, sb_ref, of_ref):
    rows = q_ref.shape[0]
    nchunk = rows // CHUNK
    kcol = lambda h: slice(h * GLA_DK, (h + 1) * GLA_DK)
    vcol = lambda h: slice(h * GLA_DV, (h + 1) * GLA_DV)
    crow = lambda c: slice(c * CHUNK, (c + 1) * CHUNK)

    @pl.when(pl.program_id(1) == 0)
    def _():
        state_ref[...] = jnp.zeros_like(state_ref)

    z = jnp.dot(small_ref[...].astype(BF16), wgu_ref[...], preferred_element_type=F32) + bg_ref[...]
    la = _log_sigmoid(z) * (1.0 / GLA_TAU)
    cb = CUM_ROWS
    ri = lax.broadcasted_iota(I32, (cb, cb), 0)
    ci = lax.broadcasted_iota(I32, (cb, cb), 1)
    shift = CHUNK.bit_length() - 1
    same = lax.shift_right_logical(ri, shift) == lax.shift_right_logical(ci, shift)
    tri = jnp.where(same & (ci <= ri), 1.0, 0.0).astype(BF16)
    pieces = _split_bf16(la, 2)
    cum = jnp.concatenate(
        [sum(jnp.dot(tri, p[r:r + cb], preferred_element_type=F32) for p in pieces) for r in range(0, rows, cb)],
        axis=0)
    last = [cum[(c + 1) * CHUNK - 1:(c + 1) * CHUNK, :] for c in range(nchunk)]
    tot = jnp.concatenate([jnp.broadcast_to(t, (CHUNK, GLA_QK)) for t in last], axis=0)
    kd = (k_ref[...].astype(F32) * jnp.exp(tot - cum)).astype(BF16)
    dec_rows = jnp.exp(jnp.concatenate(last + [jnp.zeros((LANES - nchunk, GLA_QK), F32)], axis=0))
    dec_cols = dec_rows.T

    for c in range(nchunk):
        for h in range(GLA_HEADS):
            ds_ref[c, h] = lax.dot_general(kd[crow(c), kcol(h)], v_ref[crow(c), vcol(h)],
                                           (((0,), (0,)), ((), ())), preferred_element_type=F32)
    for h in range(GLA_HEADS):
        st = state_ref[h]
        for c in range(nchunk):
            st = st * jnp.broadcast_to(dec_cols[kcol(h), c:c + 1], (GLA_DK, GLA_DV)) + ds_ref[c, h]
            sb_ref[c, h] = st.astype(BF16)
        state_ref[h] = st
    scale = float(GLA_DK) ** -0.5
    for c in range(nchunk):
        for h in range(GLA_HEADS):
            of_ref[crow(c), vcol(h)] = jnp.dot(q_ref[crow(c), kcol(h)], sb_ref[c, h],
                                               preferred_element_type=F32) * scale
    for h in range(GLA_HEADS):
        o = of_ref[:, vcol(h)]
        o = o * lax.rsqrt(jnp.mean(o * o, axis=-1, keepdims=True) + LN_EPS) * gn_ref[...]
        r = r_ref[:, vcol(h)].astype(F32)
        o_ref[:, vcol(h)] = (o * (r * jax.nn.sigmoid(r))).astype(o_ref.dtype)


def _gla(gq, gk, gv, gr, small, wgu_pad, b_gate, g_norm, batch, seq):
    rows = GLA_ROWS
    nsteps = seq // rows
    row = lambda w: pl.BlockSpec((rows, w), lambda b, i: (b * nsteps + i, 0))
    const = lambda a: pl.BlockSpec(a.shape, lambda b, i: (0, 0))
    return pl.pallas_call(
        _gla_kernel,
        out_shape=jax.ShapeDtypeStruct((batch * seq, GLA_V), BF16),
        grid=(batch, nsteps),
        in_specs=[row(GLA_QK), row(GLA_QK), row(GLA_V), row(GLA_V), row(LANES),
                  const(wgu_pad), const(b_gate), const(g_norm)],
        out_specs=row(GLA_V),
        scratch_shapes=[pltpu.VMEM((GLA_HEADS, GLA_DK, GLA_DV), F32),
                        pltpu.VMEM((rows // CHUNK, GLA_HEADS, GLA_DK, GLA_DV), F32),
                        pltpu.VMEM((rows // CHUNK, GLA_HEADS, GLA_DK, GLA_DV), BF16),
                        pltpu.VMEM((rows, GLA_V), F32)],
        compiler_params=_cparams(("parallel", "arbitrary"), 40),
        name="gla",
    )(gq, gk, gv, gr, small, wgu_pad, b_gate, g_norm)


def _fox_kernel(q_ref, k_ref, v_ref, kf_ref, qsel_ref, o_ref, qa_ref, vt_ref, s_ref, acc_ref, m_ref, l_ref):
    t = FOX_TQ
    seq = q_ref.shape[0]
    nq = seq // t
    c = (float(FOX_HD) ** -0.5) * LOG2E
    hd = FOX_HD
    heads = range(FOX_HEADS_PER_STEP)
    for h in heads:
        for r in range(0, seq, t):
            qa_ref[h, 0:hd, r:r + t] = (q_ref[r:r + t, h * hd:(h + 1) * hd].astype(F32) * c).T.astype(BF16)
            vt_ref[h, :, r:r + t] = v_ref[r:r + t, h * hd:(h + 1) * hd].astype(F32).T.astype(BF16)
        qa_ref[h, hd:hd + LANES, :] = jnp.tile(qsel_ref[h], (1, seq // LANES))

    def key_rows(j):
        return pl.ds(pl.multiple_of(j * t, t), t)

    def scores(h, i, j):
        ka = jnp.concatenate([k_ref[key_rows(j), h * hd:(h + 1) * hd], kf_ref[key_rows(j), :]], axis=1)
        return jnp.dot(ka, qa_ref[h, :, i * t:(i + 1) * t], preferred_element_type=F32)

    def values(h, j, p):
        return jnp.dot(vt_ref[h, :, key_rows(j)], p, preferred_element_type=F32)

    def step(j, slot, masked, nxt):
        if nxt is not None:
            for h in heads:
                s_ref[1 - slot, h] = scores(h, *nxt)
        for h in heads:
            s = s_ref[slot, h]
            if masked:
                causal = lax.broadcasted_iota(I32, (t, t), 0) <= lax.broadcasted_iota(I32, (t, t), 1)
                s = jnp.where(causal, s, -jnp.inf)
            m_old = m_ref[h]
            m_new = jnp.maximum(m_old, jnp.broadcast_to(jnp.max(s, axis=0, keepdims=True), (SUBLANES, t)))
            alpha = jnp.exp2(m_old - m_new)
            p = jnp.exp2(s - m_new[0:1, :])
            m_ref[h] = m_new
            l_ref[h] = alpha * l_ref[h] + jnp.sum(p, axis=0, keepdims=True)
            acc_ref[h] = alpha[0:1, :] * acc_ref[h] + values(h, j, p.astype(BF16))

    for h in heads:
        s_ref[0, h] = scores(h, 0, 0)
    slot = 0
    for i in range(nq):
        for h in heads:
            acc_ref[h] = jnp.zeros((hd, t), F32)
            m_ref[h] = jnp.full((SUBLANES, t), -jnp.inf, F32)
            l_ref[h] = jnp.zeros((SUBLANES, t), F32)
        if i >= 2:
            def pair(jj, _, i=i, slot=slot):
                step(2 * jj, slot, False, (i, 2 * jj + 1))
                step(2 * jj + 1, 1 - slot, False, (i, 2 * jj + 2))
                return 0

            lax.fori_loop(0, i // 2, pair, 0)
        if i % 2 == 1:
            step(i - 1, slot, False, (i, i))
            slot = 1 - slot
        step(i, slot, True, (i + 1, 0) if i + 1 < nq else None)
        slot = 1 - slot
        for h in heads:
            o_ref[i * t:(i + 1) * t, h * hd:(h + 1) * hd] = (acc_ref[h] / l_ref[h][0:1, :]).T.astype(o_ref.dtype)


def _fox(fq, fk, fv, kf, qsel, batch, seq):
    hps = FOX_HEADS_PER_STEP
    w = hps * FOX_HD
    t = FOX_TQ
    head_cols = pl.BlockSpec((seq, w), lambda b, h: (b, h))
    return pl.pallas_call(
        _fox_kernel,
        out_shape=jax.ShapeDtypeStruct((batch * seq, FOX_W), BF16),
        grid=(batch, FOX_HEADS // hps),
        in_specs=[head_cols, head_cols, head_cols,
                  pl.BlockSpec((seq, LANES), lambda b, h: (b, 0)),
                  pl.BlockSpec((hps, LANES, LANES), lambda b, h: (h, 0, 0))],
        out_specs=head_cols,
        scratch_shapes=[pltpu.VMEM((hps, FOX_HD + LANES, seq), BF16), pltpu.VMEM((hps, FOX_HD, seq), BF16),
                        pltpu.VMEM((2, hps, t, t), F32), pltpu.VMEM((hps, FOX_HD, t), F32),
                        pltpu.VMEM((hps, SUBLANES, t), F32), pltpu.VMEM((hps, SUBLANES, t), F32)],
        compiler_params=_cparams(("parallel", "parallel"), 48),
        name="fox",
    )(fq, fk, fv, kf, qsel)


def _post_mix_kernel(gla_ref, fox_ref, zg_ref, zf_ref, x_ref, gin_ref, bin_ref, wg_ref, wf_ref, wo_ref, g_ref, b_ref,
                     wr_ref, br_ref, h2_ref, idx_ref, gate_ref):
    tm = x_ref.shape[0]
    y_gla = jnp.dot(gla_ref[...], wg_ref[...], preferred_element_type=F32)
    y_fox = jnp.dot(fox_ref[...], wf_ref[...], preferred_element_type=F32)
    mixed = (jax.nn.sigmoid(zg_ref[...].astype(F32)) * y_gla
             + jax.nn.sigmoid(zf_ref[...].astype(F32)) * y_fox)
    h = _layer_norm(x_ref[...], gin_ref[...], bin_ref[...])
    pre = DEEPNORM_ALPHA * h + jnp.dot(mixed.astype(BF16), wo_ref[...], preferred_element_type=F32)
    h2 = _layer_norm(pre, g_ref[...], b_ref[...])
    h2_ref[...] = h2

    logits = lax.dot_general(wr_ref[...], h2.astype(BF16), (((1,), (1,)), ((), ())),
                             preferred_element_type=F32)
    logits = logits + jnp.tile(br_ref[...], (1, tm // LANES))
    eidx = lax.broadcasted_iota(I32, (N_EXPERTS, tm), 0)
    vals, idxs = [], []
    for _ in range(TOP_K):
        mx = jnp.max(logits, axis=0, keepdims=True)
        ik = jnp.min(jnp.where(logits == mx, eidx, N_EXPERTS), axis=0, keepdims=True)
        vals.append(mx)
        idxs.append(ik)
        logits = jnp.where(eidx == ik, -jnp.inf, logits)
    exps = [jnp.exp(v - vals[0]) for v in vals]
    denom = exps[0] + exps[1] + exps[2] + exps[3]
    rid = lax.broadcasted_iota(I32, (SUBLANES, tm), 0)
    idx8 = jnp.zeros((SUBLANES, tm), I32)
    gate8 = jnp.zeros((SUBLANES, tm), F32)
    for k in range(TOP_K):
        idx8 = jnp.where(rid == k, idxs[k], idx8)
        gate8 = jnp.where(rid == k, exps[k] / denom, gate8)
    idx_ref[...] = idx8
    gate_ref[...] = gate8


def _post_mix(gla_o, fox_o, zg, zf, x2, g_in, b_in, wg, wf, wo, g, b, wr_t, br_rep):
    t = x2.shape[0]
    tm = TM_PROJ
    row = lambda w: pl.BlockSpec((tm, w), lambda i: (i, 0))
    const = lambda a: pl.BlockSpec(a.shape, lambda i: (0, 0))
    return pl.pallas_call(
        _post_mix_kernel,
        out_shape=[jax.ShapeDtypeStruct((t, D_MODEL), F32),
                   jax.ShapeDtypeStruct((SUBLANES, t), I32),
                   jax.ShapeDtypeStruct((SUBLANES, t), F32)],
        grid=(t // tm,),
        in_specs=[row(GLA_V), row(FOX_W), row(D_MODEL), row(D_MODEL), row(D_MODEL), const(g_in), const(b_in),
                  const(wg), const(wf), const(wo), const(g), const(b), const(wr_t), const(br_rep)],
        out_specs=[row(D_MODEL),
                   pl.BlockSpec((SUBLANES, tm), lambda i: (0, i)), pl.BlockSpec((SUBLANES, tm), lambda i: (0, i))],
        compiler_params=_cparams(("parallel",), 48),
        name="post_mix",
    )(gla_o, fox_o, zg, zf, x2, g_in, b_in, wg, wf, wo, g, b, wr_t, br_rep)


def _onehot_rows(idx8, tw):
    eidx = lax.broadcasted_iota(I32, (N_EXPERTS, tw), 0)
    hit = eidx == idx8[0:1, :]
    for k in range(1, TOP_K):
        hit = hit | (eidx == idx8[k:k + 1, :])
    return eidx, hit


def _rank_kernel(idx_ref, rank_ref, counts_ref, carry_ref):
    tw = idx_ref.shape[1]

    @pl.when(pl.program_id(0) == 0)
    def _():
        carry_ref[...] = jnp.zeros_like(carry_ref)

    idx8 = idx_ref[...]
    eidx, hit = _onehot_rows(idx8, tw)
    onehot = jnp.where(hit, 1.0, 0.0).astype(BF16)
    ri = lax.broadcasted_iota(I32, (tw, tw + LANES), 0)
    ci = lax.broadcasted_iota(I32, (tw, tw + LANES), 1)
    upper = jnp.where((ri < ci) | (ci >= tw), 1.0, 0.0).astype(BF16)
    cnt = jnp.dot(onehot, upper, preferred_element_type=F32)
    before = cnt[:, :tw] + jnp.tile(carry_ref[...], (1, tw // LANES))
    rid = lax.broadcasted_iota(I32, (SUBLANES, tw), 0)
    rank8 = jnp.zeros((SUBLANES, tw), I32)
    for k in range(TOP_K):
        rk = jnp.sum(jnp.where(eidx == idx8[k:k + 1, :], before, 0.0), axis=0, keepdims=True)
        rank8 = jnp.where(rid == k, rk.astype(I32), rank8)
    rank_ref[...] = rank8
    carry_ref[...] = carry_ref[...] + cnt[:, tw:]
    counts_ref[...] = carry_ref[...]


def _rank(idx_t):
    t = idx_t.shape[1]
    tw = RANK_TW
    return pl.pallas_call(
        _rank_kernel,
        out_shape=[jax.ShapeDtypeStruct((SUBLANES, t), I32),
                   jax.ShapeDtypeStruct((N_EXPERTS, LANES), F32)],
        grid=(t // tw,),
        in_specs=[pl.BlockSpec((SUBLANES, tw), lambda i: (0, i))],
        out_specs=[pl.BlockSpec((SUBLANES, tw), lambda i: (0, i)),
                   pl.BlockSpec((N_EXPERTS, LANES), lambda i: (0, 0))],
        scratch_shapes=[pltpu.VMEM((N_EXPERTS, LANES), F32)],
        compiler_params=_cparams(("arbitrary",), 32),
        name="rank",
    )(idx_t)


def _dest_kernel(idx_ref, rank_ref, counts_ref, dest_ref, blk_e_ref, ends_ref, *, nb_pad):
    tw = idx_ref.shape[1]
    nblk = jnp.floor((counts_ref[...] + (MOE_BLK - 1)) * (1.0 / MOE_BLK))
    ri = lax.broadcasted_iota(I32, (N_EXPERTS, N_EXPERTS), 0)
    ci = lax.broadcasted_iota(I32, (N_EXPERTS, N_EXPERTS), 1)
    tri = jnp.where(ci <= ri, 1.0, 0.0).astype(BF16)
    end_blk = jnp.dot(tri, nblk.astype(BF16), preferred_element_type=F32)
    start_row = (end_blk - nblk) * float(MOE_BLK)
    idx8 = idx_ref[...]
    eidx = lax.broadcasted_iota(I32, (N_EXPERTS, tw), 0)
    start_t = jnp.tile(start_row, (1, tw // LANES))
    rid = lax.broadcasted_iota(I32, (SUBLANES, tw), 0)
    dest8 = jnp.zeros((SUBLANES, tw), I32)
    for k in range(TOP_K):
        st = jnp.sum(jnp.where(eidx == idx8[k:k + 1, :], start_t, 0.0), axis=0, keepdims=True)
        dest8 = jnp.where(rid == k, st.astype(I32), dest8)
    dest_ref[...] = dest8 + rank_ref[...]
    bid = lax.broadcasted_iota(I32, (N_EXPERTS, nb_pad), 1).astype(F32)
    ends_t = jnp.tile(end_blk, (1, nb_pad // LANES))
    be = jnp.sum(jnp.where(ends_t <= bid, 1.0, 0.0), axis=0, keepdims=True)
    blk_e_ref[...] = jnp.broadcast_to(jnp.minimum(be, N_EXPERTS - 1.0), (SUBLANES, nb_pad)).astype(I32)
    ends_ref[...] = end_blk.astype(I32)


def _dest(idx_t, rank_t, counts, nb_pad):
    t = idx_t.shape[1]
    tw = RANK_TW
    tok = pl.BlockSpec((SUBLANES, tw), lambda i: (0, i))
    return pl.pallas_call(
        functools.partial(_dest_kernel, nb_pad=nb_pad),
        out_shape=[jax.ShapeDtypeStruct((SUBLANES, t), I32),
                   jax.ShapeDtypeStruct((SUBLANES, nb_pad), I32),
                   jax.ShapeDtypeStruct((N_EXPERTS, LANES), I32)],
        grid=(t // tw,),
        in_specs=[tok, tok, pl.BlockSpec((N_EXPERTS, LANES), lambda i: (0, 0))],
        out_specs=[tok, pl.BlockSpec((SUBLANES, nb_pad), lambda i: (0, 0)),
                   pl.BlockSpec((N_EXPERTS, LANES), lambda i: (0, 0))],
        compiler_params=_cparams(("arbitrary",), 32),
        name="dest",
    )(idx_t, rank_t, counts)


def _dispatch_kernel(ends_ref, dest_ref, h2_ref, xs_ref, slab_ref, zero_ref, sem, zsem):
    td = h2_ref.shape[0]

    nb = xs_ref.shape[0] // MOE_BLK
    n_used = ends_ref[N_EXPERTS - 1]

    def zero_block(blk):
        start = pl.multiple_of(blk * MOE_BLK, MOE_BLK)
        return pltpu.make_async_copy(zero_ref, xs_ref.at[pl.ds(start, MOE_BLK)], zsem)

    def has_rows(e):
        prev = jnp.where(e == 0, 0, ends_ref[jnp.maximum(e - 1, 0)])
        return ends_ref[e] > prev

    @pl.when(pl.program_id(0) == 0)
    def _():
        zero_ref[...] = jnp.zeros_like(zero_ref)

        def each(fn):
            def last_block(e, _):
                @pl.when(has_rows(e))
                def _():
                    fn(zero_block(ends_ref[e] - 1))
                return 0

            def tail_block(blk, _):
                fn(zero_block(blk))
                return 0

            lax.fori_loop(0, N_EXPERTS, last_block, 0)
            lax.fori_loop(n_used, nb, tail_block, 0)

        each(lambda cp: cp.start())
        each(lambda cp: cp.wait())

    for c in range(SLAB_ROWS):
        slab_ref[:, c, :] = h2_ref[:, c * LANES:(c + 1) * LANES]

    def start_rows(t, _):
        for k in range(TOP_K):
            pltpu.make_async_copy(slab_ref.at[t], xs_ref.at[dest_ref[k, t]], sem).start(priority=k % 2)
        return 0

    lax.fori_loop(0, td, start_rows, 0, unroll=ROW_DMA_UNROLL)
    for k in range(TOP_K):
        pltpu.make_async_copy(slab_ref, xs_ref.at[pl.ds(0, td)], sem).wait()


def _dispatch(ends, dest_t, h2, nb):
    t = h2.shape[0]
    td = DISPATCH_TOK
    grid_spec = pltpu.PrefetchScalarGridSpec(
        num_scalar_prefetch=1,
        grid=(t // td,),
        in_specs=[pl.BlockSpec((SUBLANES, td), lambda i, ends: (0, i), memory_space=pltpu.SMEM),
                  pl.BlockSpec((td, D_MODEL), lambda i, ends: (i, 0))],
        out_specs=pl.BlockSpec(memory_space=pl.ANY),
        scratch_shapes=[pltpu.VMEM((td, SLAB_ROWS, LANES), F32), pltpu.VMEM((MOE_BLK, SLAB_ROWS, LANES), F32),
                        pltpu.SemaphoreType.DMA(()), pltpu.SemaphoreType.DMA(())],
    )
    return pl.pallas_call(
        _dispatch_kernel,
        out_shape=jax.ShapeDtypeStruct((nb * MOE_BLK, SLAB_ROWS, LANES), F32),
        grid_spec=grid_spec,
        compiler_params=_cparams(("arbitrary",), 32),
        name="dispatch",
    )(ends, dest_t, h2)


def _experts_kernel(blk_e_ref, nused_ref, ends_ref, cnt_ref, xs_hbm, wgu_hbm, bgu_ref, wd_hbm, bd_ref, ys_hbm,
                    xbuf, ybuf, wgu_f32, wd_f32, wgu_bf, wd_bf, slot_ref, sems, xsems, ysems):
    nb = xs_hbm.shape[0] // MOE_BLK

    def block(j, _):
        _experts_block(j, nb, blk_e_ref, nused_ref, ends_ref, cnt_ref, xs_hbm, wgu_hbm, bgu_ref, wd_hbm, bd_ref, ys_hbm,
                       xbuf, ybuf, wgu_f32, wd_f32, wgu_bf, wd_bf, slot_ref, sems, xsems, ysems)
        return 0

    lax.fori_loop(0, nb, block, 0)


def _experts_block(j, nb, blk_e_ref, nused_ref, ends_ref, cnt_ref, xs_hbm, wgu_hbm, bgu_ref, wd_hbm, bd_ref, ys_hbm,
                   xbuf, ybuf, wgu_f32, wd_f32, wgu_bf, wd_bf, slot_ref, sems, xsems, ysems):
    par = lax.rem(j, 2)
    n_used = nused_ref[0]
    active = j < n_used
    e = blk_e_ref[j]
    prev = blk_e_ref[jnp.maximum(j - 1, 0)]
    fresh = (j == 0) | (e != prev)

    def fetch(expert, slot):
        return (pltpu.make_async_copy(wgu_hbm.at[expert], wgu_f32.at[slot], sems.at[slot, 0]),
                pltpu.make_async_copy(wd_hbm.at[expert], wd_f32.at[slot], sems.at[slot, 1]))

    def x_load(blk, half):
        rows = pl.ds(pl.multiple_of(blk * MOE_BLK, MOE_BLK), MOE_BLK)
        return [pltpu.make_async_copy(xs_hbm.at[rows, c, :], xbuf.at[half, :, pl.ds(c * LANES, LANES)], xsems.at[half])
                for c in range(SLAB_ROWS)]

    def y_store(blk, half):
        rows = pl.ds(pl.multiple_of(blk * MOE_BLK, MOE_BLK), MOE_BLK)
        return [pltpu.make_async_copy(ybuf.at[half, :, pl.ds(c * LANES, LANES)], ys_hbm.at[rows, c, :], ysems.at[half])
                for c in range(SLAB_ROWS)]

    @pl.when(j == 0)
    def _():
        ybuf[...] = jnp.zeros_like(ybuf)
        for cp in x_load(0, 0):
            cp.start()

    @pl.when(j + 1 < n_used)
    def _():
        for cp in x_load(j + 1, 1 - par):
            cp.start()

    @pl.when(j >= 2)
    def _():
        for cp in y_store(j - 2, par):
            cp.wait()

    @pl.when(j == 0)
    def _():
        slot_ref[0] = 0
        for cp in fetch(e, 0):
            cp.start()

    @pl.when(active & fresh)
    def _():
        slot = slot_ref[0]
        for cp in fetch(e, slot):
            cp.wait()
        step = 256
        for r in range(0, D_MODEL, step):
            wgu_bf[r:r + step, :] = wgu_f32[slot, r:r + step, :].astype(BF16)
        for r in range(0, D_FF, step):
            wd_bf[r:r + step, :] = wd_f32[slot, r:r + step, :].astype(BF16)
        next_blk = ends_ref[e]

        @pl.when(next_blk < n_used)
        def _():
            for cp in fetch(blk_e_ref[jnp.minimum(next_blk, n_used - 1)], 1 - slot):
                cp.start()

        slot_ref[0] = 1 - slot

    @pl.when(active)
    def _():
        for cp in x_load(j, par):
            cp.wait()

    first_blk = jnp.where(e == 0, 0, ends_ref[jnp.maximum(e - 1, 0)])
    valid = jnp.clip(cnt_ref[e] - (j - first_blk) * MOE_BLK, 0, MOE_BLK)
    groups = lax.shift_right_logical(valid + (EXPERT_ROW_GROUP - 1), EXPERT_ROW_GROUP.bit_length() - 1)

    def compute(m):
        x = xbuf[par, 0:m].astype(BF16)
        gu = jnp.dot(x, wgu_bf[...], preferred_element_type=F32) + bgu_ref[e]
        gate = jnp.minimum(gu[:, :D_FF], SWIGLU_LIMIT)
        up = jnp.clip(gu[:, D_FF:], -SWIGLU_LIMIT, SWIGLU_LIMIT)
        glu = gate * jax.nn.sigmoid(SWIGLU_ALPHA * gate)
        act = ((up + 1.0) * glu).astype(BF16)
        ybuf[par, 0:m] = jnp.dot(act, wd_bf[...], preferred_element_type=F32) + bd_ref[e]

    for g in range(1, MOE_BLK // EXPERT_ROW_GROUP + 1):
        pl.when(active & (groups == g))(functools.partial(compute, g * EXPERT_ROW_GROUP))

    @pl.when(jnp.logical_not(active))
    def _():
        ybuf[par] = jnp.zeros((MOE_BLK, D_MODEL), F32)

    for cp in y_store(j, par):
        cp.start()

    @pl.when(j == nb - 1)
    def _():
        for cp in y_store(j, par) + y_store(j - 1, 1 - par):
            cp.wait()


def _experts(blk_e, nused, ends, cnt, xs, w_gate_up, b_gate_up, w_down, b_down, nb):
    assert nb >= 2
    whole = lambda a: pl.BlockSpec(a.shape, lambda i, be, nu, en, cn: (0,) * a.ndim)
    grid_spec = pltpu.PrefetchScalarGridSpec(
        num_scalar_prefetch=4,
        grid=(1,),
        in_specs=[pl.BlockSpec(memory_space=pl.ANY),
                  pl.BlockSpec(memory_space=pl.ANY),
                  whole(b_gate_up),
                  pl.BlockSpec(memory_space=pl.ANY),
                  whole(b_down)],
        out_specs=pl.BlockSpec(memory_space=pl.ANY),
        scratch_shapes=[pltpu.VMEM((2, MOE_BLK, D_MODEL), F32), pltpu.VMEM((2, MOE_BLK, D_MODEL), F32),
                        pltpu.VMEM((2, D_MODEL, 2 * D_FF), F32), pltpu.VMEM((2, D_FF, D_MODEL), F32),
                        pltpu.VMEM((D_MODEL, 2 * D_FF), BF16), pltpu.VMEM((D_FF, D_MODEL), BF16),
                        pltpu.SMEM((1,), I32), pltpu.SemaphoreType.DMA((2, 2)),
                        pltpu.SemaphoreType.DMA((2,)), pltpu.SemaphoreType.DMA((2,))],
    )
    return pl.pallas_call(
        _experts_kernel,
        out_shape=jax.ShapeDtypeStruct((nb * MOE_BLK, SLAB_ROWS, LANES), F32),
        grid_spec=grid_spec,
        compiler_params=_cparams(("arbitrary",), 56),
        name="experts",
    )(blk_e, nused, ends, cnt, xs, w_gate_up, b_gate_up, w_down, b_down)


def _combine_kernel(dest_ref, dest_next_ref, gate_ref, h2_ref, g_ref, b_ref, ys_ref, o_ref, buf_ref, ffn_ref, sems):
    tc = h2_ref.shape[0]
    i = pl.program_id(0)
    slot = lax.rem(i, 2)

    def gather(d_ref, s):
        def start_rows(t, _):
            for k in range(TOP_K):
                pltpu.make_async_copy(ys_ref.at[d_ref[k, t]], buf_ref.at[s, k, t], sems.at[s]).start(priority=k % 2)
            return 0

        lax.fori_loop(0, tc, start_rows, 0, unroll=ROW_DMA_UNROLL)

    @pl.when(i == 0)
    def _():
        gather(dest_ref, 0)

    @pl.when(i + 1 < pl.num_programs(0))
    def _():
        gather(dest_next_ref, 1 - slot)

    for k in range(TOP_K):
        pltpu.make_async_copy(ys_ref.at[pl.ds(0, tc)], buf_ref.at[slot, k], sems.at[slot]).wait()

    def weigh(t, _):
        acc = gate_ref[0, t] * buf_ref[slot, 0, t]
        for k in range(1, TOP_K):
            acc = acc + gate_ref[k, t] * buf_ref[slot, k, t]
        ffn_ref[t] = acc
        return 0

    lax.fori_loop(0, tc, weigh, 0, unroll=ROW_DMA_UNROLL)
    ffn = jnp.concatenate([ffn_ref[:, c, :] for c in range(SLAB_ROWS)], axis=1)
    o_ref[...] = _layer_norm(DEEPNORM_ALPHA * h2_ref[...] + ffn, g_ref[...], b_ref[...])


def _combine(dest_t, gate_t, h2, g, b, ys):
    t = h2.shape[0]
    tc = COMBINE_TOK
    last = t // tc - 1
    return pl.pallas_call(
        _combine_kernel,
        out_shape=jax.ShapeDtypeStruct((t, D_MODEL), F32),
        grid=(t // tc,),
        in_specs=[pl.BlockSpec((SUBLANES, tc), lambda i: (0, i), memory_space=pltpu.SMEM),
                  pl.BlockSpec((SUBLANES, tc), lambda i: (0, jnp.minimum(i + 1, last)), memory_space=pltpu.SMEM),
                  pl.BlockSpec((SUBLANES, tc), lambda i: (0, i), memory_space=pltpu.SMEM),
                  pl.BlockSpec((tc, D_MODEL), lambda i: (i, 0)),
                  pl.BlockSpec((1, D_MODEL), lambda i: (0, 0)),
                  pl.BlockSpec((1, D_MODEL), lambda i: (0, 0)),
                  pl.BlockSpec(memory_space=pl.ANY)],
        out_specs=pl.BlockSpec((tc, D_MODEL), lambda i: (i, 0)),
        scratch_shapes=[pltpu.VMEM((2, TOP_K, tc, SLAB_ROWS, LANES), F32), pltpu.VMEM((tc, SLAB_ROWS, LANES), F32),
                        pltpu.SemaphoreType.DMA((2,))],
        compiler_params=_cparams(("arbitrary",), 32),
        name="combine",
    )(dest_t, dest_t, gate_t, h2, g, b, ys)


def kernel(x, ln_in_g, ln_in_b, w_in, w_gla_gate_up, b_gla_gate, g_gla_norm, b_forget, w_gla_proj, w_fox_proj, w_out, ln_mix_g, ln_mix_b, w_router, b_router, w_gate_up, b_gate_up, w_down, b_down, ln_ffn_g, ln_ffn_b):
    batch, seq, d = x.shape
    assert d == D_MODEL and w_in.shape[0] == DEPTH == 1
    assert seq % max(GLA_ROWS, FOX_TQ, FOX_TK, CUM_ROWS) == 0
    t = batch * seq
    assert t % max(TM_PROJ, RANK_TW, DISPATCH_TOK, COMBINE_TOK) == 0 and t // MOE_BLK <= 256
    row = lambda v: v.reshape(1, -1).astype(F32)

    n_forget = FOX_HEADS * FORGET_PIECES
    lane = jnp.arange(LANES)
    w_wide, w_small = _regroup_w_in(jnp.transpose(w_in[0]))
    in_forget = (lane >= FORGET_COL0) & (lane < FORGET_COL0 + n_forget)
    piece = (lane - FORGET_COL0) % FORGET_PIECES
    forget_bias = jnp.zeros((1, LANES), F32).at[0, FORGET_COL0:FORGET_COL0 + n_forget].set(
        jnp.repeat(b_forget[0].astype(F32), FORGET_PIECES))
    piece_sel = jnp.stack([(in_forget & (piece == p)).astype(F32) for p in range(FORGET_PIECES)]
                          + [jnp.zeros((LANES,), F32)] * (SUBLANES - FORGET_PIECES))
    head_of_lane = (lane - FORGET_COL0) // FORGET_PIECES
    qsel = jnp.broadcast_to(jnp.stack([(in_forget & (head_of_lane == h)) for h in range(FOX_HEADS)])[:, :, None],
                            (FOX_HEADS, LANES, LANES)).astype(BF16)
    wgu_pad = jnp.concatenate([w_gla_gate_up[0], jnp.zeros((LANES - GLA_RANK, GLA_QK), F32)], axis=0).astype(BF16)

    x2 = x.reshape(t, d)
    gq, gk, gv, gr, fq, fk, fv, zg, zf, small = _ln_inproj(x2, row(ln_in_g), row(ln_in_b), w_wide, w_small)
    kf = _forget_cum(small, forget_bias, piece_sel, batch, seq)
    gla_o = _gla(gq, gk, gv, gr, small, wgu_pad, row(b_gla_gate[0]), row(g_gla_norm[0]), batch, seq)
    fox_o = _fox(fq, fk, fv, kf, qsel, batch, seq)

    br_rep = jnp.broadcast_to(b_router[0].astype(F32)[:, None], (N_EXPERTS, LANES))
    h2, idx_t, gate_t = _post_mix(
        gla_o, fox_o, zg, zf, x2, row(ln_in_g), row(ln_in_b),
        w_gla_proj[0].astype(BF16), w_fox_proj[0].astype(BF16), w_out[0].astype(BF16),
        row(ln_mix_g[0]), row(ln_mix_b[0]), w_router[0].T.astype(BF16), br_rep)

    nb = (t * TOP_K + N_EXPERTS * (MOE_BLK - 1) + MOE_BLK - 1) // MOE_BLK
    nb_pad = (nb + LANES - 1) // LANES * LANES
    rank_t, counts = _rank(idx_t)
    dest_t, blk_e8, ends = _dest(idx_t, rank_t, counts, nb_pad)
    ends1 = ends[:, 0]
    xs = _dispatch(ends1, dest_t, h2, nb)
    ys = _experts(blk_e8[0, :nb], ends1[N_EXPERTS - 1:], ends1, counts[:, 0].astype(I32), xs, w_gate_up[0], b_gate_up[0][:, None, :],
                  w_down[0], b_down[0][:, None, :], nb)
    out = _combine(dest_t, gate_t, h2, row(ln_ffn_g[0]), row(ln_ffn_b[0]), ys)
    return out.reshape(batch, seq, d)
```

```python
import functools

import jax
import jax.numpy as jnp
from jax import lax
from jax.experimental import pallas as pl
from jax.experimental.pallas import tpu as pltpu

F32 = jnp.float32
BF16 = jnp.bfloat16
I32 = jnp.int32

D_MODEL = 1024
CHUNK = 64
GLA_HEADS, GLA_DK, GLA_DV, GLA_RANK, GLA_TAU = 4, 128, 256, 16, 16.0
GLA_QK = GLA_HEADS * GLA_DK
GLA_V = GLA_HEADS * GLA_DV
FOX_HEADS, FOX_HD = 8, 128
FOX_W = FOX_HEADS * FOX_HD
N_EXPERTS, TOP_K, D_FF = 32, 4, 1024
SWIGLU_LIMIT, SWIGLU_ALPHA = 7.0, 1.702
LN_EPS = 1e-5
DEPTH = 1
DEEPNORM_ALPHA = (2 * DEPTH) ** 0.25
IN_SPLITS = (GLA_QK, GLA_QK, GLA_V, GLA_RANK, GLA_V, FOX_W, FOX_W, FOX_W, FOX_HEADS, D_MODEL, D_MODEL)

LANES = 128
SUBLANES = 8
VMEM_BYTES_V7X = 64 * 1024 * 1024
SLAB_ROWS = D_MODEL // LANES

FORGET_COL0 = GLA_RANK
FORGET_PIECES = 3

TM_PROJ = 512
GLA_ROWS = 512
FOX_TQ = 512
FOX_HEADS_PER_STEP = 2
LOG2E = 1.4426950408889634
CUM_ROWS = 256
RANK_TW = 512
MOE_BLK = 512
EXPERT_ROW_GROUP = 128
DISPATCH_TOK = 256
COMBINE_TOK = 256
ROW_DMA_UNROLL = 8


def _cparams(sem, vmem_mib):
    limit = vmem_mib * 1024 * 1024
    assert limit < VMEM_BYTES_V7X
    return pltpu.CompilerParams(dimension_semantics=sem, vmem_limit_bytes=limit)


def _log_sigmoid(z):
    return jnp.minimum(z, 0.0) - jnp.log1p(jnp.exp(-jnp.abs(z)))


def _layer_norm(x, g, b):
    mu = jnp.mean(x, axis=-1, keepdims=True)
    xc = x - mu
    var = jnp.mean(xc * xc, axis=-1, keepdims=True)
    return xc * lax.rsqrt(var + LN_EPS) * g + b


def _split_bf16(x, pieces):
    out = []
    for _ in range(pieces):
        p = x.astype(BF16)
        out.append(p)
        x = x - p.astype(F32)
    return out


_NARROW_A = sum(IN_SPLITS[:3])
_NARROW_B = sum(IN_SPLITS[:8])


_REGROUP_COLS = 512


def _regroup_kernel(wt_hbm, wide_ref, small_ref, buf, nbuf, sems, nsem):
    j = pl.program_id(0)
    cols = _REGROUP_COLS
    half = lax.rem(j, 2)

    def load(step, s):
        c0 = step * cols
        src = c0 + jnp.where(c0 >= _NARROW_A, GLA_RANK, 0) + jnp.where(c0 >= _NARROW_B - GLA_RANK, FOX_HEADS, 0)
        return pltpu.make_async_copy(wt_hbm.at[pl.ds(pl.multiple_of(src, SUBLANES), cols), :], buf.at[s], sems.at[s])

    @pl.when(j == 0)
    def _():
        load(0, 0).start()
        nbuf[...] = jnp.zeros_like(nbuf)
        copies = [pltpu.make_async_copy(wt_hbm.at[pl.ds(_NARROW_A, GLA_RANK), :], nbuf.at[pl.ds(0, GLA_RANK), :], nsem)]
        for h in range(FOX_HEADS):
            for p in range(FORGET_PIECES):
                copies.append(pltpu.make_async_copy(
                    wt_hbm.at[pl.ds(_NARROW_B + h, 1), :],
                    nbuf.at[pl.ds(FORGET_COL0 + FORGET_PIECES * h + p, 1), :], nsem))
        for cp in copies:
            cp.start()
        for cp in copies:
            cp.wait()
        small_ref[...] = nbuf[...].T.astype(small_ref.dtype)

    @pl.when(j + 1 < pl.num_programs(0))
    def _():
        load(j + 1, 1 - half).start()

    load(j, half).wait()
    wide_ref[...] = buf[half].T.astype(wide_ref.dtype)


def _regroup_w_in(wt):
    n_wide = wt.shape[0] - GLA_RANK - FOX_HEADS
    assert _NARROW_A % _REGROUP_COLS == 0 and (_NARROW_B - GLA_RANK) % _REGROUP_COLS == 0
    return pl.pallas_call(
        _regroup_kernel,
        out_shape=[jax.ShapeDtypeStruct((D_MODEL, n_wide), BF16), jax.ShapeDtypeStruct((D_MODEL, LANES), BF16)],
        grid=(n_wide // _REGROUP_COLS,),
        in_specs=[pl.BlockSpec(memory_space=pl.ANY)],
        out_specs=[pl.BlockSpec((D_MODEL, _REGROUP_COLS), lambda j: (0, j)),
                   pl.BlockSpec((D_MODEL, LANES), lambda j: (0, 0))],
        scratch_shapes=[pltpu.VMEM((2, _REGROUP_COLS, D_MODEL), F32), pltpu.VMEM((LANES, D_MODEL), F32),
                        pltpu.SemaphoreType.DMA((2,)), pltpu.SemaphoreType.DMA(())],
        compiler_params=_cparams(("arbitrary",), 32),
        name="regroup_w_in",
    )(wt)


_PROJ_WIDTHS = (GLA_QK, GLA_QK, GLA_V, GLA_V, FOX_W, FOX_W, FOX_W, D_MODEL, D_MODEL)


def _ln_inproj_kernel(x0_ref, xn_ref, g_ref, b_ref, w_ref, ws_ref, *refs):
    wide_refs, small_ref, hb_refs = refs[:-3], refs[-3], refs[-2:]
    i = pl.program_id(0)

    @pl.when(i == 0)
    def _():
        hb_refs[0][...] = _layer_norm(x0_ref[...], g_ref[...], b_ref[...]).astype(BF16)

    def project(cur_ref, nxt_ref):
        hb = cur_ref[...]
        off = 0
        for o_ref in wide_refs:
            n = o_ref.shape[1]
            o_ref[...] = jnp.dot(hb, w_ref[:, off:off + n], preferred_element_type=F32).astype(o_ref.dtype)
            off += n
        small_ref[...] = jnp.dot(hb, ws_ref[...], preferred_element_type=F32)
        nxt_ref[...] = _layer_norm(xn_ref[...], g_ref[...], b_ref[...]).astype(BF16)

    for parity in range(2):
        pl.when(lax.rem(i, 2) == parity)(functools.partial(project, hb_refs[parity], hb_refs[1 - parity]))


def _ln_inproj(x2, g, b, w_wide, w_small):
    t = x2.shape[0]
    tm = TM_PROJ
    last = t // tm - 1
    row = lambda w: pl.BlockSpec((tm, w), lambda i: (i, 0))
    const = lambda a: pl.BlockSpec(a.shape, lambda i: (0, 0), pipeline_mode=pl.Buffered(1))
    out_shape = ([jax.ShapeDtypeStruct((t, w), BF16) for w in _PROJ_WIDTHS]
                 + [jax.ShapeDtypeStruct((t, LANES), F32)])
    out_specs = [row(w) for w in _PROJ_WIDTHS] + [row(LANES)]
    return pl.pallas_call(
        _ln_inproj_kernel,
        out_shape=out_shape,
        grid=(t // tm,),
        in_specs=[pl.BlockSpec((tm, D_MODEL), lambda i: (0, 0), pipeline_mode=pl.Buffered(1)),
                  pl.BlockSpec((tm, D_MODEL), lambda i: (jnp.minimum(i + 1, last), 0)),
                  const(g), const(b), const(w_wide), const(w_small)],
        out_specs=out_specs,
        scratch_shapes=[pltpu.VMEM((tm, D_MODEL), BF16), pltpu.VMEM((tm, D_MODEL), BF16)],
        compiler_params=_cparams(("arbitrary",), 52),
        name="ln_inproj",
    )(x2, x2, g, b, w_wide, w_small)


def _forget_cum_kernel(small_ref, bias_ref, sel_ref, kf_ref):
    s = small_ref.shape[0]
    r = CUM_ROWS
    ri = lax.broadcasted_iota(I32, (r, r), 0)
    ci = lax.broadcasted_iota(I32, (r, r), 1)
    tri = jnp.where(ci <= ri, 1.0, 0.0).astype(BF16)
    sel = sel_ref[...]
    carry = jnp.zeros((1, LANES), F32)
    for blk in range(s // r):
        rows = pl.ds(blk * r, r)
        ls = _log_sigmoid(small_ref[rows, :] + bias_ref[...])
        cum = carry
        for p in _split_bf16(ls, 3):
            cum = cum + jnp.dot(tri, p, preferred_element_type=F32)
        carry = cum[r - 1:r, :]
        neg = cum * (-LOG2E)
        p1, p2, p3 = _split_bf16(neg, FORGET_PIECES)
        kf = (p1.astype(F32) * sel[0:1, :] + p2.astype(F32) * sel[1:2, :] + p3.astype(F32) * sel[2:3, :])
        kf_ref[rows, :] = kf.astype(BF16)


def _forget_cum(small, bias_row, sel, batch, seq):
    return pl.pallas_call(
        _forget_cum_kernel,
        out_shape=jax.ShapeDtypeStruct((batch * seq, LANES), BF16),
        grid=(batch,),
        in_specs=[pl.BlockSpec((seq, LANES), lambda b: (b, 0)),
                  pl.BlockSpec((1, LANES), lambda b: (0, 0)),
                  pl.BlockSpec((SUBLANES, LANES), lambda b: (0, 0))],
        out_specs=pl.BlockSpec((seq, LANES), lambda b: (b, 0)),
        compiler_params=_cparams(("parallel",), 32),
        name="forget_cum",
    )(small, bias_row, sel)


def _gla_kernel(q_ref, k_ref, v_ref, r_ref, small_ref, wgu_ref, bg_ref, gn_ref, o_ref,
                state_ref, ds_ref, sb_ref, of_ref):
    rows = q_ref.shape[0]
    nchunk = rows // CHUNK
    kcol = lambda h: slice(h * GLA_DK, (h + 1) * GLA_DK)
    vcol = lambda h: slice(h * GLA_DV, (h + 1) * GLA_DV)
    crow = lambda c: slice(c * CHUNK, (c + 1) * CHUNK)

    @pl.when(pl.program_id(1) == 0)
    def _():
        state_ref[...] = jnp.zeros_like(state_ref)

    z = jnp.dot(small_ref[...].astype(BF16), wgu_ref[...], preferred_element_type=F32) + bg_ref[...]
    la = _log_sigmoid(z) * (1.0 / GLA_TAU)
    cb = CUM_ROWS
    ri = lax.broadcasted_iota(I32, (cb, cb), 0)
    ci = lax.broadcasted_iota(I32, (cb, cb), 1)
    shift = CHUNK.bit_length() - 1
    same = lax.shift_right_logical(ri, shift) == lax.shift_right_logical(ci, shift)
    tri = jnp.where(same & (ci <= ri), 1.0, 0.0).astype(BF16)
    pieces = _split_bf16(la, 2)
    cum = jnp.concatenate(
        [sum(jnp.dot(tri, p[r:r + cb], preferred_element_type=F32) for p in pieces) for r in range(0, rows, cb)],
        axis=0)
    last = [cum[(c + 1) * CHUNK - 1:(c + 1) * CHUNK, :] for c in range(nchunk)]
    tot = jnp.concatenate([jnp.broadcast_to(t, (CHUNK, GLA_QK)) for t in last], axis=0)
    kd = (k_ref[...].astype(F32) * jnp.exp(tot - cum)).astype(BF16)
    dec_rows = jnp.exp(jnp.concatenate(last + [jnp.zeros((LANES - nchunk, GLA_QK), F32)], axis=0))
    dec_cols = dec_rows.T

    for c in range(nchunk):
        for h in range(GLA_HEADS):
            ds_ref[c, h] = lax.dot_general(kd[crow(c), kcol(h)], v_ref[crow(c), vcol(h)],
                                           (((0,), (0,)), ((), ())), preferred_element_type=F32)
    for h in range(GLA_HEADS):
        st = state_ref[h]
        for c in range(nchunk):
            st = st * jnp.broadcast_to(dec_cols[kcol(h), c:c + 1], (GLA_DK, GLA_DV)) + ds_ref[c, h]
            sb_ref[c, h] = st.astype(BF16)
        state_ref[h] = st
    scale = float(GLA_DK) ** -0.5
    for c in range(nchunk):
        for h in range(GLA_HEADS):
            of_ref[crow(c), vcol(h)] = jnp.dot(q_ref[crow(c), kcol(h)], sb_ref[c, h],
                                               preferred_element_type=F32) * scale
    for h in range(GLA_HEADS):
        o = of_ref[:, vcol(h)]
        o = o * lax.rsqrt(jnp.mean(o * o, axis=-1, keepdims=True) + LN_EPS) * gn_ref[...]
        r = r_ref[:, vcol(h)].astype(F32)
        o_ref[:, vcol(h)] = (o * (r * jax.nn.sigmoid(r))).astype(o_ref.dtype)


def _gla(gq, gk, gv, gr, small, wgu_pad, b_gate, g_norm, batch, seq):
    rows = GLA_ROWS
    nsteps = seq // rows
    row = lambda w: pl.BlockSpec((rows, w), lambda b, i: (b * nsteps + i, 0))
    const = lambda a: pl.BlockSpec(a.shape, lambda b, i: (0, 0))
    return pl.pallas_call(
        _gla_kernel,
        out_shape=jax.ShapeDtypeStruct((batch * seq, GLA_V), BF16),
        grid=(batch, nsteps),
        in_specs=[row(GLA_QK), row(GLA_QK), row(GLA_V), row(GLA_V), row(LANES),
                  const(wgu_pad), const(b_gate), const(g_norm)],
        out_specs=row(GLA_V),
        scratch_shapes=[pltpu.VMEM((GLA_HEADS, GLA_DK, GLA_DV), F32),
                        pltpu.VMEM((rows // CHUNK, GLA_HEADS, GLA_DK, GLA_DV), F32),
                        pltpu.VMEM((rows // CHUNK, GLA_HEADS, GLA_DK, GLA_DV), BF16),
                        pltpu.VMEM((rows, GLA_V), F32)],
        compiler_params=_cparams(("parallel", "arbitrary"), 40),
        name="gla",
    )(gq, gk, gv, gr, small, wgu_pad, b_gate, g_norm)


def _fox_kernel(q_ref, k_ref, v_ref, kf_ref, qsel_ref, o_ref, qa_ref, vt_ref, s_ref, acc_ref, m_ref, l_ref):
    t = FOX_TQ
    seq = q_ref.shape[0]
    nq = seq // t
    c = (float(FOX_HD) ** -0.5) * LOG2E
    hd = FOX_HD
    heads = range(FOX_HEADS_PER_STEP)
    for h in heads:
        for r in range(0, seq, t):
            qa_ref[h, 0:hd, r:r + t] = (q_ref[r:r + t, h * hd:(h + 1) * hd].astype(F32) * c).T.astype(BF16)
            vt_ref[h, :, r:r + t] = v_ref[r:r + t, h * hd:(h + 1) * hd].astype(F32).T.astype(BF16)
        qa_ref[h, hd:hd + LANES, :] = jnp.tile(qsel_ref[h], (1, seq // LANES))

    def key_rows(j):
        return pl.ds(pl.multiple_of(j * t, t), t)

    def scores(h, i, j):
        ka = jnp.concatenate([k_ref[key_rows(j), h * hd:(h + 1) * hd], kf_ref[key_rows(j), :]], axis=1)
        return jnp.dot(ka, qa_ref[h, :, i * t:(i + 1) * t], preferred_element_type=F32)

    def values(h, j, p):
        return jnp.dot(vt_ref[h, :, key_rows(j)], p, preferred_element_type=F32)

    def step(j, slot, masked, nxt):
        if nxt is not None:
            for h in heads:
                s_ref[1 - slot, h] = scores(h, *nxt)
        for h in heads:
            s = s_ref[slot, h]
            if masked:
                causal = lax.broadcasted_iota(I32, (t, t), 0) <= lax.broadcasted_iota(I32, (t, t), 1)
                s = jnp.where(causal, s, -jnp.inf)
            m_old = m_ref[h]
            m_new = jnp.maximum(m_old, jnp.broadcast_to(jnp.max(s, axis=0, keepdims=True), (SUBLANES, t)))
            alpha = jnp.exp2(m_old - m_new)
            p = jnp.exp2(s - m_new[0:1, :])
            m_ref[h] = m_new
            l_ref[h] = alpha * l_ref[h] + jnp.sum(p, axis=0, keepdims=True)
            acc_ref[h] = alpha[0:1, :] * acc_ref[h] + values(h, j, p.astype(BF16))

    for h in heads:
        s_ref[0, h] = scores(h, 0, 0)
    slot = 0
    for i in range(nq):
        for h in heads:
            acc_ref[h] = jnp.zeros((hd, t), F32)
            m_ref[h] = jnp.full((SUBLANES, t), -jnp.inf, F32)
            l_ref[h] = jnp.zeros((SUBLANES, t), F32)
        if i >= 2:
            def pair(jj, _, i=i, slot=slot):
                step(2 * jj, slot, False, (i, 2 * jj + 1))
                step(2 * jj + 1, 1 - slot, False, (i, 2 * jj + 2))
                return 0

            lax.fori_loop(0, i // 2, pair, 0)
        if i % 2 == 1:
            step(i - 1, slot, False, (i, i))
            slot = 1 - slot
        step(i, slot, True, (i + 1, 0) if i + 1 < nq else None)
        slot = 1 - slot
        for h in heads:
            o_ref[i * t:(i + 1) * t, h * hd:(h + 1) * hd] = (acc_ref[h] / l_ref[h][0:1, :]).T.astype(o_ref.dtype)


def _fox(fq, fk, fv, kf, qsel, batch, seq):
    hps = FOX_HEADS_PER_STEP
    w = hps * FOX_HD
    t = FOX_TQ
    head_cols = pl.BlockSpec((seq, w), lambda b, h: (b, h))
    return pl.pallas_call(
        _fox_kernel,
        out_shape=jax.ShapeDtypeStruct((batch * seq, FOX_W), BF16),
        grid=(batch, FOX_HEADS // hps),
        in_specs=[head_cols, head_cols, head_cols,
                  pl.BlockSpec((seq, LANES), lambda b, h: (b, 0)),
                  pl.BlockSpec((hps, LANES, LANES), lambda b, h: (h, 0, 0))],
        out_specs=head_cols,
        scratch_shapes=[pltpu.VMEM((hps, FOX_HD + LANES, seq), BF16), pltpu.VMEM((hps, FOX_HD, seq), BF16),
                        pltpu.VMEM((2, hps, t, t), F32), pltpu.VMEM((hps, FOX_HD, t), F32),
                        pltpu.VMEM((hps, SUBLANES, t), F32), pltpu.VMEM((hps, SUBLANES, t), F32)],
        compiler_params=_cparams(("parallel", "parallel"), 48),
        name="fox",
    )(fq, fk, fv, kf, qsel)


def _post_mix_kernel(gla_ref, fox_ref, zg_ref, zf_ref, x_ref, gin_ref, bin_ref, wg_ref, wf_ref, wo_ref, g_ref, b_ref,
                     wr_ref, br_ref, h2_ref, idx_ref, gate_ref):
    tm = x_ref.shape[0]
    y_gla = jnp.dot(gla_ref[...], wg_ref[...], preferred_element_type=F32)
    y_fox = jnp.dot(fox_ref[...], wf_ref[...], preferred_element_type=F32)
    mixed = (jax.nn.sigmoid(zg_ref[...].astype(F32)) * y_gla
             + jax.nn.sigmoid(zf_ref[...].astype(F32)) * y_fox)
    h = _layer_norm(x_ref[...], gin_ref[...], bin_ref[...])
    pre = DEEPNORM_ALPHA * h + jnp.dot(mixed.astype(BF16), wo_ref[...], preferred_element_type=F32)
    h2 = _layer_norm(pre, g_ref[...], b_ref[...])
    h2_ref[...] = h2

    logits = lax.dot_general(wr_ref[...], h2.astype(BF16), (((1,), (1,)), ((), ())),
                             preferred_element_type=F32)
    logits = logits + jnp.tile(br_ref[...], (1, tm // LANES))
    eidx = lax.broadcasted_iota(I32, (N_EXPERTS, tm), 0)
    vals, idxs = [], []
    for _ in range(TOP_K):
        mx = jnp.max(logits, axis=0, keepdims=True)
        ik = jnp.min(jnp.where(logits == mx, eidx, N_EXPERTS), axis=0, keepdims=True)
        vals.append(mx)
        idxs.append(ik)
        logits = jnp.where(eidx == ik, -jnp.inf, logits)
    exps = [jnp.exp(v - vals[0]) for v in vals]
    denom = exps[0] + exps[1] + exps[2] + exps[3]
    rid = lax.broadcasted_iota(I32, (SUBLANES, tm), 0)
    idx8 = jnp.zeros((SUBLANES, tm), I32)
    gate8 = jnp.zeros((SUBLANES, tm), F32)
    for k in range(TOP_K):
        idx8 = jnp.where(rid == k, idxs[k], idx8)
        gate8 = jnp.where(rid == k, exps[k] / denom, gate8)
    idx_ref[...] = idx8
    gate_ref[...] = gate8


def _post_mix(gla_o, fox_o, zg, zf, x2, g_in, b_in, wg, wf, wo, g, b, wr_t, br_rep):
    t = x2.shape[0]
    tm = TM_PROJ
    row = lambda w: pl.BlockSpec((tm, w), lambda i: (i, 0))
    const = lambda a: pl.BlockSpec(a.shape, lambda i: (0, 0))
    return pl.pallas_call(
        _post_mix_kernel,
        out_shape=[jax.ShapeDtypeStruct((t, D_MODEL), F32),
                   jax.ShapeDtypeStruct((SUBLANES, t), I32),
                   jax.ShapeDtypeStruct((SUBLANES, t), F32)],
        grid=(t // tm,),
        in_specs=[row(GLA_V), row(FOX_W), row(D_MODEL), row(D_MODEL), row(D_MODEL), const(g_in), const(b_in),
                  const(wg), const(wf), const(wo), const(g), const(b), const(wr_t), const(br_rep)],
        out_specs=[row(D_MODEL),
                   pl.BlockSpec((SUBLANES, tm), lambda i: (0, i)), pl.BlockSpec((SUBLANES, tm), lambda i: (0, i))],
        compiler_params=_cparams(("parallel",), 48),
        name="post_mix",
    )(gla_o, fox_o, zg, zf, x2, g_in, b_in, wg, wf, wo, g, b, wr_t, br_rep)


def _onehot_rows(idx8, tw):
    eidx = lax.broadcasted_iota(I32, (N_EXPERTS, tw), 0)
    hit = eidx == idx8[0:1, :]
    for k in range(1, TOP_K):
        hit = hit | (eidx == idx8[k:k + 1, :])
    return eidx, hit


def _rank_kernel(idx_ref, rank_ref, counts_ref, carry_ref):
    tw = idx_ref.shape[1]

    @pl.when(pl.program_id(0) == 0)
    def _():
        carry_ref[...] = jnp.zeros_like(carry_ref)

    idx8 = idx_ref[...]
    eidx, hit = _onehot_rows(idx8, tw)
    onehot = jnp.where(hit, 1.0, 0.0).astype(BF16)
    ri = lax.broadcasted_iota(I32, (tw, tw + LANES), 0)
    ci = lax.broadcasted_iota(I32, (tw, tw + LANES), 1)
    upper = jnp.where((ri < ci) | (ci >= tw), 1.0, 0.0).astype(BF16)
    cnt = jnp.dot(onehot, upper, preferred_element_type=F32)
    before = cnt[:, :tw] + jnp.tile(carry_ref[...], (1, tw // LANES))
    rid = lax.broadcasted_iota(I32, (SUBLANES, tw), 0)
    rank8 = jnp.zeros((SUBLANES, tw), I32)
    for k in range(TOP_K):
        rk = jnp.sum(jnp.where(eidx == idx8[k:k + 1, :], before, 0.0), axis=0, keepdims=True)
        rank8 = jnp.where(rid == k, rk.astype(I32), rank8)
    rank_ref[...] = rank8
    carry_ref[...] = carry_ref[...] + cnt[:, tw:]
    counts_ref[...] = carry_ref[...]


def _rank(idx_t):
    t = idx_t.shape[1]
    tw = RANK_TW
    return pl.pallas_call(
        _rank_kernel,
        out_shape=[jax.ShapeDtypeStruct((SUBLANES, t), I32),
                   jax.ShapeDtypeStruct((N_EXPERTS, LANES), F32)],
        grid=(t // tw,),
        in_specs=[pl.BlockSpec((SUBLANES, tw), lambda i: (0, i))],
        out_specs=[pl.BlockSpec((SUBLANES, tw), lambda i: (0, i)),
                   pl.BlockSpec((N_EXPERTS, LANES), lambda i: (0, 0))],
        scratch_shapes=[pltpu.VMEM((N_EXPERTS, LANES), F32)],
        compiler_params=_cparams(("arbitrary",), 32),
        name="rank",
    )(idx_t)


def _dest_kernel(idx_ref, rank_ref, counts_ref, dest_ref, blk_e_ref, ends_ref, *, nb_pad):
    tw = idx_ref.shape[1]
    nblk = jnp.floor((counts_ref[...] + (MOE_BLK - 1)) * (1.0 / MOE_BLK))
    ri = lax.broadcasted_iota(I32, (N_EXPERTS, N_EXPERTS), 0)
    ci = lax.broadcasted_iota(I32, (N_EXPERTS, N_EXPERTS), 1)
    tri = jnp.where(ci <= ri, 1.0, 0.0).astype(BF16)
    end_blk = jnp.dot(tri, nblk.astype(BF16), preferred_element_type=F32)
    start_row = (end_blk - nblk) * float(MOE_BLK)
    idx8 = idx_ref[...]
    eidx = lax.broadcasted_iota(I32, (N_EXPERTS, tw), 0)
    start_t = jnp.tile(start_row, (1, tw // LANES))
    rid = lax.broadcasted_iota(I32, (SUBLANES, tw), 0)
    dest8 = jnp.zeros((SUBLANES, tw), I32)
    for k in range(TOP_K):
        st = jnp.sum(jnp.where(eidx == idx8[k:k + 1, :], start_t, 0.0), axis=0, keepdims=True)
        dest8 = jnp.where(rid == k, st.astype(I32), dest8)
    dest_ref[...] = dest8 + rank_ref[...]
    bid = lax.broadcasted_iota(I32, (N_EXPERTS, nb_pad), 1).astype(F32)
    ends_t = jnp.tile(end_blk, (1, nb_pad // LANES))
    be = jnp.sum(jnp.where(ends_t <= bid, 1.0, 0.0), axis=0, keepdims=True)
    blk_e_ref[...] = jnp.broadcast_to(jnp.minimum(be, N_EXPERTS - 1.0), (SUBLANES, nb_pad)).astype(I32)
    ends_ref[...] = end_blk.astype(I32)


def _dest(idx_t, rank_t, counts, nb_pad):
    t = idx_t.shape[1]
    tw = RANK_TW
    tok = pl.BlockSpec((SUBLANES, tw), lambda i: (0, i))
    return pl.pallas_call(
        functools.partial(_dest_kernel, nb_pad=nb_pad),
        out_shape=[jax.ShapeDtypeStruct((SUBLANES, t), I32),
                   jax.ShapeDtypeStruct((SUBLANES, nb_pad), I32),
                   jax.ShapeDtypeStruct((N_EXPERTS, LANES), I32)],
        grid=(t // tw,),
        in_specs=[tok, tok, pl.BlockSpec((N_EXPERTS, LANES), lambda i: (0, 0))],
        out_specs=[tok, pl.BlockSpec((SUBLANES, nb_pad), lambda i: (0, 0)),
                   pl.BlockSpec((N_EXPERTS, LANES), lambda i: (0, 0))],
        compiler_params=_cparams(("arbitrary",), 32),
        name="dest",
    )(idx_t, rank_t, counts)


def _dispatch_kernel(ends_ref, dest_ref, h2_ref, xs_ref, slab_ref, zero_ref, sem, zsem):
    td = h2_ref.shape[0]

    nb = xs_ref.shape[0] // MOE_BLK
    n_used = ends_ref[N_EXPERTS - 1]

    def zero_block(blk):
        start = pl.multiple_of(blk * MOE_BLK, MOE_BLK)
        return pltpu.make_async_copy(zero_ref, xs_ref.at[pl.ds(start, MOE_BLK)], zsem)

    def has_rows(e):
        prev = jnp.where(e == 0, 0, ends_ref[jnp.maximum(e - 1, 0)])
        return ends_ref[e] > prev

    @pl.when(pl.program_id(0) == 0)
    def _():
        zero_ref[...] = jnp.zeros_like(zero_ref)

        def each(fn):
            def last_block(e, _):
                @pl.when(has_rows(e))
                def _():
                    fn(zero_block(ends_ref[e] - 1))
                return 0

            def tail_block(blk, _):
                fn(zero_block(blk))
                return 0

            lax.fori_loop(0, N_EXPERTS, last_block, 0)
            lax.fori_loop(n_used, nb, tail_block, 0)

        each(lambda cp: cp.start())
        each(lambda cp: cp.wait())

    for c in range(SLAB_ROWS):
        slab_ref[:, c, :] = h2_ref[:, c * LANES:(c + 1) * LANES]

    def start_rows(t, _):
        for k in range(TOP_K):
            pltpu.make_async_copy(slab_ref.at[t], xs_ref.at[dest_ref[k, t]], sem).start(priority=k % 2)
        return 0

    lax.fori_loop(0, td, start_rows, 0, unroll=ROW_DMA_UNROLL)
    for k in range(TOP_K):
        pltpu.make_async_copy(slab_ref, xs_ref.at[pl.ds(0, td)], sem).wait()


def _dispatch(ends, dest_t, h2, nb):
    t = h2.shape[0]
    td = DISPATCH_TOK
    grid_spec = pltpu.PrefetchScalarGridSpec(
        num_scalar_prefetch=1,
        grid=(t // td,),
        in_specs=[pl.BlockSpec((SUBLANES, td), lambda i, ends: (0, i), memory_space=pltpu.SMEM),
                  pl.BlockSpec((td, D_MODEL), lambda i, ends: (i, 0))],
        out_specs=pl.BlockSpec(memory_space=pl.ANY),
        scratch_shapes=[pltpu.VMEM((td, SLAB_ROWS, LANES), F32), pltpu.VMEM((MOE_BLK, SLAB_ROWS, LANES), F32),
                        pltpu.SemaphoreType.DMA(()), pltpu.SemaphoreType.DMA(())],
    )
    return pl.pallas_call(
        _dispatch_kernel,
        out_shape=jax.ShapeDtypeStruct((nb * MOE_BLK, SLAB_ROWS, LANES), F32),
        grid_spec=grid_spec,
        compiler_params=_cparams(("arbitrary",), 32),
        name="dispatch",
    )(ends, dest_t, h2)


def _experts_kernel(blk_e_ref, nused_ref, ends_ref, cnt_ref, xs_hbm, wgu_hbm, bgu_ref, wd_hbm, bd_ref, ys_hbm,
                    xbuf, ybuf, wgu_f32, wd_f32, wgu_bf, wd_bf, slot_ref, sems, xsems, ysems):
    nb = xs_hbm.shape[0] // MOE_BLK

    def block(j, _):
        _experts_block(j, nb, blk_e_ref, nused_ref, ends_ref, cnt_ref, xs_hbm, wgu_hbm, bgu_ref, wd_hbm, bd_ref, ys_hbm,
                       xbuf, ybuf, wgu_f32, wd_f32, wgu_bf, wd_bf, slot_ref, sems, xsems, ysems)
        return 0

    lax.fori_loop(0, nb, block, 0)


def _experts_block(j, nb, blk_e_ref, nused_ref, ends_ref, cnt_ref, xs_hbm, wgu_hbm, bgu_ref, wd_hbm, bd_ref, ys_hbm,
                   xbuf, ybuf, wgu_f32, wd_f32, wgu_bf, wd_bf, slot_ref, sems, xsems, ysems):
    par = lax.rem(j, 2)
    n_used = nused_ref[0]
    active = j < n_used
    e = blk_e_ref[j]
    prev = blk_e_ref[jnp.maximum(j - 1, 0)]
    fresh = (j == 0) | (e != prev)

    def fetch(expert, slot):
        return (pltpu.make_async_copy(wgu_hbm.at[expert], wgu_f32.at[slot], sems.at[slot, 0]),
                pltpu.make_async_copy(wd_hbm.at[expert], wd_f32.at[slot], sems.at[slot, 1]))

    def x_load(blk, half):
        rows = pl.ds(pl.multiple_of(blk * MOE_BLK, MOE_BLK), MOE_BLK)
        return [pltpu.make_async_copy(xs_hbm.at[rows, c, :], xbuf.at[half, :, pl.ds(c * LANES, LANES)], xsems.at[half])
                for c in range(SLAB_ROWS)]

    def y_store(blk, half):
        rows = pl.ds(pl.multiple_of(blk * MOE_BLK, MOE_BLK), MOE_BLK)
        return [pltpu.make_async_copy(ybuf.at[half, :, pl.ds(c * LANES, LANES)], ys_hbm.at[rows, c, :], ysems.at[half])
                for c in range(SLAB_ROWS)]

    @pl.when(j == 0)
    def _():
        ybuf[...] = jnp.zeros_like(ybuf)
        for cp in x_load(0, 0):
            cp.start()

    @pl.when(j + 1 < n_used)
    def _():
        for cp in x_load(j + 1, 1 - par):
            cp.start()

    @pl.when(j >= 2)
    def _():
        for cp in y_store(j - 2, par):
            cp.wait()

    @pl.when(j == 0)
    def _():
        slot_ref[0] = 0
        for cp in fetch(e, 0):
            cp.start()

    @pl.when(active & fresh)
    def _():
        slot = slot_ref[0]
        for cp in fetch(e, slot):
            cp.wait()
        wgu_bf[...] = wgu_f32[slot].astype(BF16)
        wd_bf[...] = wd_f32[slot].astype(BF16)
        next_blk = ends_ref[e]

        @pl.when(next_blk < n_used)
        def _():
            for cp in fetch(blk_e_ref[jnp.minimum(next_blk, n_used - 1)], 1 - slot):
                cp.start()

        slot_ref[0] = 1 - slot

    @pl.when(active)
    def _():
        for cp in x_load(j, par):
            cp.wait()

    first_blk = jnp.where(e == 0, 0, ends_ref[jnp.maximum(e - 1, 0)])
    valid = jnp.clip(cnt_ref[e] - (j - first_blk) * MOE_BLK, 0, MOE_BLK)
    groups = lax.shift_right_logical(valid + (EXPERT_ROW_GROUP - 1), EXPERT_ROW_GROUP.bit_length() - 1)

    def compute(m):
        x = xbuf[par, 0:m].astype(BF16)
        gu = jnp.dot(x, wgu_bf[...], preferred_element_type=F32) + bgu_ref[e]
        gate = jnp.minimum(gu[:, :D_FF], SWIGLU_LIMIT)
        up = jnp.clip(gu[:, D_FF:], -SWIGLU_LIMIT, SWIGLU_LIMIT)
        glu = gate * jax.nn.sigmoid(SWIGLU_ALPHA * gate)
        act = ((up + 1.0) * glu).astype(BF16)
        ybuf[par, 0:m] = jnp.dot(act, wd_bf[...], preferred_element_type=F32) + bd_ref[e]

    for g in range(1, MOE_BLK // EXPERT_ROW_GROUP + 1):
        pl.when(active & (groups == g))(functools.partial(compute, g * EXPERT_ROW_GROUP))

    @pl.when(jnp.logical_not(active))
    def _():
        ybuf[par] = jnp.zeros((MOE_BLK, D_MODEL), F32)

    for cp in y_store(j, par):
        cp.start()

    @pl.when(j == nb - 1)
    def _():
        for cp in y_store(j, par) + y_store(j - 1, 1 - par):
            cp.wait()


def _experts(blk_e, nused, ends, cnt, xs, w_gate_up, b_gate_up, w_down, b_down, nb):
    assert nb >= 2
    whole = lambda a: pl.BlockSpec(a.shape, lambda i, be, nu, en, cn: (0,) * a.ndim)
    grid_spec = pltpu.PrefetchScalarGridSpec(
        num_scalar_prefetch=4,
        grid=(1,),
        in_specs=[pl.BlockSpec(memory_space=pl.ANY),
                  pl.BlockSpec(memory_space=pl.ANY),
                  whole(b_gate_up),
                  pl.BlockSpec(memory_space=pl.ANY),
                  whole(b_down)],
        out_specs=pl.BlockSpec(memory_space=pl.ANY),
        scratch_shapes=[pltpu.VMEM((2, MOE_BLK, D_MODEL), F32), pltpu.VMEM((2, MOE_BLK, D_MODEL), F32),
                        pltpu.VMEM((2, D_MODEL, 2 * D_FF), F32), pltpu.VMEM((2, D_FF, D_MODEL), F32),
                        pltpu.VMEM((D_MODEL, 2 * D_FF), BF16), pltpu.VMEM((D_FF, D_MODEL), BF16),
                        pltpu.SMEM((1,), I32), pltpu.SemaphoreType.DMA((2, 2)),
                        pltpu.SemaphoreType.DMA((2,)), pltpu.SemaphoreType.DMA((2,))],
    )
    return pl.pallas_call(
        _experts_kernel,
        out_shape=jax.ShapeDtypeStruct((nb * MOE_BLK, SLAB_ROWS, LANES), F32),
        grid_spec=grid_spec,
        compiler_params=_cparams(("arbitrary",), 56),
        name="experts",
    )(blk_e, nused, ends, cnt, xs, w_gate_up, b_gate_up, w_down, b_down)


def _combine_kernel(dest_ref, dest_next_ref, gate_ref, h2_ref, g_ref, b_ref, ys_ref, o_ref, buf_ref, ffn_ref, sems):
    tc = h2_ref.shape[0]
    i = pl.program_id(0)
    slot = lax.rem(i, 2)

    def gather(d_ref, s):
        def start_rows(t, _):
            for k in range(TOP_K):
                pltpu.make_async_copy(ys_ref.at[d_ref[k, t]], buf_ref.at[s, k, t], sems.at[s]).start(priority=k % 2)
            return 0

        lax.fori_loop(0, tc, start_rows, 0, unroll=ROW_DMA_UNROLL)

    @pl.when(i == 0)
    def _():
        gather(dest_ref, 0)

    @pl.when(i + 1 < pl.num_programs(0))
    def _():
        gather(dest_next_ref, 1 - slot)

    for k in range(TOP_K):
        pltpu.make_async_copy(ys_ref.at[pl.ds(0, tc)], buf_ref.at[slot, k], sems.at[slot]).wait()

    def weigh(t, _):
        acc = gate_ref[0, t] * buf_ref[slot, 0, t]
        for k in range(1, TOP_K):
            acc = acc + gate_ref[k, t] * buf_ref[slot, k, t]
        ffn_ref[t] = acc
        return 0

    lax.fori_loop(0, tc, weigh, 0, unroll=ROW_DMA_UNROLL)
    ffn = jnp.concatenate([ffn_ref[:, c, :] for c in range(SLAB_ROWS)], axis=1)
    o_ref[...] = _layer_norm(DEEPNORM_ALPHA * h2_ref[...] + ffn, g_ref[...], b_ref[...])


def _combine(dest_t, gate_t, h2, g, b, ys):
    t = h2.shape[0]
    tc = COMBINE_TOK
    last = t // tc - 1
    return pl.pallas_call(
        _combine_kernel,
        out_shape=jax.ShapeDtypeStruct((t, D_MODEL), F32),
        grid=(t // tc,),
        in_specs=[pl.BlockSpec((SUBLANES, tc), lambda i: (0, i), memory_space=pltpu.SMEM),
                  pl.BlockSpec((SUBLANES, tc), lambda i: (0, jnp.minimum(i + 1, last)), memory_space=pltpu.SMEM),
                  pl.BlockSpec((SUBLANES, tc), lambda i: (0, i), memory_space=pltpu.SMEM),
                  pl.BlockSpec((tc, D_MODEL), lambda i: (i, 0)),
                  pl.BlockSpec((1, D_MODEL), lambda i: (0, 0)),
                  pl.BlockSpec((1, D_MODEL), lambda i: (0, 0)),
                  pl.BlockSpec(memory_space=pl.ANY)],
        out_specs=pl.BlockSpec((tc, D_MODEL), lambda i: (i, 0)),
        scratch_shapes=[pltpu.VMEM((2, TOP_K, tc, SLAB_ROWS, LANES), F32), pltpu.VMEM((tc, SLAB_ROWS, LANES), F32),
                        pltpu.SemaphoreType.DMA((2,))],
        compiler_params=_cparams(("arbitrary",), 32),
        name="combine",
    )(dest_t, dest_t, gate_t, h2, g, b, ys)


def kernel(x, ln_in_g, ln_in_b, w_in, w_gla_gate_up, b_gla_gate, g_gla_norm, b_forget, w_gla_proj, w_fox_proj, w_out, ln_mix_g, ln_mix_b, w_router, b_router, w_gate_up, b_gate_up, w_down, b_down, ln_ffn_g, ln_ffn_b):
    batch, seq, d = x.shape
    assert d == D_MODEL and w_in.shape[0] == DEPTH == 1
    assert seq % max(GLA_ROWS, FOX_TQ, CUM_ROWS) == 0
    t = batch * seq
    assert t % max(TM_PROJ, RANK_TW, DISPATCH_TOK, COMBINE_TOK) == 0 and t // MOE_BLK <= 256
    row = lambda v: v.reshape(1, -1).astype(F32)

    n_forget = FOX_HEADS * FORGET_PIECES
    lane = jnp.arange(LANES)
    w_wide, w_small = _regroup_w_in(jnp.transpose(w_in[0]))
    in_forget = (lane >= FORGET_COL0) & (lane < FORGET_COL0 + n_forget)
    piece = (lane - FORGET_COL0) % FORGET_PIECES
    forget_bias = jnp.zeros((1, LANES), F32).at[0, FORGET_COL0:FORGET_COL0 + n_forget].set(
        jnp.repeat(b_forget[0].astype(F32), FORGET_PIECES))
    piece_sel = jnp.stack([(in_forget & (piece == p)).astype(F32) for p in range(FORGET_PIECES)]
                          + [jnp.zeros((LANES,), F32)] * (SUBLANES - FORGET_PIECES))
    head_of_lane = (lane - FORGET_COL0) // FORGET_PIECES
    qsel = jnp.broadcast_to(jnp.stack([(in_forget & (head_of_lane == h)) for h in range(FOX_HEADS)])[:, :, None],
                            (FOX_HEADS, LANES, LANES)).astype(BF16)
    wgu_pad = jnp.concatenate([w_gla_gate_up[0], jnp.zeros((LANES - GLA_RANK, GLA_QK), F32)], axis=0).astype(BF16)

    x2 = x.reshape(t, d)
    gq, gk, gv, gr, fq, fk, fv, zg, zf, small = _ln_inproj(x2, row(ln_in_g), row(ln_in_b), w_wide, w_small)
    kf = _forget_cum(small, forget_bias, piece_sel, batch, seq)
    gla_o = _gla(gq, gk, gv, gr, small, wgu_pad, row(b_gla_gate[0]), row(g_gla_norm[0]), batch, seq)
    fox_o = _fox(fq, fk, fv, kf, qsel, batch, seq)

    br_rep = jnp.broadcast_to(b_router[0].astype(F32)[:, None], (N_EXPERTS, LANES))
    h2, idx_t, gate_t = _post_mix(
        gla_o, fox_o, zg, zf, x2, row(ln_in_g), row(ln_in_b),
        w_gla_proj[0].astype(BF16), w_fox_proj[0].astype(BF16), w_out[0].astype(BF16),
        row(ln_mix_g[0]), row(ln_mix_b[0]), w_router[0].T.astype(BF16), br_rep)

    nb = (t * TOP_K + N_EXPERTS * (MOE_BLK - 1) + MOE_BLK - 1) // MOE_BLK
    nb_pad = (nb + LANES - 1) // LANES * LANES
    rank_t, counts = _rank(idx_t)
    dest_t, blk_e8, ends = _dest(idx_t, rank_t, counts, nb_pad)
    ends1 = ends[:, 0]
    xs = _dispatch(ends1, dest_t, h2, nb)
    ys = _experts(blk_e8[0, :nb], ends1[N_EXPERTS - 1:], ends1, counts[:, 0].astype(I32), xs, w_gate_up[0], b_gate_up[0][:, None, :],
                  w_down[0], b_down[0][:, None, :], nb)
    out = _combine(dest_t, gate_t, h2, row(ln_ffn_g[0]), row(ln_ffn_b[0]), ys)
    return out.reshape(batch, seq, d)
```

```python
import functools

import jax
import jax.numpy as jnp
from jax import lax
from jax.experimental import pallas as pl
from jax.experimental.pallas import tpu as pltpu

F32 = jnp.float32
BF16 = jnp.bfloat16
I32 = jnp.int32

D_MODEL = 1024
CHUNK = 64
GLA_HEADS, GLA_DK, GLA_DV, GLA_RANK, GLA_TAU = 4, 128, 256, 16, 16.0
GLA_QK = GLA_HEADS * GLA_DK
GLA_V = GLA_HEADS * GLA_DV
FOX_HEADS, FOX_HD = 8, 128
FOX_W = FOX_HEADS * FOX_HD
N_EXPERTS, TOP_K, D_FF = 32, 4, 1024
SWIGLU_LIMIT, SWIGLU_ALPHA = 7.0, 1.702
LN_EPS = 1e-5
DEPTH = 1
DEEPNORM_ALPHA = (2 * DEPTH) ** 0.25
IN_SPLITS = (GLA_QK, GLA_QK, GLA_V, GLA_RANK, GLA_V, FOX_W, FOX_W, FOX_W, FOX_HEADS, D_MODEL, D_MODEL)

LANES = 128
SUBLANES = 8
VMEM_BYTES_V7X = 64 * 1024 * 1024
SLAB_ROWS = D_MODEL // LANES

FORGET_COL0 = GLA_RANK
FORGET_PIECES = 3

TM_PROJ = 512
GLA_ROWS = 512
FOX_TQ = 512
FOX_HEADS_PER_STEP = 2
LOG2E = 1.4426950408889634
CUM_ROWS = 256
RANK_TW = 512
MOE_BLK = 512
EXPERT_ROW_GROUP = 128
DISPATCH_TOK = 256
COMBINE_TOK = 256
ROW_DMA_UNROLL = 8


def _cparams(sem, vmem_mib):
    limit = vmem_mib * 1024 * 1024
    assert limit < VMEM_BYTES_V7X
    return pltpu.CompilerParams(dimension_semantics=sem, vmem_limit_bytes=limit)


def _log_sigmoid(z):
    return jnp.minimum(z, 0.0) - jnp.log1p(jnp.exp(-jnp.abs(z)))


def _layer_norm(x, g, b):
    mu = jnp.mean(x, axis=-1, keepdims=True)
    xc = x - mu
    var = jnp.mean(xc * xc, axis=-1, keepdims=True)
    return xc * lax.rsqrt(var + LN_EPS) * g + b


def _split_bf16(x, pieces):
    out = []
    for _ in range(pieces):
        p = x.astype(BF16)
        out.append(p)
        x = x - p.astype(F32)
    return out


_NARROW_A = sum(IN_SPLITS[:3])
_NARROW_B = sum(IN_SPLITS[:8])


_REGROUP_COLS = 512
_N_WIDE = sum(IN_SPLITS) - GLA_RANK - FOX_HEADS


def _regroup_w_in(wt_hbm, wide_ref, small_ref, buf, nbuf, sems, nsem):
    cols = _REGROUP_COLS
    assert _NARROW_A % cols == 0 and (_NARROW_B - GLA_RANK) % cols == 0

    def load(j):
        c0 = j * cols
        src = c0 + (GLA_RANK if c0 >= _NARROW_A else 0) + (FOX_HEADS if c0 >= _NARROW_B - GLA_RANK else 0)
        return pltpu.make_async_copy(wt_hbm.at[pl.ds(src, cols), :], buf.at[j % 2], sems.at[j % 2])

    load(0).start()
    nbuf[...] = jnp.zeros_like(nbuf)
    copies = [pltpu.make_async_copy(wt_hbm.at[pl.ds(_NARROW_A, GLA_RANK), :], nbuf.at[pl.ds(0, GLA_RANK), :], nsem)]
    for h in range(FOX_HEADS):
        for p in range(FORGET_PIECES):
            copies.append(pltpu.make_async_copy(
                wt_hbm.at[pl.ds(_NARROW_B + h, 1), :],
                nbuf.at[pl.ds(FORGET_COL0 + FORGET_PIECES * h + p, 1), :], nsem))
    for cp in copies:
        cp.start()
    for cp in copies:
        cp.wait()
    small_ref[...] = nbuf[...].T.astype(small_ref.dtype)
    n = _N_WIDE // cols
    for j in range(n):
        if j + 1 < n:
            load(j + 1).start()
        load(j).wait()
        wide_ref[:, j * cols:(j + 1) * cols] = buf[j % 2].T.astype(wide_ref.dtype)


_PROJ_WIDTHS = (GLA_QK, GLA_QK, GLA_V, GLA_V, FOX_W, FOX_W, FOX_W, D_MODEL, D_MODEL)


def _ln_inproj_kernel(x0_ref, xn_ref, g_ref, b_ref, wt_hbm, *refs):
    n_out = len(_PROJ_WIDTHS)
    wide_refs, small_ref = refs[:n_out], refs[n_out]
    hb_refs, w_ref, ws_ref = refs[n_out + 1:n_out + 3], refs[n_out + 3], refs[n_out + 4]
    i = pl.program_id(0)

    @pl.when(i == 0)
    def _():
        _regroup_w_in(wt_hbm, w_ref, ws_ref, *refs[n_out + 5:])
        hb_refs[0][...] = _layer_norm(x0_ref[...], g_ref[...], b_ref[...]).astype(BF16)

    def project(cur_ref, nxt_ref):
        hb = cur_ref[...]
        off = 0
        for o_ref in wide_refs:
            n = o_ref.shape[1]
            o_ref[...] = jnp.dot(hb, w_ref[:, off:off + n], preferred_element_type=F32).astype(o_ref.dtype)
            off += n
        small_ref[...] = jnp.dot(hb, ws_ref[...], preferred_element_type=F32)
        nxt_ref[...] = _layer_norm(xn_ref[...], g_ref[...], b_ref[...]).astype(BF16)

    for parity in range(2):
        pl.when(lax.rem(i, 2) == parity)(functools.partial(project, hb_refs[parity], hb_refs[1 - parity]))


def _ln_inproj(x2, g, b, wt):
    t = x2.shape[0]
    tm = TM_PROJ
    last = t // tm - 1
    row = lambda w: pl.BlockSpec((tm, w), lambda i: (i, 0))
    const = lambda a: pl.BlockSpec(a.shape, lambda i: (0, 0), pipeline_mode=pl.Buffered(1))
    out_shape = ([jax.ShapeDtypeStruct((t, w), BF16) for w in _PROJ_WIDTHS]
                 + [jax.ShapeDtypeStruct((t, LANES), F32)])
    out_specs = [row(w) for w in _PROJ_WIDTHS] + [row(LANES)]
    return pl.pallas_call(
        _ln_inproj_kernel,
        out_shape=out_shape,
        grid=(t // tm,),
        in_specs=[pl.BlockSpec((tm, D_MODEL), lambda i: (0, 0), pipeline_mode=pl.Buffered(1)),
                  pl.BlockSpec((tm, D_MODEL), lambda i: (jnp.minimum(i + 1, last), 0)),
                  const(g), const(b), pl.BlockSpec(memory_space=pl.ANY)],
        out_specs=out_specs,
        scratch_shapes=[pltpu.VMEM((tm, D_MODEL), BF16), pltpu.VMEM((tm, D_MODEL), BF16),
                        pltpu.VMEM((D_MODEL, _N_WIDE), BF16), pltpu.VMEM((D_MODEL, LANES), BF16),
                        pltpu.VMEM((2, _REGROUP_COLS, D_MODEL), F32), pltpu.VMEM((LANES, D_MODEL), F32),
                        pltpu.SemaphoreType.DMA((2,)), pltpu.SemaphoreType.DMA(())],
        compiler_params=_cparams(("arbitrary",), 56),
        name="ln_inproj",
    )(x2, x2, g, b, wt)


def _forget_cum_kernel(small_ref, bias_ref, sel_ref, kf_ref):
    s = small_ref.shape[0]
    r = CUM_ROWS
    ri = lax.broadcasted_iota(I32, (r, r), 0)
    ci = lax.broadcasted_iota(I32, (r, r), 1)
    tri = jnp.where(ci <= ri, 1.0, 0.0).astype(BF16)
    sel = sel_ref[...]
    carry = jnp.zeros((1, LANES), F32)
    for blk in range(s // r):
        rows = pl.ds(blk * r, r)
        ls = _log_sigmoid(small_ref[rows, :] + bias_ref[...])
        cum = carry
        for p in _split_bf16(ls, 3):
            cum = cum + jnp.dot(tri, p, preferred_element_type=F32)
        carry = cum[r - 1:r, :]
        neg = cum * (-LOG2E)
        p1, p2, p3 = _split_bf16(neg, FORGET_PIECES)
        kf = (p1.astype(F32) * sel[0:1, :] + p2.astype(F32) * sel[1:2, :] + p3.astype(F32) * sel[2:3, :])
        kf_ref[rows, :] = kf.astype(BF16)


def _forget_cum(small, bias_row, sel, batch, seq):
    return pl.pallas_call(
        _forget_cum_kernel,
        out_shape=jax.ShapeDtypeStruct((batch * seq, LANES), BF16),
        grid=(batch,),
        in_specs=[pl.BlockSpec((seq, LANES), lambda b: (b, 0)),
                  pl.BlockSpec((1, LANES), lambda b: (0, 0)),
                  pl.BlockSpec((SUBLANES, LANES), lambda b: (0, 0))],
        out_specs=pl.BlockSpec((seq, LANES), lambda b: (b, 0)),
        compiler_params=_cparams(("parallel",), 32),
        name="forget_cum",
    )(small, bias_row, sel)


def _gla_kernel(q_ref, k_ref, v_ref, r_ref, small_ref, wgu_ref, bg_ref, gn_ref, o_ref,
                state_ref, ds_ref, sb_ref, of_ref):
    rows = q_ref.shape[0]
    nchunk = rows // CHUNK
    kcol = lambda h: slice(h * GLA_DK, (h + 1) * GLA_DK)
    vcol = lambda h: slice(h * GLA_DV, (h + 1) * GLA_DV)
    crow = lambda c: slice(c * CHUNK, (c + 1) * CHUNK)

    @pl.when(pl.program_id(1) == 0)
    def _():
        state_ref[...] = jnp.zeros_like(state_ref)

    z = jnp.dot(small_ref[...].astype(BF16), wgu_ref[...], preferred_element_type=F32) + bg_ref[...]
    la = _log_sigmoid(z) * (1.0 / GLA_TAU)
    cb = CUM_ROWS
    ri = lax.broadcasted_iota(I32, (cb, cb), 0)
    ci = lax.broadcasted_iota(I32, (cb, cb), 1)
    shift = CHUNK.bit_length() - 1
    same = lax.shift_right_logical(ri, shift) == lax.shift_right_logical(ci, shift)
    tri = jnp.where(same & (ci <= ri), 1.0, 0.0).astype(BF16)
    pieces = _split_bf16(la, 2)
    cum = jnp.concatenate(
        [sum(jnp.dot(tri, p[r:r + cb], preferred_element_type=F32) for p in pieces) for r in range(0, rows, cb)],
        axis=0)
    last = [cum[(c + 1) * CHUNK - 1:(c + 1) * CHUNK, :] for c in range(nchunk)]
    tot = jnp.concatenate([jnp.broadcast_to(t, (CHUNK, GLA_QK)) for t in last], axis=0)
    kd = (k_ref[...].astype(F32) * jnp.exp(tot - cum)).astype(BF16)
    dec_rows = jnp.exp(jnp.concatenate(last + [jnp.zeros((LANES - nchunk, GLA_QK), F32)], axis=0))
    dec_cols = dec_rows.T

    for c in range(nchunk):
        for h in range(GLA_HEADS):
            ds_ref[c, h] = lax.dot_general(kd[crow(c), kcol(h)], v_ref[crow(c), vcol(h)],
                                           (((0,), (0,)), ((), ())), preferred_element_type=F32)
    for h in range(GLA_HEADS):
        st = state_ref[h]
        for c in range(nchunk):
            st = st * jnp.broadcast_to(dec_cols[kcol(h), c:c + 1], (GLA_DK, GLA_DV)) + ds_ref[c, h]
            sb_ref[c, h] = st.astype(BF16)
        state_ref[h] = st
    scale = float(GLA_DK) ** -0.5
    for c in range(nchunk):
        for h in range(GLA_HEADS):
            of_ref[crow(c), vcol(h)] = jnp.dot(q_ref[crow(c), kcol(h)], sb_ref[c, h],
                                               preferred_element_type=F32) * scale
    for h in range(GLA_HEADS):
        o = of_ref[:, vcol(h)]
        o = o * lax.rsqrt(jnp.mean(o * o, axis=-1, keepdims=True) + LN_EPS) * gn_ref[...]
        r = r_ref[:, vcol(h)].astype(F32)
        o_ref[:, vcol(h)] = (o * (r * jax.nn.sigmoid(r))).astype(o_ref.dtype)


def _gla(gq, gk, gv, gr, small, wgu_pad, b_gate, g_norm, batch, seq):
    rows = GLA_ROWS
    nsteps = seq // rows
    row = lambda w: pl.BlockSpec((rows, w), lambda b, i: (b * nsteps + i, 0))
    const = lambda a: pl.BlockSpec(a.shape, lambda b, i: (0, 0))
    return pl.pallas_call(
        _gla_kernel,
        out_shape=jax.ShapeDtypeStruct((batch * seq, GLA_V), BF16),
        grid=(batch, nsteps),
        in_specs=[row(GLA_QK), row(GLA_QK), row(GLA_V), row(GLA_V), row(LANES),
                  const(wgu_pad), const(b_gate), const(g_norm)],
        out_specs=row(GLA_V),
        scratch_shapes=[pltpu.VMEM((GLA_HEADS, GLA_DK, GLA_DV), F32),
                        pltpu.VMEM((rows // CHUNK, GLA_HEADS, GLA_DK, GLA_DV), F32),
                        pltpu.VMEM((rows // CHUNK, GLA_HEADS, GLA_DK, GLA_DV), BF16),
                        pltpu.VMEM((rows, GLA_V), F32)],
        compiler_params=_cparams(("parallel", "arbitrary"), 40),
        name="gla",
    )(gq, gk, gv, gr, small, wgu_pad, b_gate, g_norm)


def _fox_kernel(q_ref, k_ref, v_ref, kf_ref, qsel_ref, o_ref, qa_ref, vt_ref, s_ref, acc_ref, m_ref, l_ref):
    t = FOX_TQ
    seq = q_ref.shape[0]
    nq = seq // t
    c = (float(FOX_HD) ** -0.5) * LOG2E
    hd = FOX_HD
    heads = range(FOX_HEADS_PER_STEP)
    for h in heads:
        for r in range(0, seq, t):
            qa_ref[h, 0:hd, r:r + t] = (q_ref[r:r + t, h * hd:(h + 1) * hd].astype(F32) * c).T.astype(BF16)
            vt_ref[h, :, r:r + t] = v_ref[r:r + t, h * hd:(h + 1) * hd].astype(F32).T.astype(BF16)
        qa_ref[h, hd:hd + LANES, :] = jnp.tile(qsel_ref[h], (1, seq // LANES))

    def key_rows(j):
        return pl.ds(pl.multiple_of(j * t, t), t)

    def scores(h, i, j):
        ka = jnp.concatenate([k_ref[key_rows(j), h * hd:(h + 1) * hd], kf_ref[key_rows(j), :]], axis=1)
        return jnp.dot(ka, qa_ref[h, :, i * t:(i + 1) * t], preferred_element_type=F32)

    def values(h, j, p):
        return jnp.dot(vt_ref[h, :, key_rows(j)], p, preferred_element_type=F32)

    def step(j, slot, masked, nxt):
        if nxt is not None:
            for h in heads:
                s_ref[1 - slot, h] = scores(h, *nxt)
        for h in heads:
            s = s_ref[slot, h]
            if masked:
                causal = lax.broadcasted_iota(I32, (t, t), 0) <= lax.broadcasted_iota(I32, (t, t), 1)
                s = jnp.where(causal, s, -jnp.inf)
            m_old = m_ref[h]
            m_new = jnp.maximum(m_old, jnp.broadcast_to(jnp.max(s, axis=0, keepdims=True), (SUBLANES, t)))
            alpha = jnp.exp2(m_old - m_new)
            p = jnp.exp2(s - m_new[0:1, :])
            m_ref[h] = m_new
            l_ref[h] = alpha * l_ref[h] + jnp.sum(p, axis=0, keepdims=True)
            acc_ref[h] = alpha[0:1, :] * acc_ref[h] + values(h, j, p.astype(BF16))

    for h in heads:
        s_ref[0, h] = scores(h, 0, 0)
    slot = 0
    for i in range(nq):
        for h in heads:
            acc_ref[h] = jnp.zeros((hd, t), F32)
            m_ref[h] = jnp.full((SUBLANES, t), -jnp.inf, F32)
            l_ref[h] = jnp.zeros((SUBLANES, t), F32)
        if i >= 2:
            def pair(jj, _, i=i, slot=slot):
                step(2 * jj, slot, False, (i, 2 * jj + 1))
                step(2 * jj + 1, 1 - slot, False, (i, 2 * jj + 2))
                return 0

            lax.fori_loop(0, i // 2, pair, 0)
        if i % 2 == 1:
            step(i - 1, slot, False, (i, i))
            slot = 1 - slot
        step(i, slot, True, (i + 1, 0) if i + 1 < nq else None)
        slot = 1 - slot
        for h in heads:
            o_ref[i * t:(i + 1) * t, h * hd:(h + 1) * hd] = (acc_ref[h] / l_ref[h][0:1, :]).T.astype(o_ref.dtype)


def _fox(fq, fk, fv, kf, qsel, batch, seq):
    hps = FOX_HEADS_PER_STEP
    w = hps * FOX_HD
    t = FOX_TQ
    head_cols = pl.BlockSpec((seq, w), lambda b, h: (b, h))
    return pl.pallas_call(
        _fox_kernel,
        out_shape=jax.ShapeDtypeStruct((batch * seq, FOX_W), BF16),
        grid=(batch, FOX_HEADS // hps),
        in_specs=[head_cols, head_cols, head_cols,
                  pl.BlockSpec((seq, LANES), lambda b, h: (b, 0)),
                  pl.BlockSpec((hps, LANES, LANES), lambda b, h: (h, 0, 0))],
        out_specs=head_cols,
        scratch_shapes=[pltpu.VMEM((hps, FOX_HD + LANES, seq), BF16), pltpu.VMEM((hps, FOX_HD, seq), BF16),
                        pltpu.VMEM((2, hps, t, t), F32), pltpu.VMEM((hps, FOX_HD, t), F32),
                        pltpu.VMEM((hps, SUBLANES, t), F32), pltpu.VMEM((hps, SUBLANES, t), F32)],
        compiler_params=_cparams(("parallel", "parallel"), 48),
        name="fox",
    )(fq, fk, fv, kf, qsel)


def _post_mix_kernel(gla_ref, fox_ref, zg_ref, zf_ref, x_ref, gin_ref, bin_ref, wg_ref, wf_ref, wo_ref, g_ref, b_ref,
                     wr_ref, br_ref, h2_ref, idx_ref, gate_ref):
    tm = x_ref.shape[0]
    y_gla = jnp.dot(gla_ref[...], wg_ref[...], preferred_element_type=F32)
    y_fox = jnp.dot(fox_ref[...], wf_ref[...], preferred_element_type=F32)
    mixed = (jax.nn.sigmoid(zg_ref[...].astype(F32)) * y_gla
             + jax.nn.sigmoid(zf_ref[...].astype(F32)) * y_fox)
    h = _layer_norm(x_ref[...], gin_ref[...], bin_ref[...])
    pre = DEEPNORM_ALPHA * h + jnp.dot(mixed.astype(BF16), wo_ref[...], preferred_element_type=F32)
    h2 = _layer_norm(pre, g_ref[...], b_ref[...])
    h2_ref[...] = h2

    logits = lax.dot_general(wr_ref[...], h2.astype(BF16), (((1,), (1,)), ((), ())),
                             preferred_element_type=F32)
    logits = logits + jnp.tile(br_ref[...], (1, tm // LANES))
    eidx = lax.broadcasted_iota(I32, (N_EXPERTS, tm), 0)
    vals, idxs = [], []
    for _ in range(TOP_K):
        mx = jnp.max(logits, axis=0, keepdims=True)
        ik = jnp.min(jnp.where(logits == mx, eidx, N_EXPERTS), axis=0, keepdims=True)
        vals.append(mx)
        idxs.append(ik)
        logits = jnp.where(eidx == ik, -jnp.inf, logits)
    exps = [jnp.exp(v - vals[0]) for v in vals]
    denom = exps[0] + exps[1] + exps[2] + exps[3]
    rid = lax.broadcasted_iota(I32, (SUBLANES, tm), 0)
    idx8 = jnp.zeros((SUBLANES, tm), I32)
    gate8 = jnp.zeros((SUBLANES, tm), F32)
    for k in range(TOP_K):
        idx8 = jnp.where(rid == k, idxs[k], idx8)
        gate8 = jnp.where(rid == k, exps[k] / denom, gate8)
    idx_ref[...] = idx8
    gate_ref[...] = gate8


def _post_mix(gla_o, fox_o, zg, zf, x2, g_in, b_in, wg, wf, wo, g, b, wr_t, br_rep):
    t = x2.shape[0]
    tm = TM_PROJ
    row = lambda w: pl.BlockSpec((tm, w), lambda i: (i, 0))
    const = lambda a: pl.BlockSpec(a.shape, lambda i: (0, 0))
    return pl.pallas_call(
        _post_mix_kernel,
        out_shape=[jax.ShapeDtypeStruct((t, D_MODEL), F32),
                   jax.ShapeDtypeStruct((SUBLANES, t), I32),
                   jax.ShapeDtypeStruct((SUBLANES, t), F32)],
        grid=(t // tm,),
        in_specs=[row(GLA_V), row(FOX_W), row(D_MODEL), row(D_MODEL), row(D_MODEL), const(g_in), const(b_in),
                  const(wg), const(wf), const(wo), const(g), const(b), const(wr_t), const(br_rep)],
        out_specs=[row(D_MODEL),
                   pl.BlockSpec((SUBLANES, tm), lambda i: (0, i)), pl.BlockSpec((SUBLANES, tm), lambda i: (0, i))],
        compiler_params=_cparams(("parallel",), 48),
        name="post_mix",
    )(gla_o, fox_o, zg, zf, x2, g_in, b_in, wg, wf, wo, g, b, wr_t, br_rep)


def _onehot_rows(idx8, tw):
    eidx = lax.broadcasted_iota(I32, (N_EXPERTS, tw), 0)
    hit = eidx == idx8[0:1, :]
    for k in range(1, TOP_K):
        hit = hit | (eidx == idx8[k:k + 1, :])
    return eidx, hit


def _rank_kernel(idx_ref, rank_ref, counts_ref, carry_ref):
    tw = idx_ref.shape[1]

    @pl.when(pl.program_id(0) == 0)
    def _():
        carry_ref[...] = jnp.zeros_like(carry_ref)

    idx8 = idx_ref[...]
    eidx, hit = _onehot_rows(idx8, tw)
    onehot = jnp.where(hit, 1.0, 0.0).astype(BF16)
    ri = lax.broadcasted_iota(I32, (tw, tw + LANES), 0)
    ci = lax.broadcasted_iota(I32, (tw, tw + LANES), 1)
    upper = jnp.where((ri < ci) | (ci >= tw), 1.0, 0.0).astype(BF16)
    cnt = jnp.dot(onehot, upper, preferred_element_type=F32)
    before = cnt[:, :tw] + jnp.tile(carry_ref[...], (1, tw // LANES))
    rid = lax.broadcasted_iota(I32, (SUBLANES, tw), 0)
    rank8 = jnp.zeros((SUBLANES, tw), I32)
    for k in range(TOP_K):
        rk = jnp.sum(jnp.where(eidx == idx8[k:k + 1, :], before, 0.0), axis=0, keepdims=True)
        rank8 = jnp.where(rid == k, rk.astype(I32), rank8)
    rank_ref[...] = rank8
    carry_ref[...] = carry_ref[...] + cnt[:, tw:]
    counts_ref[...] = carry_ref[...]


def _rank(idx_t):
    t = idx_t.shape[1]
    tw = RANK_TW
    return pl.pallas_call(
        _rank_kernel,
        out_shape=[jax.ShapeDtypeStruct((SUBLANES, t), I32),
                   jax.ShapeDtypeStruct((N_EXPERTS, LANES), F32)],
        grid=(t // tw,),
        in_specs=[pl.BlockSpec((SUBLANES, tw), lambda i: (0, i))],
        out_specs=[pl.BlockSpec((SUBLANES, tw), lambda i: (0, i)),
                   pl.BlockSpec((N_EXPERTS, LANES), lambda i: (0, 0))],
        scratch_shapes=[pltpu.VMEM((N_EXPERTS, LANES), F32)],
        compiler_params=_cparams(("arbitrary",), 32),
        name="rank",
    )(idx_t)


def _dest_kernel(idx_ref, rank_ref, counts_ref, dest_ref, blk_e_ref, ends_ref, *, nb_pad):
    tw = idx_ref.shape[1]
    nblk = jnp.floor((counts_ref[...] + (MOE_BLK - 1)) * (1.0 / MOE_BLK))
    ri = lax.broadcasted_iota(I32, (N_EXPERTS, N_EXPERTS), 0)
    ci = lax.broadcasted_iota(I32, (N_EXPERTS, N_EXPERTS), 1)
    tri = jnp.where(ci <= ri, 1.0, 0.0).astype(BF16)
    end_blk = jnp.dot(tri, nblk.astype(BF16), preferred_element_type=F32)
    start_row = (end_blk - nblk) * float(MOE_BLK)
    idx8 = idx_ref[...]
    eidx = lax.broadcasted_iota(I32, (N_EXPERTS, tw), 0)
    start_t = jnp.tile(start_row, (1, tw // LANES))
    rid = lax.broadcasted_iota(I32, (SUBLANES, tw), 0)
    dest8 = jnp.zeros((SUBLANES, tw), I32)
    for k in range(TOP_K):
        st = jnp.sum(jnp.where(eidx == idx8[k:k + 1, :], start_t, 0.0), axis=0, keepdims=True)
        dest8 = jnp.where(rid == k, st.astype(I32), dest8)
    dest_ref[...] = dest8 + rank_ref[...]
    bid = lax.broadcasted_iota(I32, (N_EXPERTS, nb_pad), 1).astype(F32)
    ends_t = jnp.tile(end_blk, (1, nb_pad // LANES))
    be = jnp.sum(jnp.where(ends_t <= bid, 1.0, 0.0), axis=0, keepdims=True)
    blk_e_ref[...] = jnp.broadcast_to(jnp.minimum(be, N_EXPERTS - 1.0), (SUBLANES, nb_pad)).astype(I32)
    ends_ref[...] = end_blk.astype(I32)


def _dest(idx_t, rank_t, counts, nb_pad):
    t = idx_t.shape[1]
    tw = RANK_TW
    tok = pl.BlockSpec((SUBLANES, tw), lambda i: (0, i))
    return pl.pallas_call(
        functools.partial(_dest_kernel, nb_pad=nb_pad),
        out_shape=[jax.ShapeDtypeStruct((SUBLANES, t), I32),
                   jax.ShapeDtypeStruct((SUBLANES, nb_pad), I32),
                   jax.ShapeDtypeStruct((N_EXPERTS, LANES), I32)],
        grid=(t // tw,),
        in_specs=[tok, tok, pl.BlockSpec((N_EXPERTS, LANES), lambda i: (0, 0))],
        out_specs=[tok, pl.BlockSpec((SUBLANES, nb_pad), lambda i: (0, 0)),
                   pl.BlockSpec((N_EXPERTS, LANES), lambda i: (0, 0))],
        compiler_params=_cparams(("arbitrary",), 32),
        name="dest",
    )(idx_t, rank_t, counts)


def _dispatch_kernel(ends_ref, dest_ref, h2_ref, xs_ref, slab_ref, zero_ref, sem, zsem):
    td = h2_ref.shape[0]

    nb = xs_ref.shape[0] // MOE_BLK
    n_used = ends_ref[N_EXPERTS - 1]

    def zero_block(blk):
        start = pl.multiple_of(blk * MOE_BLK, MOE_BLK)
        return pltpu.make_async_copy(zero_ref, xs_ref.at[pl.ds(start, MOE_BLK)], zsem)

    def has_rows(e):
        prev = jnp.where(e == 0, 0, ends_ref[jnp.maximum(e - 1, 0)])
        return ends_ref[e] > prev

    @pl.when(pl.program_id(0) == 0)
    def _():
        zero_ref[...] = jnp.zeros_like(zero_ref)

        def each(fn):
            def last_block(e, _):
                @pl.when(has_rows(e))
                def _():
                    fn(zero_block(ends_ref[e] - 1))
                return 0

            def tail_block(blk, _):
                fn(zero_block(blk))
                return 0

            lax.fori_loop(0, N_EXPERTS, last_block, 0)
            lax.fori_loop(n_used, nb, tail_block, 0)

        each(lambda cp: cp.start())
        each(lambda cp: cp.wait())

    for c in range(SLAB_ROWS):
        slab_ref[:, c, :] = h2_ref[:, c * LANES:(c + 1) * LANES]

    def start_rows(t, _):
        for k in range(TOP_K):
            pltpu.make_async_copy(slab_ref.at[t], xs_ref.at[dest_ref[k, t]], sem).start(priority=k % 2)
        return 0

    lax.fori_loop(0, td, start_rows, 0, unroll=ROW_DMA_UNROLL)
    for k in range(TOP_K):
        pltpu.make_async_copy(slab_ref, xs_ref.at[pl.ds(0, td)], sem).wait()


def _dispatch(ends, dest_t, h2, nb):
    t = h2.shape[0]
    td = DISPATCH_TOK
    grid_spec = pltpu.PrefetchScalarGridSpec(
        num_scalar_prefetch=1,
        grid=(t // td,),
        in_specs=[pl.BlockSpec((SUBLANES, td), lambda i, ends: (0, i), memory_space=pltpu.SMEM),
                  pl.BlockSpec((td, D_MODEL), lambda i, ends: (i, 0))],
        out_specs=pl.BlockSpec(memory_space=pl.ANY),
        scratch_shapes=[pltpu.VMEM((td, SLAB_ROWS, LANES), F32), pltpu.VMEM((MOE_BLK, SLAB_ROWS, LANES), F32),
                        pltpu.SemaphoreType.DMA(()), pltpu.SemaphoreType.DMA(())],
    )
    return pl.pallas_call(
        _dispatch_kernel,
        out_shape=jax.ShapeDtypeStruct((nb * MOE_BLK, SLAB_ROWS, LANES), F32),
        grid_spec=grid_spec,
        compiler_params=_cparams(("arbitrary",), 32),
        name="dispatch",
    )(ends, dest_t, h2)


def _experts_kernel(blk_e_ref, nused_ref, ends_ref, cnt_ref, xs_hbm, wgu_hbm, bgu_ref, wd_hbm, bd_ref, ys_hbm,
                    xbuf, ybuf, wgu_f32, wd_f32, wgu_bf, wd_bf, slot_ref, sems, xsems, ysems):
    nb = xs_hbm.shape[0] // MOE_BLK

    def block(j, _):
        _experts_block(j, nb, blk_e_ref, nused_ref, ends_ref, cnt_ref, xs_hbm, wgu_hbm, bgu_ref, wd_hbm, bd_ref, ys_hbm,
                       xbuf, ybuf, wgu_f32, wd_f32, wgu_bf, wd_bf, slot_ref, sems, xsems, ysems)
        return 0

    lax.fori_loop(0, nb, block, 0)


def _experts_block(j, nb, blk_e_ref, nused_ref, ends_ref, cnt_ref, xs_hbm, wgu_hbm, bgu_ref, wd_hbm, bd_ref, ys_hbm,
                   xbuf, ybuf, wgu_f32, wd_f32, wgu_bf, wd_bf, slot_ref, sems, xsems, ysems):
    par = lax.rem(j, 2)
    n_used = nused_ref[0]
    active = j < n_used
    e = blk_e_ref[j]
    prev = blk_e_ref[jnp.maximum(j - 1, 0)]
    fresh = (j == 0) | (e != prev)

    def fetch(expert, slot):
        return (pltpu.make_async_copy(wgu_hbm.at[expert], wgu_f32.at[slot], sems.at[slot, 0]),
                pltpu.make_async_copy(wd_hbm.at[expert], wd_f32.at[slot], sems.at[slot, 1]))

    def x_load(blk, half):
        rows = pl.ds(pl.multiple_of(blk * MOE_BLK, MOE_BLK), MOE_BLK)
        return [pltpu.make_async_copy(xs_hbm.at[rows, c, :], xbuf.at[half, :, pl.ds(c * LANES, LANES)], xsems.at[half])
                for c in range(SLAB_ROWS)]

    def y_store(blk, half):
        rows = pl.ds(pl.multiple_of(blk * MOE_BLK, MOE_BLK), MOE_BLK)
        return [pltpu.make_async_copy(ybuf.at[half, :, pl.ds(c * LANES, LANES)], ys_hbm.at[rows, c, :], ysems.at[half])
                for c in range(SLAB_ROWS)]

    @pl.when(j == 0)
    def _():
        ybuf[...] = jnp.zeros_like(ybuf)
        for cp in x_load(0, 0):
            cp.start()

    @pl.when(j + 1 < n_used)
    def _():
        for cp in x_load(j + 1, 1 - par):
            cp.start()

    @pl.when(j >= 2)
    def _():
        for cp in y_store(j - 2, par):
            cp.wait()

    @pl.when(j == 0)
    def _():
        slot_ref[0] = 0
        for cp in fetch(e, 0):
            cp.start()

    @pl.when(active & fresh)
    def _():
        slot = slot_ref[0]
        for cp in fetch(e, slot):
            cp.wait()
        wgu_bf[...] = wgu_f32[slot].astype(BF16)
        wd_bf[...] = wd_f32[slot].astype(BF16)
        next_blk = ends_ref[e]

        @pl.when(next_blk < n_used)
        def _():
            for cp in fetch(blk_e_ref[jnp.minimum(next_blk, n_used - 1)], 1 - slot):
                cp.start()

        slot_ref[0] = 1 - slot

    @pl.when(active)
    def _():
        for cp in x_load(j, par):
            cp.wait()

    first_blk = jnp.where(e == 0, 0, ends_ref[jnp.maximum(e - 1, 0)])
    valid = jnp.clip(cnt_ref[e] - (j - first_blk) * MOE_BLK, 0, MOE_BLK)
    groups = lax.shift_right_logical(valid + (EXPERT_ROW_GROUP - 1), EXPERT_ROW_GROUP.bit_length() - 1)

    def compute(m):
        x = xbuf[par, 0:m].astype(BF16)
        gu = jnp.dot(x, wgu_bf[...], preferred_element_type=F32) + bgu_ref[e]
        gate = jnp.minimum(gu[:, :D_FF], SWIGLU_LIMIT)
        up = jnp.clip(gu[:, D_FF:], -SWIGLU_LIMIT, SWIGLU_LIMIT)
        glu = gate * jax.nn.sigmoid(SWIGLU_ALPHA * gate)
        act = ((up + 1.0) * glu).astype(BF16)
        ybuf[par, 0:m] = jnp.dot(act, wd_bf[...], preferred_element_type=F32) + bd_ref[e]

    for g in range(1, MOE_BLK // EXPERT_ROW_GROUP + 1):
        pl.when(active & (groups == g))(functools.partial(compute, g * EXPERT_ROW_GROUP))

    @pl.when(jnp.logical_not(active))
    def _():
        ybuf[par] = jnp.zeros((MOE_BLK, D_MODEL), F32)

    for cp in y_store(j, par):
        cp.start()

    @pl.when(j == nb - 1)
    def _():
        for cp in y_store(j, par) + y_store(j - 1, 1 - par):
            cp.wait()


def _experts(blk_e, nused, ends, cnt, xs, w_gate_up, b_gate_up, w_down, b_down, nb):
    assert nb >= 2
    whole = lambda a: pl.BlockSpec(a.shape, lambda i, be, nu, en, cn: (0,) * a.ndim)
    grid_spec = pltpu.PrefetchScalarGridSpec(
        num_scalar_prefetch=4,
        grid=(1,),
        in_specs=[pl.BlockSpec(memory_space=pl.ANY),
                  pl.BlockSpec(memory_space=pl.ANY),
                  whole(b_gate_up),
                  pl.BlockSpec(memory_space=pl.ANY),
                  whole(b_down)],
        out_specs=pl.BlockSpec(memory_space=pl.ANY),
        scratch_shapes=[pltpu.VMEM((2, MOE_BLK, D_MODEL), F32), pltpu.VMEM((2, MOE_BLK, D_MODEL), F32),
                        pltpu.VMEM((2, D_MODEL, 2 * D_FF), F32), pltpu.VMEM((2, D_FF, D_MODEL), F32),
                        pltpu.VMEM((D_MODEL, 2 * D_FF), BF16), pltpu.VMEM((D_FF, D_MODEL), BF16),
                        pltpu.SMEM((1,), I32), pltpu.SemaphoreType.DMA((2, 2)),
                        pltpu.SemaphoreType.DMA((2,)), pltpu.SemaphoreType.DMA((2,))],
    )
    return pl.pallas_call(
        _experts_kernel,
        out_shape=jax.ShapeDtypeStruct((nb * MOE_BLK, SLAB_ROWS, LANES), F32),
        grid_spec=grid_spec,
        compiler_params=_cparams(("arbitrary",), 56),
        name="experts",
    )(blk_e, nused, ends, cnt, xs, w_gate_up, b_gate_up, w_down, b_down)


def _combine_kernel(dest_ref, dest_next_ref, gate_ref, h2_ref, g_ref, b_ref, ys_ref, o_ref, buf_ref, ffn_ref, sems):
    tc = h2_ref.shape[0]
    i = pl.program_id(0)
    slot = lax.rem(i, 2)

    def gather(d_ref, s):
        def start_rows(t, _):
            for k in range(TOP_K):
                pltpu.make_async_copy(ys_ref.at[d_ref[k, t]], buf_ref.at[s, k, t], sems.at[s]).start(priority=k % 2)
            return 0

        lax.fori_loop(0, tc, start_rows, 0, unroll=ROW_DMA_UNROLL)

    @pl.when(i == 0)
    def _():
        gather(dest_ref, 0)

    @pl.when(i + 1 < pl.num_programs(0))
    def _():
        gather(dest_next_ref, 1 - slot)

    for k in range(TOP_K):
        pltpu.make_async_copy(ys_ref.at[pl.ds(0, tc)], buf_ref.at[slot, k], sems.at[slot]).wait()

    def weigh(t, _):
        acc = gate_ref[0, t] * buf_ref[slot, 0, t]
        for k in range(1, TOP_K):
            acc = acc + gate_ref[k, t] * buf_ref[slot, k, t]
        ffn_ref[t] = acc
        return 0

    lax.fori_loop(0, tc, weigh, 0, unroll=ROW_DMA_UNROLL)
    ffn = jnp.concatenate([ffn_ref[:, c, :] for c in range(SLAB_ROWS)], axis=1)
    o_ref[...] = _layer_norm(DEEPNORM_ALPHA * h2_ref[...] + ffn, g_ref[...], b_ref[...])


def _combine(dest_t, gate_t, h2, g, b, ys):
    t = h2.shape[0]
    tc = COMBINE_TOK
    last = t // tc - 1
    return pl.pallas_call(
        _combine_kernel,
        out_shape=jax.ShapeDtypeStruct((t, D_MODEL), F32),
        grid=(t // tc,),
        in_specs=[pl.BlockSpec((SUBLANES, tc), lambda i: (0, i), memory_space=pltpu.SMEM),
                  pl.BlockSpec((SUBLANES, tc), lambda i: (0, jnp.minimum(i + 1, last)), memory_space=pltpu.SMEM),
                  pl.BlockSpec((SUBLANES, tc), lambda i: (0, i), memory_space=pltpu.SMEM),
                  pl.BlockSpec((tc, D_MODEL), lambda i: (i, 0)),
                  pl.BlockSpec((1, D_MODEL), lambda i: (0, 0)),
                  pl.BlockSpec((1, D_MODEL), lambda i: (0, 0)),
                  pl.BlockSpec(memory_space=pl.ANY)],
        out_specs=pl.BlockSpec((tc, D_MODEL), lambda i: (i, 0)),
        scratch_shapes=[pltpu.VMEM((2, TOP_K, tc, SLAB_ROWS, LANES), F32), pltpu.VMEM((tc, SLAB_ROWS, LANES), F32),
                        pltpu.SemaphoreType.DMA((2,))],
        compiler_params=_cparams(("arbitrary",), 32),
        name="combine",
    )(dest_t, dest_t, gate_t, h2, g, b, ys)


def kernel(x, ln_in_g, ln_in_b, w_in, w_gla_gate_up, b_gla_gate, g_gla_norm, b_forget, w_gla_proj, w_fox_proj, w_out, ln_mix_g, ln_mix_b, w_router, b_router, w_gate_up, b_gate_up, w_down, b_down, ln_ffn_g, ln_ffn_b):
    batch, seq, d = x.shape
    assert d == D_MODEL and w_in.shape[0] == DEPTH == 1
    assert seq % max(GLA_ROWS, FOX_TQ, CUM_ROWS) == 0
    t = batch * seq
    assert t % max(TM_PROJ, RANK_TW, DISPATCH_TOK, COMBINE_TOK) == 0 and t // MOE_BLK <= 256
    row = lambda v: v.reshape(1, -1).astype(F32)

    n_forget = FOX_HEADS * FORGET_PIECES
    lane = jnp.arange(LANES)
    in_forget = (lane >= FORGET_COL0) & (lane < FORGET_COL0 + n_forget)
    piece = (lane - FORGET_COL0) % FORGET_PIECES
    forget_bias = jnp.zeros((1, LANES), F32).at[0, FORGET_COL0:FORGET_COL0 + n_forget].set(
        jnp.repeat(b_forget[0].astype(F32), FORGET_PIECES))
    piece_sel = jnp.stack([(in_forget & (piece == p)).astype(F32) for p in range(FORGET_PIECES)]
                          + [jnp.zeros((LANES,), F32)] * (SUBLANES - FORGET_PIECES))
    head_of_lane = (lane - FORGET_COL0) // FORGET_PIECES
    qsel = jnp.broadcast_to(jnp.stack([(in_forget & (head_of_lane == h)) for h in range(FOX_HEADS)])[:, :, None],
                            (FOX_HEADS, LANES, LANES)).astype(BF16)
    wgu_pad = jnp.concatenate([w_gla_gate_up[0], jnp.zeros((LANES - GLA_RANK, GLA_QK), F32)], axis=0).astype(BF16)

    x2 = x.reshape(t, d)
    gq, gk, gv, gr, fq, fk, fv, zg, zf, small = _ln_inproj(x2, row(ln_in_g), row(ln_in_b), jnp.transpose(w_in[0]))
    kf = _forget_cum(small, forget_bias, piece_sel, batch, seq)
    gla_o = _gla(gq, gk, gv, gr, small, wgu_pad, row(b_gla_gate[0]), row(g_gla_norm[0]), batch, seq)
    fox_o = _fox(fq, fk, fv, kf, qsel, batch, seq)

    br_rep = jnp.broadcast_to(b_router[0].astype(F32)[:, None], (N_EXPERTS, LANES))
    h2, idx_t, gate_t = _post_mix(
        gla_o, fox_o, zg, zf, x2, row(ln_in_g), row(ln_in_b),
        w_gla_proj[0].astype(BF16), w_fox_proj[0].astype(BF16), w_out[0].astype(BF16),
        row(ln_mix_g[0]), row(ln_mix_b[0]), w_router[0].T.astype(BF16), br_rep)

    nb = (t * TOP_K + N_EXPERTS * (MOE_BLK - 1) + MOE_BLK - 1) // MOE_BLK
    nb_pad = (nb + LANES - 1) // LANES * LANES
    rank_t, counts = _rank(idx_t)
    dest_t, blk_e8, ends = _dest(idx_t, rank_t, counts, nb_pad)
    ends1 = ends[:, 0]
    xs = _dispatch(ends1, dest_t, h2, nb)
    ys = _experts(blk_e8[0, :nb], ends1[N_EXPERTS - 1:], ends1, counts[:, 0].astype(I32), xs, w_gate_up[0], b_gate_up[0][:, None, :],
                  w_down[0], b_down[0][:, None, :], nb)
    out = _combine(dest_t, gate_t, h2, row(ln_ffn_g[0]), row(ln_ffn_b[0]), ys)
    return out.reshape(batch, seq, d)
```

```python
import functools

import jax
import jax.numpy as jnp
from jax import lax
from jax.experimental import pallas as pl
from jax.experimental.pallas import tpu as pltpu

F32 = jnp.float32
BF16 = jnp.bfloat16
I32 = jnp.int32

D_MODEL = 1024
CHUNK = 64
GLA_HEADS, GLA_DK, GLA_DV, GLA_RANK, GLA_TAU = 4, 128, 256, 16, 16.0
GLA_QK = GLA_HEADS * GLA_DK
GLA_V = GLA_HEADS * GLA_DV
FOX_HEADS, FOX_HD = 8, 128
FOX_W = FOX_HEADS * FOX_HD
N_EXPERTS, TOP_K, D_FF = 32, 4, 1024
SWIGLU_LIMIT, SWIGLU_ALPHA = 7.0, 1.702
LN_EPS = 1e-5
DEPTH = 1
DEEPNORM_ALPHA = (2 * DEPTH) ** 0.25
IN_SPLITS = (GLA_QK, GLA_QK, GLA_V, GLA_RANK, GLA_V, FOX_W, FOX_W, FOX_W, FOX_HEADS, D_MODEL, D_MODEL)

LANES = 128
SUBLANES = 8
VMEM_BYTES_V7X = 64 * 1024 * 1024
SLAB_ROWS = D_MODEL // LANES

FORGET_COL0 = GLA_RANK
FORGET_PIECES = 3

TM_PROJ = 512
GLA_ROWS = 512
FOX_TQ = 512
FOX_HEADS_PER_STEP = 2
LOG2E = 1.4426950408889634
CUM_ROWS = 256
RANK_TW = 512
MOE_BLK = 512
EXPERT_ROW_GROUP = 128
DISPATCH_TOK = 256
COMBINE_TOK = 256
ROW_DMA_UNROLL = 8


def _cparams(sem, vmem_mib):
    limit = vmem_mib * 1024 * 1024
    assert limit < VMEM_BYTES_V7X
    return pltpu.CompilerParams(dimension_semantics=sem, vmem_limit_bytes=limit)


def _log_sigmoid(z):
    return jnp.minimum(z, 0.0) - jnp.log1p(jnp.exp(-jnp.abs(z)))


def _layer_norm(x, g, b):
    mu = jnp.mean(x, axis=-1, keepdims=True)
    xc = x - mu
    var = jnp.mean(xc * xc, axis=-1, keepdims=True)
    return xc * lax.rsqrt(var + LN_EPS) * g + b


def _split_bf16(x, pieces):
    out = []
    for _ in range(pieces):
        p = x.astype(BF16)
        out.append(p)
        x = x - p.astype(F32)
    return out


_NARROW_A = sum(IN_SPLITS[:3])
_NARROW_B = sum(IN_SPLITS[:8])


_REGROUP_COLS = 512
_N_WIDE = sum(IN_SPLITS) - GLA_RANK - FOX_HEADS


def _regroup_w_in(wt_hbm, wide_ref, small_ref, buf, nbuf, sems, nsem):
    cols = _REGROUP_COLS
    assert _NARROW_A % cols == 0 and (_NARROW_B - GLA_RANK) % cols == 0

    def load(j):
        c0 = j * cols
        src = c0 + (GLA_RANK if c0 >= _NARROW_A else 0) + (FOX_HEADS if c0 >= _NARROW_B - GLA_RANK else 0)
        return pltpu.make_async_copy(wt_hbm.at[pl.ds(src, cols), :], buf.at[j % 2], sems.at[j % 2])

    load(0).start()
    nbuf[...] = jnp.zeros_like(nbuf)
    copies = [pltpu.make_async_copy(wt_hbm.at[pl.ds(_NARROW_A, GLA_RANK), :], nbuf.at[pl.ds(0, GLA_RANK), :], nsem)]
    for h in range(FOX_HEADS):
        for p in range(FORGET_PIECES):
            copies.append(pltpu.make_async_copy(
                wt_hbm.at[pl.ds(_NARROW_B + h, 1), :],
                nbuf.at[pl.ds(FORGET_COL0 + FORGET_PIECES * h + p, 1), :], nsem))
    for cp in copies:
        cp.start()
    for cp in copies:
        cp.wait()
    small_ref[...] = nbuf[...].T.astype(small_ref.dtype)
    n = _N_WIDE // cols
    for j in range(n):
        if j + 1 < n:
            load(j + 1).start()
        load(j).wait()
        wide_ref[:, j * cols:(j + 1) * cols] = buf[j % 2].T.astype(wide_ref.dtype)


_PROJ_WIDTHS = (GLA_QK, GLA_QK, GLA_V, GLA_V, FOX_W, FOX_W, FOX_W, D_MODEL, D_MODEL)


def _ln_inproj_kernel(x0_ref, xn_ref, g_ref, b_ref, wt_hbm, *refs):
    n_out = len(_PROJ_WIDTHS)
    wide_refs, small_ref = refs[:n_out], refs[n_out]
    hb_refs, w_ref, ws_ref = refs[n_out + 1:n_out + 3], refs[n_out + 3], refs[n_out + 4]
    i = pl.program_id(0)

    @pl.when(i == 0)
    def _():
        _regroup_w_in(wt_hbm, w_ref, ws_ref, *refs[n_out + 5:])
        hb_refs[0][...] = _layer_norm(x0_ref[...], g_ref[...], b_ref[...]).astype(BF16)

    def project(cur_ref, nxt_ref):
        hb = cur_ref[...]
        off = 0
        for o_ref in wide_refs:
            n = o_ref.shape[1]
            o_ref[...] = jnp.dot(hb, w_ref[:, off:off + n], preferred_element_type=F32).astype(o_ref.dtype)
            off += n
        small_ref[...] = jnp.dot(hb, ws_ref[...], preferred_element_type=F32)
        nxt_ref[...] = _layer_norm(xn_ref[...], g_ref[...], b_ref[...]).astype(BF16)

    for parity in range(2):
        pl.when(lax.rem(i, 2) == parity)(functools.partial(project, hb_refs[parity], hb_refs[1 - parity]))


def _ln_inproj(x2, g, b, wt):
    t = x2.shape[0]
    tm = TM_PROJ
    last = t // tm - 1
    row = lambda w: pl.BlockSpec((tm, w), lambda i: (i, 0))
    const = lambda a: pl.BlockSpec(a.shape, lambda i: (0, 0), pipeline_mode=pl.Buffered(1))
    out_shape = ([jax.ShapeDtypeStruct((t, w), BF16) for w in _PROJ_WIDTHS]
                 + [jax.ShapeDtypeStruct((t, LANES), F32)])
    out_specs = [row(w) for w in _PROJ_WIDTHS] + [row(LANES)]
    return pl.pallas_call(
        _ln_inproj_kernel,
        out_shape=out_shape,
        grid=(t // tm,),
        in_specs=[pl.BlockSpec((tm, D_MODEL), lambda i: (0, 0), pipeline_mode=pl.Buffered(1)),
                  pl.BlockSpec((tm, D_MODEL), lambda i: (jnp.minimum(i + 1, last), 0)),
                  const(g), const(b), pl.BlockSpec(memory_space=pl.ANY)],
        out_specs=out_specs,
        scratch_shapes=[pltpu.VMEM((tm, D_MODEL), BF16), pltpu.VMEM((tm, D_MODEL), BF16),
                        pltpu.VMEM((D_MODEL, _N_WIDE), BF16), pltpu.VMEM((D_MODEL, LANES), BF16),
                        pltpu.VMEM((2, _REGROUP_COLS, D_MODEL), F32), pltpu.VMEM((LANES, D_MODEL), F32),
                        pltpu.SemaphoreType.DMA((2,)), pltpu.SemaphoreType.DMA(())],
        compiler_params=_cparams(("arbitrary",), 56),
        name="ln_inproj",
    )(x2, x2, g, b, wt)


def _forget_cum_kernel(small_ref, bias_ref, sel_ref, kf_ref):
    s = small_ref.shape[0]
    r = CUM_ROWS
    ri = lax.broadcasted_iota(I32, (r, r), 0)
    ci = lax.broadcasted_iota(I32, (r, r), 1)
    tri = jnp.where(ci <= ri, 1.0, 0.0).astype(BF16)
    sel = sel_ref[...]
    carry = jnp.zeros((1, LANES), F32)
    for blk in range(s // r):
        rows = pl.ds(blk * r, r)
        ls = _log_sigmoid(small_ref[rows, :] + bias_ref[...])
        cum = carry
        for p in _split_bf16(ls, 3):
            cum = cum + jnp.dot(tri, p, preferred_element_type=F32)
        carry = cum[r - 1:r, :]
        neg = cum * (-LOG2E)
        p1, p2, p3 = _split_bf16(neg, FORGET_PIECES)
        kf = (p1.astype(F32) * sel[0:1, :] + p2.astype(F32) * sel[1:2, :] + p3.astype(F32) * sel[2:3, :])
        kf_ref[rows, :] = kf.astype(BF16)


def _forget_cum(small, bias_row, sel, batch, seq):
    return pl.pallas_call(
        _forget_cum_kernel,
        out_shape=jax.ShapeDtypeStruct((batch * seq, LANES), BF16),
        grid=(batch,),
        in_specs=[pl.BlockSpec((seq, LANES), lambda b: (b, 0)),
                  pl.BlockSpec((1, LANES), lambda b: (0, 0)),
                  pl.BlockSpec((SUBLANES, LANES), lambda b: (0, 0))],
        out_specs=pl.BlockSpec((seq, LANES), lambda b: (b, 0)),
        compiler_params=_cparams(("parallel",), 32),
        name="forget_cum",
    )(small, bias_row, sel)


def _gla_kernel(q_ref, k_ref, v_ref, r_ref, small_ref, wgu_ref, bg_ref, gn_ref, o_ref,
                state_ref, ds_ref, sb_ref, of_ref):
    rows = q_ref.shape[0]
    nchunk = rows // CHUNK
    kcol = lambda h: slice(h * GLA_DK, (h + 1) * GLA_DK)
    vcol = lambda h: slice(h * GLA_DV, (h + 1) * GLA_DV)
    crow = lambda c: slice(c * CHUNK, (c + 1) * CHUNK)

    @pl.when(pl.program_id(1) == 0)
    def _():
        state_ref[...] = jnp.zeros_like(state_ref)

    z = jnp.dot(small_ref[...].astype(BF16), wgu_ref[...], preferred_element_type=F32) + bg_ref[...]
    la = _log_sigmoid(z) * (1.0 / GLA_TAU)
    cb = CUM_ROWS
    ri = lax.broadcasted_iota(I32, (cb, cb), 0)
    ci = lax.broadcasted_iota(I32, (cb, cb), 1)
    shift = CHUNK.bit_length() - 1
    same = lax.shift_right_logical(ri, shift) == lax.shift_right_logical(ci, shift)
    tri = jnp.where(same & (ci <= ri), 1.0, 0.0).astype(BF16)
    pieces = _split_bf16(la, 2)
    cum = jnp.concatenate(
        [sum(jnp.dot(tri, p[r:r + cb], preferred_element_type=F32) for p in pieces) for r in range(0, rows, cb)],
        axis=0)
    last = [cum[(c + 1) * CHUNK - 1:(c + 1) * CHUNK, :] for c in range(nchunk)]
    tot = jnp.concatenate([jnp.broadcast_to(t, (CHUNK, GLA_QK)) for t in last], axis=0)
    kd = (k_ref[...].astype(F32) * jnp.exp(tot - cum)).astype(BF16)
    dec_rows = jnp.exp(jnp.concatenate(last + [jnp.zeros((LANES - nchunk, GLA_QK), F32)], axis=0))
    dec_cols = dec_rows.T

    for c in range(nchunk):
        for h in range(GLA_HEADS):
            ds_ref[c, h] = lax.dot_general(kd[crow(c), kcol(h)], v_ref[crow(c), vcol(h)],
                                           (((0,), (0,)), ((), ())), preferred_element_type=F32)
    for h in range(GLA_HEADS):
        st = state_ref[h]
        for c in range(nchunk):
            st = st * jnp.broadcast_to(dec_cols[kcol(h), c:c + 1], (GLA_DK, GLA_DV)) + ds_ref[c, h]
            sb_ref[c, h] = st.astype(BF16)
        state_ref[h] = st
    scale = float(GLA_DK) ** -0.5
    for c in range(nchunk):
        for h in range(GLA_HEADS):
            of_ref[crow(c), vcol(h)] = jnp.dot(q_ref[crow(c), kcol(h)], sb_ref[c, h],
                                               preferred_element_type=F32) * scale
    for h in range(GLA_HEADS):
        o = of_ref[:, vcol(h)]
        o = o * lax.rsqrt(jnp.mean(o * o, axis=-1, keepdims=True) + LN_EPS) * gn_ref[...]
        r = r_ref[:, vcol(h)].astype(F32)
        o_ref[:, vcol(h)] = (o * (r * jax.nn.sigmoid(r))).astype(o_ref.dtype)


def _gla(gq, gk, gv, gr, small, wgu_pad, b_gate, g_norm, batch, seq):
    rows = GLA_ROWS
    nsteps = seq // rows
    row = lambda w: pl.BlockSpec((rows, w), lambda b, i: (b * nsteps + i, 0))
    const = lambda a: pl.BlockSpec(a.shape, lambda b, i: (0, 0))
    return pl.pallas_call(
        _gla_kernel,
        out_shape=jax.ShapeDtypeStruct((batch * seq, GLA_V), BF16),
        grid=(batch, nsteps),
        in_specs=[row(GLA_QK), row(GLA_QK), row(GLA_V), row(GLA_V), row(LANES),
                  const(wgu_pad), const(b_gate), const(g_norm)],
        out_specs=row(GLA_V),
        scratch_shapes=[pltpu.VMEM((GLA_HEADS, GLA_DK, GLA_DV), F32),
                        pltpu.VMEM((rows // CHUNK, GLA_HEADS, GLA_DK, GLA_DV), F32),
                        pltpu.VMEM((rows // CHUNK, GLA_HEADS, GLA_DK, GLA_DV), BF16),
                        pltpu.VMEM((rows, GLA_V), F32)],
        compiler_params=_cparams(("parallel", "arbitrary"), 40),
        name="gla",
    )(gq, gk, gv, gr, small, wgu_pad, b_gate, g_norm)


def _fox_kernel(q_ref, k_ref, v_ref, kf_ref, qsel_ref, o_ref, qa_ref, vt_ref, s_ref, acc_ref, m_ref, l_ref):
    t = FOX_TQ
    seq = q_ref.shape[0]
    nq = seq // t
    c = (float(FOX_HD) ** -0.5) * LOG2E
    hd = FOX_HD
    heads = range(FOX_HEADS_PER_STEP)
    for h in heads:
        for r in range(0, seq, t):
            qa_ref[h, 0:hd, r:r + t] = (q_ref[r:r + t, h * hd:(h + 1) * hd].astype(F32) * c).T.astype(BF16)
            vt_ref[h, :, r:r + t] = v_ref[r:r + t, h * hd:(h + 1) * hd].astype(F32).T.astype(BF16)
        qa_ref[h, hd:hd + LANES, :] = jnp.tile(qsel_ref[h], (1, seq // LANES))

    def key_rows(j):
        return pl.ds(pl.multiple_of(j * t, t), t)

    def scores(h, i, j):
        ka = jnp.concatenate([k_ref[key_rows(j), h * hd:(h + 1) * hd], kf_ref[key_rows(j), :]], axis=1)
        return jnp.dot(ka, qa_ref[h, :, i * t:(i + 1) * t], preferred_element_type=F32)

    def values(h, j, p):
        return jnp.dot(vt_ref[h, :, key_rows(j)], p, preferred_element_type=F32)

    def step(j, slot, masked, nxt):
        if nxt is not None:
            for h in heads:
                s_ref[1 - slot, h] = scores(h, *nxt)
        for h in heads:
            s = s_ref[slot, h]
            if masked:
                causal = lax.broadcasted_iota(I32, (t, t), 0) <= lax.broadcasted_iota(I32, (t, t), 1)
                s = jnp.where(causal, s, -jnp.inf)
            m_old = m_ref[h]
            m_new = jnp.maximum(m_old, jnp.broadcast_to(jnp.max(s, axis=0, keepdims=True), (SUBLANES, t)))
            alpha = jnp.exp2(m_old - m_new)
            p = jnp.exp2(s - m_new[0:1, :])
            m_ref[h] = m_new
            l_ref[h] = alpha * l_ref[h] + jnp.sum(p, axis=0, keepdims=True)
            acc_ref[h] = alpha[0:1, :] * acc_ref[h] + values(h, j, p.astype(BF16))

    for h in heads:
        s_ref[0, h] = scores(h, 0, 0)
    slot = 0
    for i in range(nq):
        for h in heads:
            acc_ref[h] = jnp.zeros((hd, t), F32)
            m_ref[h] = jnp.full((SUBLANES, t), -jnp.inf, F32)
            l_ref[h] = jnp.zeros((SUBLANES, t), F32)
        if i >= 2:
            def pair(jj, _, i=i, slot=slot):
                step(2 * jj, slot, False, (i, 2 * jj + 1))
                step(2 * jj + 1, 1 - slot, False, (i, 2 * jj + 2))
                return 0

            lax.fori_loop(0, i // 2, pair, 0)
        if i % 2 == 1:
            step(i - 1, slot, False, (i, i))
            slot = 1 - slot
        step(i, slot, True, (i + 1, 0) if i + 1 < nq else None)
        slot = 1 - slot
        for h in heads:
            o_ref[i * t:(i + 1) * t, h * hd:(h + 1) * hd] = (acc_ref[h] / l_ref[h][0:1, :]).T.astype(o_ref.dtype)


def _fox(fq, fk, fv, kf, qsel, batch, seq):
    hps = FOX_HEADS_PER_STEP
    w = hps * FOX_HD
    t = FOX_TQ
    head_cols = pl.BlockSpec((seq, w), lambda b, h: (b, h))
    return pl.pallas_call(
        _fox_kernel,
        out_shape=jax.ShapeDtypeStruct((batch * seq, FOX_W), BF16),
        grid=(batch, FOX_HEADS // hps),
        in_specs=[head_cols, head_cols, head_cols,
                  pl.BlockSpec((seq, LANES), lambda b, h: (b, 0)),
                  pl.BlockSpec((hps, LANES, LANES), lambda b, h: (h, 0, 0))],
        out_specs=head_cols,
        scratch_shapes=[pltpu.VMEM((hps, FOX_HD + LANES, seq), BF16), pltpu.VMEM((hps, FOX_HD, seq), BF16),
                        pltpu.VMEM((2, hps, t, t), F32), pltpu.VMEM((hps, FOX_HD, t), F32),
                        pltpu.VMEM((hps, SUBLANES, t), F32), pltpu.VMEM((hps, SUBLANES, t), F32)],
        compiler_params=_cparams(("parallel", "parallel"), 48),
        name="fox",
    )(fq, fk, fv, kf, qsel)


def _post_mix_kernel(gla_ref, fox_ref, zg_ref, zf_ref, x_ref, gin_ref, bin_ref, wg_hbm, wf_hbm, wo_hbm, g_ref, b_ref,
                     wr_ref, br_ref, h2_ref, idx_ref, gate_ref, wg_ref, wf_ref, wo_ref, stage_ref, sem):
    tm = x_ref.shape[0]

    @pl.when(pl.program_id(0) == 0)
    def _():
        for src, dst in ((wg_hbm, wg_ref), (wf_hbm, wf_ref), (wo_hbm, wo_ref)):
            cp = pltpu.make_async_copy(src, stage_ref, sem)
            cp.start()
            cp.wait()
            dst[...] = stage_ref[...].astype(BF16)

    y_gla = jnp.dot(gla_ref[...], wg_ref[...], preferred_element_type=F32)
    y_fox = jnp.dot(fox_ref[...], wf_ref[...], preferred_element_type=F32)
    mixed = (jax.nn.sigmoid(zg_ref[...].astype(F32)) * y_gla
             + jax.nn.sigmoid(zf_ref[...].astype(F32)) * y_fox)
    h = _layer_norm(x_ref[...], gin_ref[...], bin_ref[...])
    pre = DEEPNORM_ALPHA * h + jnp.dot(mixed.astype(BF16), wo_ref[...], preferred_element_type=F32)
    h2 = _layer_norm(pre, g_ref[...], b_ref[...])
    h2_ref[...] = h2

    logits = lax.dot_general(wr_ref[...], h2.astype(BF16), (((1,), (1,)), ((), ())),
                             preferred_element_type=F32)
    logits = logits + jnp.tile(br_ref[...], (1, tm // LANES))
    eidx = lax.broadcasted_iota(I32, (N_EXPERTS, tm), 0)
    vals, idxs = [], []
    for _ in range(TOP_K):
        mx = jnp.max(logits, axis=0, keepdims=True)
        ik = jnp.min(jnp.where(logits == mx, eidx, N_EXPERTS), axis=0, keepdims=True)
        vals.append(mx)
        idxs.append(ik)
        logits = jnp.where(eidx == ik, -jnp.inf, logits)
    exps = [jnp.exp(v - vals[0]) for v in vals]
    denom = exps[0] + exps[1] + exps[2] + exps[3]
    rid = lax.broadcasted_iota(I32, (SUBLANES, tm), 0)
    idx8 = jnp.zeros((SUBLANES, tm), I32)
    gate8 = jnp.zeros((SUBLANES, tm), F32)
    for k in range(TOP_K):
        idx8 = jnp.where(rid == k, idxs[k], idx8)
        gate8 = jnp.where(rid == k, exps[k] / denom, gate8)
    idx_ref[...] = idx8
    gate_ref[...] = gate8


def _post_mix(gla_o, fox_o, zg, zf, x2, g_in, b_in, wg, wf, wo, g, b, wr_t, br_rep):
    t = x2.shape[0]
    tm = TM_PROJ
    row = lambda w: pl.BlockSpec((tm, w), lambda i: (i, 0))
    const = lambda a: pl.BlockSpec(a.shape, lambda i: (0, 0))
    hbm = pl.BlockSpec(memory_space=pl.ANY)
    assert wg.shape == wf.shape == wo.shape
    return pl.pallas_call(
        _post_mix_kernel,
        out_shape=[jax.ShapeDtypeStruct((t, D_MODEL), F32),
                   jax.ShapeDtypeStruct((SUBLANES, t), I32),
                   jax.ShapeDtypeStruct((SUBLANES, t), F32)],
        grid=(t // tm,),
        in_specs=[row(GLA_V), row(FOX_W), row(D_MODEL), row(D_MODEL), row(D_MODEL), const(g_in), const(b_in),
                  hbm, hbm, hbm, const(g), const(b), const(wr_t), const(br_rep)],
        out_specs=[row(D_MODEL),
                   pl.BlockSpec((SUBLANES, tm), lambda i: (0, i)), pl.BlockSpec((SUBLANES, tm), lambda i: (0, i))],
        scratch_shapes=[pltpu.VMEM(wg.shape, BF16), pltpu.VMEM(wf.shape, BF16), pltpu.VMEM(wo.shape, BF16),
                        pltpu.VMEM(wg.shape, F32), pltpu.SemaphoreType.DMA(())],
        compiler_params=_cparams(("arbitrary",), 48),
        name="post_mix",
    )(gla_o, fox_o, zg, zf, x2, g_in, b_in, wg, wf, wo, g, b, wr_t, br_rep)


def _onehot_rows(idx8, tw):
    eidx = lax.broadcasted_iota(I32, (N_EXPERTS, tw), 0)
    hit = eidx == idx8[0:1, :]
    for k in range(1, TOP_K):
        hit = hit | (eidx == idx8[k:k + 1, :])
    return eidx, hit


def _rank_kernel(idx_ref, rank_ref, counts_ref, carry_ref):
    tw = idx_ref.shape[1]

    @pl.when(pl.program_id(0) == 0)
    def _():
        carry_ref[...] = jnp.zeros_like(carry_ref)

    idx8 = idx_ref[...]
    eidx, hit = _onehot_rows(idx8, tw)
    onehot = jnp.where(hit, 1.0, 0.0).astype(BF16)
    ri = lax.broadcasted_iota(I32, (tw, tw + LANES), 0)
    ci = lax.broadcasted_iota(I32, (tw, tw + LANES), 1)
    upper = jnp.where((ri < ci) | (ci >= tw), 1.0, 0.0).astype(BF16)
    cnt = jnp.dot(onehot, upper, preferred_element_type=F32)
    before = cnt[:, :tw] + jnp.tile(carry_ref[...], (1, tw // LANES))
    rid = lax.broadcasted_iota(I32, (SUBLANES, tw), 0)
    rank8 = jnp.zeros((SUBLANES, tw), I32)
    for k in range(TOP_K):
        rk = jnp.sum(jnp.where(eidx == idx8[k:k + 1, :], before, 0.0), axis=0, keepdims=True)
        rank8 = jnp.where(rid == k, rk.astype(I32), rank8)
    rank_ref[...] = rank8
    carry_ref[...] = carry_ref[...] + cnt[:, tw:]
    counts_ref[...] = carry_ref[...]


def _rank(idx_t):
    t = idx_t.shape[1]
    tw = RANK_TW
    return pl.pallas_call(
        _rank_kernel,
        out_shape=[jax.ShapeDtypeStruct((SUBLANES, t), I32),
                   jax.ShapeDtypeStruct((N_EXPERTS, LANES), F32)],
        grid=(t // tw,),
        in_specs=[pl.BlockSpec((SUBLANES, tw), lambda i: (0, i))],
        out_specs=[pl.BlockSpec((SUBLANES, tw), lambda i: (0, i)),
                   pl.BlockSpec((N_EXPERTS, LANES), lambda i: (0, 0))],
        scratch_shapes=[pltpu.VMEM((N_EXPERTS, LANES), F32)],
        compiler_params=_cparams(("arbitrary",), 32),
        name="rank",
    )(idx_t)


def _dest_kernel(idx_ref, rank_ref, counts_ref, dest_ref, blk_e_ref, ends_ref, *, nb_pad):
    tw = idx_ref.shape[1]
    nblk = jnp.floor((counts_ref[...] + (MOE_BLK - 1)) * (1.0 / MOE_BLK))
    ri = lax.broadcasted_iota(I32, (N_EXPERTS, N_EXPERTS), 0)
    ci = lax.broadcasted_iota(I32, (N_EXPERTS, N_EXPERTS), 1)
    tri = jnp.where(ci <= ri, 1.0, 0.0).astype(BF16)
    end_blk = jnp.dot(tri, nblk.astype(BF16), preferred_element_type=F32)
    start_row = (end_blk - nblk) * float(MOE_BLK)
    idx8 = idx_ref[...]
    eidx = lax.broadcasted_iota(I32, (N_EXPERTS, tw), 0)
    start_t = jnp.tile(start_row, (1, tw // LANES))
    rid = lax.broadcasted_iota(I32, (SUBLANES, tw), 0)
    dest8 = jnp.zeros((SUBLANES, tw), I32)
    for k in range(TOP_K):
        st = jnp.sum(jnp.where(eidx == idx8[k:k + 1, :], start_t, 0.0), axis=0, keepdims=True)
        dest8 = jnp.where(rid == k, st.astype(I32), dest8)
    dest_ref[...] = dest8 + rank_ref[...]
    bid = lax.broadcasted_iota(I32, (N_EXPERTS, nb_pad), 1).astype(F32)
    ends_t = jnp.tile(end_blk, (1, nb_pad // LANES))
    be = jnp.sum(jnp.where(ends_t <= bid, 1.0, 0.0), axis=0, keepdims=True)
    blk_e_ref[...] = jnp.broadcast_to(jnp.minimum(be, N_EXPERTS - 1.0), (SUBLANES, nb_pad)).astype(I32)
    ends_ref[...] = end_blk.astype(I32)


def _dest(idx_t, rank_t, counts, nb_pad):
    t = idx_t.shape[1]
    tw = RANK_TW
    tok = pl.BlockSpec((SUBLANES, tw), lambda i: (0, i))
    return pl.pallas_call(
        functools.partial(_dest_kernel, nb_pad=nb_pad),
        out_shape=[jax.ShapeDtypeStruct((SUBLANES, t), I32),
                   jax.ShapeDtypeStruct((SUBLANES, nb_pad), I32),
                   jax.ShapeDtypeStruct((N_EXPERTS, LANES), I32)],
        grid=(t // tw,),
        in_specs=[tok, tok, pl.BlockSpec((N_EXPERTS, LANES), lambda i: (0, 0))],
        out_specs=[tok, pl.BlockSpec((SUBLANES, nb_pad), lambda i: (0, 0)),
                   pl.BlockSpec((N_EXPERTS, LANES), lambda i: (0, 0))],
        compiler_params=_cparams(("arbitrary",), 32),
        name="dest",
    )(idx_t, rank_t, counts)


def _dispatch_kernel(ends_ref, dest_ref, h2_ref, xs_ref, slab_ref, zero_ref, sem, zsem):
    td = h2_ref.shape[0]

    nb = xs_ref.shape[0] // MOE_BLK
    n_used = ends_ref[N_EXPERTS - 1]

    def zero_block(blk):
        start = pl.multiple_of(blk * MOE_BLK, MOE_BLK)
        return pltpu.make_async_copy(zero_ref, xs_ref.at[pl.ds(start, MOE_BLK)], zsem)

    def has_rows(e):
        prev = jnp.where(e == 0, 0, ends_ref[jnp.maximum(e - 1, 0)])
        return ends_ref[e] > prev

    @pl.when(pl.program_id(0) == 0)
    def _():
        zero_ref[...] = jnp.zeros_like(zero_ref)

        def each(fn):
            def last_block(e, _):
                @pl.when(has_rows(e))
                def _():
                    fn(zero_block(ends_ref[e] - 1))
                return 0

            def tail_block(blk, _):
                fn(zero_block(blk))
                return 0

            lax.fori_loop(0, N_EXPERTS, last_block, 0)
            lax.fori_loop(n_used, nb, tail_block, 0)

        each(lambda cp: cp.start())
        each(lambda cp: cp.wait())

    for c in range(SLAB_ROWS):
        slab_ref[:, c, :] = h2_ref[:, c * LANES:(c + 1) * LANES]

    def start_rows(t, _):
        for k in range(TOP_K):
            pltpu.make_async_copy(slab_ref.at[t], xs_ref.at[dest_ref[k, t]], sem).start(priority=k % 2)
        return 0

    lax.fori_loop(0, td, start_rows, 0, unroll=ROW_DMA_UNROLL)
    for k in range(TOP_K):
        pltpu.make_async_copy(slab_ref, xs_ref.at[pl.ds(0, td)], sem).wait()


def _dispatch(ends, dest_t, h2, nb):
    t = h2.shape[0]
    td = DISPATCH_TOK
    grid_spec = pltpu.PrefetchScalarGridSpec(
        num_scalar_prefetch=1,
        grid=(t // td,),
        in_specs=[pl.BlockSpec((SUBLANES, td), lambda i, ends: (0, i), memory_space=pltpu.SMEM),
                  pl.BlockSpec((td, D_MODEL), lambda i, ends: (i, 0))],
        out_specs=pl.BlockSpec(memory_space=pl.ANY),
        scratch_shapes=[pltpu.VMEM((td, SLAB_ROWS, LANES), F32), pltpu.VMEM((MOE_BLK, SLAB_ROWS, LANES), F32),
                        pltpu.SemaphoreType.DMA(()), pltpu.SemaphoreType.DMA(())],
    )
    return pl.pallas_call(
        _dispatch_kernel,
        out_shape=jax.ShapeDtypeStruct((nb * MOE_BLK, SLAB_ROWS, LANES), F32),
        grid_spec=grid_spec,
        compiler_params=_cparams(("arbitrary",), 32),
        name="dispatch",
    )(ends, dest_t, h2)


def _experts_kernel(blk_e_ref, nused_ref, ends_ref, cnt_ref, xs_hbm, wgu_hbm, bgu_ref, wd_hbm, bd_ref, ys_hbm,
                    xbuf, ybuf, wgu_f32, wd_f32, wgu_bf, wd_bf, slot_ref, sems, xsems, ysems):
    nb = xs_hbm.shape[0] // MOE_BLK

    def block(j, _):
        _experts_block(j, nb, blk_e_ref, nused_ref, ends_ref, cnt_ref, xs_hbm, wgu_hbm, bgu_ref, wd_hbm, bd_ref, ys_hbm,
                       xbuf, ybuf, wgu_f32, wd_f32, wgu_bf, wd_bf, slot_ref, sems, xsems, ysems)
        return 0

    lax.fori_loop(0, nb, block, 0)


def _experts_block(j, nb, blk_e_ref, nused_ref, ends_ref, cnt_ref, xs_hbm, wgu_hbm, bgu_ref, wd_hbm, bd_ref, ys_hbm,
                   xbuf, ybuf, wgu_f32, wd_f32, wgu_bf, wd_bf, slot_ref, sems, xsems, ysems):
    par = lax.rem(j, 2)
    n_used = nused_ref[0]
    active = j < n_used
    e = blk_e_ref[j]
    prev = blk_e_ref[jnp.maximum(j - 1, 0)]
    fresh = (j == 0) | (e != prev)

    def fetch(expert, slot):
        return (pltpu.make_async_copy(wgu_hbm.at[expert], wgu_f32.at[slot], sems.at[slot, 0]),
                pltpu.make_async_copy(wd_hbm.at[expert], wd_f32.at[slot], sems.at[slot, 1]))

    def x_load(blk, half):
        rows = pl.ds(pl.multiple_of(blk * MOE_BLK, MOE_BLK), MOE_BLK)
        return [pltpu.make_async_copy(xs_hbm.at[rows, c, :], xbuf.at[half, :, pl.ds(c * LANES, LANES)], xsems.at[half])
                for c in range(SLAB_ROWS)]

    def y_store(blk, half):
        rows = pl.ds(pl.multiple_of(blk * MOE_BLK, MOE_BLK), MOE_BLK)
        return [pltpu.make_async_copy(ybuf.at[half, :, pl.ds(c * LANES, LANES)], ys_hbm.at[rows, c, :], ysems.at[half])
                for c in range(SLAB_ROWS)]

    @pl.when(j == 0)
    def _():
        ybuf[...] = jnp.zeros_like(ybuf)
        for cp in x_load(0, 0):
            cp.start()

    @pl.when(j + 1 < n_used)
    def _():
        for cp in x_load(j + 1, 1 - par):
            cp.start()

    @pl.when(j >= 2)
    def _():
        for cp in y_store(j - 2, par):
            cp.wait()

    @pl.when(j == 0)
    def _():
        slot_ref[0] = 0
        for cp in fetch(e, 0):
            cp.start()

    @pl.when(active & fresh)
    def _():
        slot = slot_ref[0]
        for cp in fetch(e, slot):
            cp.wait()
        wgu_bf[...] = wgu_f32[slot].astype(BF16)
        wd_bf[...] = wd_f32[slot].astype(BF16)
        next_blk = ends_ref[e]

        @pl.when(next_blk < n_used)
        def _():
            for cp in fetch(blk_e_ref[jnp.minimum(next_blk, n_used - 1)], 1 - slot):
                cp.start()

        slot_ref[0] = 1 - slot

    @pl.when(active)
    def _():
        for cp in x_load(j, par):
            cp.wait()

    first_blk = jnp.where(e == 0, 0, ends_ref[jnp.maximum(e - 1, 0)])
    valid = jnp.clip(cnt_ref[e] - (j - first_blk) * MOE_BLK, 0, MOE_BLK)
    groups = lax.shift_right_logical(valid + (EXPERT_ROW_GROUP - 1), EXPERT_ROW_GROUP.bit_length() - 1)

    def compute(m):
        x = xbuf[par, 0:m].astype(BF16)
        gu = jnp.dot(x, wgu_bf[...], preferred_element_type=F32) + bgu_ref[pl.ds(e, 1), :]
        gate = jnp.minimum(gu[:, :D_FF], SWIGLU_LIMIT)
        up = jnp.clip(gu[:, D_FF:], -SWIGLU_LIMIT, SWIGLU_LIMIT)
        glu = gate * jax.nn.sigmoid(SWIGLU_ALPHA * gate)
        act = ((up + 1.0) * glu).astype(BF16)
        ybuf[par, 0:m] = jnp.dot(act, wd_bf[...], preferred_element_type=F32) + bd_ref[pl.ds(e, 1), :]

    for g in range(1, MOE_BLK // EXPERT_ROW_GROUP + 1):
        pl.when(active & (groups == g))(functools.partial(compute, g * EXPERT_ROW_GROUP))

    @pl.when(jnp.logical_not(active))
    def _():
        ybuf[par] = jnp.zeros((MOE_BLK, D_MODEL), F32)

    for cp in y_store(j, par):
        cp.start()

    @pl.when(j == nb - 1)
    def _():
        for cp in y_store(j, par) + y_store(j - 1, 1 - par):
            cp.wait()


def _experts(blk_e, nused, ends, cnt, xs, w_gate_up, b_gate_up, w_down, b_down, nb):
    assert nb >= 2
    whole = lambda a: pl.BlockSpec(a.shape, lambda i, be, nu, en, cn: (0,) * a.ndim)
    grid_spec = pltpu.PrefetchScalarGridSpec(
        num_scalar_prefetch=4,
        grid=(1,),
        in_specs=[pl.BlockSpec(memory_space=pl.ANY),
                  pl.BlockSpec(memory_space=pl.ANY),
                  whole(b_gate_up),
                  pl.BlockSpec(memory_space=pl.ANY),
                  whole(b_down)],
        out_specs=pl.BlockSpec(memory_space=pl.ANY),
        scratch_shapes=[pltpu.VMEM((2, MOE_BLK, D_MODEL), F32), pltpu.VMEM((2, MOE_BLK, D_MODEL), F32),
                        pltpu.VMEM((2, D_MODEL, 2 * D_FF), F32), pltpu.VMEM((2, D_FF, D_MODEL), F32),
                        pltpu.VMEM((D_MODEL, 2 * D_FF), BF16), pltpu.VMEM((D_FF, D_MODEL), BF16),
                        pltpu.SMEM((1,), I32), pltpu.SemaphoreType.DMA((2, 2)),
                        pltpu.SemaphoreType.DMA((2,)), pltpu.SemaphoreType.DMA((2,))],
    )
    return pl.pallas_call(
        _experts_kernel,
        out_shape=jax.ShapeDtypeStruct((nb * MOE_BLK, SLAB_ROWS, LANES), F32),
        grid_spec=grid_spec,
        compiler_params=_cparams(("arbitrary",), 56),
        name="experts",
    )(blk_e, nused, ends, cnt, xs, w_gate_up, b_gate_up, w_down, b_down)


def _combine_kernel(dest_ref, dest_next_ref, gate_ref, h2_ref, g_ref, b_ref, ys_ref, o_ref, buf_ref, ffn_ref, sems):
    tc = h2_ref.shape[0]
    i = pl.program_id(0)
    slot = lax.rem(i, 2)

    def gather(d_ref, s):
        def start_rows(t, _):
            for k in range(TOP_K):
                pltpu.make_async_copy(ys_ref.at[d_ref[k, t]], buf_ref.at[s, k, t], sems.at[s]).start(priority=k % 2)
            return 0

        lax.fori_loop(0, tc, start_rows, 0, unroll=ROW_DMA_UNROLL)

    @pl.when(i == 0)
    def _():
        gather(dest_ref, 0)

    @pl.when(i + 1 < pl.num_programs(0))
    def _():
        gather(dest_next_ref, 1 - slot)

    for k in range(TOP_K):
        pltpu.make_async_copy(ys_ref.at[pl.ds(0, tc)], buf_ref.at[slot, k], sems.at[slot]).wait()

    def weigh(t, _):
        acc = gate_ref[0, t] * buf_ref[slot, 0, t]
        for k in range(1, TOP_K):
            acc = acc + gate_ref[k, t] * buf_ref[slot, k, t]
        ffn_ref[t] = acc
        return 0

    lax.fori_loop(0, tc, weigh, 0, unroll=ROW_DMA_UNROLL)
    ffn = jnp.concatenate([ffn_ref[:, c, :] for c in range(SLAB_ROWS)], axis=1)
    o_ref[...] = _layer_norm(DEEPNORM_ALPHA * h2_ref[...] + ffn, g_ref[...], b_ref[...])


def _combine(dest_t, gate_t, h2, g, b, ys):
    t = h2.shape[0]
    tc = COMBINE_TOK
    last = t // tc - 1
    return pl.pallas_call(
        _combine_kernel,
        out_shape=jax.ShapeDtypeStruct((t, D_MODEL), F32),
        grid=(t // tc,),
        in_specs=[pl.BlockSpec((SUBLANES, tc), lambda i: (0, i), memory_space=pltpu.SMEM),
                  pl.BlockSpec((SUBLANES, tc), lambda i: (0, jnp.minimum(i + 1, last)), memory_space=pltpu.SMEM),
                  pl.BlockSpec((SUBLANES, tc), lambda i: (0, i), memory_space=pltpu.SMEM),
                  pl.BlockSpec((tc, D_MODEL), lambda i: (i, 0)),
                  pl.BlockSpec((1, D_MODEL), lambda i: (0, 0)),
                  pl.BlockSpec((1, D_MODEL), lambda i: (0, 0)),
                  pl.BlockSpec(memory_space=pl.ANY)],
        out_specs=pl.BlockSpec((tc, D_MODEL), lambda i: (i, 0)),
        scratch_shapes=[pltpu.VMEM((2, TOP_K, tc, SLAB_ROWS, LANES), F32), pltpu.VMEM((tc, SLAB_ROWS, LANES), F32),
                        pltpu.SemaphoreType.DMA((2,))],
        compiler_params=_cparams(("arbitrary",), 32),
        name="combine",
    )(dest_t, dest_t, gate_t, h2, g, b, ys)


def kernel(x, ln_in_g, ln_in_b, w_in, w_gla_gate_up, b_gla_gate, g_gla_norm, b_forget, w_gla_proj, w_fox_proj, w_out, ln_mix_g, ln_mix_b, w_router, b_router, w_gate_up, b_gate_up, w_down, b_down, ln_ffn_g, ln_ffn_b):
    batch, seq, d = x.shape
    assert d == D_MODEL and w_in.shape[0] == DEPTH == 1
    assert seq % max(GLA_ROWS, FOX_TQ, CUM_ROWS) == 0
    t = batch * seq
    assert t % max(TM_PROJ, RANK_TW, DISPATCH_TOK, COMBINE_TOK) == 0 and t // MOE_BLK <= 256
    row = lambda v: v.reshape(1, -1).astype(F32)

    n_forget = FOX_HEADS * FORGET_PIECES
    lane = jnp.arange(LANES)
    in_forget = (lane >= FORGET_COL0) & (lane < FORGET_COL0 + n_forget)
    piece = (lane - FORGET_COL0) % FORGET_PIECES
    forget_bias = jnp.zeros((1, LANES), F32).at[0, FORGET_COL0:FORGET_COL0 + n_forget].set(
        jnp.repeat(b_forget[0].astype(F32), FORGET_PIECES))
    piece_sel = jnp.stack([(in_forget & (piece == p)).astype(F32) for p in range(FORGET_PIECES)]
                          + [jnp.zeros((LANES,), F32)] * (SUBLANES - FORGET_PIECES))
    head_of_lane = (lane - FORGET_COL0) // FORGET_PIECES
    qsel = jnp.broadcast_to(jnp.stack([(in_forget & (head_of_lane == h)) for h in range(FOX_HEADS)])[:, :, None],
                            (FOX_HEADS, LANES, LANES)).astype(BF16)
    wgu_pad = jnp.concatenate([w_gla_gate_up[0], jnp.zeros((LANES - GLA_RANK, GLA_QK), F32)], axis=0).astype(BF16)

    x2 = x.reshape(t, d)
    gq, gk, gv, gr, fq, fk, fv, zg, zf, small = _ln_inproj(x2, row(ln_in_g), row(ln_in_b), jnp.transpose(w_in[0]))
    kf = _forget_cum(small, forget_bias, piece_sel, batch, seq)
    gla_o = _gla(gq, gk, gv, gr, small, wgu_pad, row(b_gla_gate[0]), row(g_gla_norm[0]), batch, seq)
    fox_o = _fox(fq, fk, fv, kf, qsel, batch, seq)

    br_rep = jnp.broadcast_to(b_router[0].astype(F32)[:, None], (N_EXPERTS, LANES))
    h2, idx_t, gate_t = _post_mix(
        gla_o, fox_o, zg, zf, x2, row(ln_in_g), row(ln_in_b),
        w_gla_proj[0], w_fox_proj[0], w_out[0],
        row(ln_mix_g[0]), row(ln_mix_b[0]), w_router[0].T.astype(BF16), br_rep)

    nb = (t * TOP_K + N_EXPERTS * (MOE_BLK - 1) + MOE_BLK - 1) // MOE_BLK
    nb_pad = (nb + LANES - 1) // LANES * LANES
    rank_t, counts = _rank(idx_t)
    dest_t, blk_e8, ends = _dest(idx_t, rank_t, counts, nb_pad)
    ends1 = ends[:, 0]
    xs = _dispatch(ends1, dest_t, h2, nb)
    ys = _experts(blk_e8[0, :nb], ends1[N_EXPERTS - 1:], ends1, counts[:, 0].astype(I32), xs, w_gate_up[0], b_gate_up[0],
                  w_down[0], b_down[0], nb)
    out = _combine(dest_t, gate_t, h2, row(ln_ffn_g[0]), row(ln_ffn_b[0]), ys)
    return out.reshape(batch, seq, d)
```

```python
import functools

import jax
import jax.numpy as jnp
from jax import lax
from jax.experimental import pallas as pl
from jax.experimental.pallas import tpu as pltpu

F32 = jnp.float32
BF16 = jnp.bfloat16
I32 = jnp.int32

D_MODEL = 1024
CHUNK = 64
GLA_HEADS, GLA_DK, GLA_DV, GLA_RANK, GLA_TAU = 4, 128, 256, 16, 16.0
GLA_QK = GLA_HEADS * GLA_DK
GLA_V = GLA_HEADS * GLA_DV
FOX_HEADS, FOX_HD = 8, 128
FOX_W = FOX_HEADS * FOX_HD
N_EXPERTS, TOP_K, D_FF = 32, 4, 1024
SWIGLU_LIMIT, SWIGLU_ALPHA = 7.0, 1.702
LN_EPS = 1e-5
DEPTH = 1
DEEPNORM_ALPHA = (2 * DEPTH) ** 0.25
IN_SPLITS = (GLA_QK, GLA_QK, GLA_V, GLA_RANK, GLA_V, FOX_W, FOX_W, FOX_W, FOX_HEADS, D_MODEL, D_MODEL)

LANES = 128
SUBLANES = 8
VMEM_BYTES_V7X = 64 * 1024 * 1024
SLAB_ROWS = D_MODEL // LANES

FORGET_COL0 = GLA_RANK
FORGET_PIECES = 3

TM_PROJ = 512
GLA_ROWS = 1024
FOX_TQ = 512
FOX_HEADS_PER_STEP = 2
LOG2E = 1.4426950408889634
CUM_ROWS = 256
RANK_TW = 512
MOE_BLK = 512
EXPERT_ROW_GROUP = 64
DISPATCH_TOK = 256
COMBINE_TOK = 256
ROW_DMA_UNROLL = 8


def _cparams(sem, vmem_mib):
    limit = vmem_mib * 1024 * 1024
    assert limit < VMEM_BYTES_V7X
    return pltpu.CompilerParams(dimension_semantics=sem, vmem_limit_bytes=limit)


def _log_sigmoid(z):
    return jnp.minimum(z, 0.0) - jnp.log1p(jnp.exp(-jnp.abs(z)))


def _layer_norm(x, g, b):
    mu = jnp.mean(x, axis=-1, keepdims=True)
    xc = x - mu
    var = jnp.mean(xc * xc, axis=-1, keepdims=True)
    return xc * lax.rsqrt(var + LN_EPS) * g + b


def _split_bf16(x, pieces):
    out = []
    for _ in range(pieces):
        p = x.astype(BF16)
        out.append(p)
        x = x - p.astype(F32)
    return out


_NARROW_A = sum(IN_SPLITS[:3])
_NARROW_B = sum(IN_SPLITS[:8])


_REGROUP_COLS = 512
_N_WIDE = sum(IN_SPLITS) - GLA_RANK - FOX_HEADS


def _regroup_w_in(wt_hbm, wide_ref, small_ref, buf, nbuf, sems, nsem):
    cols = _REGROUP_COLS
    assert _NARROW_A % cols == 0 and (_NARROW_B - GLA_RANK) % cols == 0

    def load(j):
        c0 = j * cols
        src = c0 + (GLA_RANK if c0 >= _NARROW_A else 0) + (FOX_HEADS if c0 >= _NARROW_B - GLA_RANK else 0)
        return pltpu.make_async_copy(wt_hbm.at[pl.ds(src, cols), :], buf.at[j % 2], sems.at[j % 2])

    load(0).start()
    nbuf[...] = jnp.zeros_like(nbuf)
    copies = [pltpu.make_async_copy(wt_hbm.at[pl.ds(_NARROW_A, GLA_RANK), :], nbuf.at[pl.ds(0, GLA_RANK), :], nsem)]
    for h in range(FOX_HEADS):
        for p in range(FORGET_PIECES):
            copies.append(pltpu.make_async_copy(
                wt_hbm.at[pl.ds(_NARROW_B + h, 1), :],
                nbuf.at[pl.ds(FORGET_COL0 + FORGET_PIECES * h + p, 1), :], nsem))
    for cp in copies:
        cp.start()
    for cp in copies:
        cp.wait()
    small_ref[...] = nbuf[...].T.astype(small_ref.dtype)
    n = _N_WIDE // cols
    for j in range(n):
        if j + 1 < n:
            load(j + 1).start()
        load(j).wait()
        wide_ref[:, j * cols:(j + 1) * cols] = buf[j % 2].T.astype(wide_ref.dtype)


_PROJ_WIDTHS = (GLA_QK, GLA_QK, GLA_V, GLA_V, FOX_W, FOX_W, FOX_W, D_MODEL, D_MODEL)


def _ln_inproj_kernel(x0_ref, xn_ref, g_ref, b_ref, wt_hbm, *refs):
    n_out = len(_PROJ_WIDTHS)
    wide_refs, small_ref = refs[:n_out], refs[n_out]
    hb_refs, w_ref, ws_ref = refs[n_out + 1:n_out + 3], refs[n_out + 3], refs[n_out + 4]
    i = pl.program_id(0)

    @pl.when(i == 0)
    def _():
        _regroup_w_in(wt_hbm, w_ref, ws_ref, *refs[n_out + 5:])
        hb_refs[0][...] = _layer_norm(x0_ref[...], g_ref[...], b_ref[...]).astype(BF16)

    def project(cur_ref, nxt_ref):
        hb = cur_ref[...]
        off = 0
        for o_ref in wide_refs:
            n = o_ref.shape[1]
            o_ref[...] = jnp.dot(hb, w_ref[:, off:off + n], preferred_element_type=F32).astype(o_ref.dtype)
            off += n
        small_ref[...] = jnp.dot(hb, ws_ref[...], preferred_element_type=F32)
        nxt_ref[...] = _layer_norm(xn_ref[...], g_ref[...], b_ref[...]).astype(BF16)

    for parity in range(2):
        pl.when(lax.rem(i, 2) == parity)(functools.partial(project, hb_refs[parity], hb_refs[1 - parity]))


def _ln_inproj(x2, g, b, wt):
    t = x2.shape[0]
    tm = TM_PROJ
    last = t // tm - 1
    row = lambda w: pl.BlockSpec((tm, w), lambda i: (i, 0))
    const = lambda a: pl.BlockSpec(a.shape, lambda i: (0, 0), pipeline_mode=pl.Buffered(1))
    out_shape = ([jax.ShapeDtypeStruct((t, w), BF16) for w in _PROJ_WIDTHS]
                 + [jax.ShapeDtypeStruct((t, LANES), F32)])
    out_specs = [row(w) for w in _PROJ_WIDTHS] + [row(LANES)]
    return pl.pallas_call(
        _ln_inproj_kernel,
        out_shape=out_shape,
        grid=(t // tm,),
        in_specs=[pl.BlockSpec((tm, D_MODEL), lambda i: (0, 0), pipeline_mode=pl.Buffered(1)),
                  pl.BlockSpec((tm, D_MODEL), lambda i: (jnp.minimum(i + 1, last), 0)),
                  const(g), const(b), pl.BlockSpec(memory_space=pl.ANY)],
        out_specs=out_specs,
        scratch_shapes=[pltpu.VMEM((tm, D_MODEL), BF16), pltpu.VMEM((tm, D_MODEL), BF16),
                        pltpu.VMEM((D_MODEL, _N_WIDE), BF16), pltpu.VMEM((D_MODEL, LANES), BF16),
                        pltpu.VMEM((2, _REGROUP_COLS, D_MODEL), F32), pltpu.VMEM((LANES, D_MODEL), F32),
                        pltpu.SemaphoreType.DMA((2,)), pltpu.SemaphoreType.DMA(())],
        compiler_params=_cparams(("arbitrary",), 56),
        name="ln_inproj",
    )(x2, x2, g, b, wt)


def _forget_cum_kernel(small_ref, bias_ref, sel_ref, kf_ref):
    s = small_ref.shape[0]
    r = CUM_ROWS
    ri = lax.broadcasted_iota(I32, (r, r), 0)
    ci = lax.broadcasted_iota(I32, (r, r), 1)
    tri = jnp.where(ci <= ri, 1.0, 0.0).astype(BF16)
    sel = sel_ref[...]
    carry = jnp.zeros((1, LANES), F32)
    for blk in range(s // r):
        rows = pl.ds(blk * r, r)
        ls = _log_sigmoid(small_ref[rows, :] + bias_ref[...])
        cum = carry
        for p in _split_bf16(ls, 3):
            cum = cum + jnp.dot(tri, p, preferred_element_type=F32)
        carry = cum[r - 1:r, :]
        neg = cum * (-LOG2E)
        p1, p2, p3 = _split_bf16(neg, FORGET_PIECES)
        kf = (p1.astype(F32) * sel[0:1, :] + p2.astype(F32) * sel[1:2, :] + p3.astype(F32) * sel[2:3, :])
        kf_ref[rows, :] = kf.astype(BF16)


def _forget_cum(small, bias_row, sel, batch, seq):
    return pl.pallas_call(
        _forget_cum_kernel,
        out_shape=jax.ShapeDtypeStruct((batch * seq, LANES), BF16),
        grid=(batch,),
        in_specs=[pl.BlockSpec((seq, LANES), lambda b: (b, 0)),
                  pl.BlockSpec((1, LANES), lambda b: (0, 0)),
                  pl.BlockSpec((SUBLANES, LANES), lambda b: (0, 0))],
        out_specs=pl.BlockSpec((seq, LANES), lambda b: (b, 0)),
        compiler_params=_cparams(("parallel",), 32),
        name="forget_cum",
    )(small, bias_row, sel)


def _gla_kernel(q_ref, k_ref, v_ref, r_ref, small_ref, wgu_ref, bg_ref, gn_ref, o_ref,
                state_ref, ds_ref, sb_ref, of_ref):
    rows = q_ref.shape[0]
    nchunk = rows // CHUNK
    kcol = lambda h: slice(h * GLA_DK, (h + 1) * GLA_DK)
    vcol = lambda h: slice(h * GLA_DV, (h + 1) * GLA_DV)
    crow = lambda c: slice(c * CHUNK, (c + 1) * CHUNK)

    @pl.when(pl.program_id(1) == 0)
    def _():
        state_ref[...] = jnp.zeros_like(state_ref)

    z = jnp.dot(small_ref[...].astype(BF16), wgu_ref[...], preferred_element_type=F32) + bg_ref[...]
    la = _log_sigmoid(z) * (1.0 / GLA_TAU)
    cb = CUM_ROWS
    ri = lax.broadcasted_iota(I32, (cb, cb), 0)
    ci = lax.broadcasted_iota(I32, (cb, cb), 1)
    shift = CHUNK.bit_length() - 1
    same = lax.shift_right_logical(ri, shift) == lax.shift_right_logical(ci, shift)
    tri = jnp.where(same & (ci <= ri), 1.0, 0.0).astype(BF16)
    pieces = _split_bf16(la, 2)
    cum = jnp.concatenate(
        [sum(jnp.dot(tri, p[r:r + cb], preferred_element_type=F32) for p in pieces) for r in range(0, rows, cb)],
        axis=0)
    last = [cum[(c + 1) * CHUNK - 1:(c + 1) * CHUNK, :] for c in range(nchunk)]
    tot = jnp.concatenate([jnp.broadcast_to(t, (CHUNK, GLA_QK)) for t in last], axis=0)
    kd = (k_ref[...].astype(F32) * jnp.exp(tot - cum)).astype(BF16)
    dec_rows = jnp.exp(jnp.concatenate(last + [jnp.zeros((LANES - nchunk, GLA_QK), F32)], axis=0))
    dec_cols = dec_rows.T

    for c in range(nchunk):
        for h in range(GLA_HEADS):
            ds_ref[c, h] = lax.dot_general(kd[crow(c), kcol(h)], v_ref[crow(c), vcol(h)],
                                           (((0,), (0,)), ((), ())), preferred_element_type=F32)
    for h in range(GLA_HEADS):
        st = state_ref[h]
        for c in range(nchunk):
            st = st * jnp.broadcast_to(dec_cols[kcol(h), c:c + 1], (GLA_DK, GLA_DV)) + ds_ref[c, h]
            sb_ref[c, h] = st.astype(BF16)
        state_ref[h] = st
    scale = float(GLA_DK) ** -0.5
    for c in range(nchunk):
        for h in range(GLA_HEADS):
            of_ref[crow(c), vcol(h)] = jnp.dot(q_ref[crow(c), kcol(h)], sb_ref[c, h],
                                               preferred_element_type=F32) * scale
    for h in range(GLA_HEADS):
        o = of_ref[:, vcol(h)]
        o = o * lax.rsqrt(jnp.mean(o * o, axis=-1, keepdims=True) + LN_EPS) * gn_ref[...]
        r = r_ref[:, vcol(h)].astype(F32)
        o_ref[:, vcol(h)] = (o * (r * jax.nn.sigmoid(r))).astype(o_ref.dtype)


def _gla(gq, gk, gv, gr, small, wgu_pad, b_gate, g_norm, batch, seq):
    rows = GLA_ROWS
    nsteps = seq // rows
    row = lambda w: pl.BlockSpec((rows, w), lambda b, i: (b * nsteps + i, 0))
    const = lambda a: pl.BlockSpec(a.shape, lambda b, i: (0, 0))
    return pl.pallas_call(
        _gla_kernel,
        out_shape=jax.ShapeDtypeStruct((batch * seq, GLA_V), BF16),
        grid=(batch, nsteps),
        in_specs=[row(GLA_QK), row(GLA_QK), row(GLA_V), row(GLA_V), row(LANES),
                  const(wgu_pad), const(b_gate), const(g_norm)],
        out_specs=row(GLA_V),
        scratch_shapes=[pltpu.VMEM((GLA_HEADS, GLA_DK, GLA_DV), F32),
                        pltpu.VMEM((rows // CHUNK, GLA_HEADS, GLA_DK, GLA_DV), F32),
                        pltpu.VMEM((rows // CHUNK, GLA_HEADS, GLA_DK, GLA_DV), BF16),
                        pltpu.VMEM((rows, GLA_V), F32)],
        compiler_params=_cparams(("parallel", "arbitrary"), 56),
        name="gla",
    )(gq, gk, gv, gr, small, wgu_pad, b_gate, g_norm)


def _fox_kernel(q_ref, k_ref, v_ref, kf_ref, qsel_ref, o_ref, qa_ref, vt_ref, s_ref, acc_ref, m_ref, l_ref):
    t = FOX_TQ
    seq = q_ref.shape[0]
    nq = seq // t
    c = (float(FOX_HD) ** -0.5) * LOG2E
    hd = FOX_HD
    heads = range(FOX_HEADS_PER_STEP)
    for h in heads:
        for r in range(0, seq, t):
            qa_ref[h, 0:hd, r:r + t] = (q_ref[r:r + t, h * hd:(h + 1) * hd].astype(F32) * c).T.astype(BF16)
            vt_ref[h, :, r:r + t] = v_ref[r:r + t, h * hd:(h + 1) * hd].astype(F32).T.astype(BF16)
        qa_ref[h, hd:hd + LANES, :] = jnp.tile(qsel_ref[h], (1, seq // LANES))

    def key_rows(j):
        return pl.ds(pl.multiple_of(j * t, t), t)

    def scores(h, i, j):
        ka = jnp.concatenate([k_ref[key_rows(j), h * hd:(h + 1) * hd], kf_ref[key_rows(j), :]], axis=1)
        return jnp.dot(ka, qa_ref[h, :, i * t:(i + 1) * t], preferred_element_type=F32)

    def values(h, j, p):
        return jnp.dot(vt_ref[h, :, key_rows(j)], p, preferred_element_type=F32)

    def step(j, slot, masked, nxt):
        if nxt is not None:
            for h in heads:
                s_ref[1 - slot, h] = scores(h, *nxt)
        for h in heads:
            s = s_ref[slot, h]
            if masked:
                causal = lax.broadcasted_iota(I32, (t, t), 0) <= lax.broadcasted_iota(I32, (t, t), 1)
                s = jnp.where(causal, s, -jnp.inf)
            m_old = m_ref[h]
            m_new = jnp.maximum(m_old, jnp.broadcast_to(jnp.max(s, axis=0, keepdims=True), (SUBLANES, t)))
            alpha = jnp.exp2(m_old - m_new)
            p = jnp.exp2(s - m_new[0:1, :])
            m_ref[h] = m_new
            l_ref[h] = alpha * l_ref[h] + jnp.sum(p, axis=0, keepdims=True)
            acc_ref[h] = alpha[0:1, :] * acc_ref[h] + values(h, j, p.astype(BF16))

    for h in heads:
        s_ref[0, h] = scores(h, 0, 0)
    slot = 0
    for i in range(nq):
        for h in heads:
            acc_ref[h] = jnp.zeros((hd, t), F32)
            m_ref[h] = jnp.full((SUBLANES, t), -jnp.inf, F32)
            l_ref[h] = jnp.zeros((SUBLANES, t), F32)
        if i >= 2:
            def pair(jj, _, i=i, slot=slot):
                step(2 * jj, slot, False, (i, 2 * jj + 1))
                step(2 * jj + 1, 1 - slot, False, (i, 2 * jj + 2))
                return 0

            lax.fori_loop(0, i // 2, pair, 0)
        if i % 2 == 1:
            step(i - 1, slot, False, (i, i))
            slot = 1 - slot
        step(i, slot, True, (i + 1, 0) if i + 1 < nq else None)
        slot = 1 - slot
        for h in heads:
            o_ref[i * t:(i + 1) * t, h * hd:(h + 1) * hd] = (acc_ref[h] / l_ref[h][0:1, :]).T.astype(o_ref.dtype)


def _fox(fq, fk, fv, kf, qsel, batch, seq):
    hps = FOX_HEADS_PER_STEP
    w = hps * FOX_HD
    t = FOX_TQ
    head_cols = pl.BlockSpec((seq, w), lambda b, h: (b, h))
    return pl.pallas_call(
        _fox_kernel,
        out_shape=jax.ShapeDtypeStruct((batch * seq, FOX_W), BF16),
        grid=(batch, FOX_HEADS // hps),
        in_specs=[head_cols, head_cols, head_cols,
                  pl.BlockSpec((seq, LANES), lambda b, h: (b, 0)),
                  pl.BlockSpec((hps, LANES, LANES), lambda b, h: (h, 0, 0))],
        out_specs=head_cols,
        scratch_shapes=[pltpu.VMEM((hps, FOX_HD + LANES, seq), BF16), pltpu.VMEM((hps, FOX_HD, seq), BF16),
                        pltpu.VMEM((2, hps, t, t), F32), pltpu.VMEM((hps, FOX_HD, t), F32),
                        pltpu.VMEM((hps, SUBLANES, t), F32), pltpu.VMEM((hps, SUBLANES, t), F32)],
        compiler_params=_cparams(("parallel", "parallel"), 48),
        name="fox",
    )(fq, fk, fv, kf, qsel)


def _post_mix_kernel(gla_ref, fox_ref, zg_ref, zf_ref, x_ref, gin_ref, bin_ref, wg_hbm, wf_hbm, wo_hbm, g_ref, b_ref,
                     wr_ref, br_ref, h2_ref, idx_ref, gate_ref, wg_ref, wf_ref, wo_ref, stage_ref, sem):
    tm = x_ref.shape[0]

    @pl.when(pl.program_id(0) == 0)
    def _():
        for src, dst in ((wg_hbm, wg_ref), (wf_hbm, wf_ref), (wo_hbm, wo_ref)):
            cp = pltpu.make_async_copy(src, stage_ref, sem)
            cp.start()
            cp.wait()
            dst[...] = stage_ref[...].astype(BF16)

    y_gla = jnp.dot(gla_ref[...], wg_ref[...], preferred_element_type=F32)
    y_fox = jnp.dot(fox_ref[...], wf_ref[...], preferred_element_type=F32)
    mixed = (jax.nn.sigmoid(zg_ref[...].astype(F32)) * y_gla
             + jax.nn.sigmoid(zf_ref[...].astype(F32)) * y_fox)
    h = _layer_norm(x_ref[...], gin_ref[...], bin_ref[...])
    pre = DEEPNORM_ALPHA * h + jnp.dot(mixed.astype(BF16), wo_ref[...], preferred_element_type=F32)
    h2 = _layer_norm(pre, g_ref[...], b_ref[...])
    h2_ref[...] = h2

    logits = lax.dot_general(wr_ref[...], h2.astype(BF16), (((1,), (1,)), ((), ())),
                             preferred_element_type=F32)
    logits = logits + jnp.tile(br_ref[...], (1, tm // LANES))
    eidx = lax.broadcasted_iota(I32, (N_EXPERTS, tm), 0)
    vals, idxs = [], []
    for _ in range(TOP_K):
        mx = jnp.max(logits, axis=0, keepdims=True)
        ik = jnp.min(jnp.where(logits == mx, eidx, N_EXPERTS), axis=0, keepdims=True)
        vals.append(mx)
        idxs.append(ik)
        logits = jnp.where(eidx == ik, -jnp.inf, logits)
    exps = [jnp.exp(v - vals[0]) for v in vals]
    denom = exps[0] + exps[1] + exps[2] + exps[3]
    rid = lax.broadcasted_iota(I32, (SUBLANES, tm), 0)
    idx8 = jnp.zeros((SUBLANES, tm), I32)
    gate8 = jnp.zeros((SUBLANES, tm), F32)
    for k in range(TOP_K):
        idx8 = jnp.where(rid == k, idxs[k], idx8)
        gate8 = jnp.where(rid == k, exps[k] / denom, gate8)
    idx_ref[...] = idx8
    gate_ref[...] = gate8


def _post_mix(gla_o, fox_o, zg, zf, x2, g_in, b_in, wg, wf, wo, g, b, wr_t, br_rep):
    t = x2.shape[0]
    tm = TM_PROJ
    row = lambda w: pl.BlockSpec((tm, w), lambda i: (i, 0))
    const = lambda a: pl.BlockSpec(a.shape, lambda i: (0, 0))
    hbm = pl.BlockSpec(memory_space=pl.ANY)
    assert wg.shape == wf.shape == wo.shape
    return pl.pallas_call(
        _post_mix_kernel,
        out_shape=[jax.ShapeDtypeStruct((t, D_MODEL), F32),
                   jax.ShapeDtypeStruct((SUBLANES, t), I32),
                   jax.ShapeDtypeStruct((SUBLANES, t), F32)],
        grid=(t // tm,),
        in_specs=[row(GLA_V), row(FOX_W), row(D_MODEL), row(D_MODEL), row(D_MODEL), const(g_in), const(b_in),
                  hbm, hbm, hbm, const(g), const(b), const(wr_t), const(br_rep)],
        out_specs=[row(D_MODEL),
                   pl.BlockSpec((SUBLANES, tm), lambda i: (0, i)), pl.BlockSpec((SUBLANES, tm), lambda i: (0, i))],
        scratch_shapes=[pltpu.VMEM(wg.shape, BF16), pltpu.VMEM(wf.shape, BF16), pltpu.VMEM(wo.shape, BF16),
                        pltpu.VMEM(wg.shape, F32), pltpu.SemaphoreType.DMA(())],
        compiler_params=_cparams(("arbitrary",), 48),
        name="post_mix",
    )(gla_o, fox_o, zg, zf, x2, g_in, b_in, wg, wf, wo, g, b, wr_t, br_rep)


def _onehot_rows(idx8, tw):
    eidx = lax.broadcasted_iota(I32, (N_EXPERTS, tw), 0)
    hit = eidx == idx8[0:1, :]
    for k in range(1, TOP_K):
        hit = hit | (eidx == idx8[k:k + 1, :])
    return eidx, hit


def _rank_kernel(idx_ref, rank_ref, counts_ref, carry_ref):
    tw = idx_ref.shape[1]

    @pl.when(pl.program_id(0) == 0)
    def _():
        carry_ref[...] = jnp.zeros_like(carry_ref)

    idx8 = idx_ref[...]
    eidx, hit = _onehot_rows(idx8, tw)
    onehot = jnp.where(hit, 1.0, 0.0).astype(BF16)
    ri = lax.broadcasted_iota(I32, (tw, tw + LANES), 0)
    ci = lax.broadcasted_iota(I32, (tw, tw + LANES), 1)
    upper = jnp.where((ri < ci) | (ci >= tw), 1.0, 0.0).astype(BF16)
    cnt = jnp.dot(onehot, upper, preferred_element_type=F32)
    before = cnt[:, :tw] + jnp.tile(carry_ref[...], (1, tw // LANES))
    rid = lax.broadcasted_iota(I32, (SUBLANES, tw), 0)
    rank8 = jnp.zeros((SUBLANES, tw), I32)
    for k in range(TOP_K):
        rk = jnp.sum(jnp.where(eidx == idx8[k:k + 1, :], before, 0.0), axis=0, keepdims=True)
        rank8 = jnp.where(rid == k, rk.astype(I32), rank8)
    rank_ref[...] = rank8
    carry_ref[...] = carry_ref[...] + cnt[:, tw:]
    counts_ref[...] = carry_ref[...]


def _rank(idx_t):
    t = idx_t.shape[1]
    tw = RANK_TW
    return pl.pallas_call(
        _rank_kernel,
        out_shape=[jax.ShapeDtypeStruct((SUBLANES, t), I32),
                   jax.ShapeDtypeStruct((N_EXPERTS, LANES), F32)],
        grid=(t // tw,),
        in_specs=[pl.BlockSpec((SUBLANES, tw), lambda i: (0, i))],
        out_specs=[pl.BlockSpec((SUBLANES, tw), lambda i: (0, i)),
                   pl.BlockSpec((N_EXPERTS, LANES), lambda i: (0, 0))],
        scratch_shapes=[pltpu.VMEM((N_EXPERTS, LANES), F32)],
        compiler_params=_cparams(("arbitrary",), 32),
        name="rank",
    )(idx_t)


def _dest_kernel(idx_ref, rank_ref, counts_ref, dest_ref, blk_e_ref, ends_ref, *, nb_pad):
    tw = idx_ref.shape[1]
    nblk = jnp.floor((counts_ref[...] + (MOE_BLK - 1)) * (1.0 / MOE_BLK))
    ri = lax.broadcasted_iota(I32, (N_EXPERTS, N_EXPERTS), 0)
    ci = lax.broadcasted_iota(I32, (N_EXPERTS, N_EXPERTS), 1)
    tri = jnp.where(ci <= ri, 1.0, 0.0).astype(BF16)
    end_blk = jnp.dot(tri, nblk.astype(BF16), preferred_element_type=F32)
    start_row = (end_blk - nblk) * float(MOE_BLK)
    idx8 = idx_ref[...]
    eidx = lax.broadcasted_iota(I32, (N_EXPERTS, tw), 0)
    start_t = jnp.tile(start_row, (1, tw // LANES))
    rid = lax.broadcasted_iota(I32, (SUBLANES, tw), 0)
    dest8 = jnp.zeros((SUBLANES, tw), I32)
    for k in range(TOP_K):
        st = jnp.sum(jnp.where(eidx == idx8[k:k + 1, :], start_t, 0.0), axis=0, keepdims=True)
        dest8 = jnp.where(rid == k, st.astype(I32), dest8)
    dest_ref[...] = dest8 + rank_ref[...]
    bid = lax.broadcasted_iota(I32, (N_EXPERTS, nb_pad), 1).astype(F32)
    ends_t = jnp.tile(end_blk, (1, nb_pad // LANES))
    be = jnp.sum(jnp.where(ends_t <= bid, 1.0, 0.0), axis=0, keepdims=True)
    blk_e_ref[...] = jnp.broadcast_to(jnp.minimum(be, N_EXPERTS - 1.0), (SUBLANES, nb_pad)).astype(I32)
    ends_ref[...] = end_blk.astype(I32)


def _dest(idx_t, rank_t, counts, nb_pad):
    t = idx_t.shape[1]
    tw = RANK_TW
    tok = pl.BlockSpec((SUBLANES, tw), lambda i: (0, i))
    return pl.pallas_call(
        functools.partial(_dest_kernel, nb_pad=nb_pad),
        out_shape=[jax.ShapeDtypeStruct((SUBLANES, t), I32),
                   jax.ShapeDtypeStruct((SUBLANES, nb_pad), I32),
                   jax.ShapeDtypeStruct((N_EXPERTS, LANES), I32)],
        grid=(t // tw,),
        in_specs=[tok, tok, pl.BlockSpec((N_EXPERTS, LANES), lambda i: (0, 0))],
        out_specs=[tok, pl.BlockSpec((SUBLANES, nb_pad), lambda i: (0, 0)),
                   pl.BlockSpec((N_EXPERTS, LANES), lambda i: (0, 0))],
        compiler_params=_cparams(("arbitrary",), 32),
        name="dest",
    )(idx_t, rank_t, counts)


def _dispatch_kernel(ends_ref, dest_ref, h2_ref, xs_ref, slab_ref, zero_ref, sem, zsem):
    td = h2_ref.shape[0]

    nb = xs_ref.shape[0] // MOE_BLK
    n_used = ends_ref[N_EXPERTS - 1]

    def zero_block(blk):
        start = pl.multiple_of(blk * MOE_BLK, MOE_BLK)
        return pltpu.make_async_copy(zero_ref, xs_ref.at[pl.ds(start, MOE_BLK)], zsem)

    def has_rows(e):
        prev = jnp.where(e == 0, 0, ends_ref[jnp.maximum(e - 1, 0)])
        return ends_ref[e] > prev

    @pl.when(pl.program_id(0) == 0)
    def _():
        zero_ref[...] = jnp.zeros_like(zero_ref)

        def each(fn):
            def last_block(e, _):
                @pl.when(has_rows(e))
                def _():
                    fn(zero_block(ends_ref[e] - 1))
                return 0

            def tail_block(blk, _):
                fn(zero_block(blk))
                return 0

            lax.fori_loop(0, N_EXPERTS, last_block, 0)
            lax.fori_loop(n_used, nb, tail_block, 0)

        each(lambda cp: cp.start())
        each(lambda cp: cp.wait())

    for c in range(SLAB_ROWS):
        slab_ref[:, c, :] = h2_ref[:, c * LANES:(c + 1) * LANES]

    def start_rows(t, _):
        for k in range(TOP_K):
            pltpu.make_async_copy(slab_ref.at[t], xs_ref.at[dest_ref[k, t]], sem).start(priority=k % 2)
        return 0

    lax.fori_loop(0, td, start_rows, 0, unroll=ROW_DMA_UNROLL)
    for k in range(TOP_K):
        pltpu.make_async_copy(slab_ref, xs_ref.at[pl.ds(0, td)], sem).wait()


def _dispatch(ends, dest_t, h2, nb):
    t = h2.shape[0]
    td = DISPATCH_TOK
    grid_spec = pltpu.PrefetchScalarGridSpec(
        num_scalar_prefetch=1,
        grid=(t // td,),
        in_specs=[pl.BlockSpec((SUBLANES, td), lambda i, ends: (0, i), memory_space=pltpu.SMEM),
                  pl.BlockSpec((td, D_MODEL), lambda i, ends: (i, 0))],
        out_specs=pl.BlockSpec(memory_space=pl.ANY),
        scratch_shapes=[pltpu.VMEM((td, SLAB_ROWS, LANES), F32), pltpu.VMEM((MOE_BLK, SLAB_ROWS, LANES), F32),
                        pltpu.SemaphoreType.DMA(()), pltpu.SemaphoreType.DMA(())],
    )
    return pl.pallas_call(
        _dispatch_kernel,
        out_shape=jax.ShapeDtypeStruct((nb * MOE_BLK, SLAB_ROWS, LANES), F32),
        grid_spec=grid_spec,
        compiler_params=_cparams(("arbitrary",), 32),
        name="dispatch",
    )(ends, dest_t, h2)


def _experts_kernel(blk_e_ref, nused_ref, ends_ref, cnt_ref, xs_hbm, wgu_hbm, bgu_ref, wd_hbm, bd_ref, ys_hbm,
                    xbuf, ybuf, wgu_f32, wd_f32, wgu_bf, wd_bf, slot_ref, sems, xsems, ysems):
    nb = xs_hbm.shape[0] // MOE_BLK

    def block(j, _):
        _experts_block(j, nb, blk_e_ref, nused_ref, ends_ref, cnt_ref, xs_hbm, wgu_hbm, bgu_ref, wd_hbm, bd_ref, ys_hbm,
                       xbuf, ybuf, wgu_f32, wd_f32, wgu_bf, wd_bf, slot_ref, sems, xsems, ysems)
        return 0

    lax.fori_loop(0, nb, block, 0)


def _experts_block(j, nb, blk_e_ref, nused_ref, ends_ref, cnt_ref, xs_hbm, wgu_hbm, bgu_ref, wd_hbm, bd_ref, ys_hbm,
                   xbuf, ybuf, wgu_f32, wd_f32, wgu_bf, wd_bf, slot_ref, sems, xsems, ysems):
    par = lax.rem(j, 2)
    n_used = nused_ref[0]
    active = j < n_used
    e = blk_e_ref[j]
    prev = blk_e_ref[jnp.maximum(j - 1, 0)]
    fresh = (j == 0) | (e != prev)

    def fetch(expert, slot):
        return (pltpu.make_async_copy(wgu_hbm.at[expert], wgu_f32.at[slot], sems.at[slot, 0]),
                pltpu.make_async_copy(wd_hbm.at[expert], wd_f32.at[slot], sems.at[slot, 1]))

    def x_load(blk, half):
        rows = pl.ds(pl.multiple_of(blk * MOE_BLK, MOE_BLK), MOE_BLK)
        return [pltpu.make_async_copy(xs_hbm.at[rows, c, :], xbuf.at[half, :, pl.ds(c * LANES, LANES)], xsems.at[half])
                for c in range(SLAB_ROWS)]

    def y_store(blk, half):
        rows = pl.ds(pl.multiple_of(blk * MOE_BLK, MOE_BLK), MOE_BLK)
        return [pltpu.make_async_copy(ybuf.at[half, :, pl.ds(c * LANES, LANES)], ys_hbm.at[rows, c, :], ysems.at[half])
                for c in range(SLAB_ROWS)]

    @pl.when(j == 0)
    def _():
        ybuf[...] = jnp.zeros_like(ybuf)
        for cp in x_load(0, 0):
            cp.start()

    @pl.when(j + 1 < n_used)
    def _():
        for cp in x_load(j + 1, 1 - par):
            cp.start()

    @pl.when(j >= 2)
    def _():
        for cp in y_store(j - 2, par):
            cp.wait()

    @pl.when(j == 0)
    def _():
        slot_ref[0] = 0
        for cp in fetch(e, 0):
            cp.start()

    @pl.when(active & fresh)
    def _():
        slot = slot_ref[0]
        for cp in fetch(e, slot):
            cp.wait()
        wgu_bf[...] = wgu_f32[slot].astype(BF16)
        wd_bf[...] = wd_f32[slot].astype(BF16)
        next_blk = ends_ref[e]

        @pl.when(next_blk < n_used)
        def _():
            for cp in fetch(blk_e_ref[jnp.minimum(next_blk, n_used - 1)], 1 - slot):
                cp.start()

        slot_ref[0] = 1 - slot

    @pl.when(active)
    def _():
        for cp in x_load(j, par):
            cp.wait()

    first_blk = jnp.where(e == 0, 0, ends_ref[jnp.maximum(e - 1, 0)])
    valid = jnp.clip(cnt_ref[e] - (j - first_blk) * MOE_BLK, 0, MOE_BLK)
    groups = lax.shift_right_logical(valid + (EXPERT_ROW_GROUP - 1), EXPERT_ROW_GROUP.bit_length() - 1)

    def compute(m):
        x = xbuf[par, 0:m].astype(BF16)
        gu = jnp.dot(x, wgu_bf[...], preferred_element_type=F32) + bgu_ref[pl.ds(e, 1), :]
        gate = jnp.minimum(gu[:, :D_FF], SWIGLU_LIMIT)
        up = jnp.clip(gu[:, D_FF:], -SWIGLU_LIMIT, SWIGLU_LIMIT)
        glu = gate * jax.nn.sigmoid(SWIGLU_ALPHA * gate)
        act = ((up + 1.0) * glu).astype(BF16)
        ybuf[par, 0:m] = jnp.dot(act, wd_bf[...], preferred_element_type=F32) + bd_ref[pl.ds(e, 1), :]

    for g in range(1, MOE_BLK // EXPERT_ROW_GROUP + 1):
        pl.when(active & (groups == g))(functools.partial(compute, g * EXPERT_ROW_GROUP))

    @pl.when(jnp.logical_not(active))
    def _():
        ybuf[par] = jnp.zeros((MOE_BLK, D_MODEL), F32)

    for cp in y_store(j, par):
        cp.start()

    @pl.when(j == nb - 1)
    def _():
        for cp in y_store(j, par) + y_store(j - 1, 1 - par):
            cp.wait()


def _experts(blk_e, nused, ends, cnt, xs, w_gate_up, b_gate_up, w_down, b_down, nb):
    assert nb >= 2
    whole = lambda a: pl.BlockSpec(a.shape, lambda i, be, nu, en, cn: (0,) * a.ndim)
    grid_spec = pltpu.PrefetchScalarGridSpec(
        num_scalar_prefetch=4,
        grid=(1,),
        in_specs=[pl.BlockSpec(memory_space=pl.ANY),
                  pl.BlockSpec(memory_space=pl.ANY),
                  whole(b_gate_up),
                  pl.BlockSpec(memory_space=pl.ANY),
                  whole(b_down)],
        out_specs=pl.BlockSpec(memory_space=pl.ANY),
        scratch_shapes=[pltpu.VMEM((2, MOE_BLK, D_MODEL), F32), pltpu.VMEM((2, MOE_BLK, D_MODEL), F32),
                        pltpu.VMEM((2, D_MODEL, 2 * D_FF), F32), pltpu.VMEM((2, D_FF, D_MODEL), F32),
                        pltpu.VMEM((D_MODEL, 2 * D_FF), BF16), pltpu.VMEM((D_FF, D_MODEL), BF16),
                        pltpu.SMEM((1,), I32), pltpu.SemaphoreType.DMA((2, 2)),
                        pltpu.SemaphoreType.DMA((2,)), pltpu.SemaphoreType.DMA((2,))],
    )
    return pl.pallas_call(
        _experts_kernel,
        out_shape=jax.ShapeDtypeStruct((nb * MOE_BLK, SLAB_ROWS, LANES), F32),
        grid_spec=grid_spec,
        compiler_params=_cparams(("arbitrary",), 56),
        name="experts",
    )(blk_e, nused, ends, cnt, xs, w_gate_up, b_gate_up, w_down, b_down)


def _combine_kernel(dest_ref, dest_next_ref, gate_ref, h2_ref, g_ref, b_ref, ys_ref, o_ref, buf_ref, ffn_ref, sems):
    tc = h2_ref.shape[0]
    i = pl.program_id(0)
    slot = lax.rem(i, 2)

    def gather(d_ref, s):
        def start_rows(t, _):
            for k in range(TOP_K):
                pltpu.make_async_copy(ys_ref.at[d_ref[k, t]], buf_ref.at[s, k, t], sems.at[s]).start(priority=k % 2)
            return 0

        lax.fori_loop(0, tc, start_rows, 0, unroll=ROW_DMA_UNROLL)

    @pl.when(i == 0)
    def _():
        gather(dest_ref, 0)

    @pl.when(i + 1 < pl.num_programs(0))
    def _():
        gather(dest_next_ref, 1 - slot)

    for k in range(TOP_K):
        pltpu.make_async_copy(ys_ref.at[pl.ds(0, tc)], buf_ref.at[slot, k], sems.at[slot]).wait()

    def weigh(t, _):
        acc = gate_ref[0, t] * buf_ref[slot, 0, t]
        for k in range(1, TOP_K):
            acc = acc + gate_ref[k, t] * buf_ref[slot, k, t]
        ffn_ref[t] = acc
        return 0

    lax.fori_loop(0, tc, weigh, 0, unroll=ROW_DMA_UNROLL)
    ffn = jnp.concatenate([ffn_ref[:, c, :] for c in range(SLAB_ROWS)], axis=1)
    o_ref[...] = _layer_norm(DEEPNORM_ALPHA * h2_ref[...] + ffn, g_ref[...], b_ref[...])


def _combine(dest_t, gate_t, h2, g, b, ys):
    t = h2.shape[0]
    tc = COMBINE_TOK
    last = t // tc - 1
    return pl.pallas_call(
        _combine_kernel,
        out_shape=jax.ShapeDtypeStruct((t, D_MODEL), F32),
        grid=(t // tc,),
        in_specs=[pl.BlockSpec((SUBLANES, tc), lambda i: (0, i), memory_space=pltpu.SMEM),
                  pl.BlockSpec((SUBLANES, tc), lambda i: (0, jnp.minimum(i + 1, last)), memory_space=pltpu.SMEM),
                  pl.BlockSpec((SUBLANES, tc), lambda i: (0, i), memory_space=pltpu.SMEM),
                  pl.BlockSpec((tc, D_MODEL), lambda i: (i, 0)),
                  pl.BlockSpec((1, D_MODEL), lambda i: (0, 0)),
                  pl.BlockSpec((1, D_MODEL), lambda i: (0, 0)),
                  pl.BlockSpec(memory_space=pl.ANY)],
        out_specs=pl.BlockSpec((tc, D_MODEL), lambda i: (i, 0)),
        scratch_shapes=[pltpu.VMEM((2, TOP_K, tc, SLAB_ROWS, LANES), F32), pltpu.VMEM((tc, SLAB_ROWS, LANES), F32),
                        pltpu.SemaphoreType.DMA((2,))],
        compiler_params=_cparams(("arbitrary",), 32),
        name="combine",
    )(dest_t, dest_t, gate_t, h2, g, b, ys)


def kernel(x, ln_in_g, ln_in_b, w_in, w_gla_gate_up, b_gla_gate, g_gla_norm, b_forget, w_gla_proj, w_fox_proj, w_out, ln_mix_g, ln_mix_b, w_router, b_router, w_gate_up, b_gate_up, w_down, b_down, ln_ffn_g, ln_ffn_b):
    batch, seq, d = x.shape
    assert d == D_MODEL and w_in.shape[0] == DEPTH == 1
    assert seq % max(GLA_ROWS, FOX_TQ, CUM_ROWS) == 0
    t = batch * seq
    assert t % max(TM_PROJ, RANK_TW, DISPATCH_TOK, COMBINE_TOK) == 0 and t // MOE_BLK <= 256
    row = lambda v: v.reshape(1, -1).astype(F32)

    n_forget = FOX_HEADS * FORGET_PIECES
    lane = jnp.arange(LANES)
    in_forget = (lane >= FORGET_COL0) & (lane < FORGET_COL0 + n_forget)
    piece = (lane - FORGET_COL0) % FORGET_PIECES
    forget_bias = jnp.zeros((1, LANES), F32).at[0, FORGET_COL0:FORGET_COL0 + n_forget].set(
        jnp.repeat(b_forget[0].astype(F32), FORGET_PIECES))
    piece_sel = jnp.stack([(in_forget & (piece == p)).astype(F32) for p in range(FORGET_PIECES)]
                          + [jnp.zeros((LANES,), F32)] * (SUBLANES - FORGET_PIECES))
    head_of_lane = (lane - FORGET_COL0) // FORGET_PIECES
    qsel = jnp.broadcast_to(jnp.stack([(in_forget & (head_of_lane == h)) for h in range(FOX_HEADS)])[:, :, None],
                            (FOX_HEADS, LANES, LANES)).astype(BF16)
    wgu_pad = jnp.concatenate([w_gla_gate_up[0], jnp.zeros((LANES - GLA_RANK, GLA_QK), F32)], axis=0).astype(BF16)

    x2 = x.reshape(t, d)
    gq, gk, gv, gr, fq, fk, fv, zg, zf, small = _ln_inproj(x2, row(ln_in_g), row(ln_in_b), jnp.transpose(w_in[0]))
    kf = _forget_cum(small, forget_bias, piece_sel, batch, seq)
    gla_o = _gla(gq, gk, gv, gr, small, wgu_pad, row(b_gla_gate[0]), row(g_gla_norm[0]), batch, seq)
    fox_o = _fox(fq, fk, fv, kf, qsel, batch, seq)

    br_rep = jnp.broadcast_to(b_router[0].astype(F32)[:, None], (N_EXPERTS, LANES))
    h2, idx_t, gate_t = _post_mix(
        gla_o, fox_o, zg, zf, x2, row(ln_in_g), row(ln_in_b),
        w_gla_proj[0], w_fox_proj[0], w_out[0],
        row(ln_mix_g[0]), row(ln_mix_b[0]), w_router[0].T.astype(BF16), br_rep)

    nb = (t * TOP_K + N_EXPERTS * (MOE_BLK - 1) + MOE_BLK - 1) // MOE_BLK
    nb_pad = (nb + LANES - 1) // LANES * LANES
    rank_t, counts = _rank(idx_t)
    dest_t, blk_e8, ends = _dest(idx_t, rank_t, counts, nb_pad)
    ends1 = ends[:, 0]
    xs = _dispatch(ends1, dest_t, h2, nb)
    ys = _experts(blk_e8[0, :nb], ends1[N_EXPERTS - 1:], ends1, counts[:, 0].astype(I32), xs, w_gate_up[0], b_gate_up[0],
                  w_down[0], b_down[0], nb)
    out = _combine(dest_t, gate_t, h2, row(ln_ffn_g[0]), row(ln_ffn_b[0]), ys)
    return out.reshape(batch, seq, d)
```

```python
import functools

import jax
import jax.numpy as jnp
from jax import lax
from jax.experimental import pallas as pl
from jax.experimental.pallas import tpu as pltpu

F32 = jnp.float32
BF16 = jnp.bfloat16
I32 = jnp.int32

D_MODEL = 1024
CHUNK = 64
GLA_HEADS, GLA_DK, GLA_DV, GLA_RANK, GLA_TAU = 4, 128, 256, 16, 16.0
GLA_QK = GLA_HEADS * GLA_DK
GLA_V = GLA_HEADS * GLA_DV
FOX_HEADS, FOX_HD = 8, 128
FOX_W = FOX_HEADS * FOX_HD
N_EXPERTS, TOP_K, D_FF = 32, 4, 1024
SWIGLU_LIMIT, SWIGLU_ALPHA = 7.0, 1.702
LN_EPS = 1e-5
DEPTH = 1
DEEPNORM_ALPHA = (2 * DEPTH) ** 0.25
IN_SPLITS = (GLA_QK, GLA_QK, GLA_V, GLA_RANK, GLA_V, FOX_W, FOX_W, FOX_W, FOX_HEADS, D_MODEL, D_MODEL)

LANES = 128
SUBLANES = 8
VMEM_BYTES_V7X = 64 * 1024 * 1024
SLAB_ROWS = D_MODEL // LANES

FORGET_COL0 = GLA_RANK
FORGET_PIECES = 3

TM_PROJ = 512
GLA_ROWS = 1024
FOX_TQ = 512
FOX_HEADS_PER_STEP = 2
LOG2E = 1.4426950408889634
CUM_ROWS = 256
RANK_TW = 512
MOE_BLK = 512
EXPERT_ROW_GROUP = 64
DISPATCH_TOK = 256
COMBINE_TOK = 256
ROW_DMA_UNROLL = 8


def _cparams(sem, vmem_mib):
    limit = vmem_mib * 1024 * 1024
    assert limit < VMEM_BYTES_V7X
    return pltpu.CompilerParams(dimension_semantics=sem, vmem_limit_bytes=limit)


def _log_sigmoid(z):
    return jnp.minimum(z, 0.0) - jnp.log1p(jnp.exp(-jnp.abs(z)))


def _layer_norm(x, g, b):
    mu = jnp.mean(x, axis=-1, keepdims=True)
    xc = x - mu
    var = jnp.mean(xc * xc, axis=-1, keepdims=True)
    return xc * lax.rsqrt(var + LN_EPS) * g + b


def _split_bf16(x, pieces):
    out = []
    for _ in range(pieces):
        p = x.astype(BF16)
        out.append(p)
        x = x - p.astype(F32)
    return out


_NARROW_A = sum(IN_SPLITS[:3])
_NARROW_B = sum(IN_SPLITS[:8])


_REGROUP_COLS = 512
_N_WIDE = sum(IN_SPLITS) - GLA_RANK - FOX_HEADS


def _regroup_w_in(wt_hbm, wide_ref, small_ref, buf, nbuf, sems, nsem):
    cols = _REGROUP_COLS
    assert _NARROW_A % cols == 0 and (_NARROW_B - GLA_RANK) % cols == 0

    def load(j):
        c0 = j * cols
        src = c0 + (GLA_RANK if c0 >= _NARROW_A else 0) + (FOX_HEADS if c0 >= _NARROW_B - GLA_RANK else 0)
        return pltpu.make_async_copy(wt_hbm.at[pl.ds(src, cols), :], buf.at[j % 2], sems.at[j % 2])

    load(0).start()
    nbuf[...] = jnp.zeros_like(nbuf)
    copies = [pltpu.make_async_copy(wt_hbm.at[pl.ds(_NARROW_A, GLA_RANK), :], nbuf.at[pl.ds(0, GLA_RANK), :], nsem)]
    for h in range(FOX_HEADS):
        for p in range(FORGET_PIECES):
            copies.append(pltpu.make_async_copy(
                wt_hbm.at[pl.ds(_NARROW_B + h, 1), :],
                nbuf.at[pl.ds(FORGET_COL0 + FORGET_PIECES * h + p, 1), :], nsem))
    for cp in copies:
        cp.start()
    for cp in copies:
        cp.wait()
    small_ref[...] = nbuf[...].T.astype(small_ref.dtype)
    n = _N_WIDE // cols
    for j in range(n):
        if j + 1 < n:
            load(j + 1).start()
        load(j).wait()
        wide_ref[:, j * cols:(j + 1) * cols] = buf[j % 2].T.astype(wide_ref.dtype)


_PROJ_WIDTHS = (GLA_QK, GLA_QK, GLA_V, GLA_V, FOX_W, FOX_W, FOX_W, D_MODEL, D_MODEL)


def _ln_inproj_kernel(x0_ref, xn_ref, g_ref, b_ref, wt_hbm, *refs):
    n_out = len(_PROJ_WIDTHS)
    wide_refs, small_ref = refs[:n_out], refs[n_out]
    hb_refs, w_ref, ws_ref = refs[n_out + 1:n_out + 3], refs[n_out + 3], refs[n_out + 4]
    i = pl.program_id(0)

    @pl.when(i == 0)
    def _():
        _regroup_w_in(wt_hbm, w_ref, ws_ref, *refs[n_out + 5:])
        hb_refs[0][...] = _layer_norm(x0_ref[...], g_ref[...], b_ref[...]).astype(BF16)

    def project(cur_ref, nxt_ref):
        hb = cur_ref[...]
        off = 0
        for o_ref in wide_refs:
            n = o_ref.shape[1]
            o_ref[...] = jnp.dot(hb, w_ref[:, off:off + n], preferred_element_type=F32).astype(o_ref.dtype)
            off += n
        small_ref[...] = jnp.dot(hb, ws_ref[...], preferred_element_type=F32)
        nxt_ref[...] = _layer_norm(xn_ref[...], g_ref[...], b_ref[...]).astype(BF16)

    for parity in range(2):
        pl.when(lax.rem(i, 2) == parity)(functools.partial(project, hb_refs[parity], hb_refs[1 - parity]))


def _ln_inproj(x2, g, b, wt):
    t = x2.shape[0]
    tm = TM_PROJ
    last = t // tm - 1
    row = lambda w: pl.BlockSpec((tm, w), lambda i: (i, 0))
    const = lambda a: pl.BlockSpec(a.shape, lambda i: (0, 0), pipeline_mode=pl.Buffered(1))
    out_shape = ([jax.ShapeDtypeStruct((t, w), BF16) for w in _PROJ_WIDTHS]
                 + [jax.ShapeDtypeStruct((t, LANES), F32)])
    out_specs = [row(w) for w in _PROJ_WIDTHS] + [row(LANES)]
    return pl.pallas_call(
        _ln_inproj_kernel,
        out_shape=out_shape,
        grid=(t // tm,),
        in_specs=[pl.BlockSpec((tm, D_MODEL), lambda i: (0, 0), pipeline_mode=pl.Buffered(1)),
                  pl.BlockSpec((tm, D_MODEL), lambda i: (jnp.minimum(i + 1, last), 0)),
                  const(g), const(b), pl.BlockSpec(memory_space=pl.ANY)],
        out_specs=out_specs,
        scratch_shapes=[pltpu.VMEM((tm, D_MODEL), BF16), pltpu.VMEM((tm, D_MODEL), BF16),
                        pltpu.VMEM((D_MODEL, _N_WIDE), BF16), pltpu.VMEM((D_MODEL, LANES), BF16),
                        pltpu.VMEM((2, _REGROUP_COLS, D_MODEL), F32), pltpu.VMEM((LANES, D_MODEL), F32),
                        pltpu.SemaphoreType.DMA((2,)), pltpu.SemaphoreType.DMA(())],
        compiler_params=_cparams(("arbitrary",), 56),
        name="ln_inproj",
    )(x2, x2, g, b, wt)


def _forget_cum_kernel(small_ref, bias_ref, sel_ref, kf_ref):
    s = small_ref.shape[0]
    r = CUM_ROWS
    ri = lax.broadcasted_iota(I32, (r, r), 0)
    ci = lax.broadcasted_iota(I32, (r, r), 1)
    tri = jnp.where(ci <= ri, 1.0, 0.0).astype(BF16)
    sel = sel_ref[...]
    carry = jnp.zeros((1, LANES), F32)
    for blk in range(s // r):
        rows = pl.ds(blk * r, r)
        ls = _log_sigmoid(small_ref[rows, :] + bias_ref[...])
        cum = carry
        for p in _split_bf16(ls, 3):
            cum = cum + jnp.dot(tri, p, preferred_element_type=F32)
        carry = cum[r - 1:r, :]
        neg = cum * (-LOG2E)
        p1, p2, p3 = _split_bf16(neg, FORGET_PIECES)
        kf = (p1.astype(F32) * sel[0:1, :] + p2.astype(F32) * sel[1:2, :] + p3.astype(F32) * sel[2:3, :])
        kf_ref[rows, :] = kf.astype(BF16)


def _forget_cum(small, bias_row, sel, batch, seq):
    return pl.pallas_call(
        _forget_cum_kernel,
        out_shape=jax.ShapeDtypeStruct((batch * seq, LANES), BF16),
        grid=(batch,),
        in_specs=[pl.BlockSpec((seq, LANES), lambda b: (b, 0)),
                  pl.BlockSpec((1, LANES), lambda b: (0, 0)),
                  pl.BlockSpec((SUBLANES, LANES), lambda b: (0, 0))],
        out_specs=pl.BlockSpec((seq, LANES), lambda b: (b, 0)),
        compiler_params=_cparams(("parallel",), 32),
        name="forget_cum",
    )(small, bias_row, sel)


def _gla_kernel(q_ref, k_ref, v_ref, r_ref, small_ref, wgu_ref, bg_ref, gn_ref, o_ref,
                state_ref, ds_ref, sb_ref, of_ref):
    rows = q_ref.shape[0]
    nchunk = rows // CHUNK
    kcol = lambda h: slice(h * GLA_DK, (h + 1) * GLA_DK)
    vcol = lambda h: slice(h * GLA_DV, (h + 1) * GLA_DV)
    crow = lambda c: slice(c * CHUNK, (c + 1) * CHUNK)

    @pl.when(pl.program_id(1) == 0)
    def _():
        state_ref[...] = jnp.zeros_like(state_ref)

    z = jnp.dot(small_ref[...].astype(BF16), wgu_ref[...], preferred_element_type=F32) + bg_ref[...]
    la = _log_sigmoid(z) * (1.0 / GLA_TAU)
    cb = CUM_ROWS
    ri = lax.broadcasted_iota(I32, (cb, cb), 0)
    ci = lax.broadcasted_iota(I32, (cb, cb), 1)
    shift = CHUNK.bit_length() - 1
    same = lax.shift_right_logical(ri, shift) == lax.shift_right_logical(ci, shift)
    tri = jnp.where(same & (ci <= ri), 1.0, 0.0).astype(BF16)
    pieces = _split_bf16(la, 2)
    cum = jnp.concatenate(
        [sum(jnp.dot(tri, p[r:r + cb], preferred_element_type=F32) for p in pieces) for r in range(0, rows, cb)],
        axis=0)
    last = [cum[(c + 1) * CHUNK - 1:(c + 1) * CHUNK, :] for c in range(nchunk)]
    tot = jnp.concatenate([jnp.broadcast_to(t, (CHUNK, GLA_QK)) for t in last], axis=0)
    kd = (k_ref[...].astype(F32) * jnp.exp(tot - cum)).astype(BF16)
    dec_rows = jnp.exp(jnp.concatenate(last + [jnp.zeros((LANES - nchunk, GLA_QK), F32)], axis=0))
    dec_cols = dec_rows.T

    for c in range(nchunk):
        for h in range(GLA_HEADS):
            ds_ref[c, h] = lax.dot_general(kd[crow(c), kcol(h)], v_ref[crow(c), vcol(h)],
                                           (((0,), (0,)), ((), ())), preferred_element_type=F32)
    for h in range(GLA_HEADS):
        st = state_ref[h]
        for c in range(nchunk):
            st = st * jnp.broadcast_to(dec_cols[kcol(h), c:c + 1], (GLA_DK, GLA_DV)) + ds_ref[c, h]
            sb_ref[c, h] = st.astype(BF16)
        state_ref[h] = st
    scale = float(GLA_DK) ** -0.5
    for c in range(nchunk):
        for h in range(GLA_HEADS):
            of_ref[crow(c), vcol(h)] = jnp.dot(q_ref[crow(c), kcol(h)], sb_ref[c, h],
                                               preferred_element_type=F32) * scale
    for h in range(GLA_HEADS):
        o = of_ref[:, vcol(h)]
        o = o * lax.rsqrt(jnp.mean(o * o, axis=-1, keepdims=True) + LN_EPS) * gn_ref[...]
        r = r_ref[:, vcol(h)].astype(F32)
        o_ref[:, vcol(h)] = (o * (r * jax.nn.sigmoid(r))).astype(o_ref.dtype)


def _gla(gq, gk, gv, gr, small, wgu_pad, b_gate, g_norm, batch, seq):
    rows = GLA_ROWS
    nsteps = seq // rows
    row = lambda w: pl.BlockSpec((rows, w), lambda b, i: (b * nsteps + i, 0))
    const = lambda a: pl.BlockSpec(a.shape, lambda b, i: (0, 0))
    return pl.pallas_call(
        _gla_kernel,
        out_shape=jax.ShapeDtypeStruct((batch * seq, GLA_V), BF16),
        grid=(batch, nsteps),
        in_specs=[row(GLA_QK), row(GLA_QK), row(GLA_V), row(GLA_V), row(LANES),
                  const(wgu_pad), const(b_gate), const(g_norm)],
        out_specs=row(GLA_V),
        scratch_shapes=[pltpu.VMEM((GLA_HEADS, GLA_DK, GLA_DV), F32),
                        pltpu.VMEM((rows // CHUNK, GLA_HEADS, GLA_DK, GLA_DV), F32),
                        pltpu.VMEM((rows // CHUNK, GLA_HEADS, GLA_DK, GLA_DV), BF16),
                        pltpu.VMEM((rows, GLA_V), F32)],
        compiler_params=_cparams(("parallel", "arbitrary"), 56),
        name="gla",
    )(gq, gk, gv, gr, small, wgu_pad, b_gate, g_norm)


def _fox_kernel(q_ref, k_ref, v_ref, kf_ref, qsel_ref, o_ref, qa_ref, vt_ref, s_ref, acc_ref, m_ref, l_ref):
    t = FOX_TQ
    seq = q_ref.shape[0]
    nq = seq // t
    c = (float(FOX_HD) ** -0.5) * LOG2E
    hd = FOX_HD
    heads = range(FOX_HEADS_PER_STEP)
    for h in heads:
        for r in range(0, seq, t):
            qa_ref[h, 0:hd, r:r + t] = (q_ref[r:r + t, h * hd:(h + 1) * hd].astype(F32) * c).T.astype(BF16)
            vt_ref[h, :, r:r + t] = v_ref[r:r + t, h * hd:(h + 1) * hd].astype(F32).T.astype(BF16)
        qa_ref[h, hd:hd + LANES, :] = jnp.tile(qsel_ref[h], (1, seq // LANES))

    def key_rows(j):
        return pl.ds(pl.multiple_of(j * t, t), t)

    def scores(h, i, j):
        ka = jnp.concatenate([k_ref[key_rows(j), h * hd:(h + 1) * hd], kf_ref[key_rows(j), :]], axis=1)
        return jnp.dot(ka, qa_ref[h, :, i * t:(i + 1) * t], preferred_element_type=F32)

    def values(h, j, p):
        return jnp.dot(vt_ref[h, :, key_rows(j)], p, preferred_element_type=F32)

    def step(j, slot, masked, nxt):
        if nxt is not None:
            for h in heads:
                s_ref[1 - slot, h] = scores(h, *nxt)
        for h in heads:
            s = s_ref[slot, h]
            if masked:
                causal = lax.broadcasted_iota(I32, (t, t), 0) <= lax.broadcasted_iota(I32, (t, t), 1)
                s = jnp.where(causal, s, -jnp.inf)
            m_old = m_ref[h]
            m_new = jnp.maximum(m_old, jnp.broadcast_to(jnp.max(s, axis=0, keepdims=True), (SUBLANES, t)))
            alpha = jnp.exp2(m_old - m_new)
            p = jnp.exp2(s - m_new[0:1, :])
            m_ref[h] = m_new
            l_ref[h] = alpha * l_ref[h] + jnp.sum(p, axis=0, keepdims=True)
            acc_ref[h] = alpha[0:1, :] * acc_ref[h] + values(h, j, p.astype(BF16))

    for h in heads:
        s_ref[0, h] = scores(h, 0, 0)
    slot = 0
    for i in range(nq):
        for h in heads:
            acc_ref[h] = jnp.zeros((hd, t), F32)
            m_ref[h] = jnp.full((SUBLANES, t), -jnp.inf, F32)
            l_ref[h] = jnp.zeros((SUBLANES, t), F32)
        if i >= 2:
            def pair(jj, _, i=i, slot=slot):
                step(2 * jj, slot, False, (i, 2 * jj + 1))
                step(2 * jj + 1, 1 - slot, False, (i, 2 * jj + 2))
                return 0

            lax.fori_loop(0, i // 2, pair, 0)
        if i % 2 == 1:
            step(i - 1, slot, False, (i, i))
            slot = 1 - slot
        step(i, slot, True, (i + 1, 0) if i + 1 < nq else None)
        slot = 1 - slot
        for h in heads:
            o_ref[i * t:(i + 1) * t, h * hd:(h + 1) * hd] = (acc_ref[h] / l_ref[h][0:1, :]).T.astype(o_ref.dtype)


def _fox(fq, fk, fv, kf, qsel, batch, seq):
    hps = FOX_HEADS_PER_STEP
    w = hps * FOX_HD
    t = FOX_TQ
    head_cols = pl.BlockSpec((seq, w), lambda b, h: (b, h))
    return pl.pallas_call(
        _fox_kernel,
        out_shape=jax.ShapeDtypeStruct((batch * seq, FOX_W), BF16),
        grid=(batch, FOX_HEADS // hps),
        in_specs=[head_cols, head_cols, head_cols,
                  pl.BlockSpec((seq, LANES), lambda b, h: (b, 0)),
                  pl.BlockSpec((hps, LANES, LANES), lambda b, h: (h, 0, 0))],
        out_specs=head_cols,
        scratch_shapes=[pltpu.VMEM((hps, FOX_HD + LANES, seq), BF16), pltpu.VMEM((hps, FOX_HD, seq), BF16),
                        pltpu.VMEM((2, hps, t, t), F32), pltpu.VMEM((hps, FOX_HD, t), F32),
                        pltpu.VMEM((hps, SUBLANES, t), F32), pltpu.VMEM((hps, SUBLANES, t), F32)],
        compiler_params=_cparams(("parallel", "parallel"), 48),
        name="fox",
    )(fq, fk, fv, kf, qsel)


def _post_mix_kernel(gla_ref, fox_ref, zg_ref, zf_ref, x_ref, gin_ref, bin_ref, wg_hbm, wf_hbm, wo_hbm, g_ref, b_ref,
                     wr_ref, br_ref, h2_ref, idx_ref, gate_ref, wg_ref, wf_ref, wo_ref, stage_ref, sem):
    tm = x_ref.shape[0]

    @pl.when(pl.program_id(0) == 0)
    def _():
        for src, dst in ((wg_hbm, wg_ref), (wf_hbm, wf_ref), (wo_hbm, wo_ref)):
            cp = pltpu.make_async_copy(src, stage_ref, sem)
            cp.start()
            cp.wait()
            dst[...] = stage_ref[...].astype(BF16)

    y_gla = jnp.dot(gla_ref[...], wg_ref[...], preferred_element_type=F32)
    y_fox = jnp.dot(fox_ref[...], wf_ref[...], preferred_element_type=F32)
    mixed = (jax.nn.sigmoid(zg_ref[...].astype(F32)) * y_gla
             + jax.nn.sigmoid(zf_ref[...].astype(F32)) * y_fox)
    h = _layer_norm(x_ref[...], gin_ref[...], bin_ref[...])
    pre = DEEPNORM_ALPHA * h + jnp.dot(mixed.astype(BF16), wo_ref[...], preferred_element_type=F32)
    h2 = _layer_norm(pre, g_ref[...], b_ref[...])
    h2_ref[...] = h2

    logits = lax.dot_general(wr_ref[...], h2.astype(BF16), (((1,), (1,)), ((), ())),
                             preferred_element_type=F32)
    logits = logits + jnp.tile(br_ref[...], (1, tm // LANES))
    eidx = lax.broadcasted_iota(I32, (N_EXPERTS, tm), 0)
    vals, idxs = [], []
    for _ in range(TOP_K):
        mx = jnp.max(logits, axis=0, keepdims=True)
        ik = jnp.min(jnp.where(logits == mx, eidx, N_EXPERTS), axis=0, keepdims=True)
        vals.append(mx)
        idxs.append(ik)
        logits = jnp.where(eidx == ik, -jnp.inf, logits)
    exps = [jnp.exp(v - vals[0]) for v in vals]
    denom = exps[0] + exps[1] + exps[2] + exps[3]
    rid = lax.broadcasted_iota(I32, (SUBLANES, tm), 0)
    idx8 = jnp.zeros((SUBLANES, tm), I32)
    gate8 = jnp.zeros((SUBLANES, tm), F32)
    for k in range(TOP_K):
        idx8 = jnp.where(rid == k, idxs[k], idx8)
        gate8 = jnp.where(rid == k, exps[k] / denom, gate8)
    idx_ref[...] = idx8
    gate_ref[...] = gate8


def _post_mix(gla_o, fox_o, zg, zf, x2, g_in, b_in, wg, wf, wo, g, b, wr_t, br_rep):
    t = x2.shape[0]
    tm = TM_PROJ
    row = lambda w: pl.BlockSpec((tm, w), lambda i: (i, 0))
    const = lambda a: pl.BlockSpec(a.shape, lambda i: (0, 0))
    hbm = pl.BlockSpec(memory_space=pl.ANY)
    assert wg.shape == wf.shape == wo.shape
    return pl.pallas_call(
        _post_mix_kernel,
        out_shape=[jax.ShapeDtypeStruct((t, D_MODEL), F32),
                   jax.ShapeDtypeStruct((SUBLANES, t), I32),
                   jax.ShapeDtypeStruct((SUBLANES, t), F32)],
        grid=(t // tm,),
        in_specs=[row(GLA_V), row(FOX_W), row(D_MODEL), row(D_MODEL), row(D_MODEL), const(g_in), const(b_in),
                  hbm, hbm, hbm, const(g), const(b), const(wr_t), const(br_rep)],
        out_specs=[row(D_MODEL),
                   pl.BlockSpec((SUBLANES, tm), lambda i: (0, i)), pl.BlockSpec((SUBLANES, tm), lambda i: (0, i))],
        scratch_shapes=[pltpu.VMEM(wg.shape, BF16), pltpu.VMEM(wf.shape, BF16), pltpu.VMEM(wo.shape, BF16),
                        pltpu.VMEM(wg.shape, F32), pltpu.SemaphoreType.DMA(())],
        compiler_params=_cparams(("arbitrary",), 48),
        name="post_mix",
    )(gla_o, fox_o, zg, zf, x2, g_in, b_in, wg, wf, wo, g, b, wr_t, br_rep)


def _onehot_rows(idx8, tw):
    eidx = lax.broadcasted_iota(I32, (N_EXPERTS, tw), 0)
    hit = eidx == idx8[0:1, :]
    for k in range(1, TOP_K):
        hit = hit | (eidx == idx8[k:k + 1, :])
    return eidx, hit


def _rank_kernel(idx_ref, rank_ref, counts_ref, carry_ref):
    tw = idx_ref.shape[1]

    @pl.when(pl.program_id(0) == 0)
    def _():
        carry_ref[...] = jnp.zeros_like(carry_ref)

    idx8 = idx_ref[...]
    eidx, hit = _onehot_rows(idx8, tw)
    onehot = jnp.where(hit, 1.0, 0.0).astype(BF16)
    ri = lax.broadcasted_iota(I32, (tw, tw + LANES), 0)
    ci = lax.broadcasted_iota(I32, (tw, tw + LANES), 1)
    upper = jnp.where((ri < ci) | (ci >= tw), 1.0, 0.0).astype(BF16)
    cnt = jnp.dot(onehot, upper, preferred_element_type=F32)
    before = cnt[:, :tw] + jnp.tile(carry_ref[...], (1, tw // LANES))
    rid = lax.broadcasted_iota(I32, (SUBLANES, tw), 0)
    rank8 = jnp.zeros((SUBLANES, tw), I32)
    for k in range(TOP_K):
        rk = jnp.sum(jnp.where(eidx == idx8[k:k + 1, :], before, 0.0), axis=0, keepdims=True)
        rank8 = jnp.where(rid == k, rk.astype(I32), rank8)
    rank_ref[...] = rank8
    carry_ref[...] = carry_ref[...] + cnt[:, tw:]
    counts_ref[...] = carry_ref[...]


def _rank(idx_t):
    t = idx_t.shape[1]
    tw = RANK_TW
    return pl.pallas_call(
        _rank_kernel,
        out_shape=[jax.ShapeDtypeStruct((SUBLANES, t), I32),
                   jax.ShapeDtypeStruct((N_EXPERTS, LANES), F32)],
        grid=(t // tw,),
        in_specs=[pl.BlockSpec((SUBLANES, tw), lambda i: (0, i))],
        out_specs=[pl.BlockSpec((SUBLANES, tw), lambda i: (0, i)),
                   pl.BlockSpec((N_EXPERTS, LANES), lambda i: (0, 0))],
        scratch_shapes=[pltpu.VMEM((N_EXPERTS, LANES), F32)],
        compiler_params=_cparams(("arbitrary",), 32),
        name="rank",
    )(idx_t)


def _dest_kernel(idx_ref, rank_ref, counts_ref, dest_ref, blk_e_ref, ends_ref, *, nb_pad):
    tw = idx_ref.shape[1]
    nblk = jnp.floor((counts_ref[...] + (MOE_BLK - 1)) * (1.0 / MOE_BLK))
    ri = lax.broadcasted_iota(I32, (N_EXPERTS, N_EXPERTS), 0)
    ci = lax.broadcasted_iota(I32, (N_EXPERTS, N_EXPERTS), 1)
    tri = jnp.where(ci <= ri, 1.0, 0.0).astype(BF16)
    end_blk = jnp.dot(tri, nblk.astype(BF16), preferred_element_type=F32)
    start_row = (end_blk - nblk) * float(MOE_BLK)
    idx8 = idx_ref[...]
    eidx = lax.broadcasted_iota(I32, (N_EXPERTS, tw), 0)
    start_t = jnp.tile(start_row, (1, tw // LANES))
    rid = lax.broadcasted_iota(I32, (SUBLANES, tw), 0)
    dest8 = jnp.zeros((SUBLANES, tw), I32)
    for k in range(TOP_K):
        st = jnp.sum(jnp.where(eidx == idx8[k:k + 1, :], start_t, 0.0), axis=0, keepdims=True)
        dest8 = jnp.where(rid == k, st.astype(I32), dest8)
    dest_ref[...] = dest8 + rank_ref[...]
    bid = lax.broadcasted_iota(I32, (N_EXPERTS, nb_pad), 1).astype(F32)
    ends_t = jnp.tile(end_blk, (1, nb_pad // LANES))
    be = jnp.sum(jnp.where(ends_t <= bid, 1.0, 0.0), axis=0, keepdims=True)
    blk_e_ref[...] = jnp.broadcast_to(jnp.minimum(be, N_EXPERTS - 1.0), (SUBLANES, nb_pad)).astype(I32)
    ends_ref[...] = end_blk.astype(I32)


def _dest(idx_t, rank_t, counts, nb_pad):
    t = idx_t.shape[1]
    tw = RANK_TW
    tok = pl.BlockSpec((SUBLANES, tw), lambda i: (0, i))
    return pl.pallas_call(
        functools.partial(_dest_kernel, nb_pad=nb_pad),
        out_shape=[jax.ShapeDtypeStruct((SUBLANES, t), I32),
                   jax.ShapeDtypeStruct((SUBLANES, nb_pad), I32),
                   jax.ShapeDtypeStruct((N_EXPERTS, LANES), I32)],
        grid=(t // tw,),
        in_specs=[tok, tok, pl.BlockSpec((N_EXPERTS, LANES), lambda i: (0, 0))],
        out_specs=[tok, pl.BlockSpec((SUBLANES, nb_pad), lambda i: (0, 0)),
                   pl.BlockSpec((N_EXPERTS, LANES), lambda i: (0, 0))],
        compiler_params=_cparams(("arbitrary",), 32),
        name="dest",
    )(idx_t, rank_t, counts)


def _dispatch_kernel(ends_ref, dest_ref, h2_ref, xs_ref, slab_ref, zero_ref, sem, zsem):
    td = h2_ref.shape[0]

    nb = xs_ref.shape[0] // MOE_BLK
    n_used = ends_ref[N_EXPERTS - 1]

    def zero_block(blk):
        start = pl.multiple_of(blk * MOE_BLK, MOE_BLK)
        return pltpu.make_async_copy(zero_ref, xs_ref.at[pl.ds(start, MOE_BLK)], zsem)

    def has_rows(e):
        prev = jnp.where(e == 0, 0, ends_ref[jnp.maximum(e - 1, 0)])
        return ends_ref[e] > prev

    @pl.when(pl.program_id(0) == 0)
    def _():
        zero_ref[...] = jnp.zeros_like(zero_ref)

        def each(fn):
            def last_block(e, _):
                @pl.when(has_rows(e))
                def _():
                    fn(zero_block(ends_ref[e] - 1))
                return 0

            def tail_block(blk, _):
                fn(zero_block(blk))
                return 0

            lax.fori_loop(0, N_EXPERTS, last_block, 0)
            lax.fori_loop(n_used, nb, tail_block, 0)

        each(lambda cp: cp.start())
        each(lambda cp: cp.wait())

    for c in range(SLAB_ROWS):
        slab_ref[:, c, :] = h2_ref[:, c * LANES:(c + 1) * LANES]

    def start_rows(t, _):
        for k in range(TOP_K):
            pltpu.make_async_copy(slab_ref.at[t], xs_ref.at[dest_ref[k, t]], sem).start(priority=k % 2)
        return 0

    lax.fori_loop(0, td, start_rows, 0, unroll=ROW_DMA_UNROLL)
    for k in range(TOP_K):
        pltpu.make_async_copy(slab_ref, xs_ref.at[pl.ds(0, td)], sem).wait()


def _dispatch(ends, dest_t, h2, nb):
    t = h2.shape[0]
    td = DISPATCH_TOK
    grid_spec = pltpu.PrefetchScalarGridSpec(
        num_scalar_prefetch=1,
        grid=(t // td,),
        in_specs=[pl.BlockSpec((SUBLANES, td), lambda i, ends: (0, i), memory_space=pltpu.SMEM),
                  pl.BlockSpec((td, D_MODEL), lambda i, ends: (i, 0))],
        out_specs=pl.BlockSpec(memory_space=pl.ANY),
        scratch_shapes=[pltpu.VMEM((td, SLAB_ROWS, LANES), F32), pltpu.VMEM((MOE_BLK, SLAB_ROWS, LANES), F32),
                        pltpu.SemaphoreType.DMA(()), pltpu.SemaphoreType.DMA(())],
    )
    return pl.pallas_call(
        _dispatch_kernel,
        out_shape=jax.ShapeDtypeStruct((nb * MOE_BLK, SLAB_ROWS, LANES), F32),
        grid_spec=grid_spec,
        compiler_params=_cparams(("arbitrary",), 32),
        name="dispatch",
    )(ends, dest_t, h2)


def _experts_kernel(blk_e_ref, nused_ref, ends_ref, cnt_ref, xs_hbm, wgu_hbm, bgu_ref, wd_hbm, bd_ref, ys_hbm,
                    xbuf, ybuf, wgu_f32, wd_f32, wgu_bf, wd_bf, slot_ref, sems, xsems, ysems):
    nb = xs_hbm.shape[0] // MOE_BLK

    def block(j, _):
        _experts_block(j, nb, blk_e_ref, nused_ref, ends_ref, cnt_ref, xs_hbm, wgu_hbm, bgu_ref, wd_hbm, bd_ref, ys_hbm,
                       xbuf, ybuf, wgu_f32, wd_f32, wgu_bf, wd_bf, slot_ref, sems, xsems, ysems)
        return 0

    lax.fori_loop(0, nb, block, 0)


def _experts_block(j, nb, blk_e_ref, nused_ref, ends_ref, cnt_ref, xs_hbm, wgu_hbm, bgu_ref, wd_hbm, bd_ref, ys_hbm,
                   xbuf, ybuf, wgu_f32, wd_f32, wgu_bf, wd_bf, slot_ref, sems, xsems, ysems):
    par = lax.rem(j, 2)
    n_used = nused_ref[0]
    active = j < n_used
    e = blk_e_ref[j]
    prev = blk_e_ref[jnp.maximum(j - 1, 0)]
    fresh = (j == 0) | (e != prev)

    def fetch(expert, slot):
        return (pltpu.make_async_copy(wgu_hbm.at[expert], wgu_f32.at[slot], sems.at[slot, 0]),
                pltpu.make_async_copy(wd_hbm.at[expert], wd_f32.at[slot], sems.at[slot, 1]))

    def x_load(blk, half):
        rows = pl.ds(pl.multiple_of(blk * MOE_BLK, MOE_BLK), MOE_BLK)
        return [pltpu.make_async_copy(xs_hbm.at[rows, c, :], xbuf.at[half, :, pl.ds(c * LANES, LANES)], xsems.at[half])
                for c in range(SLAB_ROWS)]

    def y_store(blk, half):
        rows = pl.ds(pl.multiple_of(blk * MOE_BLK, MOE_BLK), MOE_BLK)
        return [pltpu.make_async_copy(ybuf.at[half, :, pl.ds(c * LANES, LANES)], ys_hbm.at[rows, c, :], ysems.at[half])
                for c in range(SLAB_ROWS)]

    @pl.when(j == 0)
    def _():
        ybuf[...] = jnp.zeros_like(ybuf)
        for cp in x_load(0, 0):
            cp.start()

    @pl.when(j + 1 < n_used)
    def _():
        for cp in x_load(j + 1, 1 - par):
            cp.start()

    @pl.when(j >= 2)
    def _():
        for cp in y_store(j - 2, par):
            cp.wait()

    @pl.when(j == 0)
    def _():
        slot_ref[0] = 0
        for cp in fetch(e, 0):
            cp.start()

    @pl.when(active & fresh)
    def _():
        slot = slot_ref[0]
        for cp in fetch(e, slot):
            cp.wait()
        wgu_bf[...] = wgu_f32[slot].astype(BF16)
        wd_bf[...] = wd_f32[slot].astype(BF16)
        next_blk = ends_ref[e]

        @pl.when(next_blk < n_used)
        def _():
            for cp in fetch(blk_e_ref[jnp.minimum(next_blk, n_used - 1)], 1 - slot):
                cp.start()

        slot_ref[0] = 1 - slot

    @pl.when(active)
    def _():
        for cp in x_load(j, par):
            cp.wait()

    first_blk = jnp.where(e == 0, 0, ends_ref[jnp.maximum(e - 1, 0)])
    valid = jnp.clip(cnt_ref[e] - (j - first_blk) * MOE_BLK, 0, MOE_BLK)
    groups = lax.shift_right_logical(valid + (EXPERT_ROW_GROUP - 1), EXPERT_ROW_GROUP.bit_length() - 1)

    def compute(m):
        x = xbuf[par, 0:m].astype(BF16)
        gu = jnp.dot(x, wgu_bf[...], preferred_element_type=F32) + bgu_ref[pl.ds(e, 1), :]
        gate = jnp.minimum(gu[:, :D_FF], SWIGLU_LIMIT)
        up = jnp.clip(gu[:, D_FF:], -SWIGLU_LIMIT, SWIGLU_LIMIT)
        glu = gate * jax.nn.sigmoid(SWIGLU_ALPHA * gate)
        act = ((up + 1.0) * glu).astype(BF16)
        ybuf[par, 0:m] = jnp.dot(act, wd_bf[...], preferred_element_type=F32) + bd_ref[pl.ds(e, 1), :]

    for g in range(1, MOE_BLK // EXPERT_ROW_GROUP + 1):
        pl.when(active & (groups == g))(functools.partial(compute, g * EXPERT_ROW_GROUP))

    @pl.when(jnp.logical_not(active))
    def _():
        ybuf[par] = jnp.zeros((MOE_BLK, D_MODEL), F32)

    for cp in y_store(j, par):
        cp.start()

    @pl.when(j == nb - 1)
    def _():
        for cp in y_store(j, par) + y_store(j - 1, 1 - par):
            cp.wait()


def _experts(blk_e, nused, ends, cnt, xs, w_gate_up, b_gate_up, w_down, b_down, nb):
    assert nb >= 2
    whole = lambda a: pl.BlockSpec(a.shape, lambda i, be, nu, en, cn: (0,) * a.ndim)
    grid_spec = pltpu.PrefetchScalarGridSpec(
        num_scalar_prefetch=4,
        grid=(1,),
        in_specs=[pl.BlockSpec(memory_space=pl.ANY),
                  pl.BlockSpec(memory_space=pl.ANY),
                  whole(b_gate_up),
                  pl.BlockSpec(memory_space=pl.ANY),
                  whole(b_down)],
        out_specs=pl.BlockSpec(memory_space=pl.ANY),
        scratch_shapes=[pltpu.VMEM((2, MOE_BLK, D_MODEL), F32), pltpu.VMEM((2, MOE_BLK, D_MODEL), F32),
                        pltpu.VMEM((2, D_MODEL, 2 * D_FF), F32), pltpu.VMEM((2, D_FF, D_MODEL), F32),
                        pltpu.VMEM((D_MODEL, 2 * D_FF), BF16), pltpu.VMEM((D_FF, D_MODEL), BF16),
                        pltpu.SMEM((1,), I32), pltpu.SemaphoreType.DMA((2, 2)),
                        pltpu.SemaphoreType.DMA((2,)), pltpu.SemaphoreType.DMA((2,))],
    )
    return pl.pallas_call(
        _experts_kernel,
        out_shape=jax.ShapeDtypeStruct((nb * MOE_BLK, SLAB_ROWS, LANES), F32),
        grid_spec=grid_spec,
        compiler_params=_cparams(("arbitrary",), 56),
        name="experts",
    )(blk_e, nused, ends, cnt, xs, w_gate_up, b_gate_up, w_down, b_down)


def _combine_kernel(dest_ref, dest_next_ref, gate_ref, h2_ref, g_ref, b_ref, ys_ref, o_ref, buf_ref, ffn_ref, sems):
    tc = h2_ref.shape[0]
    i = pl.program_id(0)
    slot = lax.rem(i, 2)

    def gather(d_ref, s):
        def start_rows(t, _):
            for k in range(TOP_K):
                pltpu.make_async_copy(ys_ref.at[d_ref[k, t]], buf_ref.at[s, k, t], sems.at[s]).start(priority=k % 2)
            return 0

        lax.fori_loop(0, tc, start_rows, 0, unroll=ROW_DMA_UNROLL)

    @pl.when(i == 0)
    def _():
        gather(dest_ref, 0)

    @pl.when(i + 1 < pl.num_programs(0))
    def _():
        gather(dest_next_ref, 1 - slot)

    for k in range(TOP_K):
        pltpu.make_async_copy(ys_ref.at[pl.ds(0, tc)], buf_ref.at[slot, k], sems.at[slot]).wait()

    def weigh(t, _):
        acc = gate_ref[0, t] * buf_ref[slot, 0, t]
        for k in range(1, TOP_K):
            acc = acc + gate_ref[k, t] * buf_ref[slot, k, t]
        ffn_ref[t] = acc
        return 0

    lax.fori_loop(0, tc, weigh, 0, unroll=ROW_DMA_UNROLL)
    ffn = pltpu.einshape("tcl->t(cl)", ffn_ref[...])
    o_ref[...] = _layer_norm(DEEPNORM_ALPHA * h2_ref[...] + ffn, g_ref[...], b_ref[...])


def _combine(dest_t, gate_t, h2, g, b, ys):
    t = h2.shape[0]
    tc = COMBINE_TOK
    last = t // tc - 1
    return pl.pallas_call(
        _combine_kernel,
        out_shape=jax.ShapeDtypeStruct((t, D_MODEL), F32),
        grid=(t // tc,),
        in_specs=[pl.BlockSpec((SUBLANES, tc), lambda i: (0, i), memory_space=pltpu.SMEM),
                  pl.BlockSpec((SUBLANES, tc), lambda i: (0, jnp.minimum(i + 1, last)), memory_space=pltpu.SMEM),
                  pl.BlockSpec((SUBLANES, tc), lambda i: (0, i), memory_space=pltpu.SMEM),
                  pl.BlockSpec((tc, D_MODEL), lambda i: (i, 0)),
                  pl.BlockSpec((1, D_MODEL), lambda i: (0, 0)),
                  pl.BlockSpec((1, D_MODEL), lambda i: (0, 0)),
                  pl.BlockSpec(memory_space=pl.ANY)],
        out_specs=pl.BlockSpec((tc, D_MODEL), lambda i: (i, 0)),
        scratch_shapes=[pltpu.VMEM((2, TOP_K, tc, SLAB_ROWS, LANES), F32), pltpu.VMEM((tc, SLAB_ROWS, LANES), F32),
                        pltpu.SemaphoreType.DMA((2,))],
        compiler_params=_cparams(("arbitrary",), 32),
        name="combine",
    )(dest_t, dest_t, gate_t, h2, g, b, ys)


def kernel(x, ln_in_g, ln_in_b, w_in, w_gla_gate_up, b_gla_gate, g_gla_norm, b_forget, w_gla_proj, w_fox_proj, w_out, ln_mix_g, ln_mix_b, w_router, b_router, w_gate_up, b_gate_up, w_down, b_down, ln_ffn_g, ln_ffn_b):
    batch, seq, d = x.shape
    assert d == D_MODEL and w_in.shape[0] == DEPTH == 1
    assert seq % max(GLA_ROWS, FOX_TQ, CUM_ROWS) == 0
    t = batch * seq
    assert t % max(TM_PROJ, RANK_TW, DISPATCH_TOK, COMBINE_TOK) == 0 and t // MOE_BLK <= 256
    row = lambda v: v.reshape(1, -1).astype(F32)

    n_forget = FOX_HEADS * FORGET_PIECES
    lane = jnp.arange(LANES)
    in_forget = (lane >= FORGET_COL0) & (lane < FORGET_COL0 + n_forget)
    piece = (lane - FORGET_COL0) % FORGET_PIECES
    forget_bias = jnp.zeros((1, LANES), F32).at[0, FORGET_COL0:FORGET_COL0 + n_forget].set(
        jnp.repeat(b_forget[0].astype(F32), FORGET_PIECES))
    piece_sel = jnp.stack([(in_forget & (piece == p)).astype(F32) for p in range(FORGET_PIECES)]
                          + [jnp.zeros((LANES,), F32)] * (SUBLANES - FORGET_PIECES))
    head_of_lane = (lane - FORGET_COL0) // FORGET_PIECES
    qsel = jnp.broadcast_to(jnp.stack([(in_forget & (head_of_lane == h)) for h in range(FOX_HEADS)])[:, :, None],
                            (FOX_HEADS, LANES, LANES)).astype(BF16)
    wgu_pad = jnp.concatenate([w_gla_gate_up[0], jnp.zeros((LANES - GLA_RANK, GLA_QK), F32)], axis=0).astype(BF16)

    x2 = x.reshape(t, d)
    gq, gk, gv, gr, fq, fk, fv, zg, zf, small = _ln_inproj(x2, row(ln_in_g), row(ln_in_b), jnp.transpose(w_in[0]))
    kf = _forget_cum(small, forget_bias, piece_sel, batch, seq)
    gla_o = _gla(gq, gk, gv, gr, small, wgu_pad, row(b_gla_gate[0]), row(g_gla_norm[0]), batch, seq)
    fox_o = _fox(fq, fk, fv, kf, qsel, batch, seq)

    br_rep = jnp.broadcast_to(b_router[0].astype(F32)[:, None], (N_EXPERTS, LANES))
    h2, idx_t, gate_t = _post_mix(
        gla_o, fox_o, zg, zf, x2, row(ln_in_g), row(ln_in_b),
        w_gla_proj[0], w_fox_proj[0], w_out[0],
        row(ln_mix_g[0]), row(ln_mix_b[0]), w_router[0].T.astype(BF16), br_rep)

    nb = (t * TOP_K + N_EXPERTS * (MOE_BLK - 1) + MOE_BLK - 1) // MOE_BLK
    nb_pad = (nb + LANES - 1) // LANES * LANES
    rank_t, counts = _rank(idx_t)
    dest_t, blk_e8, ends = _dest(idx_t, rank_t, counts, nb_pad)
    ends1 = ends[:, 0]
    xs = _dispatch(ends1, dest_t, h2, nb)
    ys = _experts(blk_e8[0, :nb], ends1[N_EXPERTS - 1:], ends1, counts[:, 0].astype(I32), xs, w_gate_up[0], b_gate_up[0],
                  w_down[0], b_down[0], nb)
    out = _combine(dest_t, gate_t, h2, row(ln_ffn_g[0]), row(ln_ffn_b[0]), ys)
    return out.reshape(batch, seq, d)
```

```python
import functools

import jax
import jax.numpy as jnp
from jax import lax
from jax.experimental import pallas as pl
from jax.experimental.pallas import tpu as pltpu

F32 = jnp.float32
BF16 = jnp.bfloat16
I32 = jnp.int32

D_MODEL = 1024
CHUNK = 64
GLA_HEADS, GLA_DK, GLA_DV, GLA_RANK, GLA_TAU = 4, 128, 256, 16, 16.0
GLA_QK = GLA_HEADS * GLA_DK
GLA_V = GLA_HEADS * GLA_DV
FOX_HEADS, FOX_HD = 8, 128
FOX_W = FOX_HEADS * FOX_HD
N_EXPERTS, TOP_K, D_FF = 32, 4, 1024
SWIGLU_LIMIT, SWIGLU_ALPHA = 7.0, 1.702
LN_EPS = 1e-5
DEPTH = 1
DEEPNORM_ALPHA = (2 * DEPTH) ** 0.25
IN_SPLITS = (GLA_QK, GLA_QK, GLA_V, GLA_RANK, GLA_V, FOX_W, FOX_W, FOX_W, FOX_HEADS, D_MODEL, D_MODEL)

LANES = 128
SUBLANES = 8
VMEM_BYTES_V7X = 64 * 1024 * 1024
SLAB_ROWS = D_MODEL // LANES

FORGET_COL0 = GLA_RANK
FORGET_PIECES = 3

TM_PROJ = 512
GLA_ROWS = 1024
FOX_TQ = 512
FOX_HEADS_PER_STEP = 2
LOG2E = 1.4426950408889634
CUM_ROWS = 256
RANK_TW = 512
MOE_BLK = 512
EXPERT_ROW_GROUP = 64
DISPATCH_TOK = 256
COMBINE_TOK = 256
ROW_DMA_UNROLL = 8


def _cparams(sem, vmem_mib):
    limit = vmem_mib * 1024 * 1024
    assert limit < VMEM_BYTES_V7X
    return pltpu.CompilerParams(dimension_semantics=sem, vmem_limit_bytes=limit)


def _log_sigmoid(z):
    return jnp.minimum(z, 0.0) - jnp.log1p(jnp.exp(-jnp.abs(z)))


def _layer_norm(x, g, b):
    mu = jnp.mean(x, axis=-1, keepdims=True)
    xc = x - mu
    var = jnp.mean(xc * xc, axis=-1, keepdims=True)
    return xc * lax.rsqrt(var + LN_EPS) * g + b


def _split_bf16(x, pieces):
    out = []
    for _ in range(pieces):
        p = x.astype(BF16)
        out.append(p)
        x = x - p.astype(F32)
    return out


_NARROW_A = sum(IN_SPLITS[:3])
_NARROW_B = sum(IN_SPLITS[:8])


_REGROUP_COLS = 512
_N_WIDE = sum(IN_SPLITS) - GLA_RANK - FOX_HEADS


def _regroup_w_in(wt_hbm, wide_ref, small_ref, buf, nbuf, sems, nsem):
    cols = _REGROUP_COLS
    assert _NARROW_A % cols == 0 and (_NARROW_B - GLA_RANK) % cols == 0

    def load(j):
        c0 = j * cols
        src = c0 + (GLA_RANK if c0 >= _NARROW_A else 0) + (FOX_HEADS if c0 >= _NARROW_B - GLA_RANK else 0)
        return pltpu.make_async_copy(wt_hbm.at[pl.ds(src, cols), :], buf.at[j % 2], sems.at[j % 2])

    load(0).start()
    nbuf[...] = jnp.zeros_like(nbuf)
    copies = [pltpu.make_async_copy(wt_hbm.at[pl.ds(_NARROW_A, GLA_RANK), :], nbuf.at[pl.ds(0, GLA_RANK), :], nsem)]
    for h in range(FOX_HEADS):
        for p in range(FORGET_PIECES):
            copies.append(pltpu.make_async_copy(
                wt_hbm.at[pl.ds(_NARROW_B + h, 1), :],
                nbuf.at[pl.ds(FORGET_COL0 + FORGET_PIECES * h + p, 1), :], nsem))
    for cp in copies:
        cp.start()
    for cp in copies:
        cp.wait()
    small_ref[...] = nbuf[...].T.astype(small_ref.dtype)
    n = _N_WIDE // cols
    for j in range(n):
        if j + 1 < n:
            load(j + 1).start()
        load(j).wait()
        wide_ref[:, j * cols:(j + 1) * cols] = buf[j % 2].T.astype(wide_ref.dtype)


_PROJ_WIDTHS = (GLA_QK, GLA_QK, GLA_V, GLA_V, FOX_W, FOX_W, FOX_W, D_MODEL, D_MODEL)


def _ln_inproj_kernel(x0_ref, xn_ref, g_ref, b_ref, wt_hbm, *refs):
    n_out = len(_PROJ_WIDTHS)
    wide_refs, small_ref = refs[:n_out], refs[n_out]
    hb_refs, w_ref, ws_ref = refs[n_out + 1:n_out + 3], refs[n_out + 3], refs[n_out + 4]
    i = pl.program_id(0)

    @pl.when(i == 0)
    def _():
        _regroup_w_in(wt_hbm, w_ref, ws_ref, *refs[n_out + 5:])
        hb_refs[0][...] = _layer_norm(x0_ref[...], g_ref[...], b_ref[...]).astype(BF16)

    def project(cur_ref, nxt_ref):
        hb = cur_ref[...]
        off = 0
        for o_ref in wide_refs:
            n = o_ref.shape[1]
            o_ref[...] = jnp.dot(hb, w_ref[:, off:off + n], preferred_element_type=F32).astype(o_ref.dtype)
            off += n
        small_ref[...] = jnp.dot(hb, ws_ref[...], preferred_element_type=F32)
        nxt_ref[...] = _layer_norm(xn_ref[...], g_ref[...], b_ref[...]).astype(BF16)

    for parity in range(2):
        pl.when(lax.rem(i, 2) == parity)(functools.partial(project, hb_refs[parity], hb_refs[1 - parity]))


def _ln_inproj(x2, g, b, wt):
    t = x2.shape[0]
    tm = TM_PROJ
    last = t // tm - 1
    row = lambda w: pl.BlockSpec((tm, w), lambda i: (i, 0))
    const = lambda a: pl.BlockSpec(a.shape, lambda i: (0, 0), pipeline_mode=pl.Buffered(1))
    out_shape = ([jax.ShapeDtypeStruct((t, w), BF16) for w in _PROJ_WIDTHS]
                 + [jax.ShapeDtypeStruct((t, LANES), F32)])
    out_specs = [row(w) for w in _PROJ_WIDTHS] + [row(LANES)]
    return pl.pallas_call(
        _ln_inproj_kernel,
        out_shape=out_shape,
        grid=(t // tm,),
        in_specs=[pl.BlockSpec((tm, D_MODEL), lambda i: (0, 0), pipeline_mode=pl.Buffered(1)),
                  pl.BlockSpec((tm, D_MODEL), lambda i: (jnp.minimum(i + 1, last), 0)),
                  const(g), const(b), pl.BlockSpec(memory_space=pl.ANY)],
        out_specs=out_specs,
        scratch_shapes=[pltpu.VMEM((tm, D_MODEL), BF16), pltpu.VMEM((tm, D_MODEL), BF16),
                        pltpu.VMEM((D_MODEL, _N_WIDE), BF16), pltpu.VMEM((D_MODEL, LANES), BF16),
                        pltpu.VMEM((2, _REGROUP_COLS, D_MODEL), F32), pltpu.VMEM((LANES, D_MODEL), F32),
                        pltpu.SemaphoreType.DMA((2,)), pltpu.SemaphoreType.DMA(())],
        compiler_params=_cparams(("arbitrary",), 56),
        name="ln_inproj",
    )(x2, x2, g, b, wt)


def _forget_cum_kernel(small_ref, bias_ref, sel_ref, kf_ref):
    s = small_ref.shape[0]
    r = CUM_ROWS
    ri = lax.broadcasted_iota(I32, (r, r), 0)
    ci = lax.broadcasted_iota(I32, (r, r), 1)
    tri = jnp.where(ci <= ri, 1.0, 0.0).astype(BF16)
    sel = sel_ref[...]
    carry = jnp.zeros((1, LANES), F32)
    for blk in range(s // r):
        rows = pl.ds(blk * r, r)
        ls = _log_sigmoid(small_ref[rows, :] + bias_ref[...])
        cum = carry
        for p in _split_bf16(ls, 3):
            cum = cum + jnp.dot(tri, p, preferred_element_type=F32)
        carry = cum[r - 1:r, :]
        neg = cum * (-LOG2E)
        p1, p2, p3 = _split_bf16(neg, FORGET_PIECES)
        kf = (p1.astype(F32) * sel[0:1, :] + p2.astype(F32) * sel[1:2, :] + p3.astype(F32) * sel[2:3, :])
        kf_ref[rows, :] = kf.astype(BF16)


def _forget_cum(small, bias_row, sel, batch, seq):
    return pl.pallas_call(
        _forget_cum_kernel,
        out_shape=jax.ShapeDtypeStruct((batch * seq, LANES), BF16),
        grid=(batch,),
        in_specs=[pl.BlockSpec((seq, LANES), lambda b: (b, 0)),
                  pl.BlockSpec((1, LANES), lambda b: (0, 0)),
                  pl.BlockSpec((SUBLANES, LANES), lambda b: (0, 0))],
        out_specs=pl.BlockSpec((seq, LANES), lambda b: (b, 0)),
        compiler_params=_cparams(("parallel",), 32),
        name="forget_cum",
    )(small, bias_row, sel)


def _gla_kernel(q_ref, k_ref, v_ref, r_ref, small_ref, wgu_ref, bg_ref, gn_ref, o_ref,
                state_ref, ds_ref, sb_ref, of_ref):
    rows = q_ref.shape[0]
    nchunk = rows // CHUNK
    kcol = lambda h: slice(h * GLA_DK, (h + 1) * GLA_DK)
    vcol = lambda h: slice(h * GLA_DV, (h + 1) * GLA_DV)
    crow = lambda c: slice(c * CHUNK, (c + 1) * CHUNK)

    @pl.when(pl.program_id(1) == 0)
    def _():
        state_ref[...] = jnp.zeros_like(state_ref)

    z = jnp.dot(small_ref[...].astype(BF16), wgu_ref[...], preferred_element_type=F32) + bg_ref[...]
    la = _log_sigmoid(z) * (1.0 / GLA_TAU)
    cb = CUM_ROWS
    ri = lax.broadcasted_iota(I32, (cb, cb), 0)
    ci = lax.broadcasted_iota(I32, (cb, cb), 1)
    shift = CHUNK.bit_length() - 1
    same = lax.shift_right_logical(ri, shift) == lax.shift_right_logical(ci, shift)
    tri = jnp.where(same & (ci <= ri), 1.0, 0.0).astype(BF16)
    pieces = _split_bf16(la, 2)
    cum = jnp.concatenate(
        [sum(jnp.dot(tri, p[r:r + cb], preferred_element_type=F32) for p in pieces) for r in range(0, rows, cb)],
        axis=0)
    last = [cum[(c + 1) * CHUNK - 1:(c + 1) * CHUNK, :] for c in range(nchunk)]
    tot = jnp.concatenate([jnp.broadcast_to(t, (CHUNK, GLA_QK)) for t in last], axis=0)
    kd = (k_ref[...].astype(F32) * jnp.exp(tot - cum)).astype(BF16)
    dec_rows = jnp.exp(jnp.concatenate(last + [jnp.zeros((LANES - nchunk, GLA_QK), F32)], axis=0))
    dec_cols = dec_rows.T

    for c in range(nchunk):
        for h in range(GLA_HEADS):
            ds_ref[c, h] = lax.dot_general(kd[crow(c), kcol(h)], v_ref[crow(c), vcol(h)],
                                           (((0,), (0,)), ((), ())), preferred_element_type=F32)
    for h in range(GLA_HEADS):
        st = state_ref[h]
        for c in range(nchunk):
            st = st * jnp.broadcast_to(dec_cols[kcol(h), c:c + 1], (GLA_DK, GLA_DV)) + ds_ref[c, h]
            sb_ref[c, h] = st.astype(BF16)
        state_ref[h] = st
    scale = float(GLA_DK) ** -0.5
    for c in range(nchunk):
        for h in range(GLA_HEADS):
            of_ref[crow(c), vcol(h)] = jnp.dot(q_ref[crow(c), kcol(h)], sb_ref[c, h],
                                               preferred_element_type=F32) * scale
    for h in range(GLA_HEADS):
        o = of_ref[:, vcol(h)]
        o = o * lax.rsqrt(jnp.mean(o * o, axis=-1, keepdims=True) + LN_EPS) * gn_ref[...]
        r = r_ref[:, vcol(h)].astype(F32)
        o_ref[:, vcol(h)] = (o * (r * jax.nn.sigmoid(r))).astype(o_ref.dtype)


def _gla(gq, gk, gv, gr, small, wgu_pad, b_gate, g_norm, batch, seq):
    rows = GLA_ROWS
    nsteps = seq // rows
    row = lambda w: pl.BlockSpec((rows, w), lambda b, i: (b * nsteps + i, 0))
    const = lambda a: pl.BlockSpec(a.shape, lambda b, i: (0, 0))
    return pl.pallas_call(
        _gla_kernel,
        out_shape=jax.ShapeDtypeStruct((batch * seq, GLA_V), BF16),
        grid=(batch, nsteps),
        in_specs=[row(GLA_QK), row(GLA_QK), row(GLA_V), row(GLA_V), row(LANES),
                  const(wgu_pad), const(b_gate), const(g_norm)],
        out_specs=row(GLA_V),
        scratch_shapes=[pltpu.VMEM((GLA_HEADS, GLA_DK, GLA_DV), F32),
                        pltpu.VMEM((rows // CHUNK, GLA_HEADS, GLA_DK, GLA_DV), F32),
                        pltpu.VMEM((rows // CHUNK, GLA_HEADS, GLA_DK, GLA_DV), BF16),
                        pltpu.VMEM((rows, GLA_V), F32)],
        compiler_params=_cparams(("parallel", "arbitrary"), 56),
        name="gla",
    )(gq, gk, gv, gr, small, wgu_pad, b_gate, g_norm)


def _fox_kernel(q_ref, k_ref, v_ref, kf_ref, qsel_ref, o_ref, qa_ref, vt_ref, s_ref, acc_ref, m_ref, l_ref):
    t = FOX_TQ
    seq = q_ref.shape[0]
    nq = seq // t
    c = (float(FOX_HD) ** -0.5) * LOG2E
    hd = FOX_HD
    heads = range(FOX_HEADS_PER_STEP)
    for h in heads:
        for r in range(0, seq, t):
            qa_ref[h, 0:hd, r:r + t] = (q_ref[r:r + t, h * hd:(h + 1) * hd].astype(F32) * c).T.astype(BF16)
            vt_ref[h, :, r:r + t] = v_ref[r:r + t, h * hd:(h + 1) * hd].astype(F32).T.astype(BF16)
        qa_ref[h, hd:hd + LANES, :] = jnp.tile(qsel_ref[h], (1, seq // LANES))

    def key_rows(j):
        return pl.ds(pl.multiple_of(j * t, t), t)

    def scores(h, i, j):
        ka = jnp.concatenate([k_ref[key_rows(j), h * hd:(h + 1) * hd], kf_ref[key_rows(j), :]], axis=1)
        return jnp.dot(ka, qa_ref[h, :, i * t:(i + 1) * t], preferred_element_type=F32)

    def values(h, j, p):
        return jnp.dot(vt_ref[h, :, key_rows(j)], p, preferred_element_type=F32)

    def step(j, slot, masked, nxt):
        if nxt is not None:
            for h in heads:
                s_ref[1 - slot, h] = scores(h, *nxt)
        for h in heads:
            s = s_ref[slot, h]
            if masked:
                causal = lax.broadcasted_iota(I32, (t, t), 0) <= lax.broadcasted_iota(I32, (t, t), 1)
                s = jnp.where(causal, s, -jnp.inf)
            m_old = m_ref[h]
            m_new = jnp.maximum(m_old, jnp.broadcast_to(jnp.max(s, axis=0, keepdims=True), (SUBLANES, t)))
            alpha = jnp.exp2(m_old - m_new)
            p = jnp.exp2(s - m_new[0:1, :])
            m_ref[h] = m_new
            l_ref[h] = alpha * l_ref[h] + jnp.sum(p, axis=0, keepdims=True)
            acc_ref[h] = alpha[0:1, :] * acc_ref[h] + values(h, j, p.astype(BF16))

    for h in heads:
        s_ref[0, h] = scores(h, 0, 0)
    slot = 0
    for i in range(nq):
        for h in heads:
            acc_ref[h] = jnp.zeros((hd, t), F32)
            m_ref[h] = jnp.full((SUBLANES, t), -jnp.inf, F32)
            l_ref[h] = jnp.zeros((SUBLANES, t), F32)
        if i >= 2:
            def pair(jj, _, i=i, slot=slot):
                step(2 * jj, slot, False, (i, 2 * jj + 1))
                step(2 * jj + 1, 1 - slot, False, (i, 2 * jj + 2))
                return 0

            lax.fori_loop(0, i // 2, pair, 0)
        if i % 2 == 1:
            step(i - 1, slot, False, (i, i))
            slot = 1 - slot
        step(i, slot, True, (i + 1, 0) if i + 1 < nq else None)
        slot = 1 - slot
        for h in heads:
            o_ref[i * t:(i + 1) * t, h * hd:(h + 1) * hd] = (acc_ref[h] / l_ref[h][0:1, :]).T.astype(o_ref.dtype)


def _fox(fq, fk, fv, kf, qsel, batch, seq):
    hps = FOX_HEADS_PER_STEP
    w = hps * FOX_HD
    t = FOX_TQ
    head_cols = pl.BlockSpec((seq, w), lambda b, h: (b, h))
    return pl.pallas_call(
        _fox_kernel,
        out_shape=jax.ShapeDtypeStruct((batch * seq, FOX_W), BF16),
        grid=(batch, FOX_HEADS // hps),
        in_specs=[head_cols, head_cols, head_cols,
                  pl.BlockSpec((seq, LANES), lambda b, h: (b, 0)),
                  pl.BlockSpec((hps, LANES, LANES), lambda b, h: (h, 0, 0))],
        out_specs=head_cols,
        scratch_shapes=[pltpu.VMEM((hps, FOX_HD + LANES, seq), BF16), pltpu.VMEM((hps, FOX_HD, seq), BF16),
                        pltpu.VMEM((2, hps, t, t), F32), pltpu.VMEM((hps, FOX_HD, t), F32),
                        pltpu.VMEM((hps, SUBLANES, t), F32), pltpu.VMEM((hps, SUBLANES, t), F32)],
        compiler_params=_cparams(("parallel", "parallel"), 48),
        name="fox",
    )(fq, fk, fv, kf, qsel)


def _post_mix_kernel(gla_ref, fox_ref, zg_ref, zf_ref, x_ref, gin_ref, bin_ref, wg_hbm, wf_hbm, wo_hbm, g_ref, b_ref,
                     wr_ref, br_ref, h2_ref, idx_ref, gate_ref, wg_ref, wf_ref, wo_ref, stage_ref, sem):
    tm = x_ref.shape[0]

    @pl.when(pl.program_id(0) == 0)
    def _():
        for src, dst in ((wg_hbm, wg_ref), (wf_hbm, wf_ref), (wo_hbm, wo_ref)):
            cp = pltpu.make_async_copy(src, stage_ref, sem)
            cp.start()
            cp.wait()
            dst[...] = stage_ref[...].astype(BF16)

    y_gla = jnp.dot(gla_ref[...], wg_ref[...], preferred_element_type=F32)
    y_fox = jnp.dot(fox_ref[...], wf_ref[...], preferred_element_type=F32)
    mixed = (jax.nn.sigmoid(zg_ref[...].astype(F32)) * y_gla
             + jax.nn.sigmoid(zf_ref[...].astype(F32)) * y_fox)
    h = _layer_norm(x_ref[...], gin_ref[...], bin_ref[...])
    pre = DEEPNORM_ALPHA * h + jnp.dot(mixed.astype(BF16), wo_ref[...], preferred_element_type=F32)
    h2 = _layer_norm(pre, g_ref[...], b_ref[...])
    h2_ref[...] = h2

    logits = lax.dot_general(wr_ref[...], h2.astype(BF16), (((1,), (1,)), ((), ())),
                             preferred_element_type=F32)
    logits = logits + jnp.tile(br_ref[...], (1, tm // LANES))
    eidx = lax.broadcasted_iota(I32, (N_EXPERTS, tm), 0)
    vals, idxs = [], []
    for _ in range(TOP_K):
        mx = jnp.max(logits, axis=0, keepdims=True)
        ik = jnp.min(jnp.where(logits == mx, eidx, N_EXPERTS), axis=0, keepdims=True)
        vals.append(mx)
        idxs.append(ik)
        logits = jnp.where(eidx == ik, -jnp.inf, logits)
    exps = [jnp.exp(v - vals[0]) for v in vals]
    denom = exps[0] + exps[1] + exps[2] + exps[3]
    rid = lax.broadcasted_iota(I32, (SUBLANES, tm), 0)
    idx8 = jnp.zeros((SUBLANES, tm), I32)
    gate8 = jnp.zeros((SUBLANES, tm), F32)
    for k in range(TOP_K):
        idx8 = jnp.where(rid == k, idxs[k], idx8)
        gate8 = jnp.where(rid == k, exps[k] / denom, gate8)
    idx_ref[...] = idx8
    gate_ref[...] = gate8


def _post_mix(gla_o, fox_o, zg, zf, x2, g_in, b_in, wg, wf, wo, g, b, wr_t, br_rep):
    t = x2.shape[0]
    tm = TM_PROJ
    row = lambda w: pl.BlockSpec((tm, w), lambda i: (i, 0))
    const = lambda a: pl.BlockSpec(a.shape, lambda i: (0, 0))
    hbm = pl.BlockSpec(memory_space=pl.ANY)
    assert wg.shape == wf.shape == wo.shape
    return pl.pallas_call(
        _post_mix_kernel,
        out_shape=[jax.ShapeDtypeStruct((t, D_MODEL), F32),
                   jax.ShapeDtypeStruct((SUBLANES, t), I32),
                   jax.ShapeDtypeStruct((SUBLANES, t), F32)],
        grid=(t // tm,),
        in_specs=[row(GLA_V), row(FOX_W), row(D_MODEL), row(D_MODEL), row(D_MODEL), const(g_in), const(b_in),
                  hbm, hbm, hbm, const(g), const(b), const(wr_t), const(br_rep)],
        out_specs=[row(D_MODEL),
                   pl.BlockSpec((SUBLANES, tm), lambda i: (0, i)), pl.BlockSpec((SUBLANES, tm), lambda i: (0, i))],
        scratch_shapes=[pltpu.VMEM(wg.shape, BF16), pltpu.VMEM(wf.shape, BF16), pltpu.VMEM(wo.shape, BF16),
                        pltpu.VMEM(wg.shape, F32), pltpu.SemaphoreType.DMA(())],
        compiler_params=_cparams(("arbitrary",), 48),
        name="post_mix",
    )(gla_o, fox_o, zg, zf, x2, g_in, b_in, wg, wf, wo, g, b, wr_t, br_rep)


def _onehot_rows(idx8, tw):
    eidx = lax.broadcasted_iota(I32, (N_EXPERTS, tw), 0)
    hit = eidx == idx8[0:1, :]
    for k in range(1, TOP_K):
        hit = hit | (eidx == idx8[k:k + 1, :])
    return eidx, hit


def _rank_kernel(idx_ref, rank_ref, counts_ref, carry_ref):
    tw = idx_ref.shape[1]

    @pl.when(pl.program_id(0) == 0)
    def _():
        carry_ref[...] = jnp.zeros_like(carry_ref)

    idx8 = idx_ref[...]
    eidx, hit = _onehot_rows(idx8, tw)
    onehot = jnp.where(hit, 1.0, 0.0).astype(BF16)
    ri = lax.broadcasted_iota(I32, (tw, tw + LANES), 0)
    ci = lax.broadcasted_iota(I32, (tw, tw + LANES), 1)
    upper = jnp.where((ri < ci) | (ci >= tw), 1.0, 0.0).astype(BF16)
    cnt = jnp.dot(onehot, upper, preferred_element_type=F32)
    before = cnt[:, :tw] + jnp.tile(carry_ref[...], (1, tw // LANES))
    rid = lax.broadcasted_iota(I32, (SUBLANES, tw), 0)
    rank8 = jnp.zeros((SUBLANES, tw), I32)
    for k in range(TOP_K):
        rk = jnp.sum(jnp.where(eidx == idx8[k:k + 1, :], before, 0.0), axis=0, keepdims=True)
        rank8 = jnp.where(rid == k, rk.astype(I32), rank8)
    rank_ref[...] = rank8
    carry_ref[...] = carry_ref[...] + cnt[:, tw:]
    counts_ref[...] = carry_ref[...]


def _rank(idx_t):
    t = idx_t.shape[1]
    tw = RANK_TW
    return pl.pallas_call(
        _rank_kernel,
        out_shape=[jax.ShapeDtypeStruct((SUBLANES, t), I32),
                   jax.ShapeDtypeStruct((N_EXPERTS, LANES), F32)],
        grid=(t // tw,),
        in_specs=[pl.BlockSpec((SUBLANES, tw), lambda i: (0, i))],
        out_specs=[pl.BlockSpec((SUBLANES, tw), lambda i: (0, i)),
                   pl.BlockSpec((N_EXPERTS, LANES), lambda i: (0, 0))],
        scratch_shapes=[pltpu.VMEM((N_EXPERTS, LANES), F32)],
        compiler_params=_cparams(("arbitrary",), 32),
        name="rank",
    )(idx_t)


def _dest_kernel(idx_ref, rank_ref, counts_ref, dest_ref, blk_e_ref, ends_ref, *, nb_pad):
    tw = idx_ref.shape[1]
    nblk = jnp.floor((counts_ref[...] + (MOE_BLK - 1)) * (1.0 / MOE_BLK))
    ri = lax.broadcasted_iota(I32, (N_EXPERTS, N_EXPERTS), 0)
    ci = lax.broadcasted_iota(I32, (N_EXPERTS, N_EXPERTS), 1)
    tri = jnp.where(ci <= ri, 1.0, 0.0).astype(BF16)
    end_blk = jnp.dot(tri, nblk.astype(BF16), preferred_element_type=F32)
    start_row = (end_blk - nblk) * float(MOE_BLK)
    idx8 = idx_ref[...]
    eidx = lax.broadcasted_iota(I32, (N_EXPERTS, tw), 0)
    start_t = jnp.tile(start_row, (1, tw // LANES))
    rid = lax.broadcasted_iota(I32, (SUBLANES, tw), 0)
    dest8 = jnp.zeros((SUBLANES, tw), I32)
    for k in range(TOP_K):
        st = jnp.sum(jnp.where(eidx == idx8[k:k + 1, :], start_t, 0.0), axis=0, keepdims=True)
        dest8 = jnp.where(rid == k, st.astype(I32), dest8)
    dest_ref[...] = dest8 + rank_ref[...]
    bid = lax.broadcasted_iota(I32, (N_EXPERTS, nb_pad), 1).astype(F32)
    ends_t = jnp.tile(end_blk, (1, nb_pad // LANES))
    be = jnp.sum(jnp.where(ends_t <= bid, 1.0, 0.0), axis=0, keepdims=True)
    blk_e_ref[...] = jnp.broadcast_to(jnp.minimum(be, N_EXPERTS - 1.0), (SUBLANES, nb_pad)).astype(I32)
    ends_ref[...] = end_blk.astype(I32)


def _dest(idx_t, rank_t, counts, nb_pad):
    t = idx_t.shape[1]
    tw = RANK_TW
    tok = pl.BlockSpec((SUBLANES, tw), lambda i: (0, i))
    return pl.pallas_call(
        functools.partial(_dest_kernel, nb_pad=nb_pad),
        out_shape=[jax.ShapeDtypeStruct((SUBLANES, t), I32),
                   jax.ShapeDtypeStruct((SUBLANES, nb_pad), I32),
                   jax.ShapeDtypeStruct((N_EXPERTS, LANES), I32)],
        grid=(t // tw,),
        in_specs=[tok, tok, pl.BlockSpec((N_EXPERTS, LANES), lambda i: (0, 0))],
        out_specs=[tok, pl.BlockSpec((SUBLANES, nb_pad), lambda i: (0, 0)),
                   pl.BlockSpec((N_EXPERTS, LANES), lambda i: (0, 0))],
        compiler_params=_cparams(("arbitrary",), 32),
        name="dest",
    )(idx_t, rank_t, counts)


def _dispatch_kernel(ends_ref, dest_ref, h2_ref, xs_ref, slab_ref, zero_ref, sem, zsem):
    td = h2_ref.shape[0]

    nb = xs_ref.shape[0] // MOE_BLK
    n_used = ends_ref[N_EXPERTS - 1]

    def zero_block(blk):
        start = pl.multiple_of(blk * MOE_BLK, MOE_BLK)
        return pltpu.make_async_copy(zero_ref, xs_ref.at[pl.ds(start, MOE_BLK)], zsem)

    def has_rows(e):
        prev = jnp.where(e == 0, 0, ends_ref[jnp.maximum(e - 1, 0)])
        return ends_ref[e] > prev

    @pl.when(pl.program_id(0) == 0)
    def _():
        zero_ref[...] = jnp.zeros_like(zero_ref)

        def each(fn):
            def last_block(e, _):
                @pl.when(has_rows(e))
                def _():
                    fn(zero_block(ends_ref[e] - 1))
                return 0

            def tail_block(blk, _):
                fn(zero_block(blk))
                return 0

            lax.fori_loop(0, N_EXPERTS, last_block, 0)
            lax.fori_loop(n_used, nb, tail_block, 0)

        each(lambda cp: cp.start())
        each(lambda cp: cp.wait())

    slab_ref[...] = pltpu.einshape("t(cl)->tcl", h2_ref[...], c=SLAB_ROWS)

    def start_rows(t, _):
        for k in range(TOP_K):
            pltpu.make_async_copy(slab_ref.at[t], xs_ref.at[dest_ref[k, t]], sem).start(priority=k % 2)
        return 0

    lax.fori_loop(0, td, start_rows, 0, unroll=ROW_DMA_UNROLL)
    for k in range(TOP_K):
        pltpu.make_async_copy(slab_ref, xs_ref.at[pl.ds(0, td)], sem).wait()


def _dispatch(ends, dest_t, h2, nb):
    t = h2.shape[0]
    td = DISPATCH_TOK
    grid_spec = pltpu.PrefetchScalarGridSpec(
        num_scalar_prefetch=1,
        grid=(t // td,),
        in_specs=[pl.BlockSpec((SUBLANES, td), lambda i, ends: (0, i), memory_space=pltpu.SMEM),
                  pl.BlockSpec((td, D_MODEL), lambda i, ends: (i, 0))],
        out_specs=pl.BlockSpec(memory_space=pl.ANY),
        scratch_shapes=[pltpu.VMEM((td, SLAB_ROWS, LANES), F32), pltpu.VMEM((MOE_BLK, SLAB_ROWS, LANES), F32),
                        pltpu.SemaphoreType.DMA(()), pltpu.SemaphoreType.DMA(())],
    )
    return pl.pallas_call(
        _dispatch_kernel,
        out_shape=jax.ShapeDtypeStruct((nb * MOE_BLK, SLAB_ROWS, LANES), F32),
        grid_spec=grid_spec,
        compiler_params=_cparams(("arbitrary",), 32),
        name="dispatch",
    )(ends, dest_t, h2)


def _experts_kernel(blk_e_ref, nused_ref, ends_ref, cnt_ref, xs_hbm, wgu_hbm, bgu_ref, wd_hbm, bd_ref, ys_hbm,
                    xbuf, ybuf, wgu_f32, wd_f32, wgu_bf, wd_bf, slot_ref, sems, xsems, ysems):
    nb = xs_hbm.shape[0] // MOE_BLK

    def block(j, _):
        _experts_block(j, nb, blk_e_ref, nused_ref, ends_ref, cnt_ref, xs_hbm, wgu_hbm, bgu_ref, wd_hbm, bd_ref, ys_hbm,
                       xbuf, ybuf, wgu_f32, wd_f32, wgu_bf, wd_bf, slot_ref, sems, xsems, ysems)
        return 0

    lax.fori_loop(0, nb, block, 0)


def _experts_block(j, nb, blk_e_ref, nused_ref, ends_ref, cnt_ref, xs_hbm, wgu_hbm, bgu_ref, wd_hbm, bd_ref, ys_hbm,
                   xbuf, ybuf, wgu_f32, wd_f32, wgu_bf, wd_bf, slot_ref, sems, xsems, ysems):
    par = lax.rem(j, 2)
    n_used = nused_ref[0]
    active = j < n_used
    e = blk_e_ref[j]
    prev = blk_e_ref[jnp.maximum(j - 1, 0)]
    fresh = (j == 0) | (e != prev)

    def fetch(expert, slot):
        return (pltpu.make_async_copy(wgu_hbm.at[expert], wgu_f32.at[slot], sems.at[slot, 0]),
                pltpu.make_async_copy(wd_hbm.at[expert], wd_f32.at[slot], sems.at[slot, 1]))

    def x_load(blk, half):
        rows = pl.ds(pl.multiple_of(blk * MOE_BLK, MOE_BLK), MOE_BLK)
        return [pltpu.make_async_copy(xs_hbm.at[rows, c, :], xbuf.at[half, :, pl.ds(c * LANES, LANES)], xsems.at[half])
                for c in range(SLAB_ROWS)]

    def y_store(blk, half):
        rows = pl.ds(pl.multiple_of(blk * MOE_BLK, MOE_BLK), MOE_BLK)
        return [pltpu.make_async_copy(ybuf.at[half, :, pl.ds(c * LANES, LANES)], ys_hbm.at[rows, c, :], ysems.at[half])
                for c in range(SLAB_ROWS)]

    @pl.when(j == 0)
    def _():
        ybuf[...] = jnp.zeros_like(ybuf)
        for cp in x_load(0, 0):
            cp.start()

    @pl.when(j + 1 < n_used)
    def _():
        for cp in x_load(j + 1, 1 - par):
            cp.start()

    @pl.when(j >= 2)
    def _():
        for cp in y_store(j - 2, par):
            cp.wait()

    @pl.when(j == 0)
    def _():
        slot_ref[0] = 0
        for cp in fetch(e, 0):
            cp.start()

    @pl.when(active & fresh)
    def _():
        slot = slot_ref[0]
        for cp in fetch(e, slot):
            cp.wait()
        wgu_bf[...] = wgu_f32[slot].astype(BF16)
        wd_bf[...] = wd_f32[slot].astype(BF16)
        next_blk = ends_ref[e]

        @pl.when(next_blk < n_used)
        def _():
            for cp in fetch(blk_e_ref[jnp.minimum(next_blk, n_used - 1)], 1 - slot):
                cp.start()

        slot_ref[0] = 1 - slot

    @pl.when(active)
    def _():
        for cp in x_load(j, par):
            cp.wait()

    first_blk = jnp.where(e == 0, 0, ends_ref[jnp.maximum(e - 1, 0)])
    valid = jnp.clip(cnt_ref[e] - (j - first_blk) * MOE_BLK, 0, MOE_BLK)
    groups = lax.shift_right_logical(valid + (EXPERT_ROW_GROUP - 1), EXPERT_ROW_GROUP.bit_length() - 1)

    def compute(m):
        x = xbuf[par, 0:m].astype(BF16)
        gu = jnp.dot(x, wgu_bf[...], preferred_element_type=F32) + bgu_ref[pl.ds(e, 1), :]
        gate = jnp.minimum(gu[:, :D_FF], SWIGLU_LIMIT)
        up = jnp.clip(gu[:, D_FF:], -SWIGLU_LIMIT, SWIGLU_LIMIT)
        glu = gate * jax.nn.sigmoid(SWIGLU_ALPHA * gate)
        act = ((up + 1.0) * glu).astype(BF16)
        ybuf[par, 0:m] = jnp.dot(act, wd_bf[...], preferred_element_type=F32) + bd_ref[pl.ds(e, 1), :]

    for g in range(1, MOE_BLK // EXPERT_ROW_GROUP + 1):
        pl.when(active & (groups == g))(functools.partial(compute, g * EXPERT_ROW_GROUP))

    @pl.when(jnp.logical_not(active))
    def _():
        ybuf[par] = jnp.zeros((MOE_BLK, D_MODEL), F32)

    for cp in y_store(j, par):
        cp.start()

    @pl.when(j == nb - 1)
    def _():
        for cp in y_store(j, par) + y_store(j - 1, 1 - par):
            cp.wait()


def _experts(blk_e, nused, ends, cnt, xs, w_gate_up, b_gate_up, w_down, b_down, nb):
    assert nb >= 2
    whole = lambda a: pl.BlockSpec(a.shape, lambda i, be, nu, en, cn: (0,) * a.ndim)
    grid_spec = pltpu.PrefetchScalarGridSpec(
        num_scalar_prefetch=4,
        grid=(1,),
        in_specs=[pl.BlockSpec(memory_space=pl.ANY),
                  pl.BlockSpec(memory_space=pl.ANY),
                  whole(b_gate_up),
                  pl.BlockSpec(memory_space=pl.ANY),
                  whole(b_down)],
        out_specs=pl.BlockSpec(memory_space=pl.ANY),
        scratch_shapes=[pltpu.VMEM((2, MOE_BLK, D_MODEL), F32), pltpu.VMEM((2, MOE_BLK, D_MODEL), F32),
                        pltpu.VMEM((2, D_MODEL, 2 * D_FF), F32), pltpu.VMEM((2, D_FF, D_MODEL), F32),
                        pltpu.VMEM((D_MODEL, 2 * D_FF), BF16), pltpu.VMEM((D_FF, D_MODEL), BF16),
                        pltpu.SMEM((1,), I32), pltpu.SemaphoreType.DMA((2, 2)),
                        pltpu.SemaphoreType.DMA((2,)), pltpu.SemaphoreType.DMA((2,))],
    )
    return pl.pallas_call(
        _experts_kernel,
        out_shape=jax.ShapeDtypeStruct((nb * MOE_BLK, SLAB_ROWS, LANES), F32),
        grid_spec=grid_spec,
        compiler_params=_cparams(("arbitrary",), 56),
        name="experts",
    )(blk_e, nused, ends, cnt, xs, w_gate_up, b_gate_up, w_down, b_down)


def _combine_kernel(dest_ref, dest_next_ref, gate_ref, h2_ref, g_ref, b_ref, ys_ref, o_ref, buf_ref, ffn_ref, sems):
    tc = h2_ref.shape[0]
    i = pl.program_id(0)
    slot = lax.rem(i, 2)

    def gather(d_ref, s):
        def start_rows(t, _):
            for k in range(TOP_K):
                pltpu.make_async_copy(ys_ref.at[d_ref[k, t]], buf_ref.at[s, k, t], sems.at[s]).start(priority=k % 2)
            return 0

        lax.fori_loop(0, tc, start_rows, 0, unroll=ROW_DMA_UNROLL)

    @pl.when(i == 0)
    def _():
        gather(dest_ref, 0)

    @pl.when(i + 1 < pl.num_programs(0))
    def _():
        gather(dest_next_ref, 1 - slot)

    for k in range(TOP_K):
        pltpu.make_async_copy(ys_ref.at[pl.ds(0, tc)], buf_ref.at[slot, k], sems.at[slot]).wait()

    def weigh(t, _):
        acc = gate_ref[0, t] * buf_ref[slot, 0, t]
        for k in range(1, TOP_K):
            acc = acc + gate_ref[k, t] * buf_ref[slot, k, t]
        ffn_ref[t] = acc
        return 0

    lax.fori_loop(0, tc, weigh, 0, unroll=ROW_DMA_UNROLL)
    ffn = pltpu.einshape("tcl->t(cl)", ffn_ref[...])
    o_ref[...] = _layer_norm(DEEPNORM_ALPHA * h2_ref[...] + ffn, g_ref[...], b_ref[...])


def _combine(dest_t, gate_t, h2, g, b, ys):
    t = h2.shape[0]
    tc = COMBINE_TOK
    last = t // tc - 1
    return pl.pallas_call(
        _combine_kernel,
        out_shape=jax.ShapeDtypeStruct((t, D_MODEL), F32),
        grid=(t // tc,),
        in_specs=[pl.BlockSpec((SUBLANES, tc), lambda i: (0, i), memory_space=pltpu.SMEM),
                  pl.BlockSpec((SUBLANES, tc), lambda i: (0, jnp.minimum(i + 1, last)), memory_space=pltpu.SMEM),
                  pl.BlockSpec((SUBLANES, tc), lambda i: (0, i), memory_space=pltpu.SMEM),
                  pl.BlockSpec((tc, D_MODEL), lambda i: (i, 0)),
                  pl.BlockSpec((1, D_MODEL), lambda i: (0, 0)),
                  pl.BlockSpec((1, D_MODEL), lambda i: (0, 0)),
                  pl.BlockSpec(memory_space=pl.ANY)],
        out_specs=pl.BlockSpec((tc, D_MODEL), lambda i: (i, 0)),
        scratch_shapes=[pltpu.VMEM((2, TOP_K, tc, SLAB_ROWS, LANES), F32), pltpu.VMEM((tc, SLAB_ROWS, LANES), F32),
                        pltpu.SemaphoreType.DMA((2,))],
        compiler_params=_cparams(("arbitrary",), 32),
        name="combine",
    )(dest_t, dest_t, gate_t, h2, g, b, ys)


def kernel(x, ln_in_g, ln_in_b, w_in, w_gla_gate_up, b_gla_gate, g_gla_norm, b_forget, w_gla_proj, w_fox_proj, w_out, ln_mix_g, ln_mix_b, w_router, b_router, w_gate_up, b_gate_up, w_down, b_down, ln_ffn_g, ln_ffn_b):
    batch, seq, d = x.shape
    assert d == D_MODEL and w_in.shape[0] == DEPTH == 1
    assert seq % max(GLA_ROWS, FOX_TQ, CUM_ROWS) == 0
    t = batch * seq
    assert t % max(TM_PROJ, RANK_TW, DISPATCH_TOK, COMBINE_TOK) == 0 and t // MOE_BLK <= 256
    row = lambda v: v.reshape(1, -1).astype(F32)

    n_forget = FOX_HEADS * FORGET_PIECES
    lane = jnp.arange(LANES)
    in_forget = (lane >= FORGET_COL0) & (lane < FORGET_COL0 + n_forget)
    piece = (lane - FORGET_COL0) % FORGET_PIECES
    forget_bias = jnp.zeros((1, LANES), F32).at[0, FORGET_COL0:FORGET_COL0 + n_forget].set(
        jnp.repeat(b_forget[0].astype(F32), FORGET_PIECES))
    piece_sel = jnp.stack([(in_forget & (piece == p)).astype(F32) for p in range(FORGET_PIECES)]
                          + [jnp.zeros((LANES,), F32)] * (SUBLANES - FORGET_PIECES))
    head_of_lane = (lane - FORGET_COL0) // FORGET_PIECES
    qsel = jnp.broadcast_to(jnp.stack([(in_forget & (head_of_lane == h)) for h in range(FOX_HEADS)])[:, :, None],
                            (FOX_HEADS, LANES, LANES)).astype(BF16)
    wgu_pad = jnp.concatenate([w_gla_gate_up[0], jnp.zeros((LANES - GLA_RANK, GLA_QK), F32)], axis=0).astype(BF16)

    x2 = x.reshape(t, d)
    gq, gk, gv, gr, fq, fk, fv, zg, zf, small = _ln_inproj(x2, row(ln_in_g), row(ln_in_b), jnp.transpose(w_in[0]))
    kf = _forget_cum(small, forget_bias, piece_sel, batch, seq)
    gla_o = _gla(gq, gk, gv, gr, small, wgu_pad, row(b_gla_gate[0]), row(g_gla_norm[0]), batch, seq)
    fox_o = _fox(fq, fk, fv, kf, qsel, batch, seq)

    br_rep = jnp.broadcast_to(b_router[0].astype(F32)[:, None], (N_EXPERTS, LANES))
    h2, idx_t, gate_t = _post_mix(
        gla_o, fox_o, zg, zf, x2, row(ln_in_g), row(ln_in_b),
        w_gla_proj[0], w_fox_proj[0], w_out[0],
        row(ln_mix_g[0]), row(ln_mix_b[0]), w_router[0].T.astype(BF16), br_rep)

    nb = (t * TOP_K + N_EXPERTS * (MOE_BLK - 1) + MOE_BLK - 1) // MOE_BLK
    nb_pad = (nb + LANES - 1) // LANES * LANES
    rank_t, counts = _rank(idx_t)
    dest_t, blk_e8, ends = _dest(idx_t, rank_t, counts, nb_pad)
    ends1 = ends[:, 0]
    xs = _dispatch(ends1, dest_t, h2, nb)
    ys = _experts(blk_e8[0, :nb], ends1[N_EXPERTS - 1:], ends1, counts[:, 0].astype(I32), xs, w_gate_up[0], b_gate_up[0],
                  w_down[0], b_down[0], nb)
    out = _combine(dest_t, gate_t, h2, row(ln_ffn_g[0]), row(ln_ffn_b[0]), ys)
    return out.reshape(batch, seq, d)
```
